```python
import math
import jax, jax.numpy as jnp
from jax import lax
import numpy as np

D_MODEL = 1024
BATCH = 8
SEQ = 4096
DEPTH = 2

N_MIXERS = 4
HEADS_PER_GROUP = 4
GROUP_WIDTH = D_MODEL // N_MIXERS
HD = GROUP_WIDTH // HEADS_PER_GROUP
MIX_WIDTH = N_MIXERS * GROUP_WIDTH

NSA_HEADS = HEADS_PER_GROUP
NSA_CMP_LEN = 32
NSA_CMP_STRIDE = 16
NSA_CMP_HIDDEN = 4 * HD
NSA_SEL_LEN = 64
NSA_TOPK = 16
NSA_WINDOW = 512

DIFF_HEADS = HEADS_PER_GROUP
DIFF_QK_DIM = HD // 2
DIFF_V_DIM = HD

GLA_HEADS = HEADS_PER_GROUP
GLA_DK = HD // 2
GLA_DV = HD
GLA_GATE_RANK = 16
GLA_TAU = 16.0

HG_HEADS = HEADS_PER_GROUP
HG_DK = HD
HG_DV = HD

CHUNK = 64
Q_BLOCK = 128

D_FF = 7 * D_MODEL // 2
N_EXPERTS = 8
TOP_K = 2
D_FF_EXPERT = 7 * D_MODEL // 2
PLE_DIM = 256
EPS = 1e-6
NEG = -1e30
BIG = 1e30

IN_SPLITS = (
    NSA_HEADS * HD, HD, HD, HD, HD, HD, HD, NSA_HEADS * 3,
    DIFF_HEADS * 2 * DIFF_QK_DIM, DIFF_HEADS * 2 * DIFF_QK_DIM, DIFF_HEADS * DIFF_V_DIM,
    GLA_HEADS * GLA_DK, GLA_HEADS * GLA_DK, GLA_HEADS * GLA_DV, GLA_GATE_RANK, GLA_HEADS * GLA_DV,
    HG_HEADS * HG_DK, HG_HEADS * HG_DK, HG_HEADS * HG_DV, HG_HEADS * HG_DV,
)
IN_COLS = sum(IN_SPLITS)

kernel_name = 'hybrid_nsa_diff_gla_hgrn2_moe_block'


def rms_norm(x, gain):
    xf = x.astype(jnp.float32)
    y = xf * lax.rsqrt(jnp.mean(xf * xf, axis=-1, keepdims=True) + EPS)
    return (y * gain.astype(jnp.float32)).astype(x.dtype)


def split_cols(u, sizes):
    offsets = np.cumsum(sizes)[:-1].tolist()
    return jnp.split(u, offsets, axis=-1)


def chunked_gated_linear_attention(q, k, v, log_g):
    B, S, H, DK = q.shape
    DV = v.shape[-1]
    n_chunks = S // CHUNK

    def to_chunks(t):
        return t.astype(jnp.float32).reshape(B, n_chunks, CHUNK, H, -1).transpose(1, 0, 3, 2, 4)

    qc, kc, vc = to_chunks(q), to_chunks(k), to_chunks(v)
    bc = jnp.cumsum(to_chunks(log_g), axis=3)
    causal = np.tril(np.ones((CHUNK, CHUNK), dtype=bool))[None, None, :, :, None]

    def step(state, inp):
        q_, k_, v_, b_ = inp
        o_inter = jnp.einsum('bhik,bhkv->bhiv', q_ * jnp.exp(b_), state)
        rel = jnp.where(causal, b_[:, :, :, None, :] - b_[:, :, None, :, :], NEG)
        attn = jnp.einsum('bhik,bhjk,bhijk->bhij', q_, k_, jnp.exp(rel))
        o_intra = jnp.einsum('bhij,bhjv->bhiv', attn, v_)
        b_last = b_[:, :, -1:, :]
        new_state = (jnp.exp(b_last[:, :, 0, :, None]) * state
                     + jnp.einsum('bhjk,bhjv->bhkv', k_ * jnp.exp(b_last - b_), v_))
        return new_state, o_inter + o_intra

    state0 = jnp.zeros((B, H, DK, DV), jnp.float32)
    _, o = lax.scan(step, state0, (qc, kc, vc, bc))
    return o.transpose(1, 0, 3, 2, 4).reshape(B, S, H, DV).astype(v.dtype)


def nsa_attention(q, k_cmp, v_cmp, k_slc, v_slc, k_win, v_win, gate_logits,
                  cmp_pos, cmp_w1, cmp_w2, qk_gain):
    B, S, H, _ = q.shape
    scale = HD ** -0.5
    q = rms_norm(q, qk_gain[0])
    t_pos = np.arange(S)

    n_cmp = (S - NSA_CMP_LEN) // NSA_CMP_STRIDE + 1
    cmp_idx = np.arange(n_cmp)[:, None] * NSA_CMP_STRIDE + np.arange(NSA_CMP_LEN)[None, :]

    def compress(t, j):
        blocks = t[:, cmp_idx] + cmp_pos[j]
        hidden = jax.nn.silu(blocks.reshape(B, n_cmp, NSA_CMP_LEN * HD) @ cmp_w1[j])
        return hidden @ cmp_w2[j]

    kc = rms_norm(compress(k_cmp, 0), qk_gain[1])
    vc = compress(v_cmp, 1)
    cmp_ok = cmp_idx[:, -1][None, :] <= t_pos[:, None]
    s_c = jnp.einsum('bthd,bnd->bhtn', q, kc).astype(jnp.float32) * scale
    p_c = jax.nn.softmax(jnp.where(cmp_ok, s_c, NEG), axis=-1) * cmp_ok
    o_c = jnp.einsum('bhtn,bnd->bthd', p_c.astype(vc.dtype), vc)

    n_sel = S // NSA_SEL_LEN
    k_sel = min(NSA_TOPK, n_sel)
    sel_start = np.arange(n_sel) * NSA_SEL_LEN
    cmp_start = cmp_idx[:, 0]
    overlap = ((cmp_start[:, None] <= sel_start[None, :] + NSA_SEL_LEN - 1)
               & (cmp_start[:, None] + NSA_CMP_LEN - 1 >= sel_start[None, :])).astype(np.float32)
    importance = jnp.einsum('bhtn,nm->btm', p_c, overlap)
    blk = np.arange(n_sel)[None, :]
    cur = (t_pos // NSA_SEL_LEN)[:, None]
    forced = (blk == 0) | (blk == cur) | (blk == cur - 1)
    importance = jnp.where(forced, BIG,
                           jnp.where(sel_start[None, :] <= t_pos[:, None], importance, NEG))
    _, sel = lax.top_k(lax.stop_gradient(importance), k_sel)

    kb = rms_norm(k_slc, qk_gain[2]).reshape(B, n_sel, NSA_SEL_LEN, HD)
    vb = v_slc.reshape(B, n_sel, NSA_SEL_LEN, HD)
    kw_pad = jnp.pad(rms_norm(k_win, qk_gain[3]), ((0, 0), (NSA_WINDOW, 0), (0, 0)))
    vw_pad = jnp.pad(v_win, ((0, 0), (NSA_WINDOW, 0), (0, 0)))
    band = NSA_WINDOW + Q_BLOCK
    gather_blocks = jax.vmap(lambda t, i: t[i])

    def block_body(args):
        qb, selb, t0 = args
        tq = t0 + jnp.arange(Q_BLOCK)
        kg = gather_blocks(kb, selb)
        vg = gather_blocks(vb, selb)
        s = jnp.einsum('bqhd,bqnld->bhqnl', qb, kg).astype(jnp.float32) * scale
        kpos = selb[..., None] * NSA_SEL_LEN + jnp.arange(NSA_SEL_LEN)
        s = jnp.where((kpos <= tq[None, :, None, None])[:, None], s, NEG)
        p = jax.nn.softmax(s.reshape(B, H, Q_BLOCK, k_sel * NSA_SEL_LEN), axis=-1).reshape(s.shape)
        o_s = jnp.einsum('bhqnl,bqnld->bqhd', p.astype(vg.dtype), vg)
        kwb = lax.dynamic_slice_in_dim(kw_pad, t0, band, axis=1)
        vwb = lax.dynamic_slice_in_dim(vw_pad, t0, band, axis=1)
        s = jnp.einsum('bqhd,bkd->bhqk', qb, kwb).astype(jnp.float32) * scale
        kp = t0 - NSA_WINDOW + jnp.arange(band)
        m = ((kp[None, :] <= tq[:, None]) & (kp[None, :] > tq[:, None] - NSA_WINDOW)
             & (kp[None, :] >= 0))
        p = jax.nn.softmax(jnp.where(m, s, NEG), axis=-1)
        o_w = jnp.einsum('bhqk,bkd->bqhd', p.astype(vwb.dtype), vwb)
        return o_s, o_w

    nq = S // Q_BLOCK
    qs = q.reshape(B, nq, Q_BLOCK, H, HD).transpose(1, 0, 2, 3, 4)
    sels = sel.reshape(B, nq, Q_BLOCK, k_sel).transpose(1, 0, 2, 3)
    t0s = jnp.arange(nq, dtype=jnp.int32) * Q_BLOCK
    o_s, o_w = lax.map(block_body, (qs, sels, t0s))
    o_s = o_s.transpose(1, 0, 2, 3, 4).reshape(B, S, H, HD)
    o_w = o_w.transpose(1, 0, 2, 3, 4).reshape(B, S, H, HD)

    g = jax.nn.sigmoid(gate_logits.astype(jnp.float32)).reshape(B, S, H, 3).astype(q.dtype)
    o = g[..., 0:1] * o_c + g[..., 1:2] * o_s + g[..., 2:3] * o_w
    return o.reshape(B, S, H * HD)


def diff_attention(q, k, v, qk_gain, lam, sub_gain, layer_idx):
    B, S, H = q.shape[:3]
    q = rms_norm(q, qk_gain[0])
    k = rms_norm(k, qk_gain[1])
    lam_init = 0.8 - 0.6 * math.exp(-0.3 * layer_idx)
    lam32 = lam.astype(jnp.float32)
    lam_full = (jnp.exp(jnp.sum(lam32[0] * lam32[1])) - jnp.exp(jnp.sum(lam32[2] * lam32[3]))
                + lam_init)
    scale = DIFF_QK_DIM ** -0.5
    outs = []
    for blk in range(S // Q_BLOCK):
        t0, t1 = blk * Q_BLOCK, (blk + 1) * Q_BLOCK
        s = jnp.einsum('bqhcd,bkhcd->bhcqk', q[:, t0:t1], k[:, :t1]).astype(jnp.float32) * scale
        mask = np.arange(t1)[None, :] <= np.arange(t0, t1)[:, None]
        p = jax.nn.softmax(jnp.where(mask, s, NEG), axis=-1)
        w = p[:, :, 0] - lam_full * p[:, :, 1]
        outs.append(jnp.einsum('bhqk,bkhd->bqhd', w.astype(v.dtype), v[:, :t1]))
    o = jnp.concatenate(outs, axis=1)
    o = rms_norm(o, sub_gain) * (1.0 - lam_init)
    return o.reshape(B, S, H * DIFF_V_DIM)


def gla_mixer(q, k, v, g_lr, og, w_gate2, b_gate, norm_gain):
    B, S = q.shape[:2]
    log_a = jax.nn.log_sigmoid((g_lr @ w_gate2 + b_gate).astype(jnp.float32)) / GLA_TAU
    shp = (B, S, GLA_HEADS, GLA_DK)
    o = chunked_gated_linear_attention(q.reshape(shp) * (GLA_DK ** -0.5), k.reshape(shp),
                                       v.reshape(B, S, GLA_HEADS, GLA_DV), log_a.reshape(shp))
    o = rms_norm(o, norm_gain) * jax.nn.silu(og.reshape(B, S, GLA_HEADS, GLA_DV))
    return o.reshape(B, S, GLA_HEADS * GLA_DV)


def hgrn2_mixer(q, f_pre, i_in, og, lb, norm_gain):
    B, S = q.shape[:2]
    z = f_pre.astype(jnp.float32)
    lb = lb.astype(jnp.float32)
    f = lb + (1.0 - lb) * jax.nn.sigmoid(z)
    log_f = jnp.log(f)
    one_minus_f = (1.0 - lb) * jax.nn.sigmoid(-z)
    shp = (B, S, HG_HEADS, HG_DK)
    o = chunked_gated_linear_attention(q.reshape(shp), one_minus_f.reshape(shp).astype(q.dtype),
                                       i_in.reshape(B, S, HG_HEADS, HG_DV), log_f.reshape(shp))
    o = rms_norm(o, norm_gain) * jax.nn.silu(og.reshape(B, S, HG_HEADS, HG_DV))
    return o.reshape(B, S, HG_HEADS * HG_DV)


def swiglu(x, w_gate, w_up, w_down):
    return (jax.nn.silu(x @ w_gate) * (x @ w_up)) @ w_down


def moe_swiglu(x, router, w_gate, w_up, w_down):
    logits = (x @ router).astype(jnp.float32)
    top_val, top_idx = lax.top_k(logits, TOP_K)
    top_w = jax.nn.softmax(top_val, axis=-1)
    combine = jnp.sum(jax.nn.one_hot(top_idx, N_EXPERTS, dtype=jnp.float32) * top_w[..., None], axis=-2)
    out = jnp.zeros_like(x)
    for e in range(N_EXPERTS):
        out = out + combine[..., e:e + 1].astype(x.dtype) * swiglu(x, w_gate[e], w_up[e], w_down[e])
    return out


def setup_inputs(seed: int = 0) -> dict:
    key = jax.random.key(seed)
    keys = iter(jax.random.split(key, 32))

    def nrm(shape, scale):
        return scale * jax.random.normal(next(keys), shape, jnp.float32)

    def gain(shape):
        return 1.0 + 0.02 * jax.random.normal(next(keys), shape, jnp.float32)

    n_dense = (DEPTH + 1) // 2
    n_moe = DEPTH // 2
    return {
        'x': nrm((BATCH, SEQ, D_MODEL), 1.0),
        'p': nrm((DEPTH, BATCH, SEQ, PLE_DIM), 1.0),
        'norm_attn': gain((DEPTH, D_MODEL)),
        'w_in': nrm((DEPTH, D_MODEL, IN_COLS), D_MODEL ** -0.5),
        'w_out': nrm((DEPTH, MIX_WIDTH, D_MODEL), MIX_WIDTH ** -0.5),
        'nsa_cmp_pos': nrm((DEPTH, 2, NSA_CMP_LEN, HD), 0.1),
        'nsa_cmp_w1': nrm((DEPTH, 2, NSA_CMP_LEN * HD, NSA_CMP_HIDDEN), (NSA_CMP_LEN * HD) ** -0.5),
        'nsa_cmp_w2': nrm((DEPTH, 2, NSA_CMP_HIDDEN, HD), NSA_CMP_HIDDEN ** -0.5),
        'nsa_qk_gain': gain((DEPTH, 4, HD)),
        'diff_qk_gain': gain((DEPTH, 2, DIFF_QK_DIM)),
        'diff_lambda': nrm((DEPTH, 4, DIFF_QK_DIM), 0.1),
        'diff_norm': gain((DEPTH, DIFF_V_DIM)),
        'gla_w_gate2': nrm((DEPTH, GLA_GATE_RANK, GLA_HEADS * GLA_DK), GLA_GATE_RANK ** -0.5),
        'gla_b_gate': nrm((DEPTH, GLA_HEADS * GLA_DK), 0.1),
        'gla_norm': gain((DEPTH, GLA_DV)),
        'hgrn_lb_logits': nrm((DEPTH, HG_HEADS * HG_DK), 1.0),
        'hgrn_norm': gain((DEPTH, HG_DV)),
        'norm_ffn': gain((DEPTH, D_MODEL)),
        'ffn_w_gate': nrm((n_dense, D_MODEL, D_FF), D_MODEL ** -0.5),
        'ffn_w_up': nrm((n_dense, D_MODEL, D_FF), D_MODEL ** -0.5),
        'ffn_w_down': nrm((n_dense, D_FF, D_MODEL), D_FF ** -0.5),
        'moe_router': nrm((n_moe, D_MODEL, N_EXPERTS), D_MODEL ** -0.5),
        'moe_w_gate': nrm((n_moe, N_EXPERTS, D_MODEL, D_FF_EXPERT), D_MODEL ** -0.5),
        'moe_w_up': nrm((n_moe, N_EXPERTS, D_MODEL, D_FF_EXPERT), D_MODEL ** -0.5),
        'moe_w_down': nrm((n_moe, N_EXPERTS, D_FF_EXPERT, D_MODEL), D_FF_EXPERT ** -0.5),
        'ple_norm': gain((DEPTH, D_MODEL)),
        'ple_w_gate': nrm((DEPTH, D_MODEL, D_MODEL), D_MODEL ** -0.5),
        'ple_w_proj': nrm((DEPTH, PLE_DIM, D_MODEL), PLE_DIM ** -0.5),
    }


def reference(x, p, norm_attn, w_in, w_out, nsa_cmp_pos, nsa_cmp_w1, nsa_cmp_w2, nsa_qk_gain,
              diff_qk_gain, diff_lambda, diff_norm, gla_w_gate2, gla_b_gate, gla_norm,
              hgrn_lb_logits, hgrn_norm, norm_ffn, ffn_w_gate, ffn_w_up, ffn_w_down,
              moe_router, moe_w_gate, moe_w_up, moe_w_down, ple_norm, ple_w_gate, ple_w_proj):
    B, S, _ = x.shape
    lb_p = jax.nn.softmax(hgrn_lb_logits.astype(jnp.float32), axis=0)
    lower_bounds = jnp.cumsum(lb_p, axis=0) - lb_p[0]
    h = x
    for i in range(DEPTH):
        a = rms_norm(h, norm_attn[i])
        u = a @ w_in[i]
        (nsa_q, k_cmp, v_cmp, k_slc, v_slc, k_win, v_win, nsa_g,
         d_q, d_k, d_v,
         g_q, g_k, g_v, g_lr, g_og,
         r_q, r_f, r_i, r_og) = split_cols(u, IN_SPLITS)
        o_a = nsa_attention(nsa_q.reshape(B, S, NSA_HEADS, HD), k_cmp, v_cmp, k_slc, v_slc,
                            k_win, v_win, nsa_g, nsa_cmp_pos[i], nsa_cmp_w1[i], nsa_cmp_w2[i],
                            nsa_qk_gain[i])
        o_b = diff_attention(d_q.reshape(B, S, DIFF_HEADS, 2, DIFF_QK_DIM),
                             d_k.reshape(B, S, DIFF_HEADS, 2, DIFF_QK_DIM),
                             d_v.reshape(B, S, DIFF_HEADS, DIFF_V_DIM),
                             diff_qk_gain[i], diff_lambda[i], diff_norm[i], i)
        o_c = gla_mixer(g_q, g_k, g_v, g_lr, g_og, gla_w_gate2[i], gla_b_gate[i], gla_norm[i])
        o_d = hgrn2_mixer(r_q, r_f, r_i, r_og, lower_bounds[i], hgrn_norm[i])
        mix = jnp.concatenate([o_a, o_b, o_c, o_d], axis=-1)
        h = h + mix @ w_out[i]
        c = rms_norm(h, norm_ffn[i])
        if i % 2 == 0:
            h = h + swiglu(c, ffn_w_gate[i // 2], ffn_w_up[i // 2], ffn_w_down[i // 2])
        else:
            h = h + moe_swiglu(c, moe_router[i // 2], moe_w_gate[i // 2], moe_w_up[i // 2],
                               moe_w_down[i // 2])
        gate = jax.nn.sigmoid(rms_norm(h, ple_norm[i]) @ ple_w_gate[i])
        h = h + (p[i] @ ple_w_proj[i]) * gate
    return h
```

```python
import functools
import math

import numpy as np
import jax
import jax.numpy as jnp
from jax import lax
from jax.experimental import pallas as pl
from jax.experimental.pallas import tpu as pltpu

F32 = jnp.float32
BF16 = jnp.bfloat16

D_MODEL = 1024
HEADS = 4
HD = 64
GROUP = HEADS * HD
NSA_CMP_LEN = 32
NSA_CMP_STRIDE = 16
NSA_CMP_HIDDEN = 4 * HD
NSA_SEL_LEN = 64
NSA_TOPK = 16
NSA_WINDOW = 512
DIFF_QK = HD // 2
GLA_DK = HD // 2
GLA_RANK = 16
GLA_TAU = 16.0
CHUNK = 64
D_FF = 7 * D_MODEL // 2
N_EXPERTS = 8
PLE_DIM = 256
EPS = 1e-6
NEG = -1e30
BIG = 1e30
LOWEST = -3.0e38

VMEM_LIMIT = 52 * 1024 * 1024
LANE = 128

_SRC = dict(nsa_q=0, k_cmp=256, v_cmp=320, k_slc=384, v_slc=448, k_win=512, v_win=576, nsa_g=640,
            d_q=652, d_k=908, d_v=1164, g_q=1420, g_k=1548, g_v=1676, g_lr=1932, g_og=1948,
            r_q=2204, r_f=2460, r_i=2716, r_og=2972)
IN_COLS = 3228

_B256 = dict(nsa_q=0, d_q=1, d_k=2, d_v=3, g_v=4, g_og=5, r_q=6, r_f=7, r_i=8, r_og=9,
             ks=10, vs=11, kw=12, vw=13, gate_c=14, gate_s=15, gate_w=16)
_B128 = dict(g_q=34, g_k=35, g_lr=36, kvcmp=37)
NC = 5120


def _column_map():
    cols = -np.ones((NC,), np.int64)

    def put(dst, src, width, rep=1):
        for r in range(rep):
            cols[dst + r * width: dst + (r + 1) * width] = np.arange(src, src + width)

    for name in ("nsa_q", "d_q", "d_k", "d_v", "g_v", "g_og", "r_q", "r_f", "r_i", "r_og"):
        put(_B256[name] * 256, _SRC[name], 256)
    put(_B256["ks"] * 256, _SRC["k_slc"], HD, HEADS)
    put(_B256["vs"] * 256, _SRC["v_slc"], HD, HEADS)
    put(_B256["kw"] * 256, _SRC["k_win"], HD, HEADS)
    put(_B256["vw"] * 256, _SRC["v_win"], HD, HEADS)
    for br, name in enumerate(("gate_c", "gate_s", "gate_w")):
        for h in range(HEADS):
            cols[_B256[name] * 256 + h * HD: _B256[name] * 256 + (h + 1) * HD] = _SRC["nsa_g"] + 3 * h + br
    put(_B128["g_q"] * 128, _SRC["g_q"], 128)
    put(_B128["g_k"] * 128, _SRC["g_k"], 128)
    put(_B128["g_lr"] * 128, _SRC["g_lr"], GLA_RANK)
    put(_B128["kvcmp"] * 128, _SRC["k_cmp"], 128)
    return cols


_COLS = _column_map()


def _dot(a, b):
    return jnp.dot(a, b, preferred_element_type=F32)


def _dot_nt(a, b):
    return lax.dot_general(a, b, (((1,), (1,)), ((), ())), preferred_element_type=F32)


def _dot_tn(a, b):
    return lax.dot_general(a, b, (((0,), (0,)), ((), ())), preferred_element_type=F32)


def _split2(x):
    hi = x.astype(BF16)
    lo = (x - hi.astype(F32)).astype(BF16)
    return hi, lo


def _split3(x):
    hi = x.astype(BF16)
    r = x - hi.astype(F32)
    mid = r.astype(BF16)
    lo = (r - mid.astype(F32)).astype(BF16)
    return hi, mid, lo


def _group_mean(x, ones_bf, group):
    hi, lo = _split2(x)
    return (_dot(hi, ones_bf) + _dot(lo, ones_bf)) * (1.0 / group)


def _group_rms(x, ones_bf, group):
    return x * lax.rsqrt(_group_mean(x * x, ones_bf, group) + EPS)


def _sigmoid(x):
    return 1.0 / (1.0 + jnp.exp(-x))


def _silu(x):
    return x * _sigmoid(x)


def _params(*sem):
    return pltpu.CompilerParams(dimension_semantics=sem, vmem_limit_bytes=VMEM_LIMIT)


def _block_ones(n, group):
    i = np.arange(n)
    return jnp.asarray((i[:, None] // group == i[None, :] // group).astype(np.float32), BF16)


def _norm_matmul_kernel(x_ref, g_ref, w_ref, o_ref, xn_ref):
    @pl.when(pl.program_id(1) == 0)
    def _():
        x = x_ref[...]
        y = x * lax.rsqrt(jnp.mean(x * x, axis=-1, keepdims=True) + EPS) * g_ref[...]
        xn_ref[...] = y.astype(BF16)

    o_ref[...] = _dot(xn_ref[...], w_ref[...])


def _norm_matmul(x, gain, w_bf, tm, tn):
    T, K = x.shape
    N = w_bf.shape[1]
    return pl.pallas_call(
        _norm_matmul_kernel,
        grid=(T // tm, N // tn),
        in_specs=[pl.BlockSpec((tm, K), lambda i, j: (i, 0)),
                  pl.BlockSpec((1, K), lambda i, j: (0, 0)),
                  pl.BlockSpec((K, tn), lambda i, j: (0, j))],
        out_specs=pl.BlockSpec((tm, tn), lambda i, j: (i, j)),
        out_shape=jax.ShapeDtypeStruct((T, N), F32),
        scratch_shapes=[pltpu.VMEM((tm, K), BF16)],
        compiler_params=_params("parallel", "arbitrary"),
        name="in_proj",
    )(x, gain.reshape(1, K), w_bf)


def _prep_kernel(nq_ref, dq_ref, dk_ref, dv_ref, ks_ref, vs_ref, kw_ref, vw_ref, gains_ref,
                 ones64_ref, ones32_ref,
                 o_nq, o_dq, o_dk, o_dv, o_ks, o_vs, o_kw, o_vw):
    ones64 = ones64_ref[...]
    ones32 = ones32_ref[...]
    o_nq[...] = (_group_rms(nq_ref[...], ones64, HD) * gains_ref[0:1, :]).astype(BF16)
    o_dq[...] = (_group_rms(dq_ref[...], ones32, DIFF_QK) * gains_ref[1:2, :]).astype(BF16)
    o_dk[...] = (_group_rms(dk_ref[...], ones32, DIFF_QK) * gains_ref[2:3, :]).astype(BF16)
    o_dv[...] = dv_ref[...].astype(BF16)
    o_ks[...] = (_group_rms(ks_ref[...], ones64, HD) * gains_ref[3:4, :]).astype(BF16)
    o_vs[...] = vs_ref[...].astype(BF16)
    o_kw[...] = (_group_rms(kw_ref[...], ones64, HD) * gains_ref[4:5, :]).astype(BF16)
    o_vw[...] = vw_ref[...].astype(BF16)


def _prep(u, gains, tm):
    T = u.shape[0]
    names = ("nsa_q", "d_q", "d_k", "d_v", "ks", "vs", "kw", "vw")

    def col_spec(name):
        c = _B256[name]
        return pl.BlockSpec((tm, GROUP), lambda i, c=c: (i, c))

    const = lambda shape: pl.BlockSpec(shape, lambda i: (0, 0))
    return pl.pallas_call(
        _prep_kernel,
        grid=(T // tm,),
        in_specs=[col_spec(n) for n in names] + [const((8, GROUP)), const((GROUP, GROUP)), const((GROUP, GROUP))],
        out_specs=[pl.BlockSpec((tm, GROUP), lambda i: (i, 0))] * 8,
        out_shape=[jax.ShapeDtypeStruct((T, GROUP), BF16)] * 8,
        compiler_params=_params("parallel"),
        name="attn_prep",
    )(*([u] * 8), gains, _block_ones(GROUP, HD), _block_ones(GROUP, DIFF_QK))


def _compress_kernel(xk_ref, xv_ref, pos_ref, w1_ref, w2_ref, gain_ref, ones64_ref, kc_ref, vc_ref):
    half = NSA_CMP_STRIDE * HD
    n_rows = xk_ref.shape[0]

    def compress(x, j):
        top = (x + pos_ref[j, 0:1, :]).astype(BF16)
        bot = (x + pos_ref[j, 1:2, :]).astype(BF16)
        a = _dot(top, w1_ref[j, 0:half, :])
        b = _dot(bot, w1_ref[j, half:2 * half, :])
        hidden = a + pltpu.roll(b, n_rows - 1, 0)
        return _dot(_silu(hidden).astype(BF16), w2_ref[j])

    kc = compress(xk_ref[...], 0)
    kc_ref[...] = _group_rms(kc, ones64_ref[...], HD) * gain_ref[...]
    vc_ref[...] = compress(xv_ref[...], 1).astype(BF16)


def _compress(xk, xv, pos, w1_bf, w2r_bf, gain_row):
    B, ncp, half = xk.shape
    sq = pl.Squeezed()
    full = lambda shape: pl.BlockSpec(shape, lambda b: (0,) * len(shape))
    return pl.pallas_call(
        _compress_kernel,
        grid=(B,),
        in_specs=[pl.BlockSpec((sq, ncp, half), lambda b: (b, 0, 0)),
                  pl.BlockSpec((sq, ncp, half), lambda b: (b, 0, 0)),
                  full((2, 2, half)), full((2, 2 * half, NSA_CMP_HIDDEN)), full((2, NSA_CMP_HIDDEN, GROUP)),
                  full((1, GROUP)), full((GROUP, GROUP))],
        out_specs=[pl.BlockSpec((sq, ncp, GROUP), lambda b: (b, 0, 0))] * 2,
        out_shape=[jax.ShapeDtypeStruct((B, ncp, GROUP), F32), jax.ShapeDtypeStruct((B, ncp, GROUP), BF16)],
        compiler_params=_params("parallel"),
        name="nsa_compress",
    )(xk, xv, pos, w1_bf, w2r_bf, gain_row, _block_ones(GROUP, HD))


def _stack_heads(x, lane, group, count):
    zero = jnp.zeros_like(x)
    return jnp.concatenate([jnp.where(lane // group == g, x, zero) for g in range(count)], axis=0)


def _unstack_heads(x4, lane, rows):
    out = jnp.where(lane // HD == 0, x4[0:rows], 0.0)
    for h in range(1, HEADS):
        out = out + jnp.where(lane // HD == h, x4[h * rows:(h + 1) * rows], 0.0)
    return out


def _softmax_step(s, v, m_ref, l_ref, acc_ref):
    m_old = m_ref[...]
    m_new = jnp.maximum(m_old, jnp.max(s, axis=1, keepdims=True))
    alpha = jnp.exp(m_old - m_new)
    p = jnp.exp(s - m_new)
    l_ref[...] = alpha * l_ref[...] + jnp.sum(p, axis=1, keepdims=True)
    acc_ref[...] = alpha * acc_ref[...] + _dot(p.astype(BF16), v)
    m_ref[...] = m_new


def _nsa_kernel(q_ref, gc_ref, gs_ref, gw_ref, kc_ref, vc_ref, ks_ref, vs_ref, kw_ref, vw_ref,
                ovl_ref, exp_ref, o_ref, m_ref, l_ref, acc_ref, *, tq, tk, tw, ksel):
    i = pl.program_id(1)
    t0 = i * tq
    ncp = kc_ref.shape[0]
    nsel = ovl_ref.shape[0]
    rows4 = HEADS * tq
    q = q_ref[...]
    lane = lax.broadcasted_iota(jnp.int32, (tq, GROUP), 1)
    qs = _stack_heads(q, lane, HD, HEADS)

    kc_hi, kc_lo = _split2(kc_ref[...])
    s = _dot_nt(qs, kc_hi) + _dot_nt(qs, kc_lo)
    n_idx = lax.broadcasted_iota(jnp.int32, (rows4, ncp), 1)
    t_row = t0 + lax.broadcasted_iota(jnp.int32, (rows4, ncp), 0) % tq
    ok = n_idx * NSA_CMP_STRIDE + (NSA_CMP_LEN - 1) <= t_row
    s = jnp.where(ok, s, NEG)
    e = jnp.exp(s - jnp.max(s, axis=1, keepdims=True))
    p = jnp.where(ok, e / jnp.sum(e, axis=1, keepdims=True), 0.0)
    o_c = _unstack_heads(_dot(p.astype(BF16), vc_ref[...]), lane, tq)

    psum = p[0:tq] + p[tq:2 * tq] + p[2 * tq:3 * tq] + p[3 * tq:4 * tq]
    p_hi, p_lo = _split2(psum)
    imp = _dot_nt(ovl_ref[...], p_hi) + _dot_nt(ovl_ref[...], p_lo)
    blk = lax.broadcasted_iota(jnp.int32, (nsel, tq), 0)
    t_col = t0 + lax.broadcasted_iota(jnp.int32, (nsel, tq), 1)
    cur = t_col // NSA_SEL_LEN
    forced = (blk == 0) | (blk == cur) | (blk == cur - 1)
    vals = jnp.where(forced, BIG, jnp.where(blk * NSA_SEL_LEN <= t_col, imp, NEG))
    sel = jnp.zeros((nsel, tq), F32)
    for _ in range(ksel):
        mx = jnp.max(vals, axis=0, keepdims=True)
        first = jnp.min(jnp.where(vals == mx, blk, nsel), axis=0, keepdims=True)
        pick = blk == first
        sel = jnp.where(pick, 1.0, sel)
        vals = jnp.where(pick, LOWEST, vals)
    if nsel < LANE:
        sel = jnp.concatenate([sel, jnp.zeros((LANE - nsel, tq), F32)], axis=0)
    sel_q = sel.T.astype(BF16)

    def reset():
        m_ref[...] = jnp.full((rows4, 1), NEG, F32)
        l_ref[...] = jnp.zeros((rows4, 1), F32)
        acc_ref[...] = jnp.zeros((rows4, GROUP), F32)

    reset()

    def sel_body(kt, c):
        k0 = pl.multiple_of(kt * tk, tk)
        s = _dot_nt(qs, ks_ref[pl.ds(k0, tk), :])
        chosen = _dot(sel_q, exp_ref[:, pl.ds(k0, tk)])
        kpos = k0 + lax.broadcasted_iota(jnp.int32, (tq, tk), 1)
        tpos = t0 + lax.broadcasted_iota(jnp.int32, (tq, tk), 0)
        bias = jnp.where((chosen > 0.5) & (kpos <= tpos), 0.0, NEG)
        s = s + jnp.concatenate([bias] * HEADS, axis=0)
        _softmax_step(s, vs_ref[pl.ds(k0, tk), :], m_ref, l_ref, acc_ref)
        return c

    lax.fori_loop(0, (t0 + tq - 1) // tk + 1, sel_body, 0)
    o_s = _unstack_heads(acc_ref[...] / l_ref[...], lane, tq)

    reset()

    def win_body(wt, c):
        k0 = pl.multiple_of(wt * tw, tw)
        s = _dot_nt(qs, kw_ref[pl.ds(k0, tw), :])
        kpos = k0 + lax.broadcasted_iota(jnp.int32, (tq, tw), 1)
        tpos = t0 + lax.broadcasted_iota(jnp.int32, (tq, tw), 0)
        bias = jnp.where((kpos <= tpos) & (kpos > tpos - NSA_WINDOW), 0.0, NEG)
        s = s + jnp.concatenate([bias] * HEADS, axis=0)
        _softmax_step(s, vw_ref[pl.ds(k0, tw), :], m_ref, l_ref, acc_ref)
        return c

    lax.fori_loop(jnp.maximum(t0 - NSA_WINDOW, 0) // tw, (t0 + tq - 1) // tw + 1, win_body, 0)
    o_w = _unstack_heads(acc_ref[...] / l_ref[...], lane, tq)

    o_ref[...] = _sigmoid(gc_ref[...]) * o_c + _sigmoid(gs_ref[...]) * o_s + _sigmoid(gw_ref[...]) * o_w


def _nsa_attention(u, qn, kc, vc, ksn, vs, kwn, vw, B, S, tq, tk, tw):
    nq = S // tq
    ncp = S // NSA_CMP_STRIDE
    nsel = S // NSA_SEL_LEN
    ksel = min(NSA_TOPK, nsel)
    n_cmp = (S - NSA_CMP_LEN) // NSA_CMP_STRIDE + 1
    cmp_start = np.arange(ncp) * NSA_CMP_STRIDE
    sel_start = np.arange(nsel) * NSA_SEL_LEN
    overlap = ((cmp_start[:, None] <= sel_start[None, :] + NSA_SEL_LEN - 1)
               & (cmp_start[:, None] + NSA_CMP_LEN - 1 >= sel_start[None, :])
               & (np.arange(ncp)[:, None] < n_cmp))
    ovl_t = jnp.asarray(overlap.T.astype(np.float32), BF16)
    expand = (jnp.arange(LANE, dtype=jnp.int32)[:, None]
              == jnp.arange(S, dtype=jnp.int32)[None, :] // NSA_SEL_LEN).astype(BF16)
    sq = pl.Squeezed()
    tile = lambda c: pl.BlockSpec((tq, GROUP), lambda b, i, c=c: (b * nq + i, c))
    seq = pl.BlockSpec((S, GROUP), lambda b, i: (b, 0))
    kern = functools.partial(_nsa_kernel, tq=tq, tk=tk, tw=tw, ksel=ksel)
    return pl.pallas_call(
        kern,
        grid=(B, nq),
        in_specs=[tile(0), tile(_B256["gate_c"]), tile(_B256["gate_s"]), tile(_B256["gate_w"]),
                  pl.BlockSpec((sq, ncp, GROUP), lambda b, i: (b, 0, 0)),
                  pl.BlockSpec((sq, ncp, GROUP), lambda b, i: (b, 0, 0)),
                  seq, seq, seq, seq,
                  pl.BlockSpec((nsel, ncp), lambda b, i: (0, 0)),
                  pl.BlockSpec((LANE, S), lambda b, i: (0, 0))],
        out_specs=pl.BlockSpec((tq, GROUP), lambda b, i: (b * nq + i, 0)),
        out_shape=jax.ShapeDtypeStruct((B * S, GROUP), F32),
        scratch_shapes=[pltpu.VMEM((HEADS * tq, 1), F32), pltpu.VMEM((HEADS * tq, 1), F32),
                        pltpu.VMEM((HEADS * tq, GROUP), F32)],
        compiler_params=_params("parallel", "arbitrary"),
        name="nsa_attention",
    )(qn, u, u, u, kc, vc, ksn, vs, kwn, vw, ovl_t, expand)


def _diff_kernel(lam_ref, q_ref, k_ref, v_ref, gain_ref, ones64_ref, o_ref, m_ref, l_ref, acc_ref,
                 *, tq, tk, lam_init):
    i = pl.program_id(1)
    t0 = i * tq
    maps = 2 * HEADS
    rows = maps * tq
    lane = lax.broadcasted_iota(jnp.int32, (tq, GROUP), 1)
    qs = _stack_heads(q_ref[...], lane, DIFF_QK, maps)
    m_ref[...] = jnp.full((rows, 1), NEG, F32)
    l_ref[...] = jnp.zeros((rows, 1), F32)
    acc_ref[...] = jnp.zeros((rows, GROUP), F32)

    def body(kt, c):
        k0 = pl.multiple_of(kt * tk, tk)
        s = _dot_nt(qs, k_ref[pl.ds(k0, tk), :])
        kpos = k0 + lax.broadcasted_iota(jnp.int32, (tq, tk), 1)
        tpos = t0 + lax.broadcasted_iota(jnp.int32, (tq, tk), 0)
        bias = jnp.where(kpos <= tpos, 0.0, NEG)
        s = s + jnp.concatenate([bias] * maps, axis=0)
        _softmax_step(s, v_ref[pl.ds(k0, tk), :], m_ref, l_ref, acc_ref)
        return c

    lax.fori_loop(0, (t0 + tq - 1) // tk + 1, body, 0)
    r = acc_ref[...] / l_ref[...]
    lam = lam_ref[...]
    lam_full = (jnp.exp(jnp.sum(lam[0:1] * lam[1:2], axis=1, keepdims=True))
                - jnp.exp(jnp.sum(lam[2:3] * lam[3:4], axis=1, keepdims=True)) + lam_init)
    o = jnp.zeros((tq, GROUP), F32)
    for h in range(HEADS):
        d = r[2 * h * tq:(2 * h + 1) * tq] - lam_full * r[(2 * h + 1) * tq:(2 * h + 2) * tq]
        o = o + jnp.where(lane // HD == h, d, 0.0)
    o_ref[...] = _group_rms(o, ones64_ref[...], HD) * gain_ref[...] * (1.0 - lam_init)


def _diff_attention(qd, kd, vd, lam, gain_row, B, S, tq, tk, layer_idx):
    nq = S // tq
    lam_init = 0.8 - 0.6 * math.exp(-0.3 * layer_idx)
    seq = pl.BlockSpec((S, GROUP), lambda b, i: (b, 0))
    kern = functools.partial(_diff_kernel, tq=tq, tk=tk, lam_init=lam_init)
    rows = 2 * HEADS * tq
    return pl.pallas_call(
        kern,
        grid=(B, nq),
        in_specs=[pl.BlockSpec((4, DIFF_QK), lambda b, i: (0, 0)),
                  pl.BlockSpec((tq, GROUP), lambda b, i: (b * nq + i, 0)),
                  seq, seq,
                  pl.BlockSpec((1, GROUP), lambda b, i: (0, 0)),
                  pl.BlockSpec((GROUP, GROUP), lambda b, i: (0, 0))],
        out_specs=pl.BlockSpec((tq, GROUP), lambda b, i: (b * nq + i, 0)),
        out_shape=jax.ShapeDtypeStruct((B * S, GROUP), F32),
        scratch_shapes=[pltpu.VMEM((rows, 1), F32), pltpu.VMEM((rows, 1), F32), pltpu.VMEM((rows, GROUP), F32)],
        compiler_params=_params("parallel", "arbitrary"),
        name="diff_attention",
    )(lam, qd, kd, vd, gain_row, _block_ones(GROUP, HD))


_LEVELS = (32, 16, 8, 4, 2, 1)


def _linear_consts():
    r = np.arange(CHUNK)[:, None]
    t = np.arange(CHUNK)[None, :]
    mats = []
    for s in _LEVELS:
        mid = (r // (2 * s)) * 2 * s + s
        upper = (r % (2 * s)) >= s
        mats.append(np.where(upper, (t > mid) & (t <= r), (t > r) & (t <= mid)))
    mats.append(t <= r)
    mats.append(t > r)
    dall = np.concatenate(mats, axis=0).astype(np.float32)
    masks = [(r // (2 * s) == t // (2 * s)) for s in _LEVELS] + [r == t]
    mall = np.stack([np.tile(m.astype(np.float32), (HEADS, 1)) for m in masks])
    return jnp.asarray(dall, BF16), jnp.asarray(mall, F32)


def _linear_chunk(q, k, v, lg, dall, mall_ref, state_ref, dk):
    dkh = HEADS * dk
    hi, mid, lo = _split3(lg)
    ex = jnp.exp(_dot(dall, hi) + _dot(dall, mid) + _dot(dall, lo))
    row = lax.broadcasted_iota(jnp.int32, (CHUNK, dkh), 0)
    lane_k = lax.broadcasted_iota(jnp.int32, (CHUNK, dkh), 1)
    lane_v = lax.broadcasted_iota(jnp.int32, (CHUNK, GROUP), 1)
    k_bf = k.astype(BF16)
    a = mall_ref[len(_LEVELS)] * _dot_nt(_stack_heads(q.astype(BF16), lane_k, dk, HEADS), k_bf)
    for li, s in enumerate(_LEVELS):
        e = ex[li * CHUNK:(li + 1) * CHUNK]
        upper = (row & s) != 0
        qt = jnp.where(upper, q * e, 0.0).astype(BF16)
        kt = jnp.where(upper, 0.0, k * e).astype(BF16)
        a = a + mall_ref[li] * _dot_nt(_stack_heads(qt, lane_k, dk, HEADS), kt)
    o_intra = _unstack_heads(_dot(a.astype(BF16), v.astype(BF16)), lane_v, CHUNK)
    e_b = ex[6 * CHUNK:7 * CHUNK]
    e_u = ex[7 * CHUNK:8 * CHUNK]
    st = state_ref[...]
    o_inter = _dot_nt((q * e_b).astype(BF16), st.astype(BF16))
    kv = _dot_tn(v.astype(BF16), (k * e_u).astype(BF16))
    srow = lax.broadcasted_iota(jnp.int32, (GROUP, dkh), 0)
    scol = lax.broadcasted_iota(jnp.int32, (GROUP, dkh), 1)
    state_ref[...] = st * e_b[CHUNK - 1:CHUNK, :] + jnp.where(srow // HD == scol // dk, kv, 0.0)
    return o_inter + o_intra


def _gla_kernel(q_ref, k_ref, v_ref, lr_ref, og_ref, w2_ref, b_ref, gain_ref, dall_ref, mall_ref,
                ones64_ref, o_ref, state_ref, *, tm):
    @pl.when(pl.program_id(1) == 0)
    def _():
        state_ref[...] = jnp.zeros_like(state_ref)

    dall = dall_ref[...]
    ones64 = ones64_ref[...]
    w_hi, w_lo = _split2(w2_ref[...])

    def body(c, carry):
        r0 = pl.multiple_of(c * CHUNK, CHUNK)
        rs = pl.ds(r0, CHUNK)
        lr_hi, lr_lo = _split2(lr_ref[rs, :])
        x = _dot(lr_hi, w_hi) + _dot(lr_lo, w_hi) + _dot(lr_hi, w_lo) + b_ref[...]
        lg = (jnp.minimum(x, 0.0) - jnp.log(1.0 + jnp.exp(-jnp.abs(x)))) * (1.0 / GLA_TAU)
        o = _linear_chunk(q_ref[rs, :] * (GLA_DK ** -0.5), k_ref[rs, :], v_ref[rs, :], lg,
                          dall, mall_ref, state_ref, GLA_DK)
        o_ref[rs, :] = _group_rms(o, ones64, HD) * gain_ref[...] * _silu(og_ref[rs, :])
        return carry

    lax.fori_loop(0, tm // CHUNK, body, 0)


def _hgrn_kernel(q_ref, f_ref, i_ref, og_ref, lbl_ref, gain_ref, dall_ref, mall_ref, ones64_ref,
                 o_ref, state_ref, *, tm, layer_idx):
    @pl.when(pl.program_id(1) == 0)
    def _():
        state_ref[...] = jnp.zeros_like(state_ref)

    dall = dall_ref[...]
    ones64 = ones64_ref[...]
    logits = lbl_ref[...]
    ez = jnp.exp(logits - jnp.max(logits, axis=0, keepdims=True))
    probs = ez / jnp.sum(ez, axis=0, keepdims=True)
    lb = jnp.zeros((1, GROUP), F32)
    for j in range(1, layer_idx + 1):
        lb = lb + probs[j:j + 1]

    def body(c, carry):
        r0 = pl.multiple_of(c * CHUNK, CHUNK)
        rs = pl.ds(r0, CHUNK)
        z = f_ref[rs, :]
        f = lb + (1.0 - lb) * _sigmoid(z)
        k = (1.0 - lb) * _sigmoid(-z)
        o = _linear_chunk(q_ref[rs, :], k, i_ref[rs, :], jnp.log(f), dall, mall_ref, state_ref, HD)
        o_ref[rs, :] = _group_rms(o, ones64, HD) * gain_ref[...] * _silu(og_ref[rs, :])
        return carry

    lax.fori_loop(0, tm // CHUNK, body, 0)


def _linear_mixers(u, w2pad, b_gate, gla_gain, lb_logits, hgrn_gain, B, S, tm, layer_idx):
    nt = S // tm
    dall, mall = _linear_consts()
    ones64 = _block_ones(GROUP, HD)
    c256 = lambda name: pl.BlockSpec((tm, GROUP), lambda b, i, c=_B256[name]: (b * nt + i, c))
    c128 = lambda name: pl.BlockSpec((tm, LANE), lambda b, i, c=_B128[name]: (b * nt + i, c))
    full = lambda shape: pl.BlockSpec(shape, lambda b, i: (0,) * len(shape))
    out_spec = pl.BlockSpec((tm, GROUP), lambda b, i: (b * nt + i, 0))
    out_shape = jax.ShapeDtypeStruct((B * S, GROUP), F32)
    consts = [full(dall.shape), full(mall.shape), full((GROUP, GROUP))]
    o_gla = pl.pallas_call(
        functools.partial(_gla_kernel, tm=tm),
        grid=(B, nt),
        in_specs=[c128("g_q"), c128("g_k"), c256("g_v"), c128("g_lr"), c256("g_og"),
                  full((LANE, LANE)), full((1, LANE)), full((1, GROUP))] + consts,
        out_specs=out_spec, out_shape=out_shape,
        scratch_shapes=[pltpu.VMEM((GROUP, HEADS * GLA_DK), F32)],
        compiler_params=_params("parallel", "arbitrary"),
        name="gla_mixer",
    )(u, u, u, u, u, w2pad, b_gate, gla_gain, dall, mall, ones64)
    depth = lb_logits.shape[0]
    o_hgrn = pl.pallas_call(
        functools.partial(_hgrn_kernel, tm=tm, layer_idx=layer_idx),
        grid=(B, nt),
        in_specs=[c256("r_q"), c256("r_f"), c256("r_i"), c256("r_og"),
                  full((depth, GROUP)), full((1, GROUP))] + consts,
        out_specs=out_spec, out_shape=out_shape,
        scratch_shapes=[pltpu.VMEM((GROUP, GROUP), F32)],
        compiler_params=_params("parallel", "arbitrary"),
        name="hgrn_mixer",
    )(u, u, u, u, lb_logits, hgrn_gain, dall, mall, ones64)
    return o_gla, o_hgrn


def _out_proj_kernel(h_ref, a_ref, b_ref, c_ref, d_ref, w_ref, o_ref):
    acc = h_ref[...]
    for j, r in enumerate((a_ref, b_ref, c_ref, d_ref)):
        acc = acc + _dot(r[...].astype(BF16), w_ref[j * GROUP:(j + 1) * GROUP, :])
    o_ref[...] = acc


def _out_proj(h, parts, w_bf, tm):
    T = h.shape[0]
    part = pl.BlockSpec((tm, GROUP), lambda i: (i, 0))
    return pl.pallas_call(
        _out_proj_kernel,
        grid=(T // tm,),
        in_specs=[pl.BlockSpec((tm, D_MODEL), lambda i: (i, 0)), part, part, part, part,
                  pl.BlockSpec((D_MODEL, D_MODEL), lambda i: (0, 0))],
        out_specs=pl.BlockSpec((tm, D_MODEL), lambda i: (i, 0)),
        out_shape=jax.ShapeDtypeStruct((T, D_MODEL), F32),
        compiler_params=_params("parallel"),
        name="out_proj",
    )(h, *parts, w_bf)


def _ffn_kernel(h_ref, g_ref, wg_ref, wu_ref, wd_ref, o_ref, xn_ref, acc_ref):
    f = pl.program_id(1)

    @pl.when(f == 0)
    def _():
        x = h_ref[...]
        y = x * lax.rsqrt(jnp.mean(x * x, axis=-1, keepdims=True) + EPS) * g_ref[...]
        xn_ref[...] = y.astype(BF16)
        acc_ref[...] = x

    xn = xn_ref[...]
    mid = _silu(_dot(xn, wg_ref[...])) * _dot(xn, wu_ref[...])
    acc_ref[...] += _dot(mid.astype(BF16), wd_ref[...])

    @pl.when(f == pl.num_programs(1) - 1)
    def _():
        o_ref[...] = acc_ref[...]


def _ffn(h, gain, wg, wu, wd, tm, tf):
    T = h.shape[0]
    F = wg.shape[1]
    return pl.pallas_call(
        _ffn_kernel,
        grid=(T // tm, F // tf),
        in_specs=[pl.BlockSpec((tm, D_MODEL), lambda i, f: (i, 0)),
                  pl.BlockSpec((1, D_MODEL), lambda i, f: (0, 0)),
                  pl.BlockSpec((D_MODEL, tf), lambda i, f: (0, f)),
                  pl.BlockSpec((D_MODEL, tf), lambda i, f: (0, f)),
                  pl.BlockSpec((tf, D_MODEL), lambda i, f: (f, 0))],
        out_specs=pl.BlockSpec((tm, D_MODEL), lambda i, f: (i, 0)),
        out_shape=jax.ShapeDtypeStruct((T, D_MODEL), F32),
        scratch_shapes=[pltpu.VMEM((tm, D_MODEL), BF16), pltpu.VMEM((tm, D_MODEL), F32)],
        compiler_params=_params("parallel", "arbitrary"),
        name="ffn_swiglu",
    )(h, gain.reshape(1, D_MODEL), wg, wu, wd)


def _router_kernel(h_ref, g_ref, r_ref, c_ref, comb_ref):
    x = h_ref[...]
    y = x * lax.rsqrt(jnp.mean(x * x, axis=-1, keepdims=True) + EPS) * g_ref[...]
    c_ref[...] = y.astype(BF16)
    y_hi, y_lo = _split2(y)
    r_hi, r_lo = _split2(r_ref[...])
    logits = _dot(y_hi, r_hi) + _dot(y_lo, r_hi) + _dot(y_hi, r_lo)
    lane = lax.broadcasted_iota(jnp.int32, logits.shape, 1)
    logits = jnp.where(lane < N_EXPERTS, logits, LOWEST)
    m1 = jnp.max(logits, axis=1, keepdims=True)
    i1 = jnp.min(jnp.where(logits == m1, lane, LANE), axis=1, keepdims=True)
    rest = jnp.where(lane == i1, LOWEST, logits)
    m2 = jnp.max(rest, axis=1, keepdims=True)
    i2 = jnp.min(jnp.where(rest == m2, lane, LANE), axis=1, keepdims=True)
    e2 = jnp.exp(m2 - m1)
    w1 = 1.0 / (1.0 + e2)
    comb_ref[...] = jnp.where(lane == i1, w1, jnp.where(lane == i2, e2 * w1, 0.0))


def _router(h, gain, router_pad, tm):
    T = h.shape[0]
    return pl.pallas_call(
        _router_kernel,
        grid=(T // tm,),
        in_specs=[pl.BlockSpec((tm, D_MODEL), lambda i: (i, 0)),
                  pl.BlockSpec((1, D_MODEL), lambda i: (0, 0)),
                  pl.BlockSpec((D_MODEL, LANE), lambda i: (0, 0))],
        out_specs=[pl.BlockSpec((tm, D_MODEL), lambda i: (i, 0)), pl.BlockSpec((tm, LANE), lambda i: (i, 0))],
        out_shape=[jax.ShapeDtypeStruct((T, D_MODEL), BF16), jax.ShapeDtypeStruct((T, LANE), F32)],
        compiler_params=_params("parallel"),
        name="moe_router",
    )(h, gain.reshape(1, D_MODEL), router_pad)


def _moe_kernel(h_ref, c_ref, comb_ref, wg_ref, wu_ref, wd_ref, o_ref, acc_ref):
    e = pl.program_id(1)
    f = pl.program_id(2)

    @pl.when((e == 0) & (f == 0))
    def _():
        acc_ref[...] = h_ref[...]

    comb = comb_ref[...]
    lane = lax.broadcasted_iota(jnp.int32, comb.shape, 1)
    cw = jnp.sum(jnp.where(lane == e, comb, 0.0), axis=1, keepdims=True)
    xn = c_ref[...]
    mid = _silu(_dot(xn, wg_ref[...])) * _dot(xn, wu_ref[...]) * cw
    acc_ref[...] += _dot(mid.astype(BF16), wd_ref[...])

    @pl.when((e == pl.num_programs(1) - 1) & (f == pl.num_programs(2) - 1))
    def _():
        o_ref[...] = acc_ref[...]


def _moe(h, c_bf, comb, wg, wu, wd, tm, tf):
    T = h.shape[0]
    E, _, F = wg.shape
    sq = pl.Squeezed()
    return pl.pallas_call(
        _moe_kernel,
        grid=(T // tm, E, F // tf),
        in_specs=[pl.BlockSpec((tm, D_MODEL), lambda i, e, f: (i, 0)),
                  pl.BlockSpec((tm, D_MODEL), lambda i, e, f: (i, 0)),
                  pl.BlockSpec((tm, LANE), lambda i, e, f: (i, 0)),
                  pl.BlockSpec((sq, D_MODEL, tf), lambda i, e, f: (e, 0, f)),
                  pl.BlockSpec((sq, D_MODEL, tf), lambda i, e, f: (e, 0, f)),
                  pl.BlockSpec((sq, tf, D_MODEL), lambda i, e, f: (e, f, 0))],
        out_specs=pl.BlockSpec((tm, D_MODEL), lambda i, e, f: (i, 0)),
        out_shape=jax.ShapeDtypeStruct((T, D_MODEL), F32),
        scratch_shapes=[pltpu.VMEM((tm, D_MODEL), F32)],
        compiler_params=_params("parallel", "arbitrary", "arbitrary"),
        name="moe_experts",
    )(h, c_bf, comb, wg, wu, wd)


def _ple_kernel(h_ref, p_ref, g_ref, wg_ref, wp_ref, o_ref):
    x = h_ref[...]
    y = x * lax.rsqrt(jnp.mean(x * x, axis=-1, keepdims=True) + EPS) * g_ref[...]
    gate = _sigmoid(_dot(y.astype(BF16), wg_ref[...]))
    o_ref[...] = x + _dot(p_ref[...].astype(BF16), wp_ref[...]) * gate


def _ple(h, p, gain, wg, wp, tm):
    T = h.shape[0]
    return pl.pallas_call(
        _ple_kernel,
        grid=(T // tm,),
        in_specs=[pl.BlockSpec((tm, D_MODEL), lambda i: (i, 0)),
                  pl.BlockSpec((tm, PLE_DIM), lambda i: (i, 0)),
                  pl.BlockSpec((1, D_MODEL), lambda i: (0, 0)),
                  pl.BlockSpec((D_MODEL, D_MODEL), lambda i: (0, 0)),
                  pl.BlockSpec((PLE_DIM, D_MODEL), lambda i: (0, 0))],
        out_specs=pl.BlockSpec((tm, D_MODEL), lambda i: (i, 0)),
        out_shape=jax.ShapeDtypeStruct((T, D_MODEL), F32),
        compiler_params=_params("parallel"),
        name="ple_gate",
    )(h, p, gain.reshape(1, D_MODEL), wg, wp)


def _tiles(T, S):
    pick = lambda n, pref: max(t for t in pref if n % t == 0)
    return dict(
        proj_m=pick(T, (1024, 512, 256, 128)), proj_n=NC // 4,
        prep_m=pick(T, (512, 256, 128)),
        attn_q=128, attn_k=pick(S, (512, 256, 128)), attn_w=128,
        lin_m=pick(S, (512, 256, 128, 64)),
        row_m=pick(T, (512, 256, 128)),
        ffn_m=pick(T, (1024, 512, 256, 128)), ffn_f=512,
    )


def kernel(x, p, norm_attn, w_in, w_out, nsa_cmp_pos, nsa_cmp_w1, nsa_cmp_w2, nsa_qk_gain, diff_qk_gain, diff_lambda, diff_norm, gla_w_gate2, gla_b_gate, gla_norm, hgrn_lb_logits, hgrn_norm, norm_ffn, ffn_w_gate, ffn_w_up, ffn_w_down, moe_router, moe_w_gate, moe_w_up, moe_w_down, ple_norm, ple_w_gate, ple_w_proj):
    B, S, _ = x.shape
    depth = w_in.shape[0]
    T = B * S
    t = _tiles(T, S)
    ncp = S // NSA_CMP_STRIDE
    half = NSA_CMP_STRIDE * HD
    cols = jnp.asarray(np.maximum(_COLS, 0), jnp.int32)
    col_mask = jnp.asarray(_COLS >= 0)
    ones_row = jnp.ones((GROUP,), F32)

    h = x.reshape(T, D_MODEL)
    for i in range(depth):
        w_in_r = jnp.where(col_mask[None, :], jnp.take(w_in[i], cols, axis=1), 0.0).astype(BF16)
        gains = jnp.stack([
            jnp.tile(nsa_qk_gain[i, 0], HEADS) * HD ** -0.5,
            jnp.tile(diff_qk_gain[i, 0], 2 * HEADS) * DIFF_QK ** -0.5,
            jnp.tile(diff_qk_gain[i, 1], 2 * HEADS),
            jnp.tile(nsa_qk_gain[i, 2], HEADS),
            jnp.tile(nsa_qk_gain[i, 3], HEADS),
            ones_row, ones_row, ones_row])
        pos = nsa_cmp_pos[i].reshape(2, 2, half)
        w1 = nsa_cmp_w1[i].astype(BF16)
        w2r = jnp.tile(nsa_cmp_w2[i], (1, 1, HEADS)).astype(BF16)
        kc_gain = jnp.tile(nsa_qk_gain[i, 1], HEADS).reshape(1, GROUP)
        w2pad = jnp.zeros((LANE, LANE), F32).at[:GLA_RANK].set(gla_w_gate2[i])

        u = _norm_matmul(h, norm_attn[i], w_in_r, t["proj_m"], t["proj_n"])
        qn, qd, kd, vd, ksn, vs, kwn, vw = _prep(u, gains, t["prep_m"])
        kv = u[:, _B128["kvcmp"] * LANE:(_B128["kvcmp"] + 1) * LANE]
        xk = kv[:, :HD].reshape(B, ncp, half)
        xv = kv[:, HD:].reshape(B, ncp, half)
        kc, vc = _compress(xk, xv, pos, w1, w2r, kc_gain)
        o_a = _nsa_attention(u, qn, kc, vc, ksn, vs, kwn, vw, B, S, t["attn_q"], t["attn_k"], t["attn_w"])
        o_b = _diff_attention(qd, kd, vd, diff_lambda[i], jnp.tile(diff_norm[i], HEADS).reshape(1, GROUP),
                              B, S, t["attn_q"], t["attn_k"], i)
        o_c, o_d = _linear_mixers(u, w2pad, gla_b_gate[i].reshape(1, LANE),
                                  jnp.tile(gla_norm[i], HEADS).reshape(1, GROUP), hgrn_lb_logits,
                                  jnp.tile(hgrn_norm[i], HEADS).reshape(1, GROUP), B, S, t["lin_m"], i)
        h = _out_proj(h, (o_a, o_b, o_c, o_d), w_out[i].astype(BF16), t["row_m"])
        if i % 2 == 0:
            j = i // 2
            h = _ffn(h, norm_ffn[i], ffn_w_gate[j].astype(BF16), ffn_w_up[j].astype(BF16),
                     ffn_w_down[j].astype(BF16), t["ffn_m"], t["ffn_f"])
        else:
            j = i // 2
            router_pad = jnp.zeros((D_MODEL, LANE), F32).at[:, :N_EXPERTS].set(moe_router[j])
            c_bf, comb = _router(h, norm_ffn[i], router_pad, t["row_m"])
            h = _moe(h, c_bf, comb, moe_w_gate[j].astype(BF16), moe_w_up[j].astype(BF16),
                     moe_w_down[j].astype(BF16), t["ffn_m"], t["ffn_f"])
        h = _ple(h, p[i].reshape(T, PLE_DIM), ple_norm[i], ple_w_gate[i].astype(BF16),
                 ple_w_proj[i].astype(BF16), t["row_m"])
    return h.reshape(B, S, D_MODEL)
```

```python
import functools
import math

import numpy as np
import jax
import jax.numpy as jnp
from jax import lax
from jax.experimental import pallas as pl
from jax.experimental.pallas import tpu as pltpu

F32 = jnp.float32
BF16 = jnp.bfloat16

D_MODEL = 1024
HEADS = 4
HD = 64
GROUP = HEADS * HD
NSA_CMP_LEN = 32
NSA_CMP_STRIDE = 16
NSA_CMP_HIDDEN = 4 * HD
NSA_SEL_LEN = 64
NSA_TOPK = 16
NSA_WINDOW = 512
DIFF_QK = HD // 2
GLA_DK = HD // 2
GLA_RANK = 16
GLA_TAU = 16.0
CHUNK = 64
D_FF = 7 * D_MODEL // 2
N_EXPERTS = 8
PLE_DIM = 256
EPS = 1e-6
NEG = -1e30
BIG = 1e30
LOWEST = -3.0e38
LOG2E = 1.4426950408889634

VMEM_LIMIT = 52 * 1024 * 1024
LANE = 128

_SRC = dict(nsa_q=0, k_cmp=256, v_cmp=320, k_slc=384, v_slc=448, k_win=512, v_win=576, nsa_g=640,
            d_q=652, d_k=908, d_v=1164, g_q=1420, g_k=1548, g_v=1676, g_lr=1932, g_og=1948,
            r_q=2204, r_f=2460, r_i=2716, r_og=2972)
IN_COLS = 3228

_B256 = dict(nsa_q=0, d_q=1, d_k=2, d_v=3, g_v=4, g_og=5, r_q=6, r_f=7, r_i=8, r_og=9)
_B128 = dict(g_q=20, g_k=21, g_lr=22, kvcmp=23, kvslc=24, kvwin=25, nsa_g=26)
NC = 27 * 128


def _column_map():
    cols = -np.ones((NC,), np.int64)

    def put(dst, src, width):
        cols[dst:dst + width] = np.arange(src, src + width)

    for name in _B256:
        put(_B256[name] * 256, _SRC[name], 256)
    put(_B128["g_q"] * 128, _SRC["g_q"], 128)
    put(_B128["g_k"] * 128, _SRC["g_k"], 128)
    put(_B128["g_lr"] * 128, _SRC["g_lr"], GLA_RANK)
    put(_B128["kvcmp"] * 128, _SRC["k_cmp"], 128)
    put(_B128["kvslc"] * 128, _SRC["k_slc"], 128)
    put(_B128["kvwin"] * 128, _SRC["k_win"], 128)
    put(_B128["nsa_g"] * 128, _SRC["nsa_g"], 3 * HEADS)
    return cols


_COLS = _column_map()


def _dot(a, b):
    return jnp.dot(a, b, preferred_element_type=F32)


def _dot_nt(a, b):
    return lax.dot_general(a, b, (((1,), (1,)), ((), ())), preferred_element_type=F32)


def _dot_tn(a, b):
    return lax.dot_general(a, b, (((0,), (0,)), ((), ())), preferred_element_type=F32)


def _split2(x):
    hi = x.astype(BF16)
    lo = (x - hi.astype(F32)).astype(BF16)
    return hi, lo


def _split3(x):
    hi = x.astype(BF16)
    r = x - hi.astype(F32)
    mid = r.astype(BF16)
    lo = (r - mid.astype(F32)).astype(BF16)
    return hi, mid, lo


def _group_mean(x, ones_bf, group):
    hi, lo = _split2(x)
    return (_dot(hi, ones_bf) + _dot(lo, ones_bf)) * (1.0 / group)


def _group_rms(x, ones_bf, group):
    return x * lax.rsqrt(_group_mean(x * x, ones_bf, group) + EPS)


def _sigmoid(x):
    return 1.0 / (1.0 + jnp.exp(-x))


def _silu(x):
    return x * _sigmoid(x)


def _params(*sem):
    return pltpu.CompilerParams(dimension_semantics=sem, vmem_limit_bytes=VMEM_LIMIT)


def _block_ones(n, group):
    i = np.arange(n)
    return jnp.asarray((i[:, None] // group == i[None, :] // group).astype(np.float32), BF16)


def _norm_matmul_kernel(x_ref, g_ref, w_ref, o_ref, xn_ref):
    @pl.when(pl.program_id(1) == 0)
    def _():
        x = x_ref[...]
        y = x * lax.rsqrt(jnp.mean(x * x, axis=-1, keepdims=True) + EPS) * g_ref[...]
        xn_ref[...] = y.astype(BF16)

    o_ref[...] = _dot(xn_ref[...], w_ref[...])


def _norm_matmul(x, gain, w_bf, tm, tn):
    T, K = x.shape
    N = w_bf.shape[1]
    return pl.pallas_call(
        _norm_matmul_kernel,
        grid=(T // tm, N // tn),
        in_specs=[pl.BlockSpec((tm, K), lambda i, j: (i, 0)),
                  pl.BlockSpec((1, K), lambda i, j: (0, 0)),
                  pl.BlockSpec((K, tn), lambda i, j: (0, j))],
        out_specs=pl.BlockSpec((tm, tn), lambda i, j: (i, j)),
        out_shape=jax.ShapeDtypeStruct((T, N), F32),
        scratch_shapes=[pltpu.VMEM((tm, K), BF16)],
        compiler_params=_params("parallel", "arbitrary"),
        name="in_proj",
    )(x, gain.reshape(1, K), w_bf)


def _prep_kernel(nq_ref, dq_ref, dk_ref, dv_ref, kvs_ref, kvw_ref, gains_ref, ones64_ref, ones32_ref,
                 ones64h_ref, o_nq, o_dq, o_dk, o_dv, o_kvs, o_vs, o_kvw, o_vw):
    ones64 = ones64_ref[...]
    ones32 = ones32_ref[...]
    ones64h = ones64h_ref[...]
    o_nq[...] = (_group_rms(nq_ref[...], ones64, HD) * gains_ref[0:1, :]).T.astype(BF16)
    o_dq[...] = (_group_rms(dq_ref[...], ones32, DIFF_QK) * gains_ref[1:2, :]).T.astype(BF16)
    o_dk[...] = (_group_rms(dk_ref[...], ones32, DIFF_QK) * gains_ref[2:3, :]).astype(BF16)
    o_dv[...] = dv_ref[...].T.astype(BF16)
    lane = lax.broadcasted_iota(jnp.int32, kvs_ref.shape, 1)
    for kv_ref, gain, o_kv, o_v in ((kvs_ref, gains_ref[3:4, 0:LANE], o_kvs, o_vs),
                                    (kvw_ref, gains_ref[4:5, 0:LANE], o_kvw, o_vw)):
        x = kv_ref[...]
        o_kv[...] = jnp.where(lane < HD, _group_rms(x, ones64h, HD) * gain, x).astype(BF16)
        o_v[...] = x.T[HD:2 * HD, :].astype(BF16)


def _prep(u, gains, tm):
    T = u.shape[0]
    c256 = lambda name: pl.BlockSpec((tm, GROUP), lambda i, c=_B256[name]: (i, c))
    c128 = lambda name: pl.BlockSpec((tm, LANE), lambda i, c=_B128[name]: (i, c))
    const = lambda shape: pl.BlockSpec(shape, lambda i: (0, 0))
    rows = lambda w: pl.BlockSpec((tm, w), lambda i: (i, 0))
    colsT = lambda h: pl.BlockSpec((h, tm), lambda i: (0, i))
    sds = jax.ShapeDtypeStruct
    return pl.pallas_call(
        _prep_kernel,
        grid=(T // tm,),
        in_specs=[c256("nsa_q"), c256("d_q"), c256("d_k"), c256("d_v"), c128("kvslc"), c128("kvwin"),
                  const((8, GROUP)), const((GROUP, GROUP)), const((GROUP, GROUP)), const((LANE, LANE))],
        out_specs=[colsT(GROUP), colsT(GROUP), rows(GROUP), colsT(GROUP),
                   rows(LANE), colsT(HD), rows(LANE), colsT(HD)],
        out_shape=[sds((GROUP, T), BF16), sds((GROUP, T), BF16), sds((T, GROUP), BF16), sds((GROUP, T), BF16),
                   sds((T, LANE), BF16), sds((HD, T), BF16), sds((T, LANE), BF16), sds((HD, T), BF16)],
        compiler_params=_params("parallel"),
        name="attn_prep",
    )(u, u, u, u, u, u, gains, _block_ones(GROUP, HD), _block_ones(GROUP, DIFF_QK), _block_ones(LANE, HD))


def _compress_kernel(xk_ref, xv_ref, pos_ref, w1_ref, w2_ref, gain_ref, ones64h_ref, kc_ref, vc_ref):
    half = NSA_CMP_STRIDE * HD
    n_rows = xk_ref.shape[0]

    def compress(x, j):
        top = (x + pos_ref[j, 0:1, :]).astype(BF16)
        bot = (x + pos_ref[j, 1:2, :]).astype(BF16)
        a = _dot(top, w1_ref[j, 0:half, :])
        b = _dot(bot, w1_ref[j, half:2 * half, :])
        hidden = a + pltpu.roll(b, n_rows - 1, 0)
        return _dot(_silu(hidden).astype(BF16), w2_ref[j])

    kc = compress(xk_ref[...], 0)
    kc_ref[...] = _group_rms(kc, ones64h_ref[...], HD) * gain_ref[...]
    vc_ref[...] = compress(xv_ref[...], 1).T[0:HD, :].astype(BF16)


def _compress(xk, xv, pos, w1_bf, w2p_bf, gain_row):
    B, ncp, half = xk.shape
    sq = pl.Squeezed()
    full = lambda shape: pl.BlockSpec(shape, lambda b: (0,) * len(shape))
    return pl.pallas_call(
        _compress_kernel,
        grid=(B,),
        in_specs=[pl.BlockSpec((sq, ncp, half), lambda b: (b, 0, 0)),
                  pl.BlockSpec((sq, ncp, half), lambda b: (b, 0, 0)),
                  full((2, 2, half)), full((2, 2 * half, NSA_CMP_HIDDEN)), full((2, NSA_CMP_HIDDEN, LANE)),
                  full((1, LANE)), full((LANE, LANE))],
        out_specs=[pl.BlockSpec((sq, ncp, LANE), lambda b: (b, 0, 0)),
                   pl.BlockSpec((sq, HD, ncp), lambda b: (b, 0, 0))],
        out_shape=[jax.ShapeDtypeStruct((B, ncp, LANE), F32), jax.ShapeDtypeStruct((B, HD, ncp), BF16)],
        compiler_params=_params("parallel"),
        name="nsa_compress",
    )(xk, xv, pos, w1_bf, w2p_bf, gain_row, _block_ones(LANE, HD))


def _softmax_step_t(s, v_t, m_old, l_old, acc_ref):
    m_new = jnp.maximum(m_old, jnp.max(s, axis=0, keepdims=True))
    alpha = jnp.exp2(m_old - m_new)
    p = jnp.exp2(s - m_new)
    acc_ref[...] = alpha * acc_ref[...] + _dot(v_t, p.astype(BF16))
    return m_new, alpha * l_old + jnp.sum(p, axis=0, keepdims=True)


def _pipelined_sweep(lo, hi, qk, process, process_last, stats0):
    def body(j, carry):
        stats, scores = carry
        nxt = qk(j + 1)
        return process(j, scores, stats), nxt

    stats, scores = lax.fori_loop(lo, hi - 1, body, (stats0, qk(lo)))
    return process_last(hi - 1, scores, stats)


def _nsa_kernel(qt_ref, g_ref, kc_ref, vct_ref, kvs_ref, vst_ref, kvw_ref, vwt_ref, ovl_ref, exp_ref,
                o_ref, qs_ref, *acc_refs, tq, tk, tw, ksel):
    i = pl.program_id(1)
    t0 = i * tq
    ncp = kc_ref.shape[0]
    nsel = ovl_ref.shape[0]
    acc_s, acc_w = acc_refs[:HEADS], acc_refs[HEADS:]

    qt = qt_ref[...]
    qs_ref[HD:LANE, :] = jnp.zeros((LANE - HD, HEADS * tq), BF16)
    for h in range(HEADS):
        qs_ref[0:HD, h * tq:(h + 1) * tq] = qt[h * HD:(h + 1) * HD, :]
        acc_s[h][...] = jnp.zeros((HD, tq), F32)
        acc_w[h][...] = jnp.zeros((HD, tq), F32)

    kc_hi, kc_lo = _split2(kc_ref[...])
    n_idx = lax.broadcasted_iota(jnp.int32, (ncp, tq), 0)
    t_lane = t0 + lax.broadcasted_iota(jnp.int32, (ncp, tq), 1)
    ok = n_idx * NSA_CMP_STRIDE + (NSA_CMP_LEN - 1) <= t_lane
    cmp_scores = _dot(kc_hi, qs_ref[...]) + _dot(kc_lo, qs_ref[...])
    o_c = []
    psum = jnp.zeros((ncp, tq), F32)
    for h in range(HEADS):
        s = jnp.where(ok, cmp_scores[:, h * tq:(h + 1) * tq], NEG)
        e = jnp.exp2(s - jnp.max(s, axis=0, keepdims=True))
        p = jnp.where(ok, e / jnp.sum(e, axis=0, keepdims=True), 0.0)
        o_c.append(_dot(vct_ref[...], p.astype(BF16)))
        psum = psum + p

    p_hi, p_lo = _split2(psum)
    imp = _dot(ovl_ref[...], p_hi) + _dot(ovl_ref[...], p_lo)
    blk = lax.broadcasted_iota(jnp.int32, (nsel, tq), 0)
    t_col = t0 + lax.broadcasted_iota(jnp.int32, (nsel, tq), 1)
    cur = t_col // NSA_SEL_LEN
    forced = (blk == 0) | (blk == cur) | (blk == cur - 1)
    vals = jnp.where(forced, BIG, jnp.where(blk * NSA_SEL_LEN <= t_col, imp, NEG))
    blk_f = blk.astype(F32)
    sel = jnp.zeros((nsel, tq), F32)
    for _ in range(ksel):
        mx = jnp.max(vals, axis=0, keepdims=True)
        first = jnp.min(jnp.where(vals == mx, blk_f, float(nsel)), axis=0, keepdims=True)
        pick = blk_f == first
        sel = jnp.where(pick, 1.0, sel)
        vals = jnp.where(pick, LOWEST, vals)
    if nsel < LANE:
        sel = jnp.concatenate([sel, jnp.zeros((LANE - nsel, tq), F32)], axis=0)
    sel_bf = sel.astype(BF16)

    stats0 = tuple((jnp.full((1, tq), NEG, F32), jnp.zeros((1, tq), F32)) for _ in range(HEADS))

    def sweep(lo, hi, k_ref, vt_ref, accs, width, bias_fn):
        def body(kt, stats):
            k0 = pl.multiple_of(kt * width, width)
            scores = _dot(k_ref[pl.ds(k0, width), :], qs_ref[...])
            kpos = k0 + lax.broadcasted_iota(jnp.int32, (width, tq), 0)
            tpos = t0 + lax.broadcasted_iota(jnp.int32, (width, tq), 1)
            bias = bias_fn(k0, kpos, tpos)
            vt = vt_ref[:, pl.ds(k0, width)]
            return tuple(_softmax_step_t(scores[:, h * tq:(h + 1) * tq] + bias, vt,
                                         stats[h][0], stats[h][1], accs[h]) for h in range(HEADS))

        return lax.fori_loop(lo, hi, body, stats0)

    def sel_bias(k0, kpos, tpos):
        chosen = _dot(exp_ref[pl.ds(k0, tk), :], sel_bf)
        return jnp.where((chosen > 0.5) & (kpos <= tpos), 0.0, NEG)

    st_s = sweep(0, (t0 + tq - 1) // tk + 1, kvs_ref, vst_ref, acc_s, tk, sel_bias)

    def win_bias(k0, kpos, tpos):
        return jnp.where((kpos <= tpos) & (kpos > tpos - NSA_WINDOW), 0.0, NEG)

    st_w = sweep(jnp.maximum(t0 - NSA_WINDOW, 0) // tw, (t0 + tq - 1) // tw + 1,
                 kvw_ref, vwt_ref, acc_w, tw, win_bias)

    gates = _sigmoid(g_ref[...].T)
    outs = []
    for h in range(HEADS):
        outs.append(gates[3 * h:3 * h + 1, :] * o_c[h]
                    + gates[3 * h + 1:3 * h + 2, :] * (acc_s[h][...] / st_s[h][1])
                    + gates[3 * h + 2:3 * h + 3, :] * (acc_w[h][...] / st_w[h][1]))
    o_ref[...] = jnp.concatenate(outs, axis=0).T


def _nsa_attention(u, nqt, kc, vct, kvs, vst, kvw, vwt, B, S, tq, tk, tw):
    nq = S // tq
    ncp = S // NSA_CMP_STRIDE
    nsel = S // NSA_SEL_LEN
    ksel = min(NSA_TOPK, nsel)
    n_cmp = (S - NSA_CMP_LEN) // NSA_CMP_STRIDE + 1
    cmp_start = np.arange(ncp) * NSA_CMP_STRIDE
    sel_start = np.arange(nsel) * NSA_SEL_LEN
    overlap = ((cmp_start[:, None] <= sel_start[None, :] + NSA_SEL_LEN - 1)
               & (cmp_start[:, None] + NSA_CMP_LEN - 1 >= sel_start[None, :])
               & (np.arange(ncp)[:, None] < n_cmp))
    ovl_t = jnp.asarray(overlap.T.astype(np.float32), BF16)
    expand = (jnp.arange(S, dtype=jnp.int32)[:, None] // NSA_SEL_LEN
              == jnp.arange(LANE, dtype=jnp.int32)[None, :]).astype(BF16)
    sq = pl.Squeezed()
    seq_rows = pl.BlockSpec((S, LANE), lambda b, i: (b, 0))
    seq_cols = pl.BlockSpec((HD, S), lambda b, i: (0, b))
    kern = functools.partial(_nsa_kernel, tq=tq, tk=tk, tw=tw, ksel=ksel)
    return pl.pallas_call(
        kern,
        grid=(B, nq),
        in_specs=[pl.BlockSpec((GROUP, tq), lambda b, i: (0, b * nq + i)),
                  pl.BlockSpec((tq, LANE), lambda b, i, c=_B128["nsa_g"]: (b * nq + i, c)),
                  pl.BlockSpec((sq, ncp, LANE), lambda b, i: (b, 0, 0)),
                  pl.BlockSpec((sq, HD, ncp), lambda b, i: (b, 0, 0)),
                  seq_rows, seq_cols, seq_rows, seq_cols,
                  pl.BlockSpec((nsel, ncp), lambda b, i: (0, 0)),
                  pl.BlockSpec((S, LANE), lambda b, i: (0, 0))],
        out_specs=pl.BlockSpec((tq, GROUP), lambda b, i: (b * nq + i, 0)),
        out_shape=jax.ShapeDtypeStruct((B * S, GROUP), F32),
        scratch_shapes=[pltpu.VMEM((LANE, HEADS * tq), BF16)] + [pltpu.VMEM((HD, tq), F32)] * (2 * HEADS),
        compiler_params=_params("parallel", "arbitrary"),
        name="nsa_attention",
    )(nqt, u, kc, vct, kvs, vst, kvw, vwt, ovl_t, expand)


def _diff_kernel(lam_ref, qt_ref, k_ref, vt_ref, gain_ref, o_ref, qs_ref, *acc_refs, tq, tk, lam_init):
    i = pl.program_id(1)
    t0 = i * tq
    lanes = 2 * tq

    qt = qt_ref[...]
    row = lax.broadcasted_iota(jnp.int32, (GROUP, tq), 0)
    zero = jnp.zeros_like(qt)
    for h in range(HEADS):
        qs_ref[h] = jnp.concatenate([jnp.where(row // DIFF_QK == 2 * h, qt, zero),
                                     jnp.where(row // DIFF_QK == 2 * h + 1, qt, zero)], axis=1)
        acc_refs[h][...] = jnp.zeros((HD, lanes), F32)

    def step(k0, width, masked, stats):
        k = k_ref[pl.ds(k0, width), :]
        scores = [_dot(k, qs_ref[h]) for h in range(HEADS)]
        out = []
        for h in range(HEADS):
            s = scores[h]
            if masked:
                kpos = k0 + lax.broadcasted_iota(jnp.int32, (width, lanes), 0)
                tpos = t0 + lax.broadcasted_iota(jnp.int32, (width, lanes), 1) % tq
                s = jnp.where(kpos <= tpos, s, NEG)
            out.append(_softmax_step_t(s, vt_ref[h * HD:(h + 1) * HD, pl.ds(k0, width)],
                                       stats[h][0], stats[h][1], acc_refs[h]))
        return tuple(out)

    stats0 = tuple((jnp.full((1, lanes), NEG, F32), jnp.zeros((1, lanes), F32)) for _ in range(HEADS))
    n_wide = t0 // tk
    stats = lax.fori_loop(0, n_wide, lambda kt, c: step(pl.multiple_of(kt * tk, tk), tk, False, c), stats0)
    stats = lax.fori_loop(n_wide * (tk // tq), t0 // tq,
                          lambda kt, c: step(pl.multiple_of(kt * tq, tq), tq, False, c), stats)
    stats = step(pl.multiple_of(t0, tq), tq, True, stats)

    lam = lam_ref[...]
    lam_full = (jnp.exp(jnp.sum(lam[0:1] * lam[1:2], axis=1, keepdims=True))
                - jnp.exp(jnp.sum(lam[2:3] * lam[3:4], axis=1, keepdims=True)) + lam_init)
    outs = []
    for h in range(HEADS):
        r = acc_refs[h][...] / stats[h][1]
        d = r[:, 0:tq] - lam_full * r[:, tq:lanes]
        d = d * lax.rsqrt(jnp.mean(d * d, axis=0, keepdims=True) + EPS)
        outs.append(d * gain_ref[...] * (1.0 - lam_init))
    o_ref[...] = jnp.concatenate(outs, axis=0).T


def _diff_attention(dqt, dk, dvt, lam, gain_col, B, S, tq, tk, layer_idx):
    nq = S // tq
    lam_init = 0.8 - 0.6 * math.exp(-0.3 * layer_idx)
    kern = functools.partial(_diff_kernel, tq=tq, tk=tk, lam_init=lam_init)
    return pl.pallas_call(
        kern,
        grid=(B, nq),
        in_specs=[pl.BlockSpec((4, DIFF_QK), lambda b, i: (0, 0)),
                  pl.BlockSpec((GROUP, tq), lambda b, i: (0, b * nq + i)),
                  pl.BlockSpec((S, GROUP), lambda b, i: (b, 0)),
                  pl.BlockSpec((GROUP, S), lambda b, i: (0, b)),
                  pl.BlockSpec((HD, tq), lambda b, i: (0, 0))],
        out_specs=pl.BlockSpec((tq, GROUP), lambda b, i: (b * nq + i, 0)),
        out_shape=jax.ShapeDtypeStruct((B * S, GROUP), F32),
        scratch_shapes=[pltpu.VMEM((HEADS, GROUP, 2 * tq), BF16)] + [pltpu.VMEM((HD, 2 * tq), F32)] * HEADS,
        compiler_params=_params("parallel", "arbitrary"),
        name="diff_attention",
    )(lam, dqt, dk, dvt, gain_col)


_LEVELS = (32, 16, 8, 4, 2, 1)


def _stack_heads(x, lane, group, count):
    zero = jnp.zeros_like(x)
    return jnp.concatenate([jnp.where(lane // group == g, x, zero) for g in range(count)], axis=0)


def _unstack_heads(x4, lane, rows):
    out = jnp.where(lane // HD == 0, x4[0:rows], 0.0)
    for h in range(1, HEADS):
        out = out + jnp.where(lane // HD == h, x4[h * rows:(h + 1) * rows], 0.0)
    return out


def _linear_consts():
    r = np.arange(CHUNK)[:, None]
    t = np.arange(CHUNK)[None, :]
    mats = []
    for s in _LEVELS:
        mid = (r // (2 * s)) * 2 * s + s
        upper = (r % (2 * s)) >= s
        mats.append(np.where(upper, (t > mid) & (t <= r), (t > r) & (t <= mid)))
    mats.append(t <= r)
    mats.append(t > r)
    dall = np.concatenate(mats, axis=0).astype(np.float32)
    masks = [(r // (2 * s) == t // (2 * s)) for s in _LEVELS] + [r == t]
    mall = np.stack([np.tile(m.astype(np.float32), (HEADS, 1)) for m in masks])
    return jnp.asarray(dall, BF16), jnp.asarray(mall, F32)


def _linear_chunk(q, k, v, lg, dall, mall_ref, state_ref, dk):
    dkh = HEADS * dk
    hi, mid, lo = _split3(lg)
    ex = jnp.exp(_dot(dall, hi) + _dot(dall, mid) + _dot(dall, lo))
    row = lax.broadcasted_iota(jnp.int32, (CHUNK, dkh), 0)
    lane_k = lax.broadcasted_iota(jnp.int32, (CHUNK, dkh), 1)
    lane_v = lax.broadcasted_iota(jnp.int32, (CHUNK, GROUP), 1)
    k_bf = k.astype(BF16)
    a = mall_ref[len(_LEVELS)] * _dot_nt(_stack_heads(q.astype(BF16), lane_k, dk, HEADS), k_bf)
    for li, s in enumerate(_LEVELS):
        e = ex[li * CHUNK:(li + 1) * CHUNK]
        upper = (row & s) != 0
        qt = jnp.where(upper, q * e, 0.0).astype(BF16)
        kt = jnp.where(upper, 0.0, k * e).astype(BF16)
        a = a + mall_ref[li] * _dot_nt(_stack_heads(qt, lane_k, dk, HEADS), kt)
    o_intra = _unstack_heads(_dot(a.astype(BF16), v.astype(BF16)), lane_v, CHUNK)
    e_b = ex[6 * CHUNK:7 * CHUNK]
    e_u = ex[7 * CHUNK:8 * CHUNK]
    st = state_ref[...]
    o_inter = _dot_nt((q * e_b).astype(BF16), st.astype(BF16))
    kv = _dot_tn(v.astype(BF16), (k * e_u).astype(BF16))
    srow = lax.broadcasted_iota(jnp.int32, (GROUP, dkh), 0)
    scol = lax.broadcasted_iota(jnp.int32, (GROUP, dkh), 1)
    state_ref[...] = st * e_b[CHUNK - 1:CHUNK, :] + jnp.where(srow // HD == scol // dk, kv, 0.0)
    return o_inter + o_intra


def _gla_kernel(q_ref, k_ref, v_ref, lr_ref, og_ref, w2_ref, b_ref, gain_ref, dall_ref, mall_ref,
                ones64_ref, o_ref, state_ref, *, tm):
    @pl.when(pl.program_id(1) == 0)
    def _():
        state_ref[...] = jnp.zeros_like(state_ref)

    dall = dall_ref[...]
    ones64 = ones64_ref[...]
    w_hi, w_lo = _split2(w2_ref[...])

    def body(c, carry):
        r0 = pl.multiple_of(c * CHUNK, CHUNK)
        rs = pl.ds(r0, CHUNK)
        lr_hi, lr_lo = _split2(lr_ref[rs, :])
        x = _dot(lr_hi, w_hi) + _dot(lr_lo, w_hi) + _dot(lr_hi, w_lo) + b_ref[...]
        lg = (jnp.minimum(x, 0.0) - jnp.log(1.0 + jnp.exp(-jnp.abs(x)))) * (1.0 / GLA_TAU)
        o = _linear_chunk(q_ref[rs, :] * (GLA_DK ** -0.5), k_ref[rs, :], v_ref[rs, :], lg,
                          dall, mall_ref, state_ref, GLA_DK)
        o_ref[rs, :] = _group_rms(o, ones64, HD) * gain_ref[...] * _silu(og_ref[rs, :])
        return carry

    lax.fori_loop(0, tm // CHUNK, body, 0)


def _hgrn_kernel(q_ref, f_ref, i_ref, og_ref, lbl_ref, gain_ref, dall_ref, mall_ref, ones64_ref,
                 o_ref, state_ref, *, tm, layer_idx):
    @pl.when(pl.program_id(1) == 0)
    def _():
        state_ref[...] = jnp.zeros_like(state_ref)

    dall = dall_ref[...]
    ones64 = ones64_ref[...]
    logits = lbl_ref[...]
    ez = jnp.exp(logits - jnp.max(logits, axis=0, keepdims=True))
    probs = ez / jnp.sum(ez, axis=0, keepdims=True)
    lb = jnp.zeros((1, GROUP), F32)
    for j in range(1, layer_idx + 1):
        lb = lb + probs[j:j + 1]

    def body(c, carry):
        r0 = pl.multiple_of(c * CHUNK, CHUNK)
        rs = pl.ds(r0, CHUNK)
        z = f_ref[rs, :]
        f = lb + (1.0 - lb) * _sigmoid(z)
        k = (1.0 - lb) * _sigmoid(-z)
        o = _linear_chunk(q_ref[rs, :], k, i_ref[rs, :], jnp.log(f), dall, mall_ref, state_ref, HD)
        o_ref[rs, :] = _group_rms(o, ones64, HD) * gain_ref[...] * _silu(og_ref[rs, :])
        return carry

    lax.fori_loop(0, tm // CHUNK, body, 0)


def _linear_mixers(u, w2pad, b_gate, gla_gain, lb_logits, hgrn_gain, B, S, tm, layer_idx):
    nt = S // tm
    dall, mall = _linear_consts()
    ones64 = _block_ones(GROUP, HD)
    c256 = lambda name: pl.BlockSpec((tm, GROUP), lambda b, i, c=_B256[name]: (b * nt + i, c))
    c128 = lambda name: pl.BlockSpec((tm, LANE), lambda b, i, c=_B128[name]: (b * nt + i, c))
    full = lambda shape: pl.BlockSpec(shape, lambda b, i: (0,) * len(shape))
    out_spec = pl.BlockSpec((tm, GROUP), lambda b, i: (b * nt + i, 0))
    out_shape = jax.ShapeDtypeStruct((B * S, GROUP), F32)
    consts = [full(dall.shape), full(mall.shape), full((GROUP, GROUP))]
    o_gla = pl.pallas_call(
        functools.partial(_gla_kernel, tm=tm),
        grid=(B, nt),
        in_specs=[c128("g_q"), c128("g_k"), c256("g_v"), c128("g_lr"), c256("g_og"),
                  full((LANE, LANE)), full((1, LANE)), full((1, GROUP))] + consts,
        out_specs=out_spec, out_shape=out_shape,
        scratch_shapes=[pltpu.VMEM((GROUP, HEADS * GLA_DK), F32)],
        compiler_params=_params("parallel", "arbitrary"),
        name="gla_mixer",
    )(u, u, u, u, u, w2pad, b_gate, gla_gain, dall, mall, ones64)
    depth = lb_logits.shape[0]
    o_hgrn = pl.pallas_call(
        functools.partial(_hgrn_kernel, tm=tm, layer_idx=layer_idx),
        grid=(B, nt),
        in_specs=[c256("r_q"), c256("r_f"), c256("r_i"), c256("r_og"),
                  full((depth, GROUP)), full((1, GROUP))] + consts,
        out_specs=out_spec, out_shape=out_shape,
        scratch_shapes=[pltpu.VMEM((GROUP, GROUP), F32)],
        compiler_params=_params("parallel", "arbitrary"),
        name="hgrn_mixer",
    )(u, u, u, u, lb_logits, hgrn_gain, dall, mall, ones64)
    return o_gla, o_hgrn


def _out_proj_kernel(h_ref, a_ref, b_ref, c_ref, d_ref, w_ref, o_ref):
    acc = h_ref[...]
    for j, r in enumerate((a_ref, b_ref, c_ref, d_ref)):
        acc = acc + _dot(r[...].astype(BF16), w_ref[j * GROUP:(j + 1) * GROUP, :])
    o_ref[...] = acc


def _out_proj(h, parts, w_bf, tm):
    T = h.shape[0]
    part = pl.BlockSpec((tm, GROUP), lambda i: (i, 0))
    return pl.pallas_call(
        _out_proj_kernel,
        grid=(T // tm,),
        in_specs=[pl.BlockSpec((tm, D_MODEL), lambda i: (i, 0)), part, part, part, part,
                  pl.BlockSpec((D_MODEL, D_MODEL), lambda i: (0, 0))],
        out_specs=pl.BlockSpec((tm, D_MODEL), lambda i: (i, 0)),
        out_shape=jax.ShapeDtypeStruct((T, D_MODEL), F32),
        compiler_params=_params("parallel"),
        name="out_proj",
    )(h, *parts, w_bf)


def _ffn_kernel(h_ref, g_ref, wg_ref, wu_ref, wd_ref, o_ref, xn_ref, acc_ref):
    f = pl.program_id(1)

    @pl.when(f == 0)
    def _():
        x = h_ref[...]
        y = x * lax.rsqrt(jnp.mean(x * x, axis=-1, keepdims=True) + EPS) * g_ref[...]
        xn_ref[...] = y.astype(BF16)
        acc_ref[...] = x

    xn = xn_ref[...]
    mid = _silu(_dot(xn, wg_ref[...])) * _dot(xn, wu_ref[...])
    acc_ref[...] += _dot(mid.astype(BF16), wd_ref[...])

    @pl.when(f == pl.num_programs(1) - 1)
    def _():
        o_ref[...] = acc_ref[...]


def _ffn(h, gain, wg, wu, wd, tm, tf):
    T = h.shape[0]
    F = wg.shape[1]
    return pl.pallas_call(
        _ffn_kernel,
        grid=(T // tm, F // tf),
        in_specs=[pl.BlockSpec((tm, D_MODEL), lambda i, f: (i, 0)),
                  pl.BlockSpec((1, D_MODEL), lambda i, f: (0, 0)),
                  pl.BlockSpec((D_MODEL, tf), lambda i, f: (0, f)),
                  pl.BlockSpec((D_MODEL, tf), lambda i, f: (0, f)),
                  pl.BlockSpec((tf, D_MODEL), lambda i, f: (f, 0))],
        out_specs=pl.BlockSpec((tm, D_MODEL), lambda i, f: (i, 0)),
        out_shape=jax.ShapeDtypeStruct((T, D_MODEL), F32),
        scratch_shapes=[pltpu.VMEM((tm, D_MODEL), BF16), pltpu.VMEM((tm, D_MODEL), F32)],
        compiler_params=_params("parallel", "arbitrary"),
        name="ffn_swiglu",
    )(h, gain.reshape(1, D_MODEL), wg, wu, wd)


def _router_kernel(h_ref, g_ref, r_ref, c_ref, comb_ref):
    x = h_ref[...]
    y = x * lax.rsqrt(jnp.mean(x * x, axis=-1, keepdims=True) + EPS) * g_ref[...]
    c_ref[...] = y.astype(BF16)
    y_hi, y_lo = _split2(y)
    r_hi, r_lo = _split2(r_ref[...])
    logits = _dot(y_hi, r_hi) + _dot(y_lo, r_hi) + _dot(y_hi, r_lo)
    lane = lax.broadcasted_iota(jnp.int32, logits.shape, 1)
    lane_f = lane.astype(F32)
    logits = jnp.where(lane < N_EXPERTS, logits, LOWEST)
    m1 = jnp.max(logits, axis=1, keepdims=True)
    i1 = jnp.min(jnp.where(logits == m1, lane_f, float(LANE)), axis=1, keepdims=True)
    rest = jnp.where(lane_f == i1, LOWEST, logits)
    m2 = jnp.max(rest, axis=1, keepdims=True)
    i2 = jnp.min(jnp.where(rest == m2, lane_f, float(LANE)), axis=1, keepdims=True)
    e2 = jnp.exp(m2 - m1)
    w1 = 1.0 / (1.0 + e2)
    comb_ref[...] = jnp.where(lane_f == i1, w1, jnp.where(lane_f == i2, e2 * w1, 0.0))


def _router(h, gain, router_pad, tm):
    T = h.shape[0]
    return pl.pallas_call(
        _router_kernel,
        grid=(T // tm,),
        in_specs=[pl.BlockSpec((tm, D_MODEL), lambda i: (i, 0)),
                  pl.BlockSpec((1, D_MODEL), lambda i: (0, 0)),
                  pl.BlockSpec((D_MODEL, LANE), lambda i: (0, 0))],
        out_specs=[pl.BlockSpec((tm, D_MODEL), lambda i: (i, 0)), pl.BlockSpec((tm, LANE), lambda i: (i, 0))],
        out_shape=[jax.ShapeDtypeStruct((T, D_MODEL), BF16), jax.ShapeDtypeStruct((T, LANE), F32)],
        compiler_params=_params("parallel"),
        name="moe_router",
    )(h, gain.reshape(1, D_MODEL), router_pad)


def _moe_kernel(h_ref, c_ref, comb_ref, wg_ref, wu_ref, wd_ref, o_ref, acc_ref):
    e = pl.program_id(1)
    f = pl.program_id(2)

    @pl.when((e == 0) & (f == 0))
    def _():
        acc_ref[...] = h_ref[...]

    comb = comb_ref[...]
    lane = lax.broadcasted_iota(jnp.int32, comb.shape, 1)
    cw = jnp.sum(jnp.where(lane == e, comb, 0.0), axis=1, keepdims=True)
    xn = c_ref[...]
    mid = _silu(_dot(xn, wg_ref[...])) * _dot(xn, wu_ref[...]) * cw
    acc_ref[...] += _dot(mid.astype(BF16), wd_ref[...])

    @pl.when((e == pl.num_programs(1) - 1) & (f == pl.num_programs(2) - 1))
    def _():
        o_ref[...] = acc_ref[...]


def _moe(h, c_bf, comb, wg, wu, wd, tm, tf):
    T = h.shape[0]
    E, _, F = wg.shape
    sq = pl.Squeezed()
    return pl.pallas_call(
        _moe_kernel,
        grid=(T // tm, E, F // tf),
        in_specs=[pl.BlockSpec((tm, D_MODEL), lambda i, e, f: (i, 0)),
                  pl.BlockSpec((tm, D_MODEL), lambda i, e, f: (i, 0)),
                  pl.BlockSpec((tm, LANE), lambda i, e, f: (i, 0)),
                  pl.BlockSpec((sq, D_MODEL, tf), lambda i, e, f: (e, 0, f)),
                  pl.BlockSpec((sq, D_MODEL, tf), lambda i, e, f: (e, 0, f)),
                  pl.BlockSpec((sq, tf, D_MODEL), lambda i, e, f: (e, f, 0))],
        out_specs=pl.BlockSpec((tm, D_MODEL), lambda i, e, f: (i, 0)),
        out_shape=jax.ShapeDtypeStruct((T, D_MODEL), F32),
        scratch_shapes=[pltpu.VMEM((tm, D_MODEL), F32)],
        compiler_params=_params("parallel", "arbitrary", "arbitrary"),
        name="moe_experts",
    )(h, c_bf, comb, wg, wu, wd)


def _ple_kernel(h_ref, p_ref, g_ref, wg_ref, wp_ref, o_ref):
    x = h_ref[...]
    y = x * lax.rsqrt(jnp.mean(x * x, axis=-1, keepdims=True) + EPS) * g_ref[...]
    gate = _sigmoid(_dot(y.astype(BF16), wg_ref[...]))
    o_ref[...] = x + _dot(p_ref[...].astype(BF16), wp_ref[...]) * gate


def _ple(h, p, gain, wg, wp, tm):
    T = h.shape[0]
    return pl.pallas_call(
        _ple_kernel,
        grid=(T // tm,),
        in_specs=[pl.BlockSpec((tm, D_MODEL), lambda i: (i, 0)),
                  pl.BlockSpec((tm, PLE_DIM), lambda i: (i, 0)),
                  pl.BlockSpec((1, D_MODEL), lambda i: (0, 0)),
                  pl.BlockSpec((D_MODEL, D_MODEL), lambda i: (0, 0)),
                  pl.BlockSpec((PLE_DIM, D_MODEL), lambda i: (0, 0))],
        out_specs=pl.BlockSpec((tm, D_MODEL), lambda i: (i, 0)),
        out_shape=jax.ShapeDtypeStruct((T, D_MODEL), F32),
        compiler_params=_params("parallel"),
        name="ple_gate",
    )(h, p, gain.reshape(1, D_MODEL), wg, wp)


def _tiles(T, S):
    pick = lambda n, pref: max(t for t in pref if n % t == 0)
    return dict(
        proj_m=pick(T, (1024, 512, 256, 128)), proj_n=NC // 3,
        prep_m=pick(T, (512, 256, 128)),
        attn_q=256, attn_k=pick(S, (512, 256)), attn_w=256,
        lin_m=pick(S, (512, 256, 128, 64)),
        row_m=pick(T, (512, 256, 128)),
        ffn_m=pick(T, (1024, 512, 256, 128)), ffn_f=512,
    )


def kernel(x, p, norm_attn, w_in, w_out, nsa_cmp_pos, nsa_cmp_w1, nsa_cmp_w2, nsa_qk_gain, diff_qk_gain, diff_lambda, diff_norm, gla_w_gate2, gla_b_gate, gla_norm, hgrn_lb_logits, hgrn_norm, norm_ffn, ffn_w_gate, ffn_w_up, ffn_w_down, moe_router, moe_w_gate, moe_w_up, moe_w_down, ple_norm, ple_w_gate, ple_w_proj):
    B, S, _ = x.shape
    depth = w_in.shape[0]
    T = B * S
    t = _tiles(T, S)
    ncp = S // NSA_CMP_STRIDE
    half = NSA_CMP_STRIDE * HD
    cols = jnp.asarray(np.maximum(_COLS, 0), jnp.int32)
    col_mask = jnp.asarray(_COLS >= 0)
    ones_row = jnp.ones((GROUP,), F32)

    h = x.reshape(T, D_MODEL)
    for i in range(depth):
        w_in_r = jnp.where(col_mask[None, :], jnp.take(w_in[i], cols, axis=1), 0.0).astype(BF16)
        gains = jnp.stack([
            jnp.tile(nsa_qk_gain[i, 0], HEADS) * (HD ** -0.5 * LOG2E),
            jnp.tile(diff_qk_gain[i, 0], 2 * HEADS) * (DIFF_QK ** -0.5 * LOG2E),
            jnp.tile(diff_qk_gain[i, 1], 2 * HEADS),
            jnp.tile(nsa_qk_gain[i, 2], HEADS),
            jnp.tile(nsa_qk_gain[i, 3], HEADS),
            ones_row, ones_row, ones_row])
        pos = nsa_cmp_pos[i].reshape(2, 2, half)
        w1 = nsa_cmp_w1[i].astype(BF16)
        w2p = jnp.pad(nsa_cmp_w2[i], ((0, 0), (0, 0), (0, LANE - HD))).astype(BF16)
        kc_gain = jnp.pad(nsa_qk_gain[i, 1], (0, LANE - HD)).reshape(1, LANE)
        w2pad = jnp.zeros((LANE, LANE), F32).at[:GLA_RANK].set(gla_w_gate2[i])
        diff_gain_col = jnp.broadcast_to(diff_norm[i][:, None], (HD, t["attn_q"]))

        u = _norm_matmul(h, norm_attn[i], w_in_r, t["proj_m"], t["proj_n"])
        nqt, dqt, dk, dvt, kvs, vst, kvw, vwt = _prep(u, gains, t["prep_m"])
        kv = u[:, _B128["kvcmp"] * LANE:(_B128["kvcmp"] + 1) * LANE]
        xk = kv[:, :HD].reshape(B, ncp, half)
        xv = kv[:, HD:].reshape(B, ncp, half)
        kc, vct = _compress(xk, xv, pos, w1, w2p, kc_gain)
        o_a = _nsa_attention(u, nqt, kc, vct, kvs, vst, kvw, vwt, B, S, t["attn_q"], t["attn_k"], t["attn_w"])
        o_b = _diff_attention(dqt, dk, dvt, diff_lambda[i], diff_gain_col, B, S, t["attn_q"], t["attn_k"], i)
        o_c, o_d = _linear_mixers(u, w2pad, gla_b_gate[i].reshape(1, LANE),
                                  jnp.tile(gla_norm[i], HEADS).reshape(1, GROUP), hgrn_lb_logits,
                                  jnp.tile(hgrn_norm[i], HEADS).reshape(1, GROUP), B, S, t["lin_m"], i)
        h = _out_proj(h, (o_a, o_b, o_c, o_d), w_out[i].astype(BF16), t["row_m"])
        if i % 2 == 0:
            j = i // 2
            h = _ffn(h, norm_ffn[i], ffn_w_gate[j].astype(BF16), ffn_w_up[j].astype(BF16),
                     ffn_w_down[j].astype(BF16), t["ffn_m"], t["ffn_f"])
        else:
            j = i // 2
            router_pad = jnp.zeros((D_MODEL, LANE), F32).at[:, :N_EXPERTS].set(moe_router[j])
            c_bf, comb = _router(h, norm_ffn[i], router_pad, t["row_m"])
            h = _moe(h, c_bf, comb, moe_w_gate[j].astype(BF16), moe_w_up[j].astype(BF16),
                     moe_w_down[j].astype(BF16), t["ffn_m"], t["ffn_f"])
        h = _ple(h, p[i].reshape(T, PLE_DIM), ple_norm[i], ple_w_gate[i].astype(BF16),
                 ple_w_proj[i].astype(BF16), t["row_m"])
    return h.reshape(B, S, D_MODEL)
```

```python
import functools
import math

import numpy as np
import jax
import jax.numpy as jnp
from jax import lax
from jax.experimental import pallas as pl
from jax.experimental.pallas import tpu as pltpu

F32 = jnp.float32
BF16 = jnp.bfloat16

D_MODEL = 1024
HEADS = 4
HD = 64
GROUP = HEADS * HD
NSA_CMP_LEN = 32
NSA_CMP_STRIDE = 16
NSA_CMP_HIDDEN = 4 * HD
NSA_SEL_LEN = 64
NSA_TOPK = 16
NSA_WINDOW = 512
DIFF_QK = HD // 2
GLA_DK = HD // 2
GLA_RANK = 16
GLA_TAU = 16.0
CHUNK = 64
D_FF = 7 * D_MODEL // 2
N_EXPERTS = 8
PLE_DIM = 256
EPS = 1e-6
NEG = -1e30
BIG = 1e30
LOWEST = -3.0e38
LOG2E = 1.4426950408889634

VMEM_LIMIT = 52 * 1024 * 1024
LANE = 128

_SRC = dict(nsa_q=0, k_cmp=256, v_cmp=320, k_slc=384, v_slc=448, k_win=512, v_win=576, nsa_g=640,
            d_q=652, d_k=908, d_v=1164, g_q=1420, g_k=1548, g_v=1676, g_lr=1932, g_og=1948,
            r_q=2204, r_f=2460, r_i=2716, r_og=2972)
IN_COLS = 3228

_B256 = dict(nsa_q=0, d_q=1, d_k=2, d_v=3, g_v=4, g_og=5, r_q=6, r_f=7, r_i=8, r_og=9)
_B128 = dict(g_q=20, g_k=21, g_lr=22, kvcmp=23, kvslc=24, kvwin=25, nsa_g=26)
NC = 27 * 128


def _column_map():
    cols = -np.ones((NC,), np.int64)

    def put(dst, src, width):
        cols[dst:dst + width] = np.arange(src, src + width)

    for name in _B256:
        put(_B256[name] * 256, _SRC[name], 256)
    put(_B128["g_q"] * 128, _SRC["g_q"], 128)
    put(_B128["g_k"] * 128, _SRC["g_k"], 128)
    put(_B128["g_lr"] * 128, _SRC["g_lr"], GLA_RANK)
    put(_B128["kvcmp"] * 128, _SRC["k_cmp"], 128)
    put(_B128["kvslc"] * 128, _SRC["k_slc"], 128)
    put(_B128["kvwin"] * 128, _SRC["k_win"], 128)
    put(_B128["nsa_g"] * 128, _SRC["nsa_g"], 3 * HEADS)
    return cols


_COLS = _column_map()


def _dot(a, b):
    return jnp.dot(a, b, preferred_element_type=F32)


def _dot_nt(a, b):
    return lax.dot_general(a, b, (((1,), (1,)), ((), ())), preferred_element_type=F32)


def _dot_tn(a, b):
    return lax.dot_general(a, b, (((0,), (0,)), ((), ())), preferred_element_type=F32)


def _split2(x):
    hi = x.astype(BF16)
    lo = (x - hi.astype(F32)).astype(BF16)
    return hi, lo


def _split3(x):
    hi = x.astype(BF16)
    r = x - hi.astype(F32)
    mid = r.astype(BF16)
    lo = (r - mid.astype(F32)).astype(BF16)
    return hi, mid, lo


def _group_mean(x, ones_bf, group):
    hi, lo = _split2(x)
    return (_dot(hi, ones_bf) + _dot(lo, ones_bf)) * (1.0 / group)


def _group_rms(x, ones_bf, group):
    return x * lax.rsqrt(_group_mean(x * x, ones_bf, group) + EPS)


def _sigmoid(x):
    return 1.0 / (1.0 + jnp.exp(-x))


def _silu(x):
    return x * _sigmoid(x)


def _params(*sem):
    return pltpu.CompilerParams(dimension_semantics=sem, vmem_limit_bytes=VMEM_LIMIT)


def _block_ones(n, group):
    i = np.arange(n)
    return jnp.asarray((i[:, None] // group == i[None, :] // group).astype(np.float32), BF16)


def _norm_matmul_kernel(x_ref, g_ref, w_ref, o_ref, xn_ref):
    @pl.when(pl.program_id(1) == 0)
    def _():
        x = x_ref[...]
        y = x * lax.rsqrt(jnp.mean(x * x, axis=-1, keepdims=True) + EPS) * g_ref[...]
        xn_ref[...] = y.astype(BF16)

    o_ref[...] = _dot(xn_ref[...], w_ref[...])


def _norm_matmul(x, gain, w_bf, tm, tn):
    T, K = x.shape
    N = w_bf.shape[1]
    return pl.pallas_call(
        _norm_matmul_kernel,
        grid=(T // tm, N // tn),
        in_specs=[pl.BlockSpec((tm, K), lambda i, j: (i, 0)),
                  pl.BlockSpec((1, K), lambda i, j: (0, 0)),
                  pl.BlockSpec((K, tn), lambda i, j: (0, j))],
        out_specs=pl.BlockSpec((tm, tn), lambda i, j: (i, j)),
        out_shape=jax.ShapeDtypeStruct((T, N), F32),
        scratch_shapes=[pltpu.VMEM((tm, K), BF16)],
        compiler_params=_params("parallel", "arbitrary"),
        name="in_proj",
    )(x, gain.reshape(1, K), w_bf)


def _prep_kernel(nq_ref, dq_ref, dk_ref, dv_ref, kvs_ref, kvw_ref, gains_ref, ones64_ref, ones32_ref,
                 ones64h_ref, o_nq, o_dq, o_dk, o_dv, o_kvs, o_vs, o_kvw, o_vw):
    ones64 = ones64_ref[...]
    ones32 = ones32_ref[...]
    ones64h = ones64h_ref[...]
    o_nq[...] = (_group_rms(nq_ref[...], ones64, HD) * gains_ref[0:1, :]).T.astype(BF16)
    o_dq[...] = (_group_rms(dq_ref[...], ones32, DIFF_QK) * gains_ref[1:2, :]).T.astype(BF16)
    o_dk[...] = (_group_rms(dk_ref[...], ones32, DIFF_QK) * gains_ref[2:3, :]).astype(BF16)
    o_dv[...] = dv_ref[...].T.astype(BF16)
    lane = lax.broadcasted_iota(jnp.int32, kvs_ref.shape, 1)
    for kv_ref, gain, o_kv, o_v in ((kvs_ref, gains_ref[3:4, 0:LANE], o_kvs, o_vs),
                                    (kvw_ref, gains_ref[4:5, 0:LANE], o_kvw, o_vw)):
        x = kv_ref[...]
        o_kv[...] = jnp.where(lane < HD, _group_rms(x, ones64h, HD) * gain, x).astype(BF16)
        o_v[...] = x.T[HD:2 * HD, :].astype(BF16)


def _prep(u, gains, tm):
    T = u.shape[0]
    c256 = lambda name: pl.BlockSpec((tm, GROUP), lambda i, c=_B256[name]: (i, c))
    c128 = lambda name: pl.BlockSpec((tm, LANE), lambda i, c=_B128[name]: (i, c))
    const = lambda shape: pl.BlockSpec(shape, lambda i: (0, 0))
    rows = lambda w: pl.BlockSpec((tm, w), lambda i: (i, 0))
    colsT = lambda h: pl.BlockSpec((h, tm), lambda i: (0, i))
    sds = jax.ShapeDtypeStruct
    return pl.pallas_call(
        _prep_kernel,
        grid=(T // tm,),
        in_specs=[c256("nsa_q"), c256("d_q"), c256("d_k"), c256("d_v"), c128("kvslc"), c128("kvwin"),
                  const((8, GROUP)), const((GROUP, GROUP)), const((GROUP, GROUP)), const((LANE, LANE))],
        out_specs=[colsT(GROUP), colsT(GROUP), rows(GROUP), colsT(GROUP),
                   rows(LANE), colsT(HD), rows(LANE), colsT(HD)],
        out_shape=[sds((GROUP, T), BF16), sds((GROUP, T), BF16), sds((T, GROUP), BF16), sds((GROUP, T), BF16),
                   sds((T, LANE), BF16), sds((HD, T), BF16), sds((T, LANE), BF16), sds((HD, T), BF16)],
        compiler_params=_params("parallel"),
        name="attn_prep",
    )(u, u, u, u, u, u, gains, _block_ones(GROUP, HD), _block_ones(GROUP, DIFF_QK), _block_ones(LANE, HD))


def _compress_kernel(xk_ref, xv_ref, pos_ref, w1_ref, w2_ref, gain_ref, ones64h_ref, kc_ref, vc_ref):
    half = NSA_CMP_STRIDE * HD
    n_rows = xk_ref.shape[0]

    def compress(x, j):
        top = (x + pos_ref[j, 0:1, :]).astype(BF16)
        bot = (x + pos_ref[j, 1:2, :]).astype(BF16)
        a = _dot(top, w1_ref[j, 0:half, :])
        b = _dot(bot, w1_ref[j, half:2 * half, :])
        hidden = a + pltpu.roll(b, n_rows - 1, 0)
        return _dot(_silu(hidden).astype(BF16), w2_ref[j])

    kc = compress(xk_ref[...], 0)
    kc_ref[...] = _group_rms(kc, ones64h_ref[...], HD) * gain_ref[...]
    vc_ref[...] = compress(xv_ref[...], 1).T[0:HD, :].astype(BF16)


def _compress(xk, xv, pos, w1_bf, w2p_bf, gain_row):
    B, ncp, half = xk.shape
    sq = pl.Squeezed()
    full = lambda shape: pl.BlockSpec(shape, lambda b: (0,) * len(shape))
    return pl.pallas_call(
        _compress_kernel,
        grid=(B,),
        in_specs=[pl.BlockSpec((sq, ncp, half), lambda b: (b, 0, 0)),
                  pl.BlockSpec((sq, ncp, half), lambda b: (b, 0, 0)),
                  full((2, 2, half)), full((2, 2 * half, NSA_CMP_HIDDEN)), full((2, NSA_CMP_HIDDEN, LANE)),
                  full((1, LANE)), full((LANE, LANE))],
        out_specs=[pl.BlockSpec((sq, ncp, LANE), lambda b: (b, 0, 0)),
                   pl.BlockSpec((sq, HD, ncp), lambda b: (b, 0, 0))],
        out_shape=[jax.ShapeDtypeStruct((B, ncp, LANE), F32), jax.ShapeDtypeStruct((B, HD, ncp), BF16)],
        compiler_params=_params("parallel"),
        name="nsa_compress",
    )(xk, xv, pos, w1_bf, w2p_bf, gain_row, _block_ones(LANE, HD))


def _softmax_step_t(s, v_t, m_old, l_old, acc_ref):
    m_new = jnp.maximum(m_old, jnp.max(s, axis=0, keepdims=True))
    alpha = jnp.exp2(m_old - m_new)
    p = jnp.exp2(s - m_new)
    acc_ref[...] = alpha * acc_ref[...] + _dot(v_t, p.astype(BF16))
    return m_new, alpha * l_old + jnp.sum(p, axis=0, keepdims=True)


def _pipelined_sweep(lo, hi, qk, process, process_last, stats0):
    def body(j, carry):
        stats, scores = carry
        nxt = qk(j + 1)
        return process(j, scores, stats), nxt

    stats, scores = lax.fori_loop(lo, hi - 1, body, (stats0, qk(lo)))
    return process_last(hi - 1, scores, stats)


def _nsa_kernel(qt_ref, g_ref, kc_ref, vct_ref, kvs_ref, vst_ref, kvw_ref, vwt_ref, ovl_ref, exp_ref,
                o_ref, qs_ref, *acc_refs, tq, tk, tw, ksel):
    i = pl.program_id(1)
    t0 = i * tq
    ncp = kc_ref.shape[0]
    nsel = ovl_ref.shape[0]
    acc_s, acc_w = acc_refs[:HEADS], acc_refs[HEADS:]

    qt = qt_ref[...]
    qs_ref[HD:LANE, :] = jnp.zeros((LANE - HD, HEADS * tq), BF16)
    for h in range(HEADS):
        qs_ref[0:HD, h * tq:(h + 1) * tq] = qt[h * HD:(h + 1) * HD, :]
        acc_s[h][...] = jnp.zeros((HD, tq), F32)
        acc_w[h][...] = jnp.zeros((HD, tq), F32)

    kc_hi, kc_lo = _split2(kc_ref[...])
    n_idx = lax.broadcasted_iota(jnp.int32, (ncp, tq), 0)
    t_lane = t0 + lax.broadcasted_iota(jnp.int32, (ncp, tq), 1)
    ok = n_idx * NSA_CMP_STRIDE + (NSA_CMP_LEN - 1) <= t_lane
    cmp_scores = _dot(kc_hi, qs_ref[...]) + _dot(kc_lo, qs_ref[...])
    o_c = []
    psum = jnp.zeros((ncp, tq), F32)
    for h in range(HEADS):
        s = jnp.where(ok, cmp_scores[:, h * tq:(h + 1) * tq], NEG)
        e = jnp.exp2(s - jnp.max(s, axis=0, keepdims=True))
        p = jnp.where(ok, e / jnp.sum(e, axis=0, keepdims=True), 0.0)
        o_c.append(_dot(vct_ref[...], p.astype(BF16)))
        psum = psum + p

    p_hi, p_lo = _split2(psum)
    imp = _dot(ovl_ref[...], p_hi) + _dot(ovl_ref[...], p_lo)
    blk = lax.broadcasted_iota(jnp.int32, (nsel, tq), 0)
    t_col = t0 + lax.broadcasted_iota(jnp.int32, (nsel, tq), 1)
    cur = t_col // NSA_SEL_LEN
    forced = (blk == 0) | (blk == cur) | (blk == cur - 1)
    vals = jnp.where(forced, BIG, jnp.where(blk * NSA_SEL_LEN <= t_col, imp, NEG))
    blk_f = blk.astype(F32)
    sel = jnp.zeros((nsel, tq), F32)
    for _ in range(ksel):
        mx = jnp.max(vals, axis=0, keepdims=True)
        first = jnp.min(jnp.where(vals == mx, blk_f, float(nsel)), axis=0, keepdims=True)
        pick = blk_f == first
        sel = jnp.where(pick, 1.0, sel)
        vals = jnp.where(pick, LOWEST, vals)
    if nsel < LANE:
        sel = jnp.concatenate([sel, jnp.zeros((LANE - nsel, tq), F32)], axis=0)
    sel_bf = sel.astype(BF16)

    stats0 = tuple((jnp.full((1, tq), NEG, F32), jnp.zeros((1, tq), F32)) for _ in range(HEADS))

    def sweep(lo, hi, k_ref, vt_ref, accs, width, bias_fn):
        def body(kt, stats):
            k0 = pl.multiple_of(kt * width, width)
            scores = _dot(k_ref[pl.ds(k0, width), :], qs_ref[...])
            kpos = k0 + lax.broadcasted_iota(jnp.int32, (width, tq), 0)
            tpos = t0 + lax.broadcasted_iota(jnp.int32, (width, tq), 1)
            bias = bias_fn(k0, kpos, tpos)
            vt = vt_ref[:, pl.ds(k0, width)]
            return tuple(_softmax_step_t(scores[:, h * tq:(h + 1) * tq] + bias, vt,
                                         stats[h][0], stats[h][1], accs[h]) for h in range(HEADS))

        return lax.fori_loop(lo, hi, body, stats0)

    def sel_bias(k0, kpos, tpos):
        chosen = _dot(exp_ref[pl.ds(k0, tk), :], sel_bf)
        return jnp.where((chosen > 0.5) & (kpos <= tpos), 0.0, NEG)

    st_s = sweep(0, (t0 + tq - 1) // tk + 1, kvs_ref, vst_ref, acc_s, tk, sel_bias)

    def win_bias(k0, kpos, tpos):
        return jnp.where((kpos <= tpos) & (kpos > tpos - NSA_WINDOW), 0.0, NEG)

    st_w = sweep(jnp.maximum(t0 - NSA_WINDOW, 0) // tw, (t0 + tq - 1) // tw + 1,
                 kvw_ref, vwt_ref, acc_w, tw, win_bias)

    gates = _sigmoid(g_ref[...].T)
    outs = []
    for h in range(HEADS):
        outs.append(gates[3 * h:3 * h + 1, :] * o_c[h]
                    + gates[3 * h + 1:3 * h + 2, :] * (acc_s[h][...] / st_s[h][1])
                    + gates[3 * h + 2:3 * h + 3, :] * (acc_w[h][...] / st_w[h][1]))
    o_ref[...] = jnp.concatenate(outs, axis=0).T


def _nsa_attention(u, nqt, kc, vct, kvs, vst, kvw, vwt, B, S, tq, tk, tw):
    nq = S // tq
    ncp = S // NSA_CMP_STRIDE
    nsel = S // NSA_SEL_LEN
    ksel = min(NSA_TOPK, nsel)
    n_cmp = (S - NSA_CMP_LEN) // NSA_CMP_STRIDE + 1
    cmp_start = np.arange(ncp) * NSA_CMP_STRIDE
    sel_start = np.arange(nsel) * NSA_SEL_LEN
    overlap = ((cmp_start[:, None] <= sel_start[None, :] + NSA_SEL_LEN - 1)
               & (cmp_start[:, None] + NSA_CMP_LEN - 1 >= sel_start[None, :])
               & (np.arange(ncp)[:, None] < n_cmp))
    ovl_t = jnp.asarray(overlap.T.astype(np.float32), BF16)
    expand = (jnp.arange(S, dtype=jnp.int32)[:, None] // NSA_SEL_LEN
              == jnp.arange(LANE, dtype=jnp.int32)[None, :]).astype(BF16)
    sq = pl.Squeezed()
    seq_rows = pl.BlockSpec((S, LANE), lambda b, i: (b, 0))
    seq_cols = pl.BlockSpec((HD, S), lambda b, i: (0, b))
    kern = functools.partial(_nsa_kernel, tq=tq, tk=tk, tw=tw, ksel=ksel)
    return pl.pallas_call(
        kern,
        grid=(B, nq),
        in_specs=[pl.BlockSpec((GROUP, tq), lambda b, i: (0, b * nq + i)),
                  pl.BlockSpec((tq, LANE), lambda b, i, c=_B128["nsa_g"]: (b * nq + i, c)),
                  pl.BlockSpec((sq, ncp, LANE), lambda b, i: (b, 0, 0)),
                  pl.BlockSpec((sq, HD, ncp), lambda b, i: (b, 0, 0)),
                  seq_rows, seq_cols, seq_rows, seq_cols,
                  pl.BlockSpec((nsel, ncp), lambda b, i: (0, 0)),
                  pl.BlockSpec((S, LANE), lambda b, i: (0, 0))],
        out_specs=pl.BlockSpec((tq, GROUP), lambda b, i: (b * nq + i, 0)),
        out_shape=jax.ShapeDtypeStruct((B * S, GROUP), F32),
        scratch_shapes=[pltpu.VMEM((LANE, HEADS * tq), BF16)] + [pltpu.VMEM((HD, tq), F32)] * (2 * HEADS),
        compiler_params=_params("parallel", "arbitrary"),
        name="nsa_attention",
    )(nqt, u, kc, vct, kvs, vst, kvw, vwt, ovl_t, expand)


def _diff_kernel(lam_ref, qt_ref, k_ref, vt_ref, gain_ref, o_ref, qs_ref, *acc_refs, tq, tk, lam_init):
    i = pl.program_id(1)
    t0 = i * tq
    lanes = 2 * tq

    qt = qt_ref[...]
    row = lax.broadcasted_iota(jnp.int32, (GROUP, tq), 0)
    zero = jnp.zeros_like(qt)
    for h in range(HEADS):
        qs_ref[h] = jnp.concatenate([jnp.where(row // DIFF_QK == 2 * h, qt, zero),
                                     jnp.where(row // DIFF_QK == 2 * h + 1, qt, zero)], axis=1)
        acc_refs[h][...] = jnp.zeros((HD, lanes), F32)

    def step(k0, width, masked, stats):
        k = k_ref[pl.ds(k0, width), :]
        scores = [_dot(k, qs_ref[h]) for h in range(HEADS)]
        out = []
        for h in range(HEADS):
            s = scores[h]
            if masked:
                kpos = k0 + lax.broadcasted_iota(jnp.int32, (width, lanes), 0)
                tpos = t0 + lax.broadcasted_iota(jnp.int32, (width, lanes), 1) % tq
                s = jnp.where(kpos <= tpos, s, NEG)
            out.append(_softmax_step_t(s, vt_ref[h * HD:(h + 1) * HD, pl.ds(k0, width)],
                                       stats[h][0], stats[h][1], acc_refs[h]))
        return tuple(out)

    stats0 = tuple((jnp.full((1, lanes), NEG, F32), jnp.zeros((1, lanes), F32)) for _ in range(HEADS))
    n_wide = t0 // tk
    stats = lax.fori_loop(0, n_wide, lambda kt, c: step(pl.multiple_of(kt * tk, tk), tk, False, c), stats0)
    stats = lax.fori_loop(n_wide * (tk // tq), t0 // tq,
                          lambda kt, c: step(pl.multiple_of(kt * tq, tq), tq, False, c), stats)
    stats = step(pl.multiple_of(t0, tq), tq, True, stats)

    lam = lam_ref[...]
    lam_full = (jnp.exp(jnp.sum(lam[0:1] * lam[1:2], axis=1, keepdims=True))
                - jnp.exp(jnp.sum(lam[2:3] * lam[3:4], axis=1, keepdims=True)) + lam_init)
    outs = []
    for h in range(HEADS):
        r = acc_refs[h][...] / stats[h][1]
        d = r[:, 0:tq] - lam_full * r[:, tq:lanes]
        d = d * lax.rsqrt(jnp.mean(d * d, axis=0, keepdims=True) + EPS)
        outs.append(d * gain_ref[...] * (1.0 - lam_init))
    o_ref[...] = jnp.concatenate(outs, axis=0).T


def _diff_attention(dqt, dk, dvt, lam, gain_col, B, S, tq, tk, layer_idx):
    nq = S // tq
    lam_init = 0.8 - 0.6 * math.exp(-0.3 * layer_idx)
    kern = functools.partial(_diff_kernel, tq=tq, tk=tk, lam_init=lam_init)
    return pl.pallas_call(
        kern,
        grid=(B, nq),
        in_specs=[pl.BlockSpec((4, DIFF_QK), lambda b, i: (0, 0)),
                  pl.BlockSpec((GROUP, tq), lambda b, i: (0, b * nq + i)),
                  pl.BlockSpec((S, GROUP), lambda b, i: (b, 0)),
                  pl.BlockSpec((GROUP, S), lambda b, i: (0, b)),
                  pl.BlockSpec((HD, tq), lambda b, i: (0, 0))],
        out_specs=pl.BlockSpec((tq, GROUP), lambda b, i: (b * nq + i, 0)),
        out_shape=jax.ShapeDtypeStruct((B * S, GROUP), F32),
        scratch_shapes=[pltpu.VMEM((HEADS, GROUP, 2 * tq), BF16)] + [pltpu.VMEM((HD, 2 * tq), F32)] * HEADS,
        compiler_params=_params("parallel", "arbitrary"),
        name="diff_attention",
    )(lam, dqt, dk, dvt, gain_col)


_LEVELS = (32, 16, 8, 4, 2, 1)


def _stack_heads(x, lane, group, count):
    zero = jnp.zeros_like(x)
    return jnp.concatenate([jnp.where(lane // group == g, x, zero) for g in range(count)], axis=0)


def _unstack_heads(x4, lane, rows):
    out = jnp.where(lane // HD == 0, x4[0:rows], 0.0)
    for h in range(1, HEADS):
        out = out + jnp.where(lane // HD == h, x4[h * rows:(h + 1) * rows], 0.0)
    return out


def _linear_consts():
    r = np.arange(CHUNK)[:, None]
    t = np.arange(CHUNK)[None, :]
    mats = []
    for s in _LEVELS:
        mid = (r // (2 * s)) * 2 * s + s
        upper = (r % (2 * s)) >= s
        mats.append(np.where(upper, (t > mid) & (t <= r), (t > r) & (t <= mid)))
    mats.append(t <= r)
    mats.append(t > r)
    dall = np.concatenate(mats, axis=0).astype(np.float32)
    masks = [(r // (2 * s) == t // (2 * s)) for s in _LEVELS] + [r == t]
    mall = np.stack([np.tile(m.astype(np.float32), (HEADS, 1)) for m in masks])
    return jnp.asarray(dall, BF16), jnp.asarray(mall, F32)


def _linear_chunk(q, k, v, lg, dall, mall_ref, state_ref, dk):
    dkh = HEADS * dk
    hi, mid, lo = _split3(lg)
    ex = jnp.exp(_dot(dall, hi) + _dot(dall, mid) + _dot(dall, lo))
    row = lax.broadcasted_iota(jnp.int32, (CHUNK, dkh), 0)
    lane_k = lax.broadcasted_iota(jnp.int32, (CHUNK, dkh), 1)
    lane_v = lax.broadcasted_iota(jnp.int32, (CHUNK, GROUP), 1)
    k_bf = k.astype(BF16)
    a = mall_ref[len(_LEVELS)] * _dot_nt(_stack_heads(q.astype(BF16), lane_k, dk, HEADS), k_bf)
    for li, s in enumerate(_LEVELS):
        e = ex[li * CHUNK:(li + 1) * CHUNK]
        upper = (row & s) != 0
        qt = jnp.where(upper, q * e, 0.0).astype(BF16)
        kt = jnp.where(upper, 0.0, k * e).astype(BF16)
        a = a + mall_ref[li] * _dot_nt(_stack_heads(qt, lane_k, dk, HEADS), kt)
    o_intra = _unstack_heads(_dot(a.astype(BF16), v.astype(BF16)), lane_v, CHUNK)
    e_b = ex[6 * CHUNK:7 * CHUNK]
    e_u = ex[7 * CHUNK:8 * CHUNK]
    st = state_ref[...]
    o_inter = _dot_nt((q * e_b).astype(BF16), st.astype(BF16))
    kv = _dot_tn(v.astype(BF16), (k * e_u).astype(BF16))
    srow = lax.broadcasted_iota(jnp.int32, (GROUP, dkh), 0)
    scol = lax.broadcasted_iota(jnp.int32, (GROUP, dkh), 1)
    state_ref[...] = st * e_b[CHUNK - 1:CHUNK, :] + jnp.where(srow // HD == scol // dk, kv, 0.0)
    return o_inter + o_intra


def _gla_kernel(q_ref, k_ref, v_ref, lr_ref, og_ref, w2_ref, b_ref, gain_ref, dall_ref, mall_ref,
                ones64_ref, o_ref, state_ref, *, tm):
    @pl.when(pl.program_id(1) == 0)
    def _():
        state_ref[...] = jnp.zeros_like(state_ref)

    dall = dall_ref[...]
    ones64 = ones64_ref[...]
    w_hi, w_lo = _split2(w2_ref[...])

    def body(c, carry):
        r0 = pl.multiple_of(c * CHUNK, CHUNK)
        rs = pl.ds(r0, CHUNK)
        lr_hi, lr_lo = _split2(lr_ref[rs, :])
        x = _dot(lr_hi, w_hi) + _dot(lr_lo, w_hi) + _dot(lr_hi, w_lo) + b_ref[...]
        lg = (jnp.minimum(x, 0.0) - jnp.log(1.0 + jnp.exp(-jnp.abs(x)))) * (1.0 / GLA_TAU)
        o = _linear_chunk(q_ref[rs, :] * (GLA_DK ** -0.5), k_ref[rs, :], v_ref[rs, :], lg,
                          dall, mall_ref, state_ref, GLA_DK)
        o_ref[rs, :] = _group_rms(o, ones64, HD) * gain_ref[...] * _silu(og_ref[rs, :])
        return carry

    lax.fori_loop(0, tm // CHUNK, body, 0)


def _hgrn_kernel(q_ref, f_ref, i_ref, og_ref, lbl_ref, gain_ref, dall_ref, mall_ref, ones64_ref,
                 o_ref, state_ref, *, tm, layer_idx):
    @pl.when(pl.program_id(1) == 0)
    def _():
        state_ref[...] = jnp.zeros_like(state_ref)

    dall = dall_ref[...]
    ones64 = ones64_ref[...]
    logits = lbl_ref[...]
    ez = jnp.exp(logits - jnp.max(logits, axis=0, keepdims=True))
    probs = ez / jnp.sum(ez, axis=0, keepdims=True)
    lb = jnp.zeros((1, GROUP), F32)
    for j in range(1, layer_idx + 1):
        lb = lb + probs[j:j + 1]

    def body(c, carry):
        r0 = pl.multiple_of(c * CHUNK, CHUNK)
        rs = pl.ds(r0, CHUNK)
        z = f_ref[rs, :]
        f = lb + (1.0 - lb) * _sigmoid(z)
        k = (1.0 - lb) * _sigmoid(-z)
        o = _linear_chunk(q_ref[rs, :], k, i_ref[rs, :], jnp.log(f), dall, mall_ref, state_ref, HD)
        o_ref[rs, :] = _group_rms(o, ones64, HD) * gain_ref[...] * _silu(og_ref[rs, :])
        return carry

    lax.fori_loop(0, tm // CHUNK, body, 0)


def _linear_mixers(u, w2pad, b_gate, gla_gain, lb_logits, hgrn_gain, B, S, tm, layer_idx):
    nt = S // tm
    dall, mall = _linear_consts()
    ones64 = _block_ones(GROUP, HD)
    c256 = lambda name: pl.BlockSpec((tm, GROUP), lambda b, i, c=_B256[name]: (b * nt + i, c))
    c128 = lambda name: pl.BlockSpec((tm, LANE), lambda b, i, c=_B128[name]: (b * nt + i, c))
    full = lambda shape: pl.BlockSpec(shape, lambda b, i: (0,) * len(shape))
    out_spec = pl.BlockSpec((tm, GROUP), lambda b, i: (b * nt + i, 0))
    out_shape = jax.ShapeDtypeStruct((B * S, GROUP), F32)
    consts = [full(dall.shape), full(mall.shape), full((GROUP, GROUP))]
    o_gla = pl.pallas_call(
        functools.partial(_gla_kernel, tm=tm),
        grid=(B, nt),
        in_specs=[c128("g_q"), c128("g_k"), c256("g_v"), c128("g_lr"), c256("g_og"),
                  full((LANE, LANE)), full((1, LANE)), full((1, GROUP))] + consts,
        out_specs=out_spec, out_shape=out_shape,
        scratch_shapes=[pltpu.VMEM((GROUP, HEADS * GLA_DK), F32)],
        compiler_params=_params("parallel", "arbitrary"),
        name="gla_mixer",
    )(u, u, u, u, u, w2pad, b_gate, gla_gain, dall, mall, ones64)
    depth = lb_logits.shape[0]
    o_hgrn = pl.pallas_call(
        functools.partial(_hgrn_kernel, tm=tm, layer_idx=layer_idx),
        grid=(B, nt),
        in_specs=[c256("r_q"), c256("r_f"), c256("r_i"), c256("r_og"),
                  full((depth, GROUP)), full((1, GROUP))] + consts,
        out_specs=out_spec, out_shape=out_shape,
        scratch_shapes=[pltpu.VMEM((GROUP, GROUP), F32)],
        compiler_params=_params("parallel", "arbitrary"),
        name="hgrn_mixer",
    )(u, u, u, u, lb_logits, hgrn_gain, dall, mall, ones64)
    return o_gla, o_hgrn


def _out_proj_kernel(h_ref, a_ref, b_ref, c_ref, d_ref, w_ref, o_ref):
    acc = h_ref[...]
    for j, r in enumerate((a_ref, b_ref, c_ref, d_ref)):
        acc = acc + _dot(r[...].astype(BF16), w_ref[j * GROUP:(j + 1) * GROUP, :])
    o_ref[...] = acc


def _out_proj(h, parts, w_bf, tm):
    T = h.shape[0]
    part = pl.BlockSpec((tm, GROUP), lambda i: (i, 0))
    return pl.pallas_call(
        _out_proj_kernel,
        grid=(T // tm,),
        in_specs=[pl.BlockSpec((tm, D_MODEL), lambda i: (i, 0)), part, part, part, part,
                  pl.BlockSpec((D_MODEL, D_MODEL), lambda i: (0, 0))],
        out_specs=pl.BlockSpec((tm, D_MODEL), lambda i: (i, 0)),
        out_shape=jax.ShapeDtypeStruct((T, D_MODEL), F32),
        compiler_params=_params("parallel"),
        name="out_proj",
    )(h, *parts, w_bf)


def _ffn_kernel(h_ref, g_ref, wg_ref, wu_ref, wd_ref, o_ref, xn_ref, acc_ref):
    f = pl.program_id(1)

    @pl.when(f == 0)
    def _():
        x = h_ref[...]
        y = x * lax.rsqrt(jnp.mean(x * x, axis=-1, keepdims=True) + EPS) * g_ref[...]
        xn_ref[...] = y.astype(BF16)
        acc_ref[...] = x

    xn = xn_ref[...]
    mid = _silu(_dot(xn, wg_ref[...])) * _dot(xn, wu_ref[...])
    acc_ref[...] += _dot(mid.astype(BF16), wd_ref[...])

    @pl.when(f == pl.num_programs(1) - 1)
    def _():
        o_ref[...] = acc_ref[...]


def _ffn(h, gain, wg, wu, wd, tm, tf):
    T = h.shape[0]
    F = wg.shape[1]
    return pl.pallas_call(
        _ffn_kernel,
        grid=(T // tm, F // tf),
        in_specs=[pl.BlockSpec((tm, D_MODEL), lambda i, f: (i, 0)),
                  pl.BlockSpec((1, D_MODEL), lambda i, f: (0, 0)),
                  pl.BlockSpec((D_MODEL, tf), lambda i, f: (0, f)),
                  pl.BlockSpec((D_MODEL, tf), lambda i, f: (0, f)),
                  pl.BlockSpec((tf, D_MODEL), lambda i, f: (f, 0))],
        out_specs=pl.BlockSpec((tm, D_MODEL), lambda i, f: (i, 0)),
        out_shape=jax.ShapeDtypeStruct((T, D_MODEL), F32),
        scratch_shapes=[pltpu.VMEM((tm, D_MODEL), BF16), pltpu.VMEM((tm, D_MODEL), F32)],
        compiler_params=_params("parallel", "arbitrary"),
        name="ffn_swiglu",
    )(h, gain.reshape(1, D_MODEL), wg, wu, wd)


def _router_kernel(h_ref, g_ref, r_ref, c_ref, comb_ref, sel_ref):
    x = h_ref[...]
    y = x * lax.rsqrt(jnp.mean(x * x, axis=-1, keepdims=True) + EPS) * g_ref[...]
    c_ref[...] = y.astype(BF16)
    y_hi, y_lo = _split2(y)
    r_hi, r_lo = _split2(r_ref[...])
    logits = _dot(y_hi, r_hi) + _dot(y_lo, r_hi) + _dot(y_hi, r_lo)
    lane = lax.broadcasted_iota(jnp.int32, logits.shape, 1)
    lane_f = lane.astype(F32)
    logits = jnp.where(lane < N_EXPERTS, logits, LOWEST)
    m1 = jnp.max(logits, axis=1, keepdims=True)
    i1 = jnp.min(jnp.where(logits == m1, lane_f, float(LANE)), axis=1, keepdims=True)
    rest = jnp.where(lane_f == i1, LOWEST, logits)
    m2 = jnp.max(rest, axis=1, keepdims=True)
    i2 = jnp.min(jnp.where(rest == m2, lane_f, float(LANE)), axis=1, keepdims=True)
    e2 = jnp.exp(m2 - m1)
    w1 = 1.0 / (1.0 + e2)
    comb_ref[...] = jnp.where(lane_f == i1, w1, jnp.where(lane_f == i2, e2 * w1, 0.0))
    sel_ref[...] = ((lane_f == i1) | (lane_f == i2)).astype(jnp.int32)


def _router(h, gain, router_pad, tm):
    T = h.shape[0]
    return pl.pallas_call(
        _router_kernel,
        grid=(T // tm,),
        in_specs=[pl.BlockSpec((tm, D_MODEL), lambda i: (i, 0)),
                  pl.BlockSpec((1, D_MODEL), lambda i: (0, 0)),
                  pl.BlockSpec((D_MODEL, LANE), lambda i: (0, 0))],
        out_specs=[pl.BlockSpec((tm, D_MODEL), lambda i: (i, 0)), pl.BlockSpec((tm, LANE), lambda i: (i, 0)),
                   pl.BlockSpec((tm, LANE), lambda i: (i, 0))],
        out_shape=[jax.ShapeDtypeStruct((T, D_MODEL), BF16), jax.ShapeDtypeStruct((T, LANE), F32),
                   jax.ShapeDtypeStruct((T, LANE), jnp.int32)],
        compiler_params=_params("parallel"),
        name="moe_router",
    )(h, gain.reshape(1, D_MODEL), router_pad)


MOE_CHUNK = 512
MOE_ROWS = 512
MOE_HALF = MOE_ROWS // 2
MOE_ALIGN = 16


def _moe_plan(sel, T):
    nch = T // MOE_CHUNK
    selc = sel[:, :N_EXPERTS].reshape(nch, MOE_CHUNK, N_EXPERTS)
    rank = jnp.cumsum(selc, axis=1) - selc
    n_ce = jnp.sum(selc, axis=1)
    cap = (n_ce + MOE_ALIGN - 1) // MOE_ALIGN * MOE_ALIGN
    tot = jnp.sum(cap, axis=0)
    ptot = (tot + MOE_ROWS - 1) // MOE_ROWS * MOE_ROWS
    start = jnp.cumsum(ptot) - ptot
    lo_ce = start[None, :] + jnp.cumsum(cap, axis=0) - cap
    lo = jnp.concatenate([lo_ce.T, (start + tot)[:, None]], axis=1)
    n = jnp.concatenate([n_ce.T, jnp.zeros((N_EXPERTS, 1), jnp.int32)], axis=1)
    rk = jnp.where(selc > 0, rank, -1).reshape(T, N_EXPERTS)
    n_tiles = _moe_tiles(T)
    tile_end = jnp.cumsum(ptot // MOE_ROWS)
    j = jnp.arange(n_tiles, dtype=jnp.int32)
    tile_e = jnp.minimum(jnp.sum((tile_end[None, :] <= j[:, None]).astype(jnp.int32), axis=1), N_EXPERTS - 1)
    valid = (j < tile_end[-1]).astype(jnp.int32)
    n_flat = jnp.concatenate([n.reshape(-1), tile_end[-1:]]).astype(jnp.int32)
    return lo.reshape(-1).astype(jnp.int32), n_flat, rk.astype(jnp.int32), tile_e, valid


def _moe_tiles(T):
    nch = T // MOE_CHUNK
    rows = 2 * T + nch * N_EXPERTS * (MOE_ALIGN - 1) + N_EXPERTS * (MOE_ROWS - 1)
    return -(-rows // MOE_ROWS) + 1


def _moe_gather_kernel(lo_ref, n_ref, c_ref, rkt_ref, x_hbm, buf, sem, *, nch):
    e = pl.program_id(0)
    c = pl.program_id(1)
    step = e * (nch + 1) + c
    slot = step % 2
    rk_row = rkt_ref[pl.ds(e, 1), :]
    rk_row = jnp.where(c < nch, rk_row, -1)
    row = lax.broadcasted_iota(jnp.int32, (MOE_HALF, MOE_CHUNK), 0)
    chunk = c_ref[...]
    onehot = jnp.where(rk_row == row, 1.0, 0.0).astype(BF16)
    buf[slot, 0:MOE_HALF, :] = _dot(onehot, chunk).astype(BF16)

    @pl.when(n_ref[step] > MOE_HALF)
    def _():
        onehot = jnp.where(rk_row == row + MOE_HALF, 1.0, 0.0).astype(BF16)
        buf[slot, MOE_HALF:MOE_ROWS, :] = _dot(onehot, chunk).astype(BF16)

    @pl.when(n_ref[step] <= MOE_HALF)
    def _():
        buf[slot, MOE_HALF:MOE_ROWS, :] = jnp.zeros((MOE_HALF, D_MODEL), BF16)

    def copy(k, s):
        dst = x_hbm.at[pl.ds(pl.multiple_of(lo_ref[k], MOE_ALIGN), MOE_ROWS)]
        return pltpu.make_async_copy(buf.at[s], dst, sem.at[s])

    @pl.when(step > 0)
    def _():
        copy(step - 1, 1 - slot).wait()

    copy(step, slot).start()

    n_steps = pl.num_programs(0) * pl.num_programs(1)

    @pl.when(step == n_steps - 1)
    def _():
        copy(step, slot).wait()
        buf[0] = jnp.zeros((MOE_ROWS, D_MODEL), BF16)

        def fill(k, carry):
            dst = x_hbm.at[pl.ds(pl.multiple_of(k * MOE_ROWS, MOE_ROWS), MOE_ROWS)]
            cp = pltpu.make_async_copy(buf.at[0], dst, sem.at[0])
            cp.start()
            cp.wait()
            return carry

        lax.fori_loop(n_ref[n_steps], x_hbm.shape[0] // MOE_ROWS, fill, 0)


def _moe_gather(c_bf, rkt, lo, n, n_tiles):
    T = c_bf.shape[0]
    nch = T // MOE_CHUNK
    last = nch - 1
    grid_spec = pltpu.PrefetchScalarGridSpec(
        num_scalar_prefetch=2,
        grid=(N_EXPERTS, nch + 1),
        in_specs=[pl.BlockSpec((MOE_CHUNK, D_MODEL), lambda e, c, lo, n: (jnp.minimum(c, last), 0)),
                  pl.BlockSpec((N_EXPERTS, MOE_CHUNK), lambda e, c, lo, n: (0, jnp.minimum(c, last)))],
        out_specs=pl.BlockSpec(memory_space=pl.ANY),
        scratch_shapes=[pltpu.VMEM((2, MOE_ROWS, D_MODEL), BF16), pltpu.SemaphoreType.DMA((2,))],
    )
    return pl.pallas_call(
        functools.partial(_moe_gather_kernel, nch=nch),
        grid_spec=grid_spec,
        out_shape=jax.ShapeDtypeStruct((n_tiles * MOE_ROWS, D_MODEL), BF16),
        compiler_params=_params("arbitrary", "arbitrary"),
        name="moe_gather",
    )(lo, n, c_bf, rkt)


def _moe_ffn_kernel(te_ref, valid_ref, x_ref, wg_ref, wu_ref, wd_ref, y_ref, acc_ref):
    j = pl.program_id(0)
    f = pl.program_id(1)
    last = pl.num_programs(1) - 1

    @pl.when(valid_ref[j] == 1)
    def _():
        x = x_ref[...]
        mid = _silu(_dot(x, wg_ref[...])) * _dot(x, wu_ref[...])
        part = _dot(mid.astype(BF16), wd_ref[...])

        @pl.when(f == 0)
        def _():
            acc_ref[...] = part

        @pl.when(f > 0)
        def _():
            acc_ref[...] += part

        @pl.when(f == last)
        def _():
            y_ref[...] = acc_ref[...].astype(BF16)

    @pl.when((valid_ref[j] == 0) & (f == last))
    def _():
        y_ref[...] = jnp.zeros(y_ref.shape, BF16)


def _moe_ffn(x_sorted, tile_e, valid, wg, wu, wd, tf):
    n_tiles = x_sorted.shape[0] // MOE_ROWS
    F = wg.shape[2]
    nf = F // tf
    sq = pl.Squeezed()
    fsel = lambda j, f, te, va: jnp.where(va[j] == 1, f, nf - 1)
    grid_spec = pltpu.PrefetchScalarGridSpec(
        num_scalar_prefetch=2,
        grid=(n_tiles, nf),
        in_specs=[pl.BlockSpec((MOE_ROWS, D_MODEL), lambda j, f, te, va: (j, 0)),
                  pl.BlockSpec((sq, D_MODEL, tf), lambda j, f, te, va: (te[j], 0, fsel(j, f, te, va))),
                  pl.BlockSpec((sq, D_MODEL, tf), lambda j, f, te, va: (te[j], 0, fsel(j, f, te, va))),
                  pl.BlockSpec((sq, tf, D_MODEL), lambda j, f, te, va: (te[j], fsel(j, f, te, va), 0))],
        out_specs=pl.BlockSpec((MOE_ROWS, D_MODEL), lambda j, f, te, va: (j, 0)),
        scratch_shapes=[pltpu.VMEM((MOE_ROWS, D_MODEL), F32)],
    )
    return pl.pallas_call(
        _moe_ffn_kernel,
        grid_spec=grid_spec,
        out_shape=jax.ShapeDtypeStruct((n_tiles * MOE_ROWS, D_MODEL), BF16),
        compiler_params=_params("parallel", "arbitrary"),
        name="moe_experts",
    )(tile_e, valid, x_sorted, wg, wu, wd)


def _moe_combine_kernel(lo_ref, n_ref, h_ref, rk_ref, comb_ref, y_hbm, o_ref, ybuf, sem, *, nch):
    c = pl.program_id(0)

    def copy(e):
        src = y_hbm.at[pl.ds(pl.multiple_of(lo_ref[e * (nch + 1) + c], MOE_ALIGN), MOE_ROWS)]
        return pltpu.make_async_copy(src, ybuf.at[e], sem.at[e])

    for e in range(N_EXPERTS):
        copy(e).start()
    o_ref[...] = h_ref[...]
    rk = rk_ref[...]
    comb = comb_ref[...]
    lane = lax.broadcasted_iota(jnp.int32, (MOE_CHUNK, MOE_HALF), 1)
    for e in range(N_EXPERTS):
        copy(e).wait()
        rank_col = rk[:, e:e + 1]
        w_col = comb[:, e:e + 1]
        onehot = jnp.where(rank_col == lane, 1.0, 0.0).astype(BF16)
        o_ref[...] += w_col * _dot(onehot, ybuf[e, 0:MOE_HALF, :])

        @pl.when(n_ref[e * (nch + 1) + c] > MOE_HALF)
        def _():
            onehot = jnp.where(rank_col == lane + MOE_HALF, 1.0, 0.0).astype(BF16)
            o_ref[...] += w_col * _dot(onehot, ybuf[e, MOE_HALF:MOE_ROWS, :])


def _moe_combine(h, rk_pad, comb, y_sorted, lo, n):
    T = h.shape[0]
    nch = T // MOE_CHUNK
    grid_spec = pltpu.PrefetchScalarGridSpec(
        num_scalar_prefetch=2,
        grid=(nch,),
        in_specs=[pl.BlockSpec((MOE_CHUNK, D_MODEL), lambda c, lo, n: (c, 0)),
                  pl.BlockSpec((MOE_CHUNK, LANE), lambda c, lo, n: (c, 0)),
                  pl.BlockSpec((MOE_CHUNK, LANE), lambda c, lo, n: (c, 0)),
                  pl.BlockSpec(memory_space=pl.ANY)],
        out_specs=pl.BlockSpec((MOE_CHUNK, D_MODEL), lambda c, lo, n: (c, 0)),
        scratch_shapes=[pltpu.VMEM((N_EXPERTS, MOE_ROWS, D_MODEL), BF16), pltpu.SemaphoreType.DMA((N_EXPERTS,))],
    )
    return pl.pallas_call(
        functools.partial(_moe_combine_kernel, nch=nch),
        grid_spec=grid_spec,
        out_shape=jax.ShapeDtypeStruct((T, D_MODEL), F32),
        compiler_params=_params("arbitrary"),
        name="moe_combine",
    )(lo, n, h, rk_pad, comb, y_sorted)


def _moe(h, c_bf, comb, sel, wg, wu, wd, tf):
    T = h.shape[0]
    lo, n, rk, tile_e, valid = _moe_plan(sel, T)
    x_sorted = _moe_gather(c_bf, rk.T, lo, n, _moe_tiles(T))
    y_sorted = _moe_ffn(x_sorted, tile_e, valid, wg, wu, wd, tf)
    rk_pad = jnp.pad(rk, ((0, 0), (0, LANE - N_EXPERTS)), constant_values=-1)
    return _moe_combine(h, rk_pad, comb, y_sorted, lo, n)


def _ple_kernel(h_ref, p_ref, g_ref, wg_ref, wp_ref, o_ref):
    x = h_ref[...]
    y = x * lax.rsqrt(jnp.mean(x * x, axis=-1, keepdims=True) + EPS) * g_ref[...]
    gate = _sigmoid(_dot(y.astype(BF16), wg_ref[...]))
    o_ref[...] = x + _dot(p_ref[...].astype(BF16), wp_ref[...]) * gate


def _ple(h, p, gain, wg, wp, tm):
    T = h.shape[0]
    return pl.pallas_call(
        _ple_kernel,
        grid=(T // tm,),
        in_specs=[pl.BlockSpec((tm, D_MODEL), lambda i: (i, 0)),
                  pl.BlockSpec((tm, PLE_DIM), lambda i: (i, 0)),
                  pl.BlockSpec((1, D_MODEL), lambda i: (0, 0)),
                  pl.BlockSpec((D_MODEL, D_MODEL), lambda i: (0, 0)),
                  pl.BlockSpec((PLE_DIM, D_MODEL), lambda i: (0, 0))],
        out_specs=pl.BlockSpec((tm, D_MODEL), lambda i: (i, 0)),
        out_shape=jax.ShapeDtypeStruct((T, D_MODEL), F32),
        compiler_params=_params("parallel"),
        name="ple_gate",
    )(h, p, gain.reshape(1, D_MODEL), wg, wp)


def _tiles(T, S):
    pick = lambda n, pref: max(t for t in pref if n % t == 0)
    return dict(
        proj_m=pick(T, (1024, 512, 256, 128)), proj_n=NC // 3,
        prep_m=pick(T, (512, 256, 128)),
        attn_q=256, attn_k=pick(S, (512, 256)), attn_w=256,
        lin_m=pick(S, (512, 256, 128, 64)),
        row_m=pick(T, (512, 256, 128)),
        ffn_m=pick(T, (1024, 512, 256, 128)), ffn_f=512,
    )


def kernel(x, p, norm_attn, w_in, w_out, nsa_cmp_pos, nsa_cmp_w1, nsa_cmp_w2, nsa_qk_gain, diff_qk_gain, diff_lambda, diff_norm, gla_w_gate2, gla_b_gate, gla_norm, hgrn_lb_logits, hgrn_norm, norm_ffn, ffn_w_gate, ffn_w_up, ffn_w_down, moe_router, moe_w_gate, moe_w_up, moe_w_down, ple_norm, ple_w_gate, ple_w_proj):
    B, S, _ = x.shape
    depth = w_in.shape[0]
    T = B * S
    t = _tiles(T, S)
    ncp = S // NSA_CMP_STRIDE
    half = NSA_CMP_STRIDE * HD
    cols = jnp.asarray(np.maximum(_COLS, 0), jnp.int32)
    col_mask = jnp.asarray(_COLS >= 0)
    ones_row = jnp.ones((GROUP,), F32)

    h = x.reshape(T, D_MODEL)
    for i in range(depth):
        w_in_r = jnp.where(col_mask[None, :], jnp.take(w_in[i], cols, axis=1), 0.0).astype(BF16)
        gains = jnp.stack([
            jnp.tile(nsa_qk_gain[i, 0], HEADS) * (HD ** -0.5 * LOG2E),
            jnp.tile(diff_qk_gain[i, 0], 2 * HEADS) * (DIFF_QK ** -0.5 * LOG2E),
            jnp.tile(diff_qk_gain[i, 1], 2 * HEADS),
            jnp.tile(nsa_qk_gain[i, 2], HEADS),
            jnp.tile(nsa_qk_gain[i, 3], HEADS),
            ones_row, ones_row, ones_row])
        pos = nsa_cmp_pos[i].reshape(2, 2, half)
        w1 = nsa_cmp_w1[i].astype(BF16)
        w2p = jnp.pad(nsa_cmp_w2[i], ((0, 0), (0, 0), (0, LANE - HD))).astype(BF16)
        kc_gain = jnp.pad(nsa_qk_gain[i, 1], (0, LANE - HD)).reshape(1, LANE)
        w2pad = jnp.zeros((LANE, LANE), F32).at[:GLA_RANK].set(gla_w_gate2[i])
        diff_gain_col = jnp.broadcast_to(diff_norm[i][:, None], (HD, t["attn_q"]))

        u = _norm_matmul(h, norm_attn[i], w_in_r, t["proj_m"], t["proj_n"])
        nqt, dqt, dk, dvt, kvs, vst, kvw, vwt = _prep(u, gains, t["prep_m"])
        kv = u[:, _B128["kvcmp"] * LANE:(_B128["kvcmp"] + 1) * LANE]
        xk = kv[:, :HD].reshape(B, ncp, half)
        xv = kv[:, HD:].reshape(B, ncp, half)
        kc, vct = _compress(xk, xv, pos, w1, w2p, kc_gain)
        o_a = _nsa_attention(u, nqt, kc, vct, kvs, vst, kvw, vwt, B, S, t["attn_q"], t["attn_k"], t["attn_w"])
        o_b = _diff_attention(dqt, dk, dvt, diff_lambda[i], diff_gain_col, B, S, t["attn_q"], t["attn_k"], i)
        o_c, o_d = _linear_mixers(u, w2pad, gla_b_gate[i].reshape(1, LANE),
                                  jnp.tile(gla_norm[i], HEADS).reshape(1, GROUP), hgrn_lb_logits,
                                  jnp.tile(hgrn_norm[i], HEADS).reshape(1, GROUP), B, S, t["lin_m"], i)
        h = _out_proj(h, (o_a, o_b, o_c, o_d), w_out[i].astype(BF16), t["row_m"])
        if i % 2 == 0:
            j = i // 2
            h = _ffn(h, norm_ffn[i], ffn_w_gate[j].astype(BF16), ffn_w_up[j].astype(BF16),
                     ffn_w_down[j].astype(BF16), t["ffn_m"], t["ffn_f"])
        else:
            j = i // 2
            router_pad = jnp.zeros((D_MODEL, LANE), F32).at[:, :N_EXPERTS].set(moe_router[j])
            c_bf, comb, sel = _router(h, norm_ffn[i], router_pad, t["row_m"])
            h = _moe(h, c_bf, comb, sel, moe_w_gate[j].astype(BF16), moe_w_up[j].astype(BF16),
                     moe_w_down[j].astype(BF16), t["ffn_f"])
        h = _ple(h, p[i].reshape(T, PLE_DIM), ple_norm[i], ple_w_gate[i].astype(BF16),
                 ple_w_proj[i].astype(BF16), t["row_m"])
    return h.reshape(B, S, D_MODEL)
```

```python
import functools
import math

import numpy as np
import jax
import jax.numpy as jnp
from jax import lax
from jax.experimental import pallas as pl
from jax.experimental.pallas import tpu as pltpu

F32 = jnp.float32
BF16 = jnp.bfloat16

D_MODEL = 1024
HEADS = 4
HD = 64
GROUP = HEADS * HD
NSA_CMP_LEN = 32
NSA_CMP_STRIDE = 16
NSA_CMP_HIDDEN = 4 * HD
NSA_SEL_LEN = 64
NSA_TOPK = 16
NSA_WINDOW = 512
NSA_SUBTILES = 2
DIFF_QK = HD // 2
GLA_DK = HD // 2
GLA_RANK = 16
GLA_TAU = 16.0
CHUNK = 64
D_FF = 7 * D_MODEL // 2
N_EXPERTS = 8
PLE_DIM = 256
EPS = 1e-6
NEG = -1e30
BIG = 1e30
LOWEST = -3.0e38
LOG2E = 1.4426950408889634

VMEM_LIMIT = 52 * 1024 * 1024
LANE = 128

_SRC = dict(nsa_q=0, k_cmp=256, v_cmp=320, k_slc=384, v_slc=448, k_win=512, v_win=576, nsa_g=640,
            d_q=652, d_k=908, d_v=1164, g_q=1420, g_k=1548, g_v=1676, g_lr=1932, g_og=1948,
            r_q=2204, r_f=2460, r_i=2716, r_og=2972)
IN_COLS = 3228

_B256 = dict(nsa_q=0, d_q=1, d_k=2, d_v=3, g_v=4, g_og=5, r_q=6, r_f=7, r_i=8, r_og=9)
_B128 = dict(g_q=20, g_k=21, g_lr=22, kvcmp=23, kvslc=24, kvwin=25, nsa_g=26)
NC = 27 * 128


def _column_map():
    cols = -np.ones((NC,), np.int64)

    def put(dst, src, width):
        cols[dst:dst + width] = np.arange(src, src + width)

    for name in _B256:
        put(_B256[name] * 256, _SRC[name], 256)
    put(_B128["g_q"] * 128, _SRC["g_q"], 128)
    put(_B128["g_k"] * 128, _SRC["g_k"], 128)
    put(_B128["g_lr"] * 128, _SRC["g_lr"], GLA_RANK)
    put(_B128["kvcmp"] * 128, _SRC["k_cmp"], 128)
    put(_B128["kvslc"] * 128, _SRC["k_slc"], 128)
    put(_B128["kvwin"] * 128, _SRC["k_win"], 128)
    put(_B128["nsa_g"] * 128, _SRC["nsa_g"], 3 * HEADS)
    return cols


_COLS = _column_map()


def _dot(a, b):
    return jnp.dot(a, b, preferred_element_type=F32)


def _dot_nt(a, b):
    return lax.dot_general(a, b, (((1,), (1,)), ((), ())), preferred_element_type=F32)


def _dot_tn(a, b):
    return lax.dot_general(a, b, (((0,), (0,)), ((), ())), preferred_element_type=F32)


def _split2(x):
    hi = x.astype(BF16)
    lo = (x - hi.astype(F32)).astype(BF16)
    return hi, lo


def _split3(x):
    hi = x.astype(BF16)
    r = x - hi.astype(F32)
    mid = r.astype(BF16)
    lo = (r - mid.astype(F32)).astype(BF16)
    return hi, mid, lo


def _group_mean(x, ones_bf, group):
    hi, lo = _split2(x)
    return (_dot(hi, ones_bf) + _dot(lo, ones_bf)) * (1.0 / group)


def _group_rms(x, ones_bf, group):
    return x * lax.rsqrt(_group_mean(x * x, ones_bf, group) + EPS)


def _sigmoid(x):
    return 1.0 / (1.0 + jnp.exp(-x))


def _silu(x):
    return x * _sigmoid(x)


def _params(*sem):
    return pltpu.CompilerParams(dimension_semantics=sem, vmem_limit_bytes=VMEM_LIMIT)


def _block_ones(n, group):
    i = np.arange(n)
    return jnp.asarray((i[:, None] // group == i[None, :] // group).astype(np.float32), BF16)


def _norm_matmul_kernel(x_ref, g_ref, w_ref, o_ref, xn_ref):
    @pl.when(pl.program_id(1) == 0)
    def _():
        x = x_ref[...]
        y = x * lax.rsqrt(jnp.mean(x * x, axis=-1, keepdims=True) + EPS) * g_ref[...]
        xn_ref[...] = y.astype(BF16)

    o_ref[...] = _dot(xn_ref[...], w_ref[...])


def _norm_matmul(x, gain, w_bf, tm, tn):
    T, K = x.shape
    N = w_bf.shape[1]
    return pl.pallas_call(
        _norm_matmul_kernel,
        grid=(T // tm, N // tn),
        in_specs=[pl.BlockSpec((tm, K), lambda i, j: (i, 0)),
                  pl.BlockSpec((1, K), lambda i, j: (0, 0)),
                  pl.BlockSpec((K, tn), lambda i, j: (0, j))],
        out_specs=pl.BlockSpec((tm, tn), lambda i, j: (i, j)),
        out_shape=jax.ShapeDtypeStruct((T, N), F32),
        scratch_shapes=[pltpu.VMEM((tm, K), BF16)],
        compiler_params=_params("parallel", "arbitrary"),
        name="in_proj",
    )(x, gain.reshape(1, K), w_bf)


def _prep_kernel(nq_ref, dq_ref, dk_ref, dv_ref, kvs_ref, kvw_ref, gains_ref, ones64_ref, ones32_ref,
                 ones64h_ref, o_nq, o_dq, o_dk, o_dv, o_kvs, o_vs, o_kvw, o_vw):
    ones64 = ones64_ref[...]
    ones32 = ones32_ref[...]
    ones64h = ones64h_ref[...]
    o_nq[...] = (_group_rms(nq_ref[...], ones64, HD) * gains_ref[0:1, :]).T.astype(BF16)
    o_dq[...] = (_group_rms(dq_ref[...], ones32, DIFF_QK) * gains_ref[1:2, :]).T.astype(BF16)
    o_dk[...] = (_group_rms(dk_ref[...], ones32, DIFF_QK) * gains_ref[2:3, :]).astype(BF16)
    o_dv[...] = dv_ref[...].T.astype(BF16)
    lane = lax.broadcasted_iota(jnp.int32, kvs_ref.shape, 1)
    for kv_ref, gain, o_kv, o_v in ((kvs_ref, gains_ref[3:4, 0:LANE], o_kvs, o_vs),
                                    (kvw_ref, gains_ref[4:5, 0:LANE], o_kvw, o_vw)):
        x = kv_ref[...]
        o_kv[...] = jnp.where(lane < HD, _group_rms(x, ones64h, HD) * gain, x).astype(BF16)
        o_v[...] = x.T[HD:2 * HD, :].astype(BF16)


def _prep(u, gains, tm):
    T = u.shape[0]
    c256 = lambda name: pl.BlockSpec((tm, GROUP), lambda i, c=_B256[name]: (i, c))
    c128 = lambda name: pl.BlockSpec((tm, LANE), lambda i, c=_B128[name]: (i, c))
    const = lambda shape: pl.BlockSpec(shape, lambda i: (0, 0))
    rows = lambda w: pl.BlockSpec((tm, w), lambda i: (i, 0))
    colsT = lambda h: pl.BlockSpec((h, tm), lambda i: (0, i))
    sds = jax.ShapeDtypeStruct
    return pl.pallas_call(
        _prep_kernel,
        grid=(T // tm,),
        in_specs=[c256("nsa_q"), c256("d_q"), c256("d_k"), c256("d_v"), c128("kvslc"), c128("kvwin"),
                  const((8, GROUP)), const((GROUP, GROUP)), const((GROUP, GROUP)), const((LANE, LANE))],
        out_specs=[colsT(GROUP), colsT(GROUP), rows(GROUP), colsT(GROUP),
                   rows(LANE), colsT(HD), rows(LANE), colsT(HD)],
        out_shape=[sds((GROUP, T), BF16), sds((GROUP, T), BF16), sds((T, GROUP), BF16), sds((GROUP, T), BF16),
                   sds((T, LANE), BF16), sds((HD, T), BF16), sds((T, LANE), BF16), sds((HD, T), BF16)],
        compiler_params=_params("parallel"),
        name="attn_prep",
    )(u, u, u, u, u, u, gains, _block_ones(GROUP, HD), _block_ones(GROUP, DIFF_QK), _block_ones(LANE, HD))


def _compress_kernel(xk_ref, xv_ref, pos_ref, w1_ref, w2_ref, gain_ref, ones64h_ref, kc_ref, vc_ref):
    half = NSA_CMP_STRIDE * HD
    n_rows = xk_ref.shape[0]

    def compress(x, j):
        top = (x + pos_ref[j, 0:1, :]).astype(BF16)
        bot = (x + pos_ref[j, 1:2, :]).astype(BF16)
        a = _dot(top, w1_ref[j, 0:half, :])
        b = _dot(bot, w1_ref[j, half:2 * half, :])
        hidden = a + pltpu.roll(b, n_rows - 1, 0)
        return _dot(_silu(hidden).astype(BF16), w2_ref[j])

    kc = compress(xk_ref[...], 0)
    kc_ref[...] = _group_rms(kc, ones64h_ref[...], HD) * gain_ref[...]
    vc_ref[...] = compress(xv_ref[...], 1).T[0:HD, :].astype(BF16)


def _compress(xk, xv, pos, w1_bf, w2p_bf, gain_row):
    B, ncp, half = xk.shape
    sq = pl.Squeezed()
    full = lambda shape: pl.BlockSpec(shape, lambda b: (0,) * len(shape))
    return pl.pallas_call(
        _compress_kernel,
        grid=(B,),
        in_specs=[pl.BlockSpec((sq, ncp, half), lambda b: (b, 0, 0)),
                  pl.BlockSpec((sq, ncp, half), lambda b: (b, 0, 0)),
                  full((2, 2, half)), full((2, 2 * half, NSA_CMP_HIDDEN)), full((2, NSA_CMP_HIDDEN, LANE)),
                  full((1, LANE)), full((LANE, LANE))],
        out_specs=[pl.BlockSpec((sq, ncp, LANE), lambda b: (b, 0, 0)),
                   pl.BlockSpec((sq, HD, ncp), lambda b: (b, 0, 0))],
        out_shape=[jax.ShapeDtypeStruct((B, ncp, LANE), F32), jax.ShapeDtypeStruct((B, HD, ncp), BF16)],
        compiler_params=_params("parallel"),
        name="nsa_compress",
    )(xk, xv, pos, w1_bf, w2p_bf, gain_row, _block_ones(LANE, HD))


def _softmax_step_t(s, v_t, m_old, l_old, acc_ref):
    m_new = jnp.maximum(m_old, jnp.max(s, axis=0, keepdims=True))
    alpha = jnp.exp2(m_old - m_new)
    p = jnp.exp2(s - m_new)
    acc_ref[...] = alpha * acc_ref[...] + _dot(v_t, p.astype(BF16))
    return m_new, alpha * l_old + jnp.sum(p, axis=0, keepdims=True)


def _pipelined_sweep(lo, hi, qk, process, process_last, stats0):
    def body(j, carry):
        stats, scores = carry
        nxt = qk(j + 1)
        return process(j, scores, stats), nxt

    stats, scores = lax.fori_loop(lo, hi - 1, body, (stats0, qk(lo)))
    return process_last(hi - 1, scores, stats)


def _nsa_kernel(qt_ref, g_ref, kc_ref, vct_ref, kvs_ref, vst_ref, kvw_ref, vwt_ref, ovl_ref, exp_ref,
                o_ref, qs_ref, *acc_refs, tq, tk, tw, ksel):
    i = pl.program_id(1)
    t0 = i * tq
    ncp = kc_ref.shape[0]
    nsel = ovl_ref.shape[0]
    acc_s, acc_w = acc_refs[:HEADS], acc_refs[HEADS:]

    qt = qt_ref[...]
    qs_ref[HD:LANE, :] = jnp.zeros((LANE - HD, HEADS * tq), BF16)
    for h in range(HEADS):
        qs_ref[0:HD, h * tq:(h + 1) * tq] = qt[h * HD:(h + 1) * HD, :]
        acc_s[h][...] = jnp.zeros((HD, tq), F32)
        acc_w[h][...] = jnp.zeros((HD, tq), F32)

    kc_hi, kc_lo = _split2(kc_ref[...])
    n_idx = lax.broadcasted_iota(jnp.int32, (ncp, tq), 0)
    t_lane = t0 + lax.broadcasted_iota(jnp.int32, (ncp, tq), 1)
    ok = n_idx * NSA_CMP_STRIDE + (NSA_CMP_LEN - 1) <= t_lane
    cmp_scores = _dot(kc_hi, qs_ref[...]) + _dot(kc_lo, qs_ref[...])
    o_c = []
    psum = jnp.zeros((ncp, tq), F32)
    for h in range(HEADS):
        s = jnp.where(ok, cmp_scores[:, h * tq:(h + 1) * tq], NEG)
        e = jnp.exp2(s - jnp.max(s, axis=0, keepdims=True))
        p = jnp.where(ok, e / jnp.sum(e, axis=0, keepdims=True), 0.0)
        o_c.append(_dot(vct_ref[...], p.astype(BF16)))
        psum = psum + p

    p_hi, p_lo = _split2(psum)
    imp = _dot(ovl_ref[...], p_hi) + _dot(ovl_ref[...], p_lo)
    blk = lax.broadcasted_iota(jnp.int32, (nsel, tq), 0)
    t_col = t0 + lax.broadcasted_iota(jnp.int32, (nsel, tq), 1)
    cur = t_col // NSA_SEL_LEN
    forced = (blk == 0) | (blk == cur) | (blk == cur - 1)
    vals = jnp.where(forced, BIG, jnp.where(blk * NSA_SEL_LEN <= t_col, imp, NEG))
    blk_f = blk.astype(F32)
    sel = jnp.zeros((nsel, tq), F32)
    for _ in range(ksel):
        mx = jnp.max(vals, axis=0, keepdims=True)
        first = jnp.min(jnp.where(vals == mx, blk_f, float(nsel)), axis=0, keepdims=True)
        pick = blk_f == first
        sel = jnp.where(pick, 1.0, sel)
        vals = jnp.where(pick, LOWEST, vals)
    if nsel < LANE:
        sel = jnp.concatenate([sel, jnp.zeros((LANE - nsel, tq), F32)], axis=0)
    sel_bf = sel.astype(BF16)

    stats0 = tuple((jnp.full((1, tq), NEG, F32), jnp.zeros((1, tq), F32)) for _ in range(HEADS))

    def sweep(lo, hi, k_ref, vt_ref, accs, width, bias_fn):
        def body(kt, stats):
            k0 = pl.multiple_of(kt * width, width)
            k = k_ref[pl.ds(k0, width), :]
            scores = [_dot(k, qs_ref[:, h * tq:(h + 1) * tq]) for h in range(HEADS)]
            kpos = k0 + lax.broadcasted_iota(jnp.int32, (width, tq), 0)
            tpos = t0 + lax.broadcasted_iota(jnp.int32, (width, tq), 1)
            bias = bias_fn(k0, kpos, tpos)
            vt = vt_ref[:, pl.ds(k0, width)]
            return tuple(_softmax_step_t(scores[h] + bias, vt, stats[h][0], stats[h][1], accs[h])
                         for h in range(HEADS))

        return lax.fori_loop(lo, hi, body, stats0)

    def sel_bias(k0, kpos, tpos):
        chosen = _dot(exp_ref[pl.ds(k0, kpos.shape[0]), :], sel_bf)
        return jnp.where((chosen > 0.5) & (kpos <= tpos), 0.0, NEG)

    st_s = sweep(0, (t0 + tq - 1) // tk + 1, kvs_ref, vst_ref, acc_s, tk, sel_bias)

    def win_bias(k0, kpos, tpos):
        return jnp.where((kpos <= tpos) & (kpos > tpos - NSA_WINDOW), 0.0, NEG)

    st_w = sweep(jnp.maximum(t0 - NSA_WINDOW, 0) // tw, (t0 + tq - 1) // tw + 1,
                 kvw_ref, vwt_ref, acc_w, tw, win_bias)

    gates = _sigmoid(g_ref[...].T)
    outs = []
    for h in range(HEADS):
        outs.append(gates[3 * h:3 * h + 1, :] * o_c[h]
                    + gates[3 * h + 1:3 * h + 2, :] * (acc_s[h][...] / st_s[h][1])
                    + gates[3 * h + 2:3 * h + 3, :] * (acc_w[h][...] / st_w[h][1]))
    o_ref[...] = jnp.concatenate(outs, axis=0).T


def _nsa_attention(u, nqt, kc, vct, kvs, vst, kvw, vwt, B, S, tq, tk, tw):
    nq = S // tq
    ncp = S // NSA_CMP_STRIDE
    nsel = S // NSA_SEL_LEN
    ksel = min(NSA_TOPK, nsel)
    n_cmp = (S - NSA_CMP_LEN) // NSA_CMP_STRIDE + 1
    cmp_start = np.arange(ncp) * NSA_CMP_STRIDE
    sel_start = np.arange(nsel) * NSA_SEL_LEN
    overlap = ((cmp_start[:, None] <= sel_start[None, :] + NSA_SEL_LEN - 1)
               & (cmp_start[:, None] + NSA_CMP_LEN - 1 >= sel_start[None, :])
               & (np.arange(ncp)[:, None] < n_cmp))
    ovl_t = jnp.asarray(overlap.T.astype(np.float32), BF16)
    expand = (jnp.arange(S, dtype=jnp.int32)[:, None] // NSA_SEL_LEN
              == jnp.arange(LANE, dtype=jnp.int32)[None, :]).astype(BF16)
    sq = pl.Squeezed()
    seq_rows = pl.BlockSpec((S, LANE), lambda b, i: (b, 0))
    seq_cols = pl.BlockSpec((HD, S), lambda b, i: (0, b))
    kern = functools.partial(_nsa_kernel, tq=tq, tk=tk, tw=tw, ksel=ksel)
    return pl.pallas_call(
        kern,
        grid=(B, nq),
        in_specs=[pl.BlockSpec((GROUP, tq), lambda b, i: (0, b * nq + i)),
                  pl.BlockSpec((tq, LANE), lambda b, i, c=_B128["nsa_g"]: (b * nq + i, c)),
                  pl.BlockSpec((sq, ncp, LANE), lambda b, i: (b, 0, 0)),
                  pl.BlockSpec((sq, HD, ncp), lambda b, i: (b, 0, 0)),
                  seq_rows, seq_cols, seq_rows, seq_cols,
                  pl.BlockSpec((nsel, ncp), lambda b, i: (0, 0)),
                  pl.BlockSpec((S, LANE), lambda b, i: (0, 0))],
        out_specs=pl.BlockSpec((tq, GROUP), lambda b, i: (b * nq + i, 0)),
        out_shape=jax.ShapeDtypeStruct((B * S, GROUP), F32),
        scratch_shapes=[pltpu.VMEM((LANE, HEADS * tq), BF16)] + [pltpu.VMEM((HD, tq), F32)] * (2 * HEADS),
        compiler_params=_params("parallel", "arbitrary"),
        name="nsa_attention",
    )(nqt, u, kc, vct, kvs, vst, kvw, vwt, ovl_t, expand)


def _diff_kernel(lam_ref, qt_ref, k_ref, vt_ref, gain_ref, o_ref, qs_ref, *acc_refs, tq, tk, lam_init):
    i = pl.program_id(1)
    t0 = i * tq
    lanes = 2 * tq

    qt = qt_ref[...]
    row = lax.broadcasted_iota(jnp.int32, (GROUP, tq), 0)
    zero = jnp.zeros_like(qt)
    for h in range(HEADS):
        qs_ref[h] = jnp.concatenate([jnp.where(row // DIFF_QK == 2 * h, qt, zero),
                                     jnp.where(row // DIFF_QK == 2 * h + 1, qt, zero)], axis=1)
        acc_refs[h][...] = jnp.zeros((HD, lanes), F32)

    def step(k0, width, masked, stats):
        k = k_ref[pl.ds(k0, width), :]
        scores = [_dot(k, qs_ref[h]) for h in range(HEADS)]
        out = []
        for h in range(HEADS):
            s = scores[h]
            if masked:
                kpos = k0 + lax.broadcasted_iota(jnp.int32, (width, lanes), 0)
                tpos = t0 + lax.broadcasted_iota(jnp.int32, (width, lanes), 1) % tq
                s = jnp.where(kpos <= tpos, s, NEG)
            out.append(_softmax_step_t(s, vt_ref[h * HD:(h + 1) * HD, pl.ds(k0, width)],
                                       stats[h][0], stats[h][1], acc_refs[h]))
        return tuple(out)

    stats0 = tuple((jnp.full((1, lanes), NEG, F32), jnp.zeros((1, lanes), F32)) for _ in range(HEADS))
    n_wide = t0 // tk
    stats = lax.fori_loop(0, n_wide, lambda kt, c: step(pl.multiple_of(kt * tk, tk), tk, False, c), stats0)
    stats = lax.fori_loop(n_wide * (tk // tq), t0 // tq,
                          lambda kt, c: step(pl.multiple_of(kt * tq, tq), tq, False, c), stats)
    stats = step(pl.multiple_of(t0, tq), tq, True, stats)

    lam = lam_ref[...]
    lam_full = (jnp.exp(jnp.sum(lam[0:1] * lam[1:2], axis=1, keepdims=True))
                - jnp.exp(jnp.sum(lam[2:3] * lam[3:4], axis=1, keepdims=True)) + lam_init)
    outs = []
    for h in range(HEADS):
        r = acc_refs[h][...] / stats[h][1]
        d = r[:, 0:tq] - lam_full * r[:, tq:lanes]
        d = d * lax.rsqrt(jnp.mean(d * d, axis=0, keepdims=True) + EPS)
        outs.append(d * gain_ref[...] * (1.0 - lam_init))
    o_ref[...] = jnp.concatenate(outs, axis=0).T


def _diff_attention(dqt, dk, dvt, lam, gain_col, B, S, tq, tk, layer_idx):
    nq = S // tq
    lam_init = 0.8 - 0.6 * math.exp(-0.3 * layer_idx)
    kern = functools.partial(_diff_kernel, tq=tq, tk=tk, lam_init=lam_init)
    return pl.pallas_call(
        kern,
        grid=(B, nq),
        in_specs=[pl.BlockSpec((4, DIFF_QK), lambda b, i: (0, 0)),
                  pl.BlockSpec((GROUP, tq), lambda b, i: (0, b * nq + i)),
                  pl.BlockSpec((S, GROUP), lambda b, i: (b, 0)),
                  pl.BlockSpec((GROUP, S), lambda b, i: (0, b)),
                  pl.BlockSpec((HD, tq), lambda b, i: (0, 0))],
        out_specs=pl.BlockSpec((tq, GROUP), lambda b, i: (b * nq + i, 0)),
        out_shape=jax.ShapeDtypeStruct((B * S, GROUP), F32),
        scratch_shapes=[pltpu.VMEM((HEADS, GROUP, 2 * tq), BF16)] + [pltpu.VMEM((HD, 2 * tq), F32)] * HEADS,
        compiler_params=_params("parallel", "arbitrary"),
        name="diff_attention",
    )(lam, dqt, dk, dvt, gain_col)


_LEVELS = (32, 16, 8, 4, 2, 1)


def _stack_heads(x, lane, group, count):
    zero = jnp.zeros_like(x)
    return jnp.concatenate([jnp.where(lane // group == g, x, zero) for g in range(count)], axis=0)


def _unstack_heads(x4, lane, rows):
    out = jnp.where(lane // HD == 0, x4[0:rows], 0.0)
    for h in range(1, HEADS):
        out = out + jnp.where(lane // HD == h, x4[h * rows:(h + 1) * rows], 0.0)
    return out


def _linear_consts():
    r = np.arange(CHUNK)[:, None]
    t = np.arange(CHUNK)[None, :]
    tri = (t <= r).astype(np.float32)
    masks = [(r // (2 * s) == t // (2 * s)) for s in _LEVELS] + [r == t]
    mall = np.stack([np.tile(m.astype(np.float32), (1, HEADS)) for m in masks])
    return jnp.asarray(tri, BF16), jnp.asarray(mall, F32)


def _level_exponent(s, lg, b, row):
    if s == 1:
        return jnp.where((row & 1) != 0, 0.0, pltpu.roll(lg, CHUNK - 1, 0))
    if s == 2:
        nxt1 = pltpu.roll(lg, CHUNK - 1, 0)
        nxt2 = pltpu.roll(lg, CHUNK - 2, 0)
        r4 = row & 3
        return jnp.where(r4 == 0, nxt1 + nxt2, jnp.where(r4 == 1, nxt1, jnp.where(r4 == 2, 0.0, lg)))
    mids = [jnp.broadcast_to(b[m:m + 1, :], (2 * s, b.shape[1])) for m in range(s, CHUNK, 2 * s)]
    d = b - (jnp.concatenate(mids, axis=0) if len(mids) > 1 else mids[0])
    return jnp.where((row & s) != 0, d, -d)


def _linear_chunk(q, k, v, lg, tri, mall_ref, state_ref, dk):
    dkh = HEADS * dk
    hi, mid, lo = _split3(lg)
    b = _dot(tri, hi) + _dot(tri, mid) + _dot(tri, lo)
    row = lax.broadcasted_iota(jnp.int32, (CHUNK, dkh), 0)
    lane_k = lax.broadcasted_iota(jnp.int32, (CHUNK, dkh), 1)
    lane_v = lax.broadcasted_iota(jnp.int32, (CHUNK, GROUP), 1)
    a_t = mall_ref[len(_LEVELS)] * _dot_nt(k.astype(BF16), _stack_heads(q.astype(BF16), lane_k, dk, HEADS))
    for li, s in enumerate(_LEVELS):
        e = jnp.exp(_level_exponent(s, lg, b, row))
        upper = (row & s) != 0
        qt = jnp.where(upper, q * e, 0.0).astype(BF16)
        kt = jnp.where(upper, 0.0, k * e).astype(BF16)
        a_t = a_t + mall_ref[li] * _dot_nt(kt, _stack_heads(qt, lane_k, dk, HEADS))
    o_intra = _unstack_heads(_dot_tn(a_t.astype(BF16), v.astype(BF16)), lane_v, CHUNK)
    e_b = jnp.exp(b)
    e_u = jnp.exp(b[CHUNK - 1:CHUNK, :] - b)
    st = state_ref[...]
    o_inter = _dot_nt((q * e_b).astype(BF16), st.astype(BF16))
    kv = _dot_tn(v.astype(BF16), (k * e_u).astype(BF16))
    srow = lax.broadcasted_iota(jnp.int32, (GROUP, dkh), 0)
    scol = lax.broadcasted_iota(jnp.int32, (GROUP, dkh), 1)
    state_ref[...] = st * e_b[CHUNK - 1:CHUNK, :] + jnp.where(srow // HD == scol // dk, kv, 0.0)
    return o_inter + o_intra


def _gla_kernel(q_ref, k_ref, v_ref, lr_ref, og_ref, w2_ref, b_ref, gain_ref, tri_ref, mall_ref,
                ones64_ref, o_ref, state_ref, *, tm):
    @pl.when(pl.program_id(1) == 0)
    def _():
        state_ref[...] = jnp.zeros_like(state_ref)

    tri = tri_ref[...]
    ones64 = ones64_ref[...]
    w_hi, w_lo = _split2(w2_ref[...])

    def body(c, carry):
        r0 = pl.multiple_of(c * CHUNK, CHUNK)
        rs = pl.ds(r0, CHUNK)
        lr_hi, lr_lo = _split2(lr_ref[rs, :])
        x = _dot(lr_hi, w_hi) + _dot(lr_lo, w_hi) + _dot(lr_hi, w_lo) + b_ref[...]
        lg = (jnp.minimum(x, 0.0) - jnp.log(1.0 + jnp.exp(-jnp.abs(x)))) * (1.0 / GLA_TAU)
        o = _linear_chunk(q_ref[rs, :] * (GLA_DK ** -0.5), k_ref[rs, :], v_ref[rs, :], lg,
                          tri, mall_ref, state_ref, GLA_DK)
        o_ref[rs, :] = _group_rms(o, ones64, HD) * gain_ref[...] * _silu(og_ref[rs, :])
        return carry

    lax.fori_loop(0, tm // CHUNK, body, 0, unroll=4)


def _hgrn_kernel(q_ref, f_ref, i_ref, og_ref, lbl_ref, gain_ref, tri_ref, mall_ref, ones64_ref,
                 o_ref, state_ref, *, tm, layer_idx):
    @pl.when(pl.program_id(1) == 0)
    def _():
        state_ref[...] = jnp.zeros_like(state_ref)

    tri = tri_ref[...]
    ones64 = ones64_ref[...]
    logits = lbl_ref[...]
    ez = jnp.exp(logits - jnp.max(logits, axis=0, keepdims=True))
    probs = ez / jnp.sum(ez, axis=0, keepdims=True)
    lb = jnp.zeros((1, GROUP), F32)
    for j in range(1, layer_idx + 1):
        lb = lb + probs[j:j + 1]

    def body(c, carry):
        r0 = pl.multiple_of(c * CHUNK, CHUNK)
        rs = pl.ds(r0, CHUNK)
        z = f_ref[rs, :]
        f = lb + (1.0 - lb) * _sigmoid(z)
        k = (1.0 - lb) * _sigmoid(-z)
        o = _linear_chunk(q_ref[rs, :], k, i_ref[rs, :], jnp.log(f), tri, mall_ref, state_ref, HD)
        o_ref[rs, :] = _group_rms(o, ones64, HD) * gain_ref[...] * _silu(og_ref[rs, :])
        return carry

    lax.fori_loop(0, tm // CHUNK, body, 0, unroll=4)


def _linear_mixers(u, w2pad, b_gate, gla_gain, lb_logits, hgrn_gain, B, S, tm, layer_idx):
    nt = S // tm
    tri, mall = _linear_consts()
    ones64 = _block_ones(GROUP, HD)
    c256 = lambda name: pl.BlockSpec((tm, GROUP), lambda b, i, c=_B256[name]: (b * nt + i, c))
    c128 = lambda name: pl.BlockSpec((tm, LANE), lambda b, i, c=_B128[name]: (b * nt + i, c))
    full = lambda shape: pl.BlockSpec(shape, lambda b, i: (0,) * len(shape))
    out_spec = pl.BlockSpec((tm, GROUP), lambda b, i: (b * nt + i, 0))
    out_shape = jax.ShapeDtypeStruct((B * S, GROUP), F32)
    consts = [full(tri.shape), full(mall.shape), full((GROUP, GROUP))]
    o_gla = pl.pallas_call(
        functools.partial(_gla_kernel, tm=tm),
        grid=(B, nt),
        in_specs=[c128("g_q"), c128("g_k"), c256("g_v"), c128("g_lr"), c256("g_og"),
                  full((LANE, LANE)), full((1, LANE)), full((1, GROUP))] + consts,
        out_specs=out_spec, out_shape=out_shape,
        scratch_shapes=[pltpu.VMEM((GROUP, HEADS * GLA_DK), F32)],
        compiler_params=_params("parallel", "arbitrary"),
        name="gla_mixer",
    )(u, u, u, u, u, w2pad, b_gate, gla_gain, tri, mall, ones64)
    depth = lb_logits.shape[0]
    o_hgrn = pl.pallas_call(
        functools.partial(_hgrn_kernel, tm=tm, layer_idx=layer_idx),
        grid=(B, nt),
        in_specs=[c256("r_q"), c256("r_f"), c256("r_i"), c256("r_og"),
                  full((depth, GROUP)), full((1, GROUP))] + consts,
        out_specs=out_spec, out_shape=out_shape,
        scratch_shapes=[pltpu.VMEM((GROUP, GROUP), F32)],
        compiler_params=_params("parallel", "arbitrary"),
        name="hgrn_mixer",
    )(u, u, u, u, lb_logits, hgrn_gain, tri, mall, ones64)
    return o_gla, o_hgrn


def _out_proj_kernel(h_ref, a_ref, b_ref, c_ref, d_ref, w_ref, o_ref):
    acc = h_ref[...]
    for j, r in enumerate((a_ref, b_ref, c_ref, d_ref)):
        acc = acc + _dot(r[...].astype(BF16), w_ref[j * GROUP:(j + 1) * GROUP, :])
    o_ref[...] = acc


def _out_proj(h, parts, w_bf, tm):
    T = h.shape[0]
    part = pl.BlockSpec((tm, GROUP), lambda i: (i, 0))
    return pl.pallas_call(
        _out_proj_kernel,
        grid=(T // tm,),
        in_specs=[pl.BlockSpec((tm, D_MODEL), lambda i: (i, 0)), part, part, part, part,
                  pl.BlockSpec((D_MODEL, D_MODEL), lambda i: (0, 0))],
        out_specs=pl.BlockSpec((tm, D_MODEL), lambda i: (i, 0)),
        out_shape=jax.ShapeDtypeStruct((T, D_MODEL), F32),
        compiler_params=_params("parallel"),
        name="out_proj",
    )(h, *parts, w_bf)


def _ffn_kernel(h_ref, g_ref, wg_ref, wu_ref, wd_ref, o_ref, xn_ref, acc_ref):
    f = pl.program_id(1)

    @pl.when(f == 0)
    def _():
        x = h_ref[...]
        y = x * lax.rsqrt(jnp.mean(x * x, axis=-1, keepdims=True) + EPS) * g_ref[...]
        xn_ref[...] = y.astype(BF16)
        acc_ref[...] = x

    xn = xn_ref[...]
    mid = _silu(_dot(xn, wg_ref[...])) * _dot(xn, wu_ref[...])
    acc_ref[...] += _dot(mid.astype(BF16), wd_ref[...])

    @pl.when(f == pl.num_programs(1) - 1)
    def _():
        o_ref[...] = acc_ref[...]


def _ffn(h, gain, wg, wu, wd, tm, tf):
    T = h.shape[0]
    F = wg.shape[1]
    return pl.pallas_call(
        _ffn_kernel,
        grid=(T // tm, F // tf),
        in_specs=[pl.BlockSpec((tm, D_MODEL), lambda i, f: (i, 0)),
                  pl.BlockSpec((1, D_MODEL), lambda i, f: (0, 0)),
                  pl.BlockSpec((D_MODEL, tf), lambda i, f: (0, f)),
                  pl.BlockSpec((D_MODEL, tf), lambda i, f: (0, f)),
                  pl.BlockSpec((tf, D_MODEL), lambda i, f: (f, 0))],
        out_specs=pl.BlockSpec((tm, D_MODEL), lambda i, f: (i, 0)),
        out_shape=jax.ShapeDtypeStruct((T, D_MODEL), F32),
        scratch_shapes=[pltpu.VMEM((tm, D_MODEL), BF16), pltpu.VMEM((tm, D_MODEL), F32)],
        compiler_params=_params("parallel", "arbitrary"),
        name="ffn_swiglu",
    )(h, gain.reshape(1, D_MODEL), wg, wu, wd)


def _router_kernel(h_ref, g_ref, r_ref, c_ref, comb_ref, sel_ref):
    x = h_ref[...]
    y = x * lax.rsqrt(jnp.mean(x * x, axis=-1, keepdims=True) + EPS) * g_ref[...]
    c_ref[...] = y.astype(BF16)
    y_hi, y_lo = _split2(y)
    r_hi, r_lo = _split2(r_ref[...])
    logits = _dot(y_hi, r_hi) + _dot(y_lo, r_hi) + _dot(y_hi, r_lo)
    lane = lax.broadcasted_iota(jnp.int32, logits.shape, 1)
    lane_f = lane.astype(F32)
    logits = jnp.where(lane < N_EXPERTS, logits, LOWEST)
    m1 = jnp.max(logits, axis=1, keepdims=True)
    i1 = jnp.min(jnp.where(logits == m1, lane_f, float(LANE)), axis=1, keepdims=True)
    rest = jnp.where(lane_f == i1, LOWEST, logits)
    m2 = jnp.max(rest, axis=1, keepdims=True)
    i2 = jnp.min(jnp.where(rest == m2, lane_f, float(LANE)), axis=1, keepdims=True)
    e2 = jnp.exp(m2 - m1)
    w1 = 1.0 / (1.0 + e2)
    comb_ref[...] = jnp.where(lane_f == i1, w1, jnp.where(lane_f == i2, e2 * w1, 0.0))
    sel_ref[...] = ((lane_f == i1) | (lane_f == i2)).astype(jnp.int32)


def _router(h, gain, router_pad, tm):
    T = h.shape[0]
    return pl.pallas_call(
        _router_kernel,
        grid=(T // tm,),
        in_specs=[pl.BlockSpec((tm, D_MODEL), lambda i: (i, 0)),
                  pl.BlockSpec((1, D_MODEL), lambda i: (0, 0)),
                  pl.BlockSpec((D_MODEL, LANE), lambda i: (0, 0))],
        out_specs=[pl.BlockSpec((tm, D_MODEL), lambda i: (i, 0)), pl.BlockSpec((tm, LANE), lambda i: (i, 0)),
                   pl.BlockSpec((tm, LANE), lambda i: (i, 0))],
        out_shape=[jax.ShapeDtypeStruct((T, D_MODEL), BF16), jax.ShapeDtypeStruct((T, LANE), F32),
                   jax.ShapeDtypeStruct((T, LANE), jnp.int32)],
        compiler_params=_params("parallel"),
        name="moe_router",
    )(h, gain.reshape(1, D_MODEL), router_pad)


MOE_CHUNK = 512
MOE_ROWS = 512
MOE_HALF = MOE_ROWS // 2
MOE_ALIGN = 16


def _moe_plan(sel, T):
    nch = T // MOE_CHUNK
    selc = sel[:, :N_EXPERTS].reshape(nch, MOE_CHUNK, N_EXPERTS)
    rank = jnp.cumsum(selc, axis=1) - selc
    n_ce = jnp.sum(selc, axis=1)
    cap = (n_ce + MOE_ALIGN - 1) // MOE_ALIGN * MOE_ALIGN
    tot = jnp.sum(cap, axis=0)
    ptot = (tot + MOE_ROWS - 1) // MOE_ROWS * MOE_ROWS
    start = jnp.cumsum(ptot) - ptot
    lo_ce = start[None, :] + jnp.cumsum(cap, axis=0) - cap
    lo = jnp.concatenate([lo_ce.T, (start + tot)[:, None]], axis=1)
    n = jnp.concatenate([n_ce.T, jnp.zeros((N_EXPERTS, 1), jnp.int32)], axis=1)
    rk = jnp.where(selc > 0, rank, -1).reshape(T, N_EXPERTS)
    n_tiles = _moe_tiles(T)
    tile_end = jnp.cumsum(ptot // MOE_ROWS)
    j = jnp.arange(n_tiles, dtype=jnp.int32)
    tile_e = jnp.minimum(jnp.sum((tile_end[None, :] <= j[:, None]).astype(jnp.int32), axis=1), N_EXPERTS - 1)
    valid = (j < tile_end[-1]).astype(jnp.int32)
    n_flat = jnp.concatenate([n.reshape(-1), tile_end[-1:]]).astype(jnp.int32)
    return lo.reshape(-1).astype(jnp.int32), n_flat, rk.astype(jnp.int32), tile_e, valid


def _moe_tiles(T):
    nch = T // MOE_CHUNK
    rows = 2 * T + nch * N_EXPERTS * (MOE_ALIGN - 1) + N_EXPERTS * (MOE_ROWS - 1)
    return -(-rows // MOE_ROWS) + 1


def _moe_gather_kernel(lo_ref, n_ref, c_ref, rkt_ref, x_hbm, buf, sem, *, nch):
    e = pl.program_id(0)
    c = pl.program_id(1)
    step = e * (nch + 1) + c
    slot = step % 2
    rk_row = rkt_ref[pl.ds(e, 1), :]
    rk_row = jnp.where(c < nch, rk_row, -1)
    row = lax.broadcasted_iota(jnp.int32, (MOE_HALF, MOE_CHUNK), 0)
    chunk = c_ref[...]
    onehot = jnp.where(rk_row == row, 1.0, 0.0).astype(BF16)
    buf[slot, 0:MOE_HALF, :] = _dot(onehot, chunk).astype(BF16)

    @pl.when(n_ref[step] > MOE_HALF)
    def _():
        onehot = jnp.where(rk_row == row + MOE_HALF, 1.0, 0.0).astype(BF16)
        buf[slot, MOE_HALF:MOE_ROWS, :] = _dot(onehot, chunk).astype(BF16)

    @pl.when(n_ref[step] <= MOE_HALF)
    def _():
        buf[slot, MOE_HALF:MOE_ROWS, :] = jnp.zeros((MOE_HALF, D_MODEL), BF16)

    def copy(k, s):
        dst = x_hbm.at[pl.ds(pl.multiple_of(lo_ref[k], MOE_ALIGN), MOE_ROWS)]
        return pltpu.make_async_copy(buf.at[s], dst, sem.at[s])

    @pl.when(step > 0)
    def _():
        copy(step - 1, 1 - slot).wait()

    copy(step, slot).start()

    n_steps = pl.num_programs(0) * pl.num_programs(1)

    @pl.when(step == n_steps - 1)
    def _():
        copy(step, slot).wait()
        buf[0] = jnp.zeros((MOE_ROWS, D_MODEL), BF16)

        def fill(k, carry):
            dst = x_hbm.at[pl.ds(pl.multiple_of(k * MOE_ROWS, MOE_ROWS), MOE_ROWS)]
            cp = pltpu.make_async_copy(buf.at[0], dst, sem.at[0])
            cp.start()
            cp.wait()
            return carry

        lax.fori_loop(n_ref[n_steps], x_hbm.shape[0] // MOE_ROWS, fill, 0)


def _moe_gather(c_bf, rkt, lo, n, n_tiles):
    T = c_bf.shape[0]
    nch = T // MOE_CHUNK
    last = nch - 1
    grid_spec = pltpu.PrefetchScalarGridSpec(
        num_scalar_prefetch=2,
        grid=(N_EXPERTS, nch + 1),
        in_specs=[pl.BlockSpec((MOE_CHUNK, D_MODEL), lambda e, c, lo, n: (jnp.minimum(c, last), 0)),
                  pl.BlockSpec((N_EXPERTS, MOE_CHUNK), lambda e, c, lo, n: (0, jnp.minimum(c, last)))],
        out_specs=pl.BlockSpec(memory_space=pl.ANY),
        scratch_shapes=[pltpu.VMEM((2, MOE_ROWS, D_MODEL), BF16), pltpu.SemaphoreType.DMA((2,))],
    )
    return pl.pallas_call(
        functools.partial(_moe_gather_kernel, nch=nch),
        grid_spec=grid_spec,
        out_shape=jax.ShapeDtypeStruct((n_tiles * MOE_ROWS, D_MODEL), BF16),
        compiler_params=_params("arbitrary", "arbitrary"),
        name="moe_gather",
    )(lo, n, c_bf, rkt)


def _moe_ffn_kernel(te_ref, valid_ref, x_ref, wg_ref, wu_ref, wd_ref, y_ref, acc_ref):
    j = pl.program_id(0)
    f = pl.program_id(1)
    last = pl.num_programs(1) - 1

    @pl.when(valid_ref[j] == 1)
    def _():
        x = x_ref[...]
        mid = _silu(_dot(x, wg_ref[...])) * _dot(x, wu_ref[...])
        part = _dot(mid.astype(BF16), wd_ref[...])

        @pl.when(f == 0)
        def _():
            acc_ref[...] = part

        @pl.when(f > 0)
        def _():
            acc_ref[...] += part

        @pl.when(f == last)
        def _():
            y_ref[...] = acc_ref[...].astype(BF16)

    @pl.when((valid_ref[j] == 0) & (f == last))
    def _():
        y_ref[...] = jnp.zeros(y_ref.shape, BF16)


def _moe_ffn(x_sorted, tile_e, valid, wg, wu, wd, tf):
    n_tiles = x_sorted.shape[0] // MOE_ROWS
    F = wg.shape[2]
    nf = F // tf
    sq = pl.Squeezed()
    fsel = lambda j, f, te, va: jnp.where(va[j] == 1, f, nf - 1)
    grid_spec = pltpu.PrefetchScalarGridSpec(
        num_scalar_prefetch=2,
        grid=(n_tiles, nf),
        in_specs=[pl.BlockSpec((MOE_ROWS, D_MODEL), lambda j, f, te, va: (j, 0)),
                  pl.BlockSpec((sq, D_MODEL, tf), lambda j, f, te, va: (te[j], 0, fsel(j, f, te, va))),
                  pl.BlockSpec((sq, D_MODEL, tf), lambda j, f, te, va: (te[j], 0, fsel(j, f, te, va))),
                  pl.BlockSpec((sq, tf, D_MODEL), lambda j, f, te, va: (te[j], fsel(j, f, te, va), 0))],
        out_specs=pl.BlockSpec((MOE_ROWS, D_MODEL), lambda j, f, te, va: (j, 0)),
        scratch_shapes=[pltpu.VMEM((MOE_ROWS, D_MODEL), F32)],
    )
    return pl.pallas_call(
        _moe_ffn_kernel,
        grid_spec=grid_spec,
        out_shape=jax.ShapeDtypeStruct((n_tiles * MOE_ROWS, D_MODEL), BF16),
        compiler_params=_params("parallel", "arbitrary"),
        name="moe_experts",
    )(tile_e, valid, x_sorted, wg, wu, wd)


def _moe_combine_kernel(lo_ref, n_ref, h_ref, rk_ref, comb_ref, y_hbm, o_ref, ybuf, sem, *, nch):
    c = pl.program_id(0)

    def copy(e):
        src = y_hbm.at[pl.ds(pl.multiple_of(lo_ref[e * (nch + 1) + c], MOE_ALIGN), MOE_ROWS)]
        return pltpu.make_async_copy(src, ybuf.at[e], sem.at[e])

    for e in range(N_EXPERTS):
        copy(e).start()
    o_ref[...] = h_ref[...]
    rk = rk_ref[...]
    comb = comb_ref[...]
    lane = lax.broadcasted_iota(jnp.int32, (MOE_CHUNK, MOE_HALF), 1)
    for e in range(N_EXPERTS):
        copy(e).wait()
        rank_col = rk[:, e:e + 1]
        w_col = comb[:, e:e + 1]
        onehot = jnp.where(rank_col == lane, 1.0, 0.0).astype(BF16)
        o_ref[...] += w_col * _dot(onehot, ybuf[e, 0:MOE_HALF, :])

        @pl.when(n_ref[e * (nch + 1) + c] > MOE_HALF)
        def _():
            onehot = jnp.where(rank_col == lane + MOE_HALF, 1.0, 0.0).astype(BF16)
            o_ref[...] += w_col * _dot(onehot, ybuf[e, MOE_HALF:MOE_ROWS, :])


def _moe_combine(h, rk_pad, comb, y_sorted, lo, n):
    T = h.shape[0]
    nch = T // MOE_CHUNK
    grid_spec = pltpu.PrefetchScalarGridSpec(
        num_scalar_prefetch=2,
        grid=(nch,),
        in_specs=[pl.BlockSpec((MOE_CHUNK, D_MODEL), lambda c, lo, n: (c, 0)),
                  pl.BlockSpec((MOE_CHUNK, LANE), lambda c, lo, n: (c, 0)),
                  pl.BlockSpec((MOE_CHUNK, LANE), lambda c, lo, n: (c, 0)),
                  pl.BlockSpec(memory_space=pl.ANY)],
        out_specs=pl.BlockSpec((MOE_CHUNK, D_MODEL), lambda c, lo, n: (c, 0)),
        scratch_shapes=[pltpu.VMEM((N_EXPERTS, MOE_ROWS, D_MODEL), BF16), pltpu.SemaphoreType.DMA((N_EXPERTS,))],
    )
    return pl.pallas_call(
        functools.partial(_moe_combine_kernel, nch=nch),
        grid_spec=grid_spec,
        out_shape=jax.ShapeDtypeStruct((T, D_MODEL), F32),
        compiler_params=_params("arbitrary"),
        name="moe_combine",
    )(lo, n, h, rk_pad, comb, y_sorted)


def _moe(h, c_bf, comb, sel, wg, wu, wd, tf):
    T = h.shape[0]
    lo, n, rk, tile_e, valid = _moe_plan(sel, T)
    x_sorted = _moe_gather(c_bf, rk.T, lo, n, _moe_tiles(T))
    y_sorted = _moe_ffn(x_sorted, tile_e, valid, wg, wu, wd, tf)
    rk_pad = jnp.pad(rk, ((0, 0), (0, LANE - N_EXPERTS)), constant_values=-1)
    return _moe_combine(h, rk_pad, comb, y_sorted, lo, n)


def _ple_kernel(h_ref, p_ref, g_ref, wg_ref, wp_ref, o_ref):
    x = h_ref[...]
    y = x * lax.rsqrt(jnp.mean(x * x, axis=-1, keepdims=True) + EPS) * g_ref[...]
    gate = _sigmoid(_dot(y.astype(BF16), wg_ref[...]))
    o_ref[...] = x + _dot(p_ref[...].astype(BF16), wp_ref[...]) * gate


def _ple(h, p, gain, wg, wp, tm):
    T = h.shape[0]
    return pl.pallas_call(
        _ple_kernel,
        grid=(T // tm,),
        in_specs=[pl.BlockSpec((tm, D_MODEL), lambda i: (i, 0)),
                  pl.BlockSpec((tm, PLE_DIM), lambda i: (i, 0)),
                  pl.BlockSpec((1, D_MODEL), lambda i: (0, 0)),
                  pl.BlockSpec((D_MODEL, D_MODEL), lambda i: (0, 0)),
                  pl.BlockSpec((PLE_DIM, D_MODEL), lambda i: (0, 0))],
        out_specs=pl.BlockSpec((tm, D_MODEL), lambda i: (i, 0)),
        out_shape=jax.ShapeDtypeStruct((T, D_MODEL), F32),
        compiler_params=_params("parallel"),
        name="ple_gate",
    )(h, p, gain.reshape(1, D_MODEL), wg, wp)


def _tiles(T, S):
    pick = lambda n, pref: max(t for t in pref if n % t == 0)
    return dict(
        proj_m=pick(T, (1024, 512, 256, 128)), proj_n=NC // 3,
        prep_m=pick(T, (512, 256, 128)),
        attn_q=256, attn_k=pick(S, (512, 256)),
        nsa_q=pick(S, (512, 256)), nsa_k=pick(S, (512, 256)),
        lin_m=pick(S, (512, 256, 128, 64)),
        row_m=pick(T, (512, 256, 128)),
        ffn_m=pick(T, (1024, 512, 256, 128)), ffn_f=512,
    )


def kernel(x, p, norm_attn, w_in, w_out, nsa_cmp_pos, nsa_cmp_w1, nsa_cmp_w2, nsa_qk_gain, diff_qk_gain, diff_lambda, diff_norm, gla_w_gate2, gla_b_gate, gla_norm, hgrn_lb_logits, hgrn_norm, norm_ffn, ffn_w_gate, ffn_w_up, ffn_w_down, moe_router, moe_w_gate, moe_w_up, moe_w_down, ple_norm, ple_w_gate, ple_w_proj):
    B, S, _ = x.shape
    depth = w_in.shape[0]
    T = B * S
    t = _tiles(T, S)
    ncp = S // NSA_CMP_STRIDE
    half = NSA_CMP_STRIDE * HD
    cols = jnp.asarray(np.maximum(_COLS, 0), jnp.int32)
    col_mask = jnp.asarray(_COLS >= 0)
    ones_row = jnp.ones((GROUP,), F32)

    h = x.reshape(T, D_MODEL)
    for i in range(depth):
        w_in_r = jnp.where(col_mask[None, :], jnp.take(w_in[i], cols, axis=1), 0.0).astype(BF16)
        gains = jnp.stack([
            jnp.tile(nsa_qk_gain[i, 0], HEADS) * (HD ** -0.5 * LOG2E),
            jnp.tile(diff_qk_gain[i, 0], 2 * HEADS) * (DIFF_QK ** -0.5 * LOG2E),
            jnp.tile(diff_qk_gain[i, 1], 2 * HEADS),
            jnp.tile(nsa_qk_gain[i, 2], HEADS),
            jnp.tile(nsa_qk_gain[i, 3], HEADS),
            ones_row, ones_row, ones_row])
        pos = nsa_cmp_pos[i].reshape(2, 2, half)
        w1 = nsa_cmp_w1[i].astype(BF16)
        w2p = jnp.pad(nsa_cmp_w2[i], ((0, 0), (0, 0), (0, LANE - HD))).astype(BF16)
        kc_gain = jnp.pad(nsa_qk_gain[i, 1], (0, LANE - HD)).reshape(1, LANE)
        w2pad = jnp.zeros((LANE, LANE), F32).at[:GLA_RANK].set(gla_w_gate2[i])
        diff_gain_col = jnp.broadcast_to(diff_norm[i][:, None], (HD, t["attn_q"]))

        u = _norm_matmul(h, norm_attn[i], w_in_r, t["proj_m"], t["proj_n"])
        nqt, dqt, dk, dvt, kvs, vst, kvw, vwt = _prep(u, gains, t["prep_m"])
        kv = u[:, _B128["kvcmp"] * LANE:(_B128["kvcmp"] + 1) * LANE]
        xk = kv[:, :HD].reshape(B, ncp, half)
        xv = kv[:, HD:].reshape(B, ncp, half)
        kc, vct = _compress(xk, xv, pos, w1, w2p, kc_gain)
        o_a = _nsa_attention(u, nqt, kc, vct, kvs, vst, kvw, vwt, B, S, t["nsa_q"], t["nsa_k"], t["nsa_q"])
        o_b = _diff_attention(dqt, dk, dvt, diff_lambda[i], diff_gain_col, B, S, t["attn_q"], t["attn_k"], i)
        o_c, o_d = _linear_mixers(u, w2pad, gla_b_gate[i].reshape(1, LANE),
                                  jnp.tile(gla_norm[i], HEADS).reshape(1, GROUP), hgrn_lb_logits,
                                  jnp.tile(hgrn_norm[i], HEADS).reshape(1, GROUP), B, S, t["lin_m"], i)
        h = _out_proj(h, (o_a, o_b, o_c, o_d), w_out[i].astype(BF16), t["row_m"])
        if i % 2 == 0:
            j = i // 2
            h = _ffn(h, norm_ffn[i], ffn_w_gate[j].astype(BF16), ffn_w_up[j].astype(BF16),
                     ffn_w_down[j].astype(BF16), t["ffn_m"], t["ffn_f"])
        else:
            j = i // 2
            router_pad = jnp.zeros((D_MODEL, LANE), F32).at[:, :N_EXPERTS].set(moe_router[j])
            c_bf, comb, sel = _router(h, norm_ffn[i], router_pad, t["row_m"])
            h = _moe(h, c_bf, comb, sel, moe_w_gate[j].astype(BF16), moe_w_up[j].astype(BF16),
                     moe_w_down[j].astype(BF16), t["ffn_f"])
        h = _ple(h, p[i].reshape(T, PLE_DIM), ple_norm[i], ple_w_gate[i].astype(BF16),
                 ple_w_proj[i].astype(BF16), t["row_m"])
    return h.reshape(B, S, D_MODEL)
```

```python
import functools
import math

import numpy as np
import jax
import jax.numpy as jnp
from jax import lax
from jax.experimental import pallas as pl
from jax.experimental.pallas import tpu as pltpu

F32 = jnp.float32
BF16 = jnp.bfloat16

D_MODEL = 1024
HEADS = 4
HD = 64
GROUP = HEADS * HD
NSA_CMP_LEN = 32
NSA_CMP_STRIDE = 16
NSA_CMP_HIDDEN = 4 * HD
NSA_SEL_LEN = 64
NSA_TOPK = 16
NSA_WINDOW = 512
NSA_SUBTILES = 2
DIFF_QK = HD // 2
GLA_DK = HD // 2
GLA_RANK = 16
GLA_TAU = 16.0
CHUNK = 64
D_FF = 7 * D_MODEL // 2
N_EXPERTS = 8
PLE_DIM = 256
EPS = 1e-6
NEG = -1e30
BIG = 1e30
LOWEST = -3.0e38
LOG2E = 1.4426950408889634

VMEM_LIMIT = 52 * 1024 * 1024
LANE = 128

_SRC = dict(nsa_q=0, k_cmp=256, v_cmp=320, k_slc=384, v_slc=448, k_win=512, v_win=576, nsa_g=640,
            d_q=652, d_k=908, d_v=1164, g_q=1420, g_k=1548, g_v=1676, g_lr=1932, g_og=1948,
            r_q=2204, r_f=2460, r_i=2716, r_og=2972)
IN_COLS = 3228

_B256 = dict(nsa_q=0, d_q=1, d_k=2, d_v=3, g_v=4, g_og=5, r_q=6, r_f=7, r_i=8, r_og=9)
_B128 = dict(g_q=20, g_k=21, g_lr=22, kvcmp=23, kvslc=24, kvwin=25, nsa_g=26)
NC = 27 * 128


def _column_map():
    cols = -np.ones((NC,), np.int64)

    def put(dst, src, width):
        cols[dst:dst + width] = np.arange(src, src + width)

    for name in _B256:
        put(_B256[name] * 256, _SRC[name], 256)
    put(_B128["g_q"] * 128, _SRC["g_q"], 128)
    put(_B128["g_k"] * 128, _SRC["g_k"], 128)
    put(_B128["g_lr"] * 128, _SRC["g_lr"], GLA_RANK)
    put(_B128["kvcmp"] * 128, _SRC["k_cmp"], 128)
    put(_B128["kvslc"] * 128, _SRC["k_slc"], 128)
    put(_B128["kvwin"] * 128, _SRC["k_win"], 128)
    put(_B128["nsa_g"] * 128, _SRC["nsa_g"], 3 * HEADS)
    return cols


_COLS = _column_map()


def _dot(a, b):
    return jnp.dot(a, b, preferred_element_type=F32)


def _dot_nt(a, b):
    return lax.dot_general(a, b, (((1,), (1,)), ((), ())), preferred_element_type=F32)


def _dot_tn(a, b):
    return lax.dot_general(a, b, (((0,), (0,)), ((), ())), preferred_element_type=F32)


def _split2(x):
    hi = x.astype(BF16)
    lo = (x - hi.astype(F32)).astype(BF16)
    return hi, lo


def _split3(x):
    hi = x.astype(BF16)
    r = x - hi.astype(F32)
    mid = r.astype(BF16)
    lo = (r - mid.astype(F32)).astype(BF16)
    return hi, mid, lo


def _group_mean(x, ones_bf, group):
    hi, lo = _split2(x)
    return (_dot(hi, ones_bf) + _dot(lo, ones_bf)) * (1.0 / group)


def _group_rms(x, ones_bf, group):
    return x * lax.rsqrt(_group_mean(x * x, ones_bf, group) + EPS)


def _sigmoid(x):
    return 1.0 / (1.0 + jnp.exp(-x))


def _silu(x):
    return x * _sigmoid(x)


def _params(*sem):
    return pltpu.CompilerParams(dimension_semantics=sem, vmem_limit_bytes=VMEM_LIMIT)


def _block_ones(n, group):
    i = np.arange(n)
    return jnp.asarray((i[:, None] // group == i[None, :] // group).astype(np.float32), BF16)


def _norm_matmul_kernel(x_ref, g_ref, w_ref, o_ref, xn_ref):
    @pl.when(pl.program_id(1) == 0)
    def _():
        x = x_ref[...]
        y = x * lax.rsqrt(jnp.mean(x * x, axis=-1, keepdims=True) + EPS) * g_ref[...]
        xn_ref[...] = y.astype(BF16)

    o_ref[...] = _dot(xn_ref[...], w_ref[...])


def _norm_matmul(x, gain, w_bf, tm, tn):
    T, K = x.shape
    N = w_bf.shape[1]
    return pl.pallas_call(
        _norm_matmul_kernel,
        grid=(T // tm, N // tn),
        in_specs=[pl.BlockSpec((tm, K), lambda i, j: (i, 0)),
                  pl.BlockSpec((1, K), lambda i, j: (0, 0)),
                  pl.BlockSpec((K, tn), lambda i, j: (0, j))],
        out_specs=pl.BlockSpec((tm, tn), lambda i, j: (i, j)),
        out_shape=jax.ShapeDtypeStruct((T, N), F32),
        scratch_shapes=[pltpu.VMEM((tm, K), BF16)],
        compiler_params=_params("parallel", "arbitrary"),
        name="in_proj",
    )(x, gain.reshape(1, K), w_bf)


def _prep_kernel(nq_ref, dq_ref, dk_ref, dv_ref, kvs_ref, kvw_ref, gains_ref, ones64_ref, ones32_ref,
                 ones64h_ref, o_nq, o_dq, o_dk, o_dv, o_kvs, o_vs, o_kvw, o_vw):
    ones64 = ones64_ref[...]
    ones32 = ones32_ref[...]
    ones64h = ones64h_ref[...]
    o_nq[...] = (_group_rms(nq_ref[...], ones64, HD) * gains_ref[0:1, :]).T.astype(BF16)
    o_dq[...] = (_group_rms(dq_ref[...], ones32, DIFF_QK) * gains_ref[1:2, :]).T.astype(BF16)
    o_dk[...] = (_group_rms(dk_ref[...], ones32, DIFF_QK) * gains_ref[2:3, :]).astype(BF16)
    o_dv[...] = dv_ref[...].T.astype(BF16)
    lane = lax.broadcasted_iota(jnp.int32, kvs_ref.shape, 1)
    for kv_ref, gain, o_kv, o_v in ((kvs_ref, gains_ref[3:4, 0:LANE], o_kvs, o_vs),
                                    (kvw_ref, gains_ref[4:5, 0:LANE], o_kvw, o_vw)):
        x = kv_ref[...]
        o_kv[...] = jnp.where(lane < HD, _group_rms(x, ones64h, HD) * gain, x).astype(BF16)
        o_v[...] = x.T[HD:2 * HD, :].astype(BF16)


def _prep(u, gains, tm):
    T = u.shape[0]
    c256 = lambda name: pl.BlockSpec((tm, GROUP), lambda i, c=_B256[name]: (i, c))
    c128 = lambda name: pl.BlockSpec((tm, LANE), lambda i, c=_B128[name]: (i, c))
    const = lambda shape: pl.BlockSpec(shape, lambda i: (0, 0))
    rows = lambda w: pl.BlockSpec((tm, w), lambda i: (i, 0))
    colsT = lambda h: pl.BlockSpec((h, tm), lambda i: (0, i))
    sds = jax.ShapeDtypeStruct
    return pl.pallas_call(
        _prep_kernel,
        grid=(T // tm,),
        in_specs=[c256("nsa_q"), c256("d_q"), c256("d_k"), c256("d_v"), c128("kvslc"), c128("kvwin"),
                  const((8, GROUP)), const((GROUP, GROUP)), const((GROUP, GROUP)), const((LANE, LANE))],
        out_specs=[colsT(GROUP), colsT(GROUP), rows(GROUP), colsT(GROUP),
                   rows(LANE), colsT(HD), rows(LANE), colsT(HD)],
        out_shape=[sds((GROUP, T), BF16), sds((GROUP, T), BF16), sds((T, GROUP), BF16), sds((GROUP, T), BF16),
                   sds((T, LANE), BF16), sds((HD, T), BF16), sds((T, LANE), BF16), sds((HD, T), BF16)],
        compiler_params=_params("parallel"),
        name="attn_prep",
    )(u, u, u, u, u, u, gains, _block_ones(GROUP, HD), _block_ones(GROUP, DIFF_QK), _block_ones(LANE, HD))


def _compress_kernel(xk_ref, xv_ref, pos_ref, w1_ref, w2_ref, gain_ref, ones64h_ref, kc_ref, vc_ref):
    half = NSA_CMP_STRIDE * HD
    n_rows = xk_ref.shape[0]

    def compress(x, j):
        top = (x + pos_ref[j, 0:1, :]).astype(BF16)
        bot = (x + pos_ref[j, 1:2, :]).astype(BF16)
        a = _dot(top, w1_ref[j, 0:half, :])
        b = _dot(bot, w1_ref[j, half:2 * half, :])
        hidden = a + pltpu.roll(b, n_rows - 1, 0)
        return _dot(_silu(hidden).astype(BF16), w2_ref[j])

    kc = compress(xk_ref[...], 0)
    kc_ref[...] = _group_rms(kc, ones64h_ref[...], HD) * gain_ref[...]
    vc_ref[...] = compress(xv_ref[...], 1).T[0:HD, :].astype(BF16)


def _compress(xk, xv, pos, w1_bf, w2p_bf, gain_row):
    B, ncp, half = xk.shape
    sq = pl.Squeezed()
    full = lambda shape: pl.BlockSpec(shape, lambda b: (0,) * len(shape))
    return pl.pallas_call(
        _compress_kernel,
        grid=(B,),
        in_specs=[pl.BlockSpec((sq, ncp, half), lambda b: (b, 0, 0)),
                  pl.BlockSpec((sq, ncp, half), lambda b: (b, 0, 0)),
                  full((2, 2, half)), full((2, 2 * half, NSA_CMP_HIDDEN)), full((2, NSA_CMP_HIDDEN, LANE)),
                  full((1, LANE)), full((LANE, LANE))],
        out_specs=[pl.BlockSpec((sq, ncp, LANE), lambda b: (b, 0, 0)),
                   pl.BlockSpec((sq, HD, ncp), lambda b: (b, 0, 0))],
        out_shape=[jax.ShapeDtypeStruct((B, ncp, LANE), F32), jax.ShapeDtypeStruct((B, HD, ncp), BF16)],
        compiler_params=_params("parallel"),
        name="nsa_compress",
    )(xk, xv, pos, w1_bf, w2p_bf, gain_row, _block_ones(LANE, HD))


def _softmax_step_t(s, v_t, m_old, l_old, acc_ref):
    m_new = jnp.maximum(m_old, jnp.max(s, axis=0, keepdims=True))
    alpha = jnp.exp2(m_old - m_new)
    p = jnp.exp2(s - m_new)
    acc_ref[...] = alpha * acc_ref[...] + _dot(v_t, p.astype(BF16))
    return m_new, alpha * l_old + jnp.sum(p, axis=0, keepdims=True)


def _pipelined_sweep(lo, hi, qk, process, process_last, stats0):
    def body(j, carry):
        stats, scores = carry
        nxt = qk(j + 1)
        return process(j, scores, stats), nxt

    stats, scores = lax.fori_loop(lo, hi - 1, body, (stats0, qk(lo)))
    return process_last(hi - 1, scores, stats)


def _nsa_kernel(qt_ref, g_ref, kc_ref, vct_ref, kvs_ref, vst_ref, kvw_ref, vwt_ref, ovl_ref, exp_ref,
                o_ref, qs_ref, *acc_refs, tq, tk, tw, ksel):
    i = pl.program_id(1)
    t0 = i * tq
    ncp = kc_ref.shape[0]
    nsel = ovl_ref.shape[0]
    acc_s, acc_w = acc_refs[:HEADS], acc_refs[HEADS:]

    qt = qt_ref[...]
    qs_ref[HD:LANE, :] = jnp.zeros((LANE - HD, HEADS * tq), BF16)
    for h in range(HEADS):
        qs_ref[0:HD, h * tq:(h + 1) * tq] = qt[h * HD:(h + 1) * HD, :]
        acc_s[h][...] = jnp.zeros((HD, tq), F32)
        acc_w[h][...] = jnp.zeros((HD, tq), F32)

    kc_hi, kc_lo = _split2(kc_ref[...])
    n_idx = lax.broadcasted_iota(jnp.int32, (ncp, tq), 0)
    t_lane = t0 + lax.broadcasted_iota(jnp.int32, (ncp, tq), 1)
    ok = n_idx * NSA_CMP_STRIDE + (NSA_CMP_LEN - 1) <= t_lane
    cmp_scores = _dot(kc_hi, qs_ref[...]) + _dot(kc_lo, qs_ref[...])
    o_c = []
    psum = jnp.zeros((ncp, tq), F32)
    for h in range(HEADS):
        s = jnp.where(ok, cmp_scores[:, h * tq:(h + 1) * tq], NEG)
        e = jnp.exp2(s - jnp.max(s, axis=0, keepdims=True))
        p = jnp.where(ok, e / jnp.sum(e, axis=0, keepdims=True), 0.0)
        o_c.append(_dot(vct_ref[...], p.astype(BF16)))
        psum = psum + p

    p_hi, p_lo = _split2(psum)
    imp = _dot(ovl_ref[...], p_hi) + _dot(ovl_ref[...], p_lo)
    blk = lax.broadcasted_iota(jnp.int32, (nsel, tq), 0)
    t_col = t0 + lax.broadcasted_iota(jnp.int32, (nsel, tq), 1)
    cur = t_col // NSA_SEL_LEN
    forced = (blk == 0) | (blk == cur) | (blk == cur - 1)
    vals = jnp.where(forced, BIG, jnp.where(blk * NSA_SEL_LEN <= t_col, imp, NEG))
    blk_f = blk.astype(F32)
    sel = jnp.zeros((nsel, tq), F32)
    for _ in range(ksel):
        mx = jnp.max(vals, axis=0, keepdims=True)
        first = jnp.min(jnp.where(vals == mx, blk_f, float(nsel)), axis=0, keepdims=True)
        pick = blk_f == first
        sel = jnp.where(pick, 1.0, sel)
        vals = jnp.where(pick, LOWEST, vals)
    if nsel < LANE:
        sel = jnp.concatenate([sel, jnp.zeros((LANE - nsel, tq), F32)], axis=0)
    sel_bf = sel.astype(BF16)

    stats0 = tuple((jnp.full((1, tq), NEG, F32), jnp.zeros((1, tq), F32)) for _ in range(HEADS))

    def sweep(lo, hi, k_ref, vt_ref, accs, width, bias_fn):
        def body(kt, stats):
            k0 = pl.multiple_of(kt * width, width)
            k = k_ref[pl.ds(k0, width), :]
            scores = [_dot(k, qs_ref[:, h * tq:(h + 1) * tq]) for h in range(HEADS)]
            kpos = k0 + lax.broadcasted_iota(jnp.int32, (width, tq), 0)
            tpos = t0 + lax.broadcasted_iota(jnp.int32, (width, tq), 1)
            bias = bias_fn(k0, kpos, tpos)
            vt = vt_ref[:, pl.ds(k0, width)]
            return tuple(_softmax_step_t(scores[h] + bias, vt, stats[h][0], stats[h][1], accs[h])
                         for h in range(HEADS))

        return lax.fori_loop(lo, hi, body, stats0)

    def sel_bias(k0, kpos, tpos):
        chosen = _dot(exp_ref[pl.ds(k0, kpos.shape[0]), :], sel_bf)
        return jnp.where((chosen > 0.5) & (kpos <= tpos), 0.0, NEG)

    st_s = sweep(0, (t0 + tq - 1) // tk + 1, kvs_ref, vst_ref, acc_s, tk, sel_bias)

    def win_bias(k0, kpos, tpos):
        return jnp.where((kpos <= tpos) & (kpos > tpos - NSA_WINDOW), 0.0, NEG)

    st_w = sweep(jnp.maximum(t0 - NSA_WINDOW, 0) // tw, (t0 + tq - 1) // tw + 1,
                 kvw_ref, vwt_ref, acc_w, tw, win_bias)

    gates = _sigmoid(g_ref[...].T)
    outs = []
    for h in range(HEADS):
        outs.append(gates[3 * h:3 * h + 1, :] * o_c[h]
                    + gates[3 * h + 1:3 * h + 2, :] * (acc_s[h][...] / st_s[h][1])
                    + gates[3 * h + 2:3 * h + 3, :] * (acc_w[h][...] / st_w[h][1]))
    o_ref[...] = jnp.concatenate(outs, axis=0).T


def _nsa_attention(u, nqt, kc, vct, kvs, vst, kvw, vwt, B, S, tq, tk, tw):
    nq = S // tq
    ncp = S // NSA_CMP_STRIDE
    nsel = S // NSA_SEL_LEN
    ksel = min(NSA_TOPK, nsel)
    n_cmp = (S - NSA_CMP_LEN) // NSA_CMP_STRIDE + 1
    cmp_start = np.arange(ncp) * NSA_CMP_STRIDE
    sel_start = np.arange(nsel) * NSA_SEL_LEN
    overlap = ((cmp_start[:, None] <= sel_start[None, :] + NSA_SEL_LEN - 1)
               & (cmp_start[:, None] + NSA_CMP_LEN - 1 >= sel_start[None, :])
               & (np.arange(ncp)[:, None] < n_cmp))
    ovl_t = jnp.asarray(overlap.T.astype(np.float32), BF16)
    expand = (jnp.arange(S, dtype=jnp.int32)[:, None] // NSA_SEL_LEN
              == jnp.arange(LANE, dtype=jnp.int32)[None, :]).astype(BF16)
    sq = pl.Squeezed()
    seq_rows = pl.BlockSpec((S, LANE), lambda b, i: (b, 0))
    seq_cols = pl.BlockSpec((HD, S), lambda b, i: (0, b))
    kern = functools.partial(_nsa_kernel, tq=tq, tk=tk, tw=tw, ksel=ksel)
    return pl.pallas_call(
        kern,
        grid=(B, nq),
        in_specs=[pl.BlockSpec((GROUP, tq), lambda b, i: (0, b * nq + i)),
                  pl.BlockSpec((tq, LANE), lambda b, i, c=_B128["nsa_g"]: (b * nq + i, c)),
                  pl.BlockSpec((sq, ncp, LANE), lambda b, i: (b, 0, 0)),
                  pl.BlockSpec((sq, HD, ncp), lambda b, i: (b, 0, 0)),
                  seq_rows, seq_cols, seq_rows, seq_cols,
                  pl.BlockSpec((nsel, ncp), lambda b, i: (0, 0)),
                  pl.BlockSpec((S, LANE), lambda b, i: (0, 0))],
        out_specs=pl.BlockSpec((tq, GROUP), lambda b, i: (b * nq + i, 0)),
        out_shape=jax.ShapeDtypeStruct((B * S, GROUP), F32),
        scratch_shapes=[pltpu.VMEM((LANE, HEADS * tq), BF16)] + [pltpu.VMEM((HD, tq), F32)] * (2 * HEADS),
        compiler_params=_params("parallel", "arbitrary"),
        name="nsa_attention",
    )(nqt, u, kc, vct, kvs, vst, kvw, vwt, ovl_t, expand)


def _diff_kernel(lam_ref, qt_ref, k_ref, vt_ref, gain_ref, o_ref, qs_ref, *acc_refs, tq, tk, lam_init):
    i = pl.program_id(1)
    t0 = i * tq
    lanes = 2 * tq

    qt = qt_ref[...]
    row = lax.broadcasted_iota(jnp.int32, (GROUP, tq), 0)
    zero = jnp.zeros_like(qt)
    for h in range(HEADS):
        qs_ref[h] = jnp.concatenate([jnp.where(row // DIFF_QK == 2 * h, qt, zero),
                                     jnp.where(row // DIFF_QK == 2 * h + 1, qt, zero)], axis=1)
        acc_refs[h][...] = jnp.zeros((HD, lanes), F32)

    def step(k0, width, masked, stats):
        k = k_ref[pl.ds(k0, width), :]
        scores = [_dot(k, qs_ref[h]) for h in range(HEADS)]
        out = []
        for h in range(HEADS):
            s = scores[h]
            if masked:
                kpos = k0 + lax.broadcasted_iota(jnp.int32, (width, lanes), 0)
                tpos = t0 + lax.broadcasted_iota(jnp.int32, (width, lanes), 1) % tq
                s = jnp.where(kpos <= tpos, s, NEG)
            out.append(_softmax_step_t(s, vt_ref[h * HD:(h + 1) * HD, pl.ds(k0, width)],
                                       stats[h][0], stats[h][1], acc_refs[h]))
        return tuple(out)

    stats0 = tuple((jnp.full((1, lanes), NEG, F32), jnp.zeros((1, lanes), F32)) for _ in range(HEADS))
    n_wide = t0 // tk
    stats = lax.fori_loop(0, n_wide, lambda kt, c: step(pl.multiple_of(kt * tk, tk), tk, False, c), stats0)
    stats = lax.fori_loop(n_wide * (tk // tq), t0 // tq,
                          lambda kt, c: step(pl.multiple_of(kt * tq, tq), tq, False, c), stats)
    stats = step(pl.multiple_of(t0, tq), tq, True, stats)

    lam = lam_ref[...]
    lam_full = (jnp.exp(jnp.sum(lam[0:1] * lam[1:2], axis=1, keepdims=True))
                - jnp.exp(jnp.sum(lam[2:3] * lam[3:4], axis=1, keepdims=True)) + lam_init)
    outs = []
    for h in range(HEADS):
        r = acc_refs[h][...] / stats[h][1]
        d = r[:, 0:tq] - lam_full * r[:, tq:lanes]
        d = d * lax.rsqrt(jnp.mean(d * d, axis=0, keepdims=True) + EPS)
        outs.append(d * gain_ref[...] * (1.0 - lam_init))
    o_ref[...] = jnp.concatenate(outs, axis=0).T


def _diff_attention(dqt, dk, dvt, lam, gain_col, B, S, tq, tk, layer_idx):
    nq = S // tq
    lam_init = 0.8 - 0.6 * math.exp(-0.3 * layer_idx)
    kern = functools.partial(_diff_kernel, tq=tq, tk=tk, lam_init=lam_init)
    return pl.pallas_call(
        kern,
        grid=(B, nq),
        in_specs=[pl.BlockSpec((4, DIFF_QK), lambda b, i: (0, 0)),
                  pl.BlockSpec((GROUP, tq), lambda b, i: (0, b * nq + i)),
                  pl.BlockSpec((S, GROUP), lambda b, i: (b, 0)),
                  pl.BlockSpec((GROUP, S), lambda b, i: (0, b)),
                  pl.BlockSpec((HD, tq), lambda b, i: (0, 0))],
        out_specs=pl.BlockSpec((tq, GROUP), lambda b, i: (b * nq + i, 0)),
        out_shape=jax.ShapeDtypeStruct((B * S, GROUP), F32),
        scratch_shapes=[pltpu.VMEM((HEADS, GROUP, 2 * tq), BF16)] + [pltpu.VMEM((HD, 2 * tq), F32)] * HEADS,
        compiler_params=_params("parallel", "arbitrary"),
        name="diff_attention",
    )(lam, dqt, dk, dvt, gain_col)


_LEVELS = (32, 16, 8, 4, 2, 1)


def _stack_heads(x, lane, group, count):
    zero = jnp.zeros_like(x)
    return jnp.concatenate([jnp.where(lane // group == g, x, zero) for g in range(count)], axis=0)


def _unstack_heads(x4, lane, rows):
    out = jnp.where(lane // HD == 0, x4[0:rows], 0.0)
    for h in range(1, HEADS):
        out = out + jnp.where(lane // HD == h, x4[h * rows:(h + 1) * rows], 0.0)
    return out


def _linear_consts():
    r = np.arange(CHUNK)[:, None]
    t = np.arange(CHUNK)[None, :]
    tri = (t <= r).astype(np.float32)
    masks = [(r // (2 * s) == t // (2 * s)) for s in _LEVELS] + [r == t]
    mall = np.stack([np.tile(m.astype(np.float32), (1, HEADS)) for m in masks])
    return jnp.asarray(tri, BF16), jnp.asarray(mall, F32)


def _level_exponent(s, lg, b, row):
    if s == 1:
        return jnp.where((row & 1) != 0, 0.0, pltpu.roll(lg, CHUNK - 1, 0))
    if s == 2:
        nxt1 = pltpu.roll(lg, CHUNK - 1, 0)
        nxt2 = pltpu.roll(lg, CHUNK - 2, 0)
        r4 = row & 3
        return jnp.where(r4 == 0, nxt1 + nxt2, jnp.where(r4 == 1, nxt1, jnp.where(r4 == 2, 0.0, lg)))
    mids = [jnp.broadcast_to(b[m:m + 1, :], (2 * s, b.shape[1])) for m in range(s, CHUNK, 2 * s)]
    d = b - (jnp.concatenate(mids, axis=0) if len(mids) > 1 else mids[0])
    return jnp.where((row & s) != 0, d, -d)


def _linear_chunk(q, k, v, lg, tri, mall_ref, state_ref, dk):
    dkh = HEADS * dk
    hi, mid, lo = _split3(lg)
    b = _dot(tri, hi) + _dot(tri, mid) + _dot(tri, lo)
    row = lax.broadcasted_iota(jnp.int32, (CHUNK, dkh), 0)
    lane_k = lax.broadcasted_iota(jnp.int32, (CHUNK, dkh), 1)
    lane_v = lax.broadcasted_iota(jnp.int32, (CHUNK, GROUP), 1)
    a_t = mall_ref[len(_LEVELS)] * _dot_nt(k.astype(BF16), _stack_heads(q.astype(BF16), lane_k, dk, HEADS))
    for li, s in enumerate(_LEVELS):
        e = jnp.exp(_level_exponent(s, lg, b, row))
        upper = (row & s) != 0
        qt = jnp.where(upper, q * e, 0.0).astype(BF16)
        kt = jnp.where(upper, 0.0, k * e).astype(BF16)
        a_t = a_t + mall_ref[li] * _dot_nt(kt, _stack_heads(qt, lane_k, dk, HEADS))
    o_intra = _unstack_heads(_dot_tn(a_t.astype(BF16), v.astype(BF16)), lane_v, CHUNK)
    e_b = jnp.exp(b)
    e_u = jnp.exp(b[CHUNK - 1:CHUNK, :] - b)
    st = state_ref[...]
    o_inter = _dot_nt((q * e_b).astype(BF16), st.astype(BF16))
    kv = _dot_tn(v.astype(BF16), (k * e_u).astype(BF16))
    srow = lax.broadcasted_iota(jnp.int32, (GROUP, dkh), 0)
    scol = lax.broadcasted_iota(jnp.int32, (GROUP, dkh), 1)
    state_ref[...] = st * e_b[CHUNK - 1:CHUNK, :] + jnp.where(srow // HD == scol // dk, kv, 0.0)
    return o_inter + o_intra


def _gla_kernel(q_ref, k_ref, v_ref, lr_ref, og_ref, w2_ref, b_ref, gain_ref, tri_ref, mall_ref,
                ones64_ref, o_ref, state_ref, *, tm):
    @pl.when(pl.program_id(1) == 0)
    def _():
        state_ref[...] = jnp.zeros_like(state_ref)

    tri = tri_ref[...]
    ones64 = ones64_ref[...]
    w_hi, w_lo = _split2(w2_ref[...])

    def body(c, carry):
        r0 = pl.multiple_of(c * CHUNK, CHUNK)
        rs = pl.ds(r0, CHUNK)
        lr_hi, lr_lo = _split2(lr_ref[rs, :])
        x = _dot(lr_hi, w_hi) + _dot(lr_lo, w_hi) + _dot(lr_hi, w_lo) + b_ref[...]
        lg = (jnp.minimum(x, 0.0) - jnp.log(1.0 + jnp.exp(-jnp.abs(x)))) * (1.0 / GLA_TAU)
        o = _linear_chunk(q_ref[rs, :] * (GLA_DK ** -0.5), k_ref[rs, :], v_ref[rs, :], lg,
                          tri, mall_ref, state_ref, GLA_DK)
        o_ref[rs, :] = _group_rms(o, ones64, HD) * gain_ref[...] * _silu(og_ref[rs, :])
        return carry

    lax.fori_loop(0, tm // CHUNK, body, 0, unroll=4)


def _hgrn_kernel(q_ref, f_ref, i_ref, og_ref, lbl_ref, gain_ref, tri_ref, mall_ref, ones64_ref,
                 o_ref, state_ref, *, tm, layer_idx):
    @pl.when(pl.program_id(1) == 0)
    def _():
        state_ref[...] = jnp.zeros_like(state_ref)

    tri = tri_ref[...]
    ones64 = ones64_ref[...]
    logits = lbl_ref[...]
    ez = jnp.exp(logits - jnp.max(logits, axis=0, keepdims=True))
    probs = ez / jnp.sum(ez, axis=0, keepdims=True)
    lb = jnp.zeros((1, GROUP), F32)
    for j in range(1, layer_idx + 1):
        lb = lb + probs[j:j + 1]

    def body(c, carry):
        r0 = pl.multiple_of(c * CHUNK, CHUNK)
        rs = pl.ds(r0, CHUNK)
        z = f_ref[rs, :]
        f = lb + (1.0 - lb) * _sigmoid(z)
        k = (1.0 - lb) * _sigmoid(-z)
        o = _linear_chunk(q_ref[rs, :], k, i_ref[rs, :], jnp.log(f), tri, mall_ref, state_ref, HD)
        o_ref[rs, :] = _group_rms(o, ones64, HD) * gain_ref[...] * _silu(og_ref[rs, :])
        return carry

    lax.fori_loop(0, tm // CHUNK, body, 0, unroll=4)


def _linear_mixers(u, w2pad, b_gate, gla_gain, lb_logits, hgrn_gain, B, S, tm, layer_idx):
    nt = S // tm
    tri, mall = _linear_consts()
    ones64 = _block_ones(GROUP, HD)
    c256 = lambda name: pl.BlockSpec((tm, GROUP), lambda b, i, c=_B256[name]: (b * nt + i, c))
    c128 = lambda name: pl.BlockSpec((tm, LANE), lambda b, i, c=_B128[name]: (b * nt + i, c))
    full = lambda shape: pl.BlockSpec(shape, lambda b, i: (0,) * len(shape))
    out_spec = pl.BlockSpec((tm, GROUP), lambda b, i: (b * nt + i, 0))
    out_shape = jax.ShapeDtypeStruct((B * S, GROUP), F32)
    consts = [full(tri.shape), full(mall.shape), full((GROUP, GROUP))]
    o_gla = pl.pallas_call(
        functools.partial(_gla_kernel, tm=tm),
        grid=(B, nt),
        in_specs=[c128("g_q"), c128("g_k"), c256("g_v"), c128("g_lr"), c256("g_og"),
                  full((LANE, LANE)), full((1, LANE)), full((1, GROUP))] + consts,
        out_specs=out_spec, out_shape=out_shape,
        scratch_shapes=[pltpu.VMEM((GROUP, HEADS * GLA_DK), F32)],
        compiler_params=_params("parallel", "arbitrary"),
        name="gla_mixer",
    )(u, u, u, u, u, w2pad, b_gate, gla_gain, tri, mall, ones64)
    depth = lb_logits.shape[0]
    o_hgrn = pl.pallas_call(
        functools.partial(_hgrn_kernel, tm=tm, layer_idx=layer_idx),
        grid=(B, nt),
        in_specs=[c256("r_q"), c256("r_f"), c256("r_i"), c256("r_og"),
                  full((depth, GROUP)), full((1, GROUP))] + consts,
        out_specs=out_spec, out_shape=out_shape,
        scratch_shapes=[pltpu.VMEM((GROUP, GROUP), F32)],
        compiler_params=_params("parallel", "arbitrary"),
        name="hgrn_mixer",
    )(u, u, u, u, lb_logits, hgrn_gain, tri, mall, ones64)
    return o_gla, o_hgrn


def _out_proj_kernel(h_ref, a_ref, b_ref, c_ref, d_ref, w_ref, o_ref):
    acc = h_ref[...]
    for j, r in enumerate((a_ref, b_ref, c_ref, d_ref)):
        acc = acc + _dot(r[...].astype(BF16), w_ref[j * GROUP:(j + 1) * GROUP, :])
    o_ref[...] = acc


def _out_proj(h, parts, w_bf, tm):
    T = h.shape[0]
    part = pl.BlockSpec((tm, GROUP), lambda i: (i, 0))
    return pl.pallas_call(
        _out_proj_kernel,
        grid=(T // tm,),
        in_specs=[pl.BlockSpec((tm, D_MODEL), lambda i: (i, 0)), part, part, part, part,
                  pl.BlockSpec((D_MODEL, D_MODEL), lambda i: (0, 0))],
        out_specs=pl.BlockSpec((tm, D_MODEL), lambda i: (i, 0)),
        out_shape=jax.ShapeDtypeStruct((T, D_MODEL), F32),
        compiler_params=_params("parallel"),
        name="out_proj",
    )(h, *parts, w_bf)


def _ffn_kernel(h_ref, g_ref, wg_ref, wu_ref, wd_ref, o_ref, xn_ref, acc_ref):
    f = pl.program_id(1)

    @pl.when(f == 0)
    def _():
        x = h_ref[...]
        y = x * lax.rsqrt(jnp.mean(x * x, axis=-1, keepdims=True) + EPS) * g_ref[...]
        xn_ref[...] = y.astype(BF16)
        acc_ref[...] = x

    xn = xn_ref[...]
    mid = _silu(_dot(xn, wg_ref[...])) * _dot(xn, wu_ref[...])
    acc_ref[...] += _dot(mid.astype(BF16), wd_ref[...])

    @pl.when(f == pl.num_programs(1) - 1)
    def _():
        o_ref[...] = acc_ref[...]


def _ffn(h, gain, wg, wu, wd, tm, tf):
    T = h.shape[0]
    F = wg.shape[1]
    return pl.pallas_call(
        _ffn_kernel,
        grid=(T // tm, F // tf),
        in_specs=[pl.BlockSpec((tm, D_MODEL), lambda i, f: (i, 0)),
                  pl.BlockSpec((1, D_MODEL), lambda i, f: (0, 0)),
                  pl.BlockSpec((D_MODEL, tf), lambda i, f: (0, f)),
                  pl.BlockSpec((D_MODEL, tf), lambda i, f: (0, f)),
                  pl.BlockSpec((tf, D_MODEL), lambda i, f: (f, 0))],
        out_specs=pl.BlockSpec((tm, D_MODEL), lambda i, f: (i, 0)),
        out_shape=jax.ShapeDtypeStruct((T, D_MODEL), F32),
        scratch_shapes=[pltpu.VMEM((tm, D_MODEL), BF16), pltpu.VMEM((tm, D_MODEL), F32)],
        compiler_params=_params("parallel", "arbitrary"),
        name="ffn_swiglu",
    )(h, gain.reshape(1, D_MODEL), wg, wu, wd)


def _router_kernel(h_ref, g_ref, r_ref, lower_ref, c_ref, comb_ref, rk_ref, cnt_ref):
    x = h_ref[...]
    y = x * lax.rsqrt(jnp.mean(x * x, axis=-1, keepdims=True) + EPS) * g_ref[...]
    c_ref[...] = y.astype(BF16)
    y_hi, y_lo = _split2(y)
    r_hi, r_lo = _split2(r_ref[...])
    logits = _dot(y_hi, r_hi) + _dot(y_lo, r_hi) + _dot(y_hi, r_lo)
    lane = lax.broadcasted_iota(jnp.int32, logits.shape, 1)
    lane_f = lane.astype(F32)
    logits = jnp.where(lane < N_EXPERTS, logits, LOWEST)
    m1 = jnp.max(logits, axis=1, keepdims=True)
    i1 = jnp.min(jnp.where(logits == m1, lane_f, float(LANE)), axis=1, keepdims=True)
    rest = jnp.where(lane_f == i1, LOWEST, logits)
    m2 = jnp.max(rest, axis=1, keepdims=True)
    i2 = jnp.min(jnp.where(rest == m2, lane_f, float(LANE)), axis=1, keepdims=True)
    e2 = jnp.exp(m2 - m1)
    w1 = 1.0 / (1.0 + e2)
    comb_ref[...] = jnp.where(lane_f == i1, w1, jnp.where(lane_f == i2, e2 * w1, 0.0))
    chosen = (lane_f == i1) | (lane_f == i2)
    sel = jnp.where(chosen, 1.0, 0.0)
    rank = _dot(lower_ref[...], sel.astype(BF16))
    rk_ref[...] = jnp.where(chosen, rank, -1.0).astype(jnp.int32)
    cnt_ref[...] = jnp.broadcast_to(jnp.sum(sel, axis=0, keepdims=True), cnt_ref.shape).astype(jnp.int32)


def _router(h, gain, router_pad, tm):
    T = h.shape[0]
    i = np.arange(tm)
    lower = jnp.asarray((i[None, :] < i[:, None]).astype(np.float32), BF16)
    sds = jax.ShapeDtypeStruct
    return pl.pallas_call(
        _router_kernel,
        grid=(T // tm,),
        in_specs=[pl.BlockSpec((tm, D_MODEL), lambda i: (i, 0)),
                  pl.BlockSpec((1, D_MODEL), lambda i: (0, 0)),
                  pl.BlockSpec((D_MODEL, LANE), lambda i: (0, 0)),
                  pl.BlockSpec((tm, tm), lambda i: (0, 0))],
        out_specs=[pl.BlockSpec((tm, D_MODEL), lambda i: (i, 0)), pl.BlockSpec((tm, LANE), lambda i: (i, 0)),
                   pl.BlockSpec((tm, LANE), lambda i: (i, 0)), pl.BlockSpec((8, LANE), lambda i: (i, 0))],
        out_shape=[sds((T, D_MODEL), BF16), sds((T, LANE), F32), sds((T, LANE), jnp.int32),
                   sds((T // tm * 8, LANE), jnp.int32)],
        compiler_params=_params("parallel"),
        name="moe_router",
    )(h, gain.reshape(1, D_MODEL), router_pad, lower)


MOE_CHUNK = 512
MOE_ROWS = 512
MOE_HALF = MOE_ROWS // 2
MOE_TILE = 1024
MOE_ALIGN = 16
MOE_PAD_SEGS = MOE_TILE // MOE_ROWS


def _moe_plan(cnt, T):
    nch = T // MOE_CHUNK
    n_ce = cnt.reshape(nch, 8, LANE)[:, 0, :N_EXPERTS]
    cap = (n_ce + MOE_ALIGN - 1) // MOE_ALIGN * MOE_ALIGN
    tot = jnp.sum(cap, axis=0)
    ptot = (tot + MOE_TILE - 1) // MOE_TILE * MOE_TILE
    start = jnp.cumsum(ptot) - ptot
    lo_ce = start[None, :] + jnp.cumsum(cap, axis=0) - cap
    pad_lo = (start + tot)[:, None] + MOE_ROWS * jnp.arange(MOE_PAD_SEGS, dtype=jnp.int32)[None, :]
    lo = jnp.concatenate([lo_ce.T, pad_lo], axis=1)
    n = jnp.concatenate([n_ce.T, jnp.full((N_EXPERTS, MOE_PAD_SEGS), MOE_ROWS, jnp.int32)], axis=1)
    n_tiles = _moe_tiles(T)
    tile_end = jnp.cumsum(ptot // MOE_TILE)
    j = jnp.arange(n_tiles, dtype=jnp.int32)
    tile_e = jnp.minimum(jnp.sum((tile_end[None, :] <= j[:, None]).astype(jnp.int32), axis=1), N_EXPERTS - 1)
    valid = (j < tile_end[-1]).astype(jnp.int32)
    n_flat = jnp.concatenate([n.reshape(-1), tile_end[-1:] * MOE_PAD_SEGS]).astype(jnp.int32)
    return lo.reshape(-1).astype(jnp.int32), n_flat, tile_e, valid


def _moe_tiles(T):
    nch = T // MOE_CHUNK
    rows = 2 * T + nch * N_EXPERTS * (MOE_ALIGN - 1) + N_EXPERTS * (MOE_TILE - 1)
    return -(-rows // MOE_TILE) + 1


def _moe_gather_kernel(lo_ref, n_ref, c_ref, rkt_ref, x_hbm, buf, sem, *, nch):
    e = pl.program_id(0)
    c = pl.program_id(1)
    n_steps = pl.num_programs(0) * pl.num_programs(1)
    step = e * pl.num_programs(1) + c
    slot = step % 2
    rk_row = rkt_ref[pl.ds(e, 1), :]
    rk_row = jnp.where(c < nch, rk_row, -1)
    row = lax.broadcasted_iota(jnp.int32, (MOE_HALF, MOE_CHUNK), 0)
    chunk = c_ref[...]
    onehot = jnp.where(rk_row == row, 1.0, 0.0).astype(BF16)
    buf[slot, 0:MOE_HALF, :] = _dot(onehot, chunk).astype(BF16)

    @pl.when(n_ref[step] > MOE_HALF)
    def _():
        onehot = jnp.where(rk_row == row + MOE_HALF, 1.0, 0.0).astype(BF16)
        buf[slot, MOE_HALF:MOE_ROWS, :] = _dot(onehot, chunk).astype(BF16)

    def copy(k, s, rows):
        dst = x_hbm.at[pl.ds(pl.multiple_of(lo_ref[k], MOE_ALIGN), rows)]
        return pltpu.make_async_copy(buf.at[s, 0:rows], dst, sem.at[s])

    def for_size(k, fn):
        @pl.when(n_ref[k] > MOE_HALF)
        def _():
            fn(MOE_ROWS)

        @pl.when(n_ref[k] <= MOE_HALF)
        def _():
            fn(MOE_HALF)

    @pl.when(step > 0)
    def _():
        for_size(step - 1, lambda rows: copy(step - 1, 1 - slot, rows).wait())

    for_size(step, lambda rows: copy(step, slot, rows).start())

    @pl.when(step == n_steps - 1)
    def _():
        for_size(step, lambda rows: copy(step, slot, rows).wait())
        buf[0] = jnp.zeros((MOE_ROWS, D_MODEL), BF16)

        def fill(k, carry):
            dst = x_hbm.at[pl.ds(pl.multiple_of(k * MOE_ROWS, MOE_ROWS), MOE_ROWS)]
            cp = pltpu.make_async_copy(buf.at[0], dst, sem.at[0])
            cp.start()
            cp.wait()
            return carry

        lax.fori_loop(n_ref[n_steps], x_hbm.shape[0] // MOE_ROWS, fill, 0)


def _moe_gather(c_bf, rkt, lo, n, n_tiles):
    T = c_bf.shape[0]
    nch = T // MOE_CHUNK
    last = nch - 1
    grid_spec = pltpu.PrefetchScalarGridSpec(
        num_scalar_prefetch=2,
        grid=(N_EXPERTS, nch + MOE_PAD_SEGS),
        in_specs=[pl.BlockSpec((MOE_CHUNK, D_MODEL), lambda e, c, lo, n: (jnp.minimum(c, last), 0)),
                  pl.BlockSpec((N_EXPERTS, MOE_CHUNK), lambda e, c, lo, n: (0, jnp.minimum(c, last)))],
        out_specs=pl.BlockSpec(memory_space=pl.ANY),
        scratch_shapes=[pltpu.VMEM((2, MOE_ROWS, D_MODEL), BF16), pltpu.SemaphoreType.DMA((2,))],
    )
    return pl.pallas_call(
        functools.partial(_moe_gather_kernel, nch=nch),
        grid_spec=grid_spec,
        out_shape=jax.ShapeDtypeStruct((n_tiles * MOE_TILE, D_MODEL), BF16),
        compiler_params=_params("arbitrary", "arbitrary"),
        name="moe_gather",
    )(lo, n, c_bf, rkt)


def _moe_ffn_kernel(te_ref, valid_ref, x_ref, wg_ref, wu_ref, wd_ref, y_ref, acc_ref):
    j = pl.program_id(0)
    f = pl.program_id(1)
    last = pl.num_programs(1) - 1

    @pl.when(valid_ref[j] == 1)
    def _():
        x = x_ref[...]
        mid = _silu(_dot(x, wg_ref[...])) * _dot(x, wu_ref[...])
        part = _dot(mid.astype(BF16), wd_ref[...])

        @pl.when(f == 0)
        def _():
            acc_ref[...] = part

        @pl.when(f > 0)
        def _():
            acc_ref[...] += part

        @pl.when(f == last)
        def _():
            y_ref[...] = acc_ref[...].astype(BF16)

    @pl.when((valid_ref[j] == 0) & (f == last))
    def _():
        y_ref[...] = jnp.zeros(y_ref.shape, BF16)


def _moe_ffn(x_sorted, tile_e, valid, wg, wu, wd, tf):
    n_tiles = x_sorted.shape[0] // MOE_TILE
    F = wg.shape[2]
    nf = F // tf
    sq = pl.Squeezed()
    fsel = lambda j, f, te, va: jnp.where(va[j] == 1, f, nf - 1)
    grid_spec = pltpu.PrefetchScalarGridSpec(
        num_scalar_prefetch=2,
        grid=(n_tiles, nf),
        in_specs=[pl.BlockSpec((MOE_TILE, D_MODEL), lambda j, f, te, va: (j, 0)),
                  pl.BlockSpec((sq, D_MODEL, tf), lambda j, f, te, va: (te[j], 0, fsel(j, f, te, va))),
                  pl.BlockSpec((sq, D_MODEL, tf), lambda j, f, te, va: (te[j], 0, fsel(j, f, te, va))),
                  pl.BlockSpec((sq, tf, D_MODEL), lambda j, f, te, va: (te[j], fsel(j, f, te, va), 0))],
        out_specs=pl.BlockSpec((MOE_TILE, D_MODEL), lambda j, f, te, va: (j, 0)),
        scratch_shapes=[pltpu.VMEM((MOE_TILE, D_MODEL), F32)],
    )
    return pl.pallas_call(
        _moe_ffn_kernel,
        grid_spec=grid_spec,
        out_shape=jax.ShapeDtypeStruct((n_tiles * MOE_TILE, D_MODEL), BF16),
        compiler_params=_params("parallel", "arbitrary"),
        name="moe_experts",
    )(tile_e, valid, x_sorted, wg, wu, wd)


def _moe_combine_kernel(lo_ref, n_ref, h_ref, rk_ref, comb_ref, y_hbm, o_ref, ybuf, sem, *, nch):
    c = pl.program_id(0)
    seg = lambda e: e * (nch + MOE_PAD_SEGS) + c

    def copy(e, rows):
        src = y_hbm.at[pl.ds(pl.multiple_of(lo_ref[seg(e)], MOE_ALIGN), rows)]
        return pltpu.make_async_copy(src, ybuf.at[e, 0:rows], sem.at[e])

    def for_size(e, fn):
        @pl.when(n_ref[seg(e)] > MOE_HALF)
        def _():
            fn(MOE_ROWS)

        @pl.when(n_ref[seg(e)] <= MOE_HALF)
        def _():
            fn(MOE_HALF)

    for e in range(N_EXPERTS):
        for_size(e, lambda rows, e=e: copy(e, rows).start())
    o_ref[...] = h_ref[...]
    rk = rk_ref[...]
    comb = comb_ref[...]
    lane = lax.broadcasted_iota(jnp.int32, (MOE_CHUNK, MOE_HALF), 1)
    for e in range(N_EXPERTS):
        for_size(e, lambda rows, e=e: copy(e, rows).wait())
        rank_col = rk[:, e:e + 1]
        w_col = comb[:, e:e + 1]
        onehot = jnp.where(rank_col == lane, 1.0, 0.0).astype(BF16)
        o_ref[...] += w_col * _dot(onehot, ybuf[e, 0:MOE_HALF, :])

        @pl.when(n_ref[seg(e)] > MOE_HALF)
        def _():
            onehot = jnp.where(rank_col == lane + MOE_HALF, 1.0, 0.0).astype(BF16)
            o_ref[...] += w_col * _dot(onehot, ybuf[e, MOE_HALF:MOE_ROWS, :])


def _moe_combine(h, rk_pad, comb, y_sorted, lo, n):
    T = h.shape[0]
    nch = T // MOE_CHUNK
    grid_spec = pltpu.PrefetchScalarGridSpec(
        num_scalar_prefetch=2,
        grid=(nch,),
        in_specs=[pl.BlockSpec((MOE_CHUNK, D_MODEL), lambda c, lo, n: (c, 0)),
                  pl.BlockSpec((MOE_CHUNK, LANE), lambda c, lo, n: (c, 0)),
                  pl.BlockSpec((MOE_CHUNK, LANE), lambda c, lo, n: (c, 0)),
                  pl.BlockSpec(memory_space=pl.ANY)],
        out_specs=pl.BlockSpec((MOE_CHUNK, D_MODEL), lambda c, lo, n: (c, 0)),
        scratch_shapes=[pltpu.VMEM((N_EXPERTS, MOE_ROWS, D_MODEL), BF16), pltpu.SemaphoreType.DMA((N_EXPERTS,))],
    )
    return pl.pallas_call(
        functools.partial(_moe_combine_kernel, nch=nch),
        grid_spec=grid_spec,
        out_shape=jax.ShapeDtypeStruct((T, D_MODEL), F32),
        compiler_params=_params("arbitrary"),
        name="moe_combine",
    )(lo, n, h, rk_pad, comb, y_sorted)


def _moe(h, c_bf, comb, rk, cnt, wg, wu, wd, tf):
    T = h.shape[0]
    lo, n, tile_e, valid = _moe_plan(cnt, T)
    x_sorted = _moe_gather(c_bf, rk[:, :N_EXPERTS].T, lo, n, _moe_tiles(T))
    y_sorted = _moe_ffn(x_sorted, tile_e, valid, wg, wu, wd, tf)
    return _moe_combine(h, rk, comb, y_sorted, lo, n)


def _ple_kernel(h_ref, p_ref, g_ref, wg_ref, wp_ref, o_ref):
    x = h_ref[...]
    y = x * lax.rsqrt(jnp.mean(x * x, axis=-1, keepdims=True) + EPS) * g_ref[...]
    gate = _sigmoid(_dot(y.astype(BF16), wg_ref[...]))
    o_ref[...] = x + _dot(p_ref[...].astype(BF16), wp_ref[...]) * gate


def _ple(h, p, gain, wg, wp, tm):
    T = h.shape[0]
    return pl.pallas_call(
        _ple_kernel,
        grid=(T // tm,),
        in_specs=[pl.BlockSpec((tm, D_MODEL), lambda i: (i, 0)),
                  pl.BlockSpec((tm, PLE_DIM), lambda i: (i, 0)),
                  pl.BlockSpec((1, D_MODEL), lambda i: (0, 0)),
                  pl.BlockSpec((D_MODEL, D_MODEL), lambda i: (0, 0)),
                  pl.BlockSpec((PLE_DIM, D_MODEL), lambda i: (0, 0))],
        out_specs=pl.BlockSpec((tm, D_MODEL), lambda i: (i, 0)),
        out_shape=jax.ShapeDtypeStruct((T, D_MODEL), F32),
        compiler_params=_params("parallel"),
        name="ple_gate",
    )(h, p, gain.reshape(1, D_MODEL), wg, wp)


def _tiles(T, S):
    pick = lambda n, pref: max(t for t in pref if n % t == 0)
    return dict(
        proj_m=pick(T, (1024, 512, 256, 128)), proj_n=NC // 3,
        prep_m=pick(T, (512, 256, 128)),
        attn_q=256, attn_k=pick(S, (512, 256)),
        nsa_q=pick(S, (512, 256)), nsa_k=pick(S, (512, 256)),
        lin_m=pick(S, (512, 256, 128, 64)),
        row_m=pick(T, (512, 256, 128)),
        ffn_m=pick(T, (1024, 512, 256, 128)), ffn_f=512,
    )


def kernel(x, p, norm_attn, w_in, w_out, nsa_cmp_pos, nsa_cmp_w1, nsa_cmp_w2, nsa_qk_gain, diff_qk_gain, diff_lambda, diff_norm, gla_w_gate2, gla_b_gate, gla_norm, hgrn_lb_logits, hgrn_norm, norm_ffn, ffn_w_gate, ffn_w_up, ffn_w_down, moe_router, moe_w_gate, moe_w_up, moe_w_down, ple_norm, ple_w_gate, ple_w_proj):
    B, S, _ = x.shape
    depth = w_in.shape[0]
    T = B * S
    t = _tiles(T, S)
    ncp = S // NSA_CMP_STRIDE
    half = NSA_CMP_STRIDE * HD
    cols = jnp.asarray(np.maximum(_COLS, 0), jnp.int32)
    col_mask = jnp.asarray(_COLS >= 0)
    ones_row = jnp.ones((GROUP,), F32)

    h = x.reshape(T, D_MODEL)
    for i in range(depth):
        w_in_r = jnp.where(col_mask[None, :], jnp.take(w_in[i], cols, axis=1), 0.0).astype(BF16)
        gains = jnp.stack([
            jnp.tile(nsa_qk_gain[i, 0], HEADS) * (HD ** -0.5 * LOG2E),
            jnp.tile(diff_qk_gain[i, 0], 2 * HEADS) * (DIFF_QK ** -0.5 * LOG2E),
            jnp.tile(diff_qk_gain[i, 1], 2 * HEADS),
            jnp.tile(nsa_qk_gain[i, 2], HEADS),
            jnp.tile(nsa_qk_gain[i, 3], HEADS),
            ones_row, ones_row, ones_row])
        pos = nsa_cmp_pos[i].reshape(2, 2, half)
        w1 = nsa_cmp_w1[i].astype(BF16)
        w2p = jnp.pad(nsa_cmp_w2[i], ((0, 0), (0, 0), (0, LANE - HD))).astype(BF16)
        kc_gain = jnp.pad(nsa_qk_gain[i, 1], (0, LANE - HD)).reshape(1, LANE)
        w2pad = jnp.zeros((LANE, LANE), F32).at[:GLA_RANK].set(gla_w_gate2[i])
        diff_gain_col = jnp.broadcast_to(diff_norm[i][:, None], (HD, t["attn_q"]))

        u = _norm_matmul(h, norm_attn[i], w_in_r, t["proj_m"], t["proj_n"])
        nqt, dqt, dk, dvt, kvs, vst, kvw, vwt = _prep(u, gains, t["prep_m"])
        kv = u[:, _B128["kvcmp"] * LANE:(_B128["kvcmp"] + 1) * LANE]
        xk = kv[:, :HD].reshape(B, ncp, half)
        xv = kv[:, HD:].reshape(B, ncp, half)
        kc, vct = _compress(xk, xv, pos, w1, w2p, kc_gain)
        o_a = _nsa_attention(u, nqt, kc, vct, kvs, vst, kvw, vwt, B, S, t["nsa_q"], t["nsa_k"], t["nsa_q"])
        o_b = _diff_attention(dqt, dk, dvt, diff_lambda[i], diff_gain_col, B, S, t["attn_q"], t["attn_k"], i)
        o_c, o_d = _linear_mixers(u, w2pad, gla_b_gate[i].reshape(1, LANE),
                                  jnp.tile(gla_norm[i], HEADS).reshape(1, GROUP), hgrn_lb_logits,
                                  jnp.tile(hgrn_norm[i], HEADS).reshape(1, GROUP), B, S, t["lin_m"], i)
        h = _out_proj(h, (o_a, o_b, o_c, o_d), w_out[i].astype(BF16), t["row_m"])
        if i % 2 == 0:
            j = i // 2
            h = _ffn(h, norm_ffn[i], ffn_w_gate[j].astype(BF16), ffn_w_up[j].astype(BF16),
                     ffn_w_down[j].astype(BF16), t["ffn_m"], t["ffn_f"])
        else:
            j = i // 2
            router_pad = jnp.zeros((D_MODEL, LANE), F32).at[:, :N_EXPERTS].set(moe_router[j])
            c_bf, comb, rk, cnt = _router(h, norm_ffn[i], router_pad, MOE_CHUNK)
            h = _moe(h, c_bf, comb, rk, cnt, moe_w_gate[j].astype(BF16), moe_w_up[j].astype(BF16),
                     moe_w_down[j].astype(BF16), t["ffn_f"])
        h = _ple(h, p[i].reshape(T, PLE_DIM), ple_norm[i], ple_w_gate[i].astype(BF16),
                 ple_w_proj[i].astype(BF16), t["row_m"])
    return h.reshape(B, S, D_MODEL)
```

```python
import functools
import math

import numpy as np
import jax
import jax.numpy as jnp
from jax import lax
from jax.experimental import pallas as pl
from jax.experimental.pallas import tpu as pltpu

F32 = jnp.float32
BF16 = jnp.bfloat16

D_MODEL = 1024
HEADS = 4
HD = 64
GROUP = HEADS * HD
NSA_CMP_LEN = 32
NSA_CMP_STRIDE = 16
NSA_CMP_HIDDEN = 4 * HD
NSA_SEL_LEN = 64
NSA_TOPK = 16
NSA_WINDOW = 512
NSA_SUBTILES = 2
DIFF_QK = HD // 2
GLA_DK = HD // 2
GLA_RANK = 16
GLA_TAU = 16.0
CHUNK = 64
D_FF = 7 * D_MODEL // 2
N_EXPERTS = 8
PLE_DIM = 256
EPS = 1e-6
NEG = -1e30
BIG = 1e30
LOWEST = -3.0e38
LOG2E = 1.4426950408889634

VMEM_LIMIT = 52 * 1024 * 1024
LANE = 128

_SRC = dict(nsa_q=0, k_cmp=256, v_cmp=320, k_slc=384, v_slc=448, k_win=512, v_win=576, nsa_g=640,
            d_q=652, d_k=908, d_v=1164, g_q=1420, g_k=1548, g_v=1676, g_lr=1932, g_og=1948,
            r_q=2204, r_f=2460, r_i=2716, r_og=2972)
IN_COLS = 3228

_B256 = dict(nsa_q=0, d_q=1, d_k=2, d_v=3, g_v=4, g_og=5, r_q=6, r_f=7, r_i=8, r_og=9)
_B128 = dict(g_q=20, g_k=21, g_lr=22, kvcmp=23, kvslc=24, kvwin=25, nsa_g=26)
NC = 27 * 128


def _column_map():
    cols = -np.ones((NC,), np.int64)

    def put(dst, src, width):
        cols[dst:dst + width] = np.arange(src, src + width)

    for name in _B256:
        put(_B256[name] * 256, _SRC[name], 256)
    put(_B128["g_q"] * 128, _SRC["g_q"], 128)
    put(_B128["g_k"] * 128, _SRC["g_k"], 128)
    put(_B128["g_lr"] * 128, _SRC["g_lr"], GLA_RANK)
    put(_B128["kvcmp"] * 128, _SRC["k_cmp"], 128)
    put(_B128["kvslc"] * 128, _SRC["k_slc"], 128)
    put(_B128["kvwin"] * 128, _SRC["k_win"], 128)
    put(_B128["nsa_g"] * 128, _SRC["nsa_g"], 3 * HEADS)
    return cols


_COLS = _column_map()


def _dot(a, b):
    return jnp.dot(a, b, preferred_element_type=F32)


def _dot_nt(a, b):
    return lax.dot_general(a, b, (((1,), (1,)), ((), ())), preferred_element_type=F32)


def _dot_tn(a, b):
    return lax.dot_general(a, b, (((0,), (0,)), ((), ())), preferred_element_type=F32)


def _split2(x):
    hi = x.astype(BF16)
    lo = (x - hi.astype(F32)).astype(BF16)
    return hi, lo


def _split3(x):
    hi = x.astype(BF16)
    r = x - hi.astype(F32)
    mid = r.astype(BF16)
    lo = (r - mid.astype(F32)).astype(BF16)
    return hi, mid, lo


def _group_mean(x, ones_bf, group):
    hi, lo = _split2(x)
    return (_dot(hi, ones_bf) + _dot(lo, ones_bf)) * (1.0 / group)


def _group_rms(x, ones_bf, group):
    return x * lax.rsqrt(_group_mean(x * x, ones_bf, group) + EPS)


def _sigmoid(x):
    return 1.0 / (1.0 + jnp.exp(-x))


def _silu(x):
    return x * _sigmoid(x)


def _params(*sem):
    return pltpu.CompilerParams(dimension_semantics=sem, vmem_limit_bytes=VMEM_LIMIT)


def _block_ones(n, group):
    i = np.arange(n)
    return jnp.asarray((i[:, None] // group == i[None, :] // group).astype(np.float32), BF16)


def _norm_matmul_kernel(x_ref, g_ref, w_ref, o_ref, xn_ref):
    @pl.when(pl.program_id(1) == 0)
    def _():
        x = x_ref[...]
        y = x * lax.rsqrt(jnp.mean(x * x, axis=-1, keepdims=True) + EPS) * g_ref[...]
        xn_ref[...] = y.astype(BF16)

    o_ref[...] = _dot(xn_ref[...], w_ref[...])


def _norm_matmul(x, gain, w_bf, tm, tn):
    T, K = x.shape
    N = w_bf.shape[1]
    return pl.pallas_call(
        _norm_matmul_kernel,
        grid=(T // tm, N // tn),
        in_specs=[pl.BlockSpec((tm, K), lambda i, j: (i, 0)),
                  pl.BlockSpec((1, K), lambda i, j: (0, 0)),
                  pl.BlockSpec((K, tn), lambda i, j: (0, j))],
        out_specs=pl.BlockSpec((tm, tn), lambda i, j: (i, j)),
        out_shape=jax.ShapeDtypeStruct((T, N), F32),
        scratch_shapes=[pltpu.VMEM((tm, K), BF16)],
        compiler_params=_params("parallel", "arbitrary"),
        name="in_proj",
    )(x, gain.reshape(1, K), w_bf)


def _prep_kernel(nq_ref, dq_ref, dk_ref, dv_ref, kvs_ref, kvw_ref, gains_ref, ones64_ref, ones32_ref,
                 ones64h_ref, o_nq, o_dq, o_dk, o_dv, o_kvs, o_vs, o_kvw, o_vw):
    ones64 = ones64_ref[...]
    ones32 = ones32_ref[...]
    ones64h = ones64h_ref[...]
    o_nq[...] = (_group_rms(nq_ref[...], ones64, HD) * gains_ref[0:1, :]).T.astype(BF16)
    o_dq[...] = (_group_rms(dq_ref[...], ones32, DIFF_QK) * gains_ref[1:2, :]).T.astype(BF16)
    o_dk[...] = (_group_rms(dk_ref[...], ones32, DIFF_QK) * gains_ref[2:3, :]).astype(BF16)
    o_dv[...] = dv_ref[...].T.astype(BF16)
    lane = lax.broadcasted_iota(jnp.int32, kvs_ref.shape, 1)
    for kv_ref, gain, o_kv, o_v in ((kvs_ref, gains_ref[3:4, 0:LANE], o_kvs, o_vs),
                                    (kvw_ref, gains_ref[4:5, 0:LANE], o_kvw, o_vw)):
        x = kv_ref[...]
        o_kv[...] = jnp.where(lane < HD, _group_rms(x, ones64h, HD) * gain, x).astype(BF16)
        o_v[...] = x.T[HD:2 * HD, :].astype(BF16)


def _prep(u, gains, tm):
    T = u.shape[0]
    c256 = lambda name: pl.BlockSpec((tm, GROUP), lambda i, c=_B256[name]: (i, c))
    c128 = lambda name: pl.BlockSpec((tm, LANE), lambda i, c=_B128[name]: (i, c))
    const = lambda shape: pl.BlockSpec(shape, lambda i: (0, 0))
    rows = lambda w: pl.BlockSpec((tm, w), lambda i: (i, 0))
    colsT = lambda h: pl.BlockSpec((h, tm), lambda i: (0, i))
    sds = jax.ShapeDtypeStruct
    return pl.pallas_call(
        _prep_kernel,
        grid=(T // tm,),
        in_specs=[c256("nsa_q"), c256("d_q"), c256("d_k"), c256("d_v"), c128("kvslc"), c128("kvwin"),
                  const((8, GROUP)), const((GROUP, GROUP)), const((GROUP, GROUP)), const((LANE, LANE))],
        out_specs=[colsT(GROUP), colsT(GROUP), rows(GROUP), colsT(GROUP),
                   rows(LANE), colsT(HD), rows(LANE), colsT(HD)],
        out_shape=[sds((GROUP, T), BF16), sds((GROUP, T), BF16), sds((T, GROUP), BF16), sds((GROUP, T), BF16),
                   sds((T, LANE), BF16), sds((HD, T), BF16), sds((T, LANE), BF16), sds((HD, T), BF16)],
        compiler_params=_params("parallel"),
        name="attn_prep",
    )(u, u, u, u, u, u, gains, _block_ones(GROUP, HD), _block_ones(GROUP, DIFF_QK), _block_ones(LANE, HD))


def _compress_kernel(xk_ref, xv_ref, pos_ref, w1_ref, w2_ref, gain_ref, ones64h_ref, kc_ref, vc_ref):
    half = NSA_CMP_STRIDE * HD
    n_rows = xk_ref.shape[0]

    def compress(x, j):
        top = (x + pos_ref[j, 0:1, :]).astype(BF16)
        bot = (x + pos_ref[j, 1:2, :]).astype(BF16)
        a = _dot(top, w1_ref[j, 0:half, :])
        b = _dot(bot, w1_ref[j, half:2 * half, :])
        hidden = a + pltpu.roll(b, n_rows - 1, 0)
        return _dot(_silu(hidden).astype(BF16), w2_ref[j])

    kc = compress(xk_ref[...], 0)
    kc_ref[...] = _group_rms(kc, ones64h_ref[...], HD) * gain_ref[...]
    vc_ref[...] = compress(xv_ref[...], 1).T[0:HD, :].astype(BF16)


def _compress(xk, xv, pos, w1_bf, w2p_bf, gain_row):
    B, ncp, half = xk.shape
    sq = pl.Squeezed()
    full = lambda shape: pl.BlockSpec(shape, lambda b: (0,) * len(shape))
    return pl.pallas_call(
        _compress_kernel,
        grid=(B,),
        in_specs=[pl.BlockSpec((sq, ncp, half), lambda b: (b, 0, 0)),
                  pl.BlockSpec((sq, ncp, half), lambda b: (b, 0, 0)),
                  full((2, 2, half)), full((2, 2 * half, NSA_CMP_HIDDEN)), full((2, NSA_CMP_HIDDEN, LANE)),
                  full((1, LANE)), full((LANE, LANE))],
        out_specs=[pl.BlockSpec((sq, ncp, LANE), lambda b: (b, 0, 0)),
                   pl.BlockSpec((sq, HD, ncp), lambda b: (b, 0, 0))],
        out_shape=[jax.ShapeDtypeStruct((B, ncp, LANE), F32), jax.ShapeDtypeStruct((B, HD, ncp), BF16)],
        compiler_params=_params("parallel"),
        name="nsa_compress",
    )(xk, xv, pos, w1_bf, w2p_bf, gain_row, _block_ones(LANE, HD))


def _softmax_step_t(s, v_t, m_old, l_old, acc_ref):
    m_new = jnp.maximum(m_old, jnp.max(s, axis=0, keepdims=True))
    alpha = jnp.exp2(m_old - m_new)
    p = jnp.exp2(s - m_new)
    acc_ref[...] = alpha * acc_ref[...] + _dot(v_t, p.astype(BF16))
    return m_new, alpha * l_old + jnp.sum(p, axis=0, keepdims=True)


def _nsa_kernel(qt_ref, g_ref, kc_ref, vct_ref, kvs_ref, vst_ref, kvw_ref, vwt_ref, ovl_ref, exp_ref,
                o_ref, qs_ref, *acc_refs, tq, tk, tw, ksel):
    i = pl.program_id(1)
    t0 = i * tq
    ncp = kc_ref.shape[0]
    nsel = ovl_ref.shape[0]
    acc_s, acc_w = acc_refs[:HEADS], acc_refs[HEADS:]

    qt = qt_ref[...]
    qs_ref[HD:LANE, :] = jnp.zeros((LANE - HD, HEADS * tq), BF16)
    for h in range(HEADS):
        qs_ref[0:HD, h * tq:(h + 1) * tq] = qt[h * HD:(h + 1) * HD, :]
        acc_s[h][...] = jnp.zeros((HD, tq), F32)
        acc_w[h][...] = jnp.zeros((HD, tq), F32)

    kc_hi, kc_lo = _split2(kc_ref[...])
    n_idx = lax.broadcasted_iota(jnp.int32, (ncp, tq), 0)
    t_lane = t0 + lax.broadcasted_iota(jnp.int32, (ncp, tq), 1)
    ok = n_idx * NSA_CMP_STRIDE + (NSA_CMP_LEN - 1) <= t_lane
    cmp_scores = _dot(kc_hi, qs_ref[...]) + _dot(kc_lo, qs_ref[...])
    o_c = []
    psum = jnp.zeros((ncp, tq), F32)
    for h in range(HEADS):
        s = jnp.where(ok, cmp_scores[:, h * tq:(h + 1) * tq], NEG)
        e = jnp.exp2(s - jnp.max(s, axis=0, keepdims=True))
        p = jnp.where(ok, e / jnp.sum(e, axis=0, keepdims=True), 0.0)
        o_c.append(_dot(vct_ref[...], p.astype(BF16)))
        psum = psum + p

    p_hi, p_lo = _split2(psum)
    imp = _dot(ovl_ref[...], p_hi) + _dot(ovl_ref[...], p_lo)
    blk = lax.broadcasted_iota(jnp.int32, (nsel, tq), 0)
    t_col = t0 + lax.broadcasted_iota(jnp.int32, (nsel, tq), 1)
    cur = t_col // NSA_SEL_LEN
    forced = (blk == 0) | (blk == cur) | (blk == cur - 1)
    vals = jnp.where(forced, BIG, jnp.where(blk * NSA_SEL_LEN <= t_col, imp, NEG))
    blk_f = blk.astype(F32)
    sel = jnp.zeros((nsel, tq), F32)
    for _ in range(ksel):
        mx = jnp.max(vals, axis=0, keepdims=True)
        first = jnp.min(jnp.where(vals == mx, blk_f, float(nsel)), axis=0, keepdims=True)
        pick = blk_f == first
        sel = jnp.where(pick, 1.0, sel)
        vals = jnp.where(pick, LOWEST, vals)
    if nsel < LANE:
        sel = jnp.concatenate([sel, jnp.zeros((LANE - nsel, tq), F32)], axis=0)
    sel_bf = sel.astype(BF16)

    stats0 = tuple((jnp.full((1, tq), NEG, F32), jnp.zeros((1, tq), F32)) for _ in range(HEADS))

    def sweep(lo, hi, k_ref, vt_ref, accs, width, bias_fn):
        def body(kt, stats):
            k0 = pl.multiple_of(kt * width, width)
            k = k_ref[pl.ds(k0, width), :]
            scores = [_dot(k, qs_ref[:, h * tq:(h + 1) * tq]) for h in range(HEADS)]
            kpos = k0 + lax.broadcasted_iota(jnp.int32, (width, tq), 0)
            tpos = t0 + lax.broadcasted_iota(jnp.int32, (width, tq), 1)
            bias = bias_fn(k0, kpos, tpos)
            vt = vt_ref[:, pl.ds(k0, width)]
            return tuple(_softmax_step_t(scores[h] + bias, vt, stats[h][0], stats[h][1], accs[h])
                         for h in range(HEADS))

        return lax.fori_loop(lo, hi, body, stats0)

    def sel_bias(k0, kpos, tpos):
        chosen = _dot(exp_ref[pl.ds(k0, kpos.shape[0]), :], sel_bf)
        return jnp.where((chosen > 0.5) & (kpos <= tpos), 0.0, NEG)

    st_s = sweep(0, (t0 + tq - 1) // tk + 1, kvs_ref, vst_ref, acc_s, tk, sel_bias)

    def win_bias(k0, kpos, tpos):
        return jnp.where((kpos <= tpos) & (kpos > tpos - NSA_WINDOW), 0.0, NEG)

    st_w = sweep(jnp.maximum(t0 - NSA_WINDOW, 0) // tw, (t0 + tq - 1) // tw + 1,
                 kvw_ref, vwt_ref, acc_w, tw, win_bias)

    gates = _sigmoid(g_ref[...].T)
    outs = []
    for h in range(HEADS):
        outs.append(gates[3 * h:3 * h + 1, :] * o_c[h]
                    + gates[3 * h + 1:3 * h + 2, :] * (acc_s[h][...] / st_s[h][1])
                    + gates[3 * h + 2:3 * h + 3, :] * (acc_w[h][...] / st_w[h][1]))
    o_ref[...] = jnp.concatenate(outs, axis=0).T


def _nsa_attention(u, nqt, kc, vct, kvs, vst, kvw, vwt, B, S, tq, tk, tw):
    nq = S // tq
    ncp = S // NSA_CMP_STRIDE
    nsel = S // NSA_SEL_LEN
    ksel = min(NSA_TOPK, nsel)
    n_cmp = (S - NSA_CMP_LEN) // NSA_CMP_STRIDE + 1
    cmp_start = np.arange(ncp) * NSA_CMP_STRIDE
    sel_start = np.arange(nsel) * NSA_SEL_LEN
    overlap = ((cmp_start[:, None] <= sel_start[None, :] + NSA_SEL_LEN - 1)
               & (cmp_start[:, None] + NSA_CMP_LEN - 1 >= sel_start[None, :])
               & (np.arange(ncp)[:, None] < n_cmp))
    ovl_t = jnp.asarray(overlap.T.astype(np.float32), BF16)
    expand = (jnp.arange(S, dtype=jnp.int32)[:, None] // NSA_SEL_LEN
              == jnp.arange(LANE, dtype=jnp.int32)[None, :]).astype(BF16)
    sq = pl.Squeezed()
    seq_rows = pl.BlockSpec((S, LANE), lambda b, i: (b, 0))
    seq_cols = pl.BlockSpec((HD, S), lambda b, i: (0, b))
    kern = functools.partial(_nsa_kernel, tq=tq, tk=tk, tw=tw, ksel=ksel)
    return pl.pallas_call(
        kern,
        grid=(B, nq),
        in_specs=[pl.BlockSpec((GROUP, tq), lambda b, i: (0, b * nq + i)),
                  pl.BlockSpec((tq, LANE), lambda b, i, c=_B128["nsa_g"]: (b * nq + i, c)),
                  pl.BlockSpec((sq, ncp, LANE), lambda b, i: (b, 0, 0)),
                  pl.BlockSpec((sq, HD, ncp), lambda b, i: (b, 0, 0)),
                  seq_rows, seq_cols, seq_rows, seq_cols,
                  pl.BlockSpec((nsel, ncp), lambda b, i: (0, 0)),
                  pl.BlockSpec((S, LANE), lambda b, i: (0, 0))],
        out_specs=pl.BlockSpec((tq, GROUP), lambda b, i: (b * nq + i, 0)),
        out_shape=jax.ShapeDtypeStruct((B * S, GROUP), F32),
        scratch_shapes=[pltpu.VMEM((LANE, HEADS * tq), BF16)] + [pltpu.VMEM((HD, tq), F32)] * (2 * HEADS),
        compiler_params=_params("parallel", "arbitrary"),
        name="nsa_attention",
    )(nqt, u, kc, vct, kvs, vst, kvw, vwt, ovl_t, expand)


def _diff_kernel(lam_ref, qt_ref, k_ref, vt_ref, gain_ref, o_ref, qs_ref, *acc_refs, tq, tk, lam_init):
    i = pl.program_id(1)
    t0 = i * tq
    lanes = 2 * tq

    qt = qt_ref[...]
    row = lax.broadcasted_iota(jnp.int32, (GROUP, tq), 0)
    zero = jnp.zeros_like(qt)
    for h in range(HEADS):
        qs_ref[h] = jnp.concatenate([jnp.where(row // DIFF_QK == 2 * h, qt, zero),
                                     jnp.where(row // DIFF_QK == 2 * h + 1, qt, zero)], axis=1)
        acc_refs[h][...] = jnp.zeros((HD, lanes), F32)

    def step(k0, width, masked, stats):
        k = k_ref[pl.ds(k0, width), :]
        scores = [_dot(k, qs_ref[h]) for h in range(HEADS)]
        out = []
        for h in range(HEADS):
            s = scores[h]
            if masked:
                kpos = k0 + lax.broadcasted_iota(jnp.int32, (width, lanes), 0)
                tpos = t0 + lax.broadcasted_iota(jnp.int32, (width, lanes), 1) % tq
                s = jnp.where(kpos <= tpos, s, NEG)
            out.append(_softmax_step_t(s, vt_ref[h * HD:(h + 1) * HD, pl.ds(k0, width)],
                                       stats[h][0], stats[h][1], acc_refs[h]))
        return tuple(out)

    stats0 = tuple((jnp.full((1, lanes), NEG, F32), jnp.zeros((1, lanes), F32)) for _ in range(HEADS))
    n_wide = t0 // tk
    stats = lax.fori_loop(0, n_wide, lambda kt, c: step(pl.multiple_of(kt * tk, tk), tk, False, c), stats0)
    stats = lax.fori_loop(n_wide * (tk // tq), t0 // tq,
                          lambda kt, c: step(pl.multiple_of(kt * tq, tq), tq, False, c), stats)
    stats = step(pl.multiple_of(t0, tq), tq, True, stats)

    lam = lam_ref[...]
    lam_full = (jnp.exp(jnp.sum(lam[0:1] * lam[1:2], axis=1, keepdims=True))
                - jnp.exp(jnp.sum(lam[2:3] * lam[3:4], axis=1, keepdims=True)) + lam_init)
    outs = []
    for h in range(HEADS):
        r = acc_refs[h][...] / stats[h][1]
        d = r[:, 0:tq] - lam_full * r[:, tq:lanes]
        d = d * lax.rsqrt(jnp.mean(d * d, axis=0, keepdims=True) + EPS)
        outs.append(d * gain_ref[...] * (1.0 - lam_init))
    o_ref[...] = jnp.concatenate(outs, axis=0).T


def _diff_attention(dqt, dk, dvt, lam, gain_col, B, S, tq, tk, layer_idx):
    nq = S // tq
    lam_init = 0.8 - 0.6 * math.exp(-0.3 * layer_idx)
    kern = functools.partial(_diff_kernel, tq=tq, tk=tk, lam_init=lam_init)
    return pl.pallas_call(
        kern,
        grid=(B, nq),
        in_specs=[pl.BlockSpec((4, DIFF_QK), lambda b, i: (0, 0)),
                  pl.BlockSpec((GROUP, tq), lambda b, i: (0, b * nq + i)),
                  pl.BlockSpec((S, GROUP), lambda b, i: (b, 0)),
                  pl.BlockSpec((GROUP, S), lambda b, i: (0, b)),
                  pl.BlockSpec((HD, tq), lambda b, i: (0, 0))],
        out_specs=pl.BlockSpec((tq, GROUP), lambda b, i: (b * nq + i, 0)),
        out_shape=jax.ShapeDtypeStruct((B * S, GROUP), F32),
        scratch_shapes=[pltpu.VMEM((HEADS, GROUP, 2 * tq), BF16)] + [pltpu.VMEM((HD, 2 * tq), F32)] * HEADS,
        compiler_params=_params("parallel", "arbitrary"),
        name="diff_attention",
    )(lam, dqt, dk, dvt, gain_col)


_LEVELS = (32, 16, 8, 4, 2, 1)
LIN_GROUP = 4


def _stack_heads(x, lane, group, count):
    zero = jnp.zeros_like(x)
    return jnp.concatenate([jnp.where(lane // group == g, x, zero) for g in range(count)], axis=0)


def _unstack_heads(x4, lane, rows):
    out = jnp.where(lane // HD == 0, x4[0:rows], 0.0)
    for h in range(1, HEADS):
        out = out + jnp.where(lane // HD == h, x4[h * rows:(h + 1) * rows], 0.0)
    return out


def _linear_consts():
    r = np.arange(CHUNK)[:, None]
    t = np.arange(CHUNK)[None, :]
    tri = (t <= r).astype(np.float32)
    masks = [(r // (2 * s) == t // (2 * s)) for s in _LEVELS] + [r == t]
    mall = np.stack([np.tile(m.astype(np.float32), (1, HEADS)) for m in masks])
    return jnp.asarray(tri, BF16), jnp.asarray(mall, F32)


def _level_exponent(s, lg, b, row):
    if s == 1:
        return jnp.where((row & 1) != 0, 0.0, pltpu.roll(lg, CHUNK - 1, 0))
    if s == 2:
        nxt1 = pltpu.roll(lg, CHUNK - 1, 0)
        nxt2 = pltpu.roll(lg, CHUNK - 2, 0)
        r4 = row & 3
        return jnp.where(r4 == 0, nxt1 + nxt2, jnp.where(r4 == 1, nxt1, jnp.where(r4 == 2, 0.0, lg)))
    mids = [jnp.broadcast_to(b[m:m + 1, :], (2 * s, b.shape[1])) for m in range(s, CHUNK, 2 * s)]
    d = b - (jnp.concatenate(mids, axis=0) if len(mids) > 1 else mids[0])
    return jnp.where((row & s) != 0, d, -d)


def _linear_chunks(items, tri, mall_ref):
    bs = []
    for q, k, v, lg, state_ref, dk in items:
        hi, mid, lo = _split3(lg)
        bs.append(_dot(tri, hi) + _dot(tri, mid) + _dot(tri, lo))
    ats = []
    for (q, k, v, lg, state_ref, dk), b in zip(items, bs):
        dkh = HEADS * dk
        row = lax.broadcasted_iota(jnp.int32, (CHUNK, dkh), 0)
        lane_k = lax.broadcasted_iota(jnp.int32, (CHUNK, dkh), 1)
        a_t = mall_ref[len(_LEVELS)] * _dot_nt(k.astype(BF16), _stack_heads(q.astype(BF16), lane_k, dk, HEADS))
        for li, s in enumerate(_LEVELS):
            e = jnp.exp(_level_exponent(s, lg, b, row))
            upper = (row & s) != 0
            qt = jnp.where(upper, q * e, 0.0).astype(BF16)
            kt = jnp.where(upper, 0.0, k * e).astype(BF16)
            a_t = a_t + mall_ref[li] * _dot_nt(kt, _stack_heads(qt, lane_k, dk, HEADS))
        ats.append(a_t)
    lane_v = lax.broadcasted_iota(jnp.int32, (CHUNK, GROUP), 1)
    partial = []
    for (q, k, v, lg, state_ref, dk), b, a_t in zip(items, bs, ats):
        v_bf = v.astype(BF16)
        o_intra = _unstack_heads(_dot_tn(a_t.astype(BF16), v_bf), lane_v, CHUNK)
        e_b = jnp.exp(b)
        e_u = jnp.exp(b[CHUNK - 1:CHUNK, :] - b)
        kv = _dot_tn(v_bf, (k * e_u).astype(BF16))
        partial.append((o_intra, e_b, kv))
    outs = []
    for (q, k, v, lg, state_ref, dk), (o_intra, e_b, kv) in zip(items, partial):
        dkh = HEADS * dk
        st = state_ref[...]
        o_inter = _dot_nt((q * e_b).astype(BF16), st.astype(BF16))
        srow = lax.broadcasted_iota(jnp.int32, (GROUP, dkh), 0)
        scol = lax.broadcasted_iota(jnp.int32, (GROUP, dkh), 1)
        state_ref[...] = st * e_b[CHUNK - 1:CHUNK, :] + jnp.where(srow // HD == scol // dk, kv, 0.0)
        outs.append(o_inter + o_intra)
    return outs


def _linear_kernel(gq_ref, gk_ref, gv_ref, lr_ref, gog_ref, w2_ref, b_ref, ggain_ref,
                   rq_ref, rf_ref, ri_ref, rog_ref, lbl_ref, rgain_ref, tri_ref, mall_ref, ones64_ref,
                   og_ref, or_ref, gstate_ref, rstate_ref, *, tm, layer_idx):
    @pl.when(pl.program_id(1) == 0)
    def _():
        gstate_ref[...] = jnp.zeros_like(gstate_ref)
        rstate_ref[...] = jnp.zeros_like(rstate_ref)

    tri = tri_ref[...]
    ones64 = ones64_ref[...]
    w_hi, w_lo = _split2(w2_ref[...])
    logits = lbl_ref[...]
    ez = jnp.exp(logits - jnp.max(logits, axis=0, keepdims=True))
    probs = ez / jnp.sum(ez, axis=0, keepdims=True)
    lb = jnp.zeros((1, GROUP), F32)
    for j in range(1, layer_idx + 1):
        lb = lb + probs[j:j + 1]

    def body(c, carry):
        items, sinks = [], []
        for g in range(LIN_GROUP):
            rs = pl.ds(pl.multiple_of((c * LIN_GROUP + g) * CHUNK, CHUNK), CHUNK)
            lr_hi, lr_lo = _split2(lr_ref[rs, :])
            x = _dot(lr_hi, w_hi) + _dot(lr_lo, w_hi) + _dot(lr_hi, w_lo) + b_ref[...]
            lg = (jnp.minimum(x, 0.0) - jnp.log(1.0 + jnp.exp(-jnp.abs(x)))) * (1.0 / GLA_TAU)
            items.append((gq_ref[rs, :] * (GLA_DK ** -0.5), gk_ref[rs, :], gv_ref[rs, :], lg, gstate_ref, GLA_DK))
            sinks.append((og_ref, ggain_ref, gog_ref, rs))
            z = rf_ref[rs, :]
            f = lb + (1.0 - lb) * _sigmoid(z)
            items.append((rq_ref[rs, :], (1.0 - lb) * _sigmoid(-z), ri_ref[rs, :], jnp.log(f), rstate_ref, HD))
            sinks.append((or_ref, rgain_ref, rog_ref, rs))
        for o, (out_ref, gain_ref, gate_ref, rs) in zip(_linear_chunks(items, tri, mall_ref), sinks):
            out_ref[rs, :] = _group_rms(o, ones64, HD) * gain_ref[...] * _silu(gate_ref[rs, :])
        return carry

    lax.fori_loop(0, tm // (CHUNK * LIN_GROUP), body, 0)


def _linear_mixers(u, w2pad, b_gate, gla_gain, lb_logits, hgrn_gain, B, S, tm, layer_idx):
    nt = S // tm
    tri, mall = _linear_consts()
    ones64 = _block_ones(GROUP, HD)
    c256 = lambda name: pl.BlockSpec((tm, GROUP), lambda b, i, c=_B256[name]: (b * nt + i, c))
    c128 = lambda name: pl.BlockSpec((tm, LANE), lambda b, i, c=_B128[name]: (b * nt + i, c))
    full = lambda shape: pl.BlockSpec(shape, lambda b, i: (0,) * len(shape))
    out_spec = pl.BlockSpec((tm, GROUP), lambda b, i: (b * nt + i, 0))
    out_shape = jax.ShapeDtypeStruct((B * S, GROUP), F32)
    depth = lb_logits.shape[0]
    return pl.pallas_call(
        functools.partial(_linear_kernel, tm=tm, layer_idx=layer_idx),
        grid=(B, nt),
        in_specs=[c128("g_q"), c128("g_k"), c256("g_v"), c128("g_lr"), c256("g_og"),
                  full((LANE, LANE)), full((1, LANE)), full((1, GROUP)),
                  c256("r_q"), c256("r_f"), c256("r_i"), c256("r_og"), full((depth, GROUP)), full((1, GROUP)),
                  full(tri.shape), full(mall.shape), full((GROUP, GROUP))],
        out_specs=[out_spec, out_spec], out_shape=[out_shape, out_shape],
        scratch_shapes=[pltpu.VMEM((GROUP, HEADS * GLA_DK), F32), pltpu.VMEM((GROUP, GROUP), F32)],
        compiler_params=_params("parallel", "arbitrary"),
        name="linear_mixers",
    )(u, u, u, u, u, w2pad, b_gate, gla_gain, u, u, u, u, lb_logits, hgrn_gain, tri, mall, ones64)


def _out_proj_kernel(h_ref, a_ref, b_ref, c_ref, d_ref, w_ref, o_ref):
    acc = h_ref[...]
    for j, r in enumerate((a_ref, b_ref, c_ref, d_ref)):
        acc = acc + _dot(r[...].astype(BF16), w_ref[j * GROUP:(j + 1) * GROUP, :])
    o_ref[...] = acc


def _out_proj(h, parts, w_bf, tm):
    T = h.shape[0]
    part = pl.BlockSpec((tm, GROUP), lambda i: (i, 0))
    return pl.pallas_call(
        _out_proj_kernel,
        grid=(T // tm,),
        in_specs=[pl.BlockSpec((tm, D_MODEL), lambda i: (i, 0)), part, part, part, part,
                  pl.BlockSpec((D_MODEL, D_MODEL), lambda i: (0, 0))],
        out_specs=pl.BlockSpec((tm, D_MODEL), lambda i: (i, 0)),
        out_shape=jax.ShapeDtypeStruct((T, D_MODEL), F32),
        compiler_params=_params("parallel"),
        name="out_proj",
    )(h, *parts, w_bf)


def _ffn_kernel(h_ref, g_ref, wg_ref, wu_ref, wd_ref, o_ref, xn_ref, acc_ref):
    f = pl.program_id(1)

    @pl.when(f == 0)
    def _():
        x = h_ref[...]
        y = x * lax.rsqrt(jnp.mean(x * x, axis=-1, keepdims=True) + EPS) * g_ref[...]
        xn_ref[...] = y.astype(BF16)
        acc_ref[...] = x

    xn = xn_ref[...]
    mid = _silu(_dot(xn, wg_ref[...])) * _dot(xn, wu_ref[...])
    acc_ref[...] += _dot(mid.astype(BF16), wd_ref[...])

    @pl.when(f == pl.num_programs(1) - 1)
    def _():
        o_ref[...] = acc_ref[...]


def _ffn(h, gain, wg, wu, wd, tm, tf):
    T = h.shape[0]
    F = wg.shape[1]
    return pl.pallas_call(
        _ffn_kernel,
        grid=(T // tm, F // tf),
        in_specs=[pl.BlockSpec((tm, D_MODEL), lambda i, f: (i, 0)),
                  pl.BlockSpec((1, D_MODEL), lambda i, f: (0, 0)),
                  pl.BlockSpec((D_MODEL, tf), lambda i, f: (0, f)),
                  pl.BlockSpec((D_MODEL, tf), lambda i, f: (0, f)),
                  pl.BlockSpec((tf, D_MODEL), lambda i, f: (f, 0))],
        out_specs=pl.BlockSpec((tm, D_MODEL), lambda i, f: (i, 0)),
        out_shape=jax.ShapeDtypeStruct((T, D_MODEL), F32),
        scratch_shapes=[pltpu.VMEM((tm, D_MODEL), BF16), pltpu.VMEM((tm, D_MODEL), F32)],
        compiler_params=_params("parallel", "arbitrary"),
        name="ffn_swiglu",
    )(h, gain.reshape(1, D_MODEL), wg, wu, wd)


def _router_kernel(h_ref, g_ref, r_ref, lower_ref, c_ref, comb_ref, rk_ref, cnt_ref):
    x = h_ref[...]
    y = x * lax.rsqrt(jnp.mean(x * x, axis=-1, keepdims=True) + EPS) * g_ref[...]
    c_ref[...] = y.astype(BF16)
    y_hi, y_lo = _split2(y)
    r_hi, r_lo = _split2(r_ref[...])
    logits = _dot(y_hi, r_hi) + _dot(y_lo, r_hi) + _dot(y_hi, r_lo)
    lane = lax.broadcasted_iota(jnp.int32, logits.shape, 1)
    lane_f = lane.astype(F32)
    logits = jnp.where(lane < N_EXPERTS, logits, LOWEST)
    m1 = jnp.max(logits, axis=1, keepdims=True)
    i1 = jnp.min(jnp.where(logits == m1, lane_f, float(LANE)), axis=1, keepdims=True)
    rest = jnp.where(lane_f == i1, LOWEST, logits)
    m2 = jnp.max(rest, axis=1, keepdims=True)
    i2 = jnp.min(jnp.where(rest == m2, lane_f, float(LANE)), axis=1, keepdims=True)
    e2 = jnp.exp(m2 - m1)
    w1 = 1.0 / (1.0 + e2)
    comb_ref[...] = jnp.where(lane_f == i1, w1, jnp.where(lane_f == i2, e2 * w1, 0.0))
    chosen = (lane_f == i1) | (lane_f == i2)
    sel = jnp.where(chosen, 1.0, 0.0)
    rank = _dot(lower_ref[...], sel.astype(BF16))
    rk_ref[...] = jnp.where(chosen, rank, -1.0).astype(jnp.int32)
    cnt_ref[...] = jnp.broadcast_to(jnp.sum(sel, axis=0, keepdims=True), cnt_ref.shape).astype(jnp.int32)


def _router(h, gain, router_pad, tm):
    T = h.shape[0]
    i = np.arange(tm)
    lower = jnp.asarray((i[None, :] < i[:, None]).astype(np.float32), BF16)
    sds = jax.ShapeDtypeStruct
    return pl.pallas_call(
        _router_kernel,
        grid=(T // tm,),
        in_specs=[pl.BlockSpec((tm, D_MODEL), lambda i: (i, 0)),
                  pl.BlockSpec((1, D_MODEL), lambda i: (0, 0)),
                  pl.BlockSpec((D_MODEL, LANE), lambda i: (0, 0)),
                  pl.BlockSpec((tm, tm), lambda i: (0, 0))],
        out_specs=[pl.BlockSpec((tm, D_MODEL), lambda i: (i, 0)), pl.BlockSpec((tm, LANE), lambda i: (i, 0)),
                   pl.BlockSpec((tm, LANE), lambda i: (i, 0)), pl.BlockSpec((8, LANE), lambda i: (i, 0))],
        out_shape=[sds((T, D_MODEL), BF16), sds((T, LANE), F32), sds((T, LANE), jnp.int32),
                   sds((T // tm * 8, LANE), jnp.int32)],
        compiler_params=_params("parallel"),
        name="moe_router",
    )(h, gain.reshape(1, D_MODEL), router_pad, lower)


MOE_CHUNK = 512
MOE_ROWS = 512
MOE_HALF = MOE_ROWS // 2
MOE_TILE = 1024
MOE_ALIGN = 16
MOE_PAD_SEGS = MOE_TILE // MOE_ROWS


def _moe_plan(cnt, T):
    nch = T // MOE_CHUNK
    n_ce = cnt.reshape(nch, 8, LANE)[:, 0, :N_EXPERTS]
    cap = (n_ce + MOE_ALIGN - 1) // MOE_ALIGN * MOE_ALIGN
    tot = jnp.sum(cap, axis=0)
    ptot = (tot + MOE_TILE - 1) // MOE_TILE * MOE_TILE
    start = jnp.cumsum(ptot) - ptot
    lo_ce = start[None, :] + jnp.cumsum(cap, axis=0) - cap
    pad_lo = (start + tot)[:, None] + MOE_ROWS * jnp.arange(MOE_PAD_SEGS, dtype=jnp.int32)[None, :]
    lo = jnp.concatenate([lo_ce.T, pad_lo], axis=1)
    n = jnp.concatenate([n_ce.T, jnp.full((N_EXPERTS, MOE_PAD_SEGS), MOE_ROWS, jnp.int32)], axis=1)
    n_tiles = _moe_tiles(T)
    tile_end = jnp.cumsum(ptot // MOE_TILE)
    j = jnp.arange(n_tiles, dtype=jnp.int32)
    tile_e = jnp.minimum(jnp.sum((tile_end[None, :] <= j[:, None]).astype(jnp.int32), axis=1), N_EXPERTS - 1)
    valid = (j < tile_end[-1]).astype(jnp.int32)
    n_flat = jnp.concatenate([n.reshape(-1), tile_end[-1:] * MOE_PAD_SEGS]).astype(jnp.int32)
    return lo.reshape(-1).astype(jnp.int32), n_flat, tile_e, valid


def _moe_tiles(T):
    nch = T // MOE_CHUNK
    rows = 2 * T + nch * N_EXPERTS * (MOE_ALIGN - 1) + N_EXPERTS * (MOE_TILE - 1)
    return -(-rows // MOE_TILE) + 1


def _moe_gather_kernel(lo_ref, n_ref, c_ref, rkt_ref, x_hbm, buf, sem, *, nch):
    e = pl.program_id(0)
    c = pl.program_id(1)
    n_steps = pl.num_programs(0) * pl.num_programs(1)
    step = e * pl.num_programs(1) + c
    slot = step % 2
    rk_row = rkt_ref[pl.ds(e, 1), :]
    rk_row = jnp.where(c < nch, rk_row, -1)
    row = lax.broadcasted_iota(jnp.int32, (MOE_HALF, MOE_CHUNK), 0)
    chunk = c_ref[...]
    onehot = jnp.where(rk_row == row, 1.0, 0.0).astype(BF16)
    buf[slot, 0:MOE_HALF, :] = _dot(onehot, chunk).astype(BF16)

    @pl.when(n_ref[step] > MOE_HALF)
    def _():
        onehot = jnp.where(rk_row == row + MOE_HALF, 1.0, 0.0).astype(BF16)
        buf[slot, MOE_HALF:MOE_ROWS, :] = _dot(onehot, chunk).astype(BF16)

    def copy(k, s, rows):
        dst = x_hbm.at[pl.ds(pl.multiple_of(lo_ref[k], MOE_ALIGN), rows)]
        return pltpu.make_async_copy(buf.at[s, 0:rows], dst, sem.at[s])

    def for_size(k, fn):
        @pl.when(n_ref[k] > MOE_HALF)
        def _():
            fn(MOE_ROWS)

        @pl.when(n_ref[k] <= MOE_HALF)
        def _():
            fn(MOE_HALF)

    @pl.when(step > 0)
    def _():
        for_size(step - 1, lambda rows: copy(step - 1, 1 - slot, rows).wait())

    for_size(step, lambda rows: copy(step, slot, rows).start())

    @pl.when(step == n_steps - 1)
    def _():
        for_size(step, lambda rows: copy(step, slot, rows).wait())
        buf[0] = jnp.zeros((MOE_ROWS, D_MODEL), BF16)

        def fill(k, carry):
            dst = x_hbm.at[pl.ds(pl.multiple_of(k * MOE_ROWS, MOE_ROWS), MOE_ROWS)]
            cp = pltpu.make_async_copy(buf.at[0], dst, sem.at[0])
            cp.start()
            cp.wait()
            return carry

        lax.fori_loop(n_ref[n_steps], x_hbm.shape[0] // MOE_ROWS, fill, 0)


def _moe_gather(c_bf, rkt, lo, n, n_tiles):
    T = c_bf.shape[0]
    nch = T // MOE_CHUNK
    last = nch - 1
    grid_spec = pltpu.PrefetchScalarGridSpec(
        num_scalar_prefetch=2,
        grid=(N_EXPERTS, nch + MOE_PAD_SEGS),
        in_specs=[pl.BlockSpec((MOE_CHUNK, D_MODEL), lambda e, c, lo, n: (jnp.minimum(c, last), 0)),
                  pl.BlockSpec((N_EXPERTS, MOE_CHUNK), lambda e, c, lo, n: (0, jnp.minimum(c, last)))],
        out_specs=pl.BlockSpec(memory_space=pl.ANY),
        scratch_shapes=[pltpu.VMEM((2, MOE_ROWS, D_MODEL), BF16), pltpu.SemaphoreType.DMA((2,))],
    )
    return pl.pallas_call(
        functools.partial(_moe_gather_kernel, nch=nch),
        grid_spec=grid_spec,
        out_shape=jax.ShapeDtypeStruct((n_tiles * MOE_TILE, D_MODEL), BF16),
        compiler_params=_params("arbitrary", "arbitrary"),
        name="moe_gather",
    )(lo, n, c_bf, rkt)


def _moe_ffn_kernel(te_ref, valid_ref, x_ref, wg_ref, wu_ref, wd_ref, y_ref, acc_ref):
    j = pl.program_id(0)
    f = pl.program_id(1)
    last = pl.num_programs(1) - 1

    @pl.when(valid_ref[j] == 1)
    def _():
        x = x_ref[...]
        mid = _silu(_dot(x, wg_ref[...])) * _dot(x, wu_ref[...])
        part = _dot(mid.astype(BF16), wd_ref[...])

        @pl.when(f == 0)
        def _():
            acc_ref[...] = part

        @pl.when(f > 0)
        def _():
            acc_ref[...] += part

        @pl.when(f == last)
        def _():
            y_ref[...] = acc_ref[...].astype(BF16)

    @pl.when((valid_ref[j] == 0) & (f == last))
    def _():
        y_ref[...] = jnp.zeros(y_ref.shape, BF16)


def _moe_ffn(x_sorted, tile_e, valid, wg, wu, wd, tf):
    n_tiles = x_sorted.shape[0] // MOE_TILE
    F = wg.shape[2]
    nf = F // tf
    sq = pl.Squeezed()
    fsel = lambda j, f, te, va: jnp.where(va[j] == 1, f, nf - 1)
    grid_spec = pltpu.PrefetchScalarGridSpec(
        num_scalar_prefetch=2,
        grid=(n_tiles, nf),
        in_specs=[pl.BlockSpec((MOE_TILE, D_MODEL), lambda j, f, te, va: (j, 0)),
                  pl.BlockSpec((sq, D_MODEL, tf), lambda j, f, te, va: (te[j], 0, fsel(j, f, te, va))),
                  pl.BlockSpec((sq, D_MODEL, tf), lambda j, f, te, va: (te[j], 0, fsel(j, f, te, va))),
                  pl.BlockSpec((sq, tf, D_MODEL), lambda j, f, te, va: (te[j], fsel(j, f, te, va), 0))],
        out_specs=pl.BlockSpec((MOE_TILE, D_MODEL), lambda j, f, te, va: (j, 0)),
        scratch_shapes=[pltpu.VMEM((MOE_TILE, D_MODEL), F32)],
    )
    return pl.pallas_call(
        _moe_ffn_kernel,
        grid_spec=grid_spec,
        out_shape=jax.ShapeDtypeStruct((n_tiles * MOE_TILE, D_MODEL), BF16),
        compiler_params=_params("parallel", "arbitrary"),
        name="moe_experts",
    )(tile_e, valid, x_sorted, wg, wu, wd)


def _moe_combine_kernel(lo_ref, n_ref, h_ref, rk_ref, comb_ref, y_hbm, o_ref, ybuf, sem, *, nch):
    c = pl.program_id(0)
    seg = lambda e: e * (nch + MOE_PAD_SEGS) + c

    def copy(e, rows):
        src = y_hbm.at[pl.ds(pl.multiple_of(lo_ref[seg(e)], MOE_ALIGN), rows)]
        return pltpu.make_async_copy(src, ybuf.at[e, 0:rows], sem.at[e])

    def for_size(e, fn):
        @pl.when(n_ref[seg(e)] > MOE_HALF)
        def _():
            fn(MOE_ROWS)

        @pl.when(n_ref[seg(e)] <= MOE_HALF)
        def _():
            fn(MOE_HALF)

    for e in range(N_EXPERTS):
        for_size(e, lambda rows, e=e: copy(e, rows).start())
    o_ref[...] = h_ref[...]
    rk = rk_ref[...]
    comb = comb_ref[...]
    lane = lax.broadcasted_iota(jnp.int32, (MOE_CHUNK, MOE_HALF), 1)
    for e in range(N_EXPERTS):
        for_size(e, lambda rows, e=e: copy(e, rows).wait())
        rank_col = rk[:, e:e + 1]
        w_col = comb[:, e:e + 1]
        onehot = jnp.where(rank_col == lane, 1.0, 0.0).astype(BF16)
        o_ref[...] += w_col * _dot(onehot, ybuf[e, 0:MOE_HALF, :])

        @pl.when(n_ref[seg(e)] > MOE_HALF)
        def _():
            onehot = jnp.where(rank_col == lane + MOE_HALF, 1.0, 0.0).astype(BF16)
            o_ref[...] += w_col * _dot(onehot, ybuf[e, MOE_HALF:MOE_ROWS, :])


def _moe_combine(h, rk_pad, comb, y_sorted, lo, n):
    T = h.shape[0]
    nch = T // MOE_CHUNK
    grid_spec = pltpu.PrefetchScalarGridSpec(
        num_scalar_prefetch=2,
        grid=(nch,),
        in_specs=[pl.BlockSpec((MOE_CHUNK, D_MODEL), lambda c, lo, n: (c, 0)),
                  pl.BlockSpec((MOE_CHUNK, LANE), lambda c, lo, n: (c, 0)),
                  pl.BlockSpec((MOE_CHUNK, LANE), lambda c, lo, n: (c, 0)),
                  pl.BlockSpec(memory_space=pl.ANY)],
        out_specs=pl.BlockSpec((MOE_CHUNK, D_MODEL), lambda c, lo, n: (c, 0)),
        scratch_shapes=[pltpu.VMEM((N_EXPERTS, MOE_ROWS, D_MODEL), BF16), pltpu.SemaphoreType.DMA((N_EXPERTS,))],
    )
    return pl.pallas_call(
        functools.partial(_moe_combine_kernel, nch=nch),
        grid_spec=grid_spec,
        out_shape=jax.ShapeDtypeStruct((T, D_MODEL), F32),
        compiler_params=_params("arbitrary"),
        name="moe_combine",
    )(lo, n, h, rk_pad, comb, y_sorted)


def _moe(h, c_bf, comb, rk, cnt, wg, wu, wd, tf):
    T = h.shape[0]
    lo, n, tile_e, valid = _moe_plan(cnt, T)
    x_sorted = _moe_gather(c_bf, rk[:, :N_EXPERTS].T, lo, n, _moe_tiles(T))
    y_sorted = _moe_ffn(x_sorted, tile_e, valid, wg, wu, wd, tf)
    return _moe_combine(h, rk, comb, y_sorted, lo, n)


def _ple_kernel(h_ref, p_ref, g_ref, wg_ref, wp_ref, o_ref):
    x = h_ref[...]
    y = x * lax.rsqrt(jnp.mean(x * x, axis=-1, keepdims=True) + EPS) * g_ref[...]
    gate = _sigmoid(_dot(y.astype(BF16), wg_ref[...]))
    o_ref[...] = x + _dot(p_ref[...].astype(BF16), wp_ref[...]) * gate


def _ple(h, p, gain, wg, wp, tm):
    T = h.shape[0]
    return pl.pallas_call(
        _ple_kernel,
        grid=(T // tm,),
        in_specs=[pl.BlockSpec((tm, D_MODEL), lambda i: (i, 0)),
                  pl.BlockSpec((tm, PLE_DIM), lambda i: (i, 0)),
                  pl.BlockSpec((1, D_MODEL), lambda i: (0, 0)),
                  pl.BlockSpec((D_MODEL, D_MODEL), lambda i: (0, 0)),
                  pl.BlockSpec((PLE_DIM, D_MODEL), lambda i: (0, 0))],
        out_specs=pl.BlockSpec((tm, D_MODEL), lambda i: (i, 0)),
        out_shape=jax.ShapeDtypeStruct((T, D_MODEL), F32),
        compiler_params=_params("parallel"),
        name="ple_gate",
    )(h, p, gain.reshape(1, D_MODEL), wg, wp)


def _tiles(T, S):
    pick = lambda n, pref: max(t for t in pref if n % t == 0)
    return dict(
        proj_m=pick(T, (1024, 512, 256, 128)), proj_n=NC // 3,
        prep_m=pick(T, (512, 256, 128)),
        attn_q=256, attn_k=pick(S, (512, 256)),
        nsa_q=pick(S, (512, 256)), nsa_k=pick(S, (512, 256)),
        lin_m=pick(S, (512, 256, 128, 64)),
        row_m=pick(T, (512, 256, 128)),
        ffn_m=pick(T, (1024, 512, 256, 128)), ffn_f=512,
    )


def kernel(x, p, norm_attn, w_in, w_out, nsa_cmp_pos, nsa_cmp_w1, nsa_cmp_w2, nsa_qk_gain, diff_qk_gain, diff_lambda, diff_norm, gla_w_gate2, gla_b_gate, gla_norm, hgrn_lb_logits, hgrn_norm, norm_ffn, ffn_w_gate, ffn_w_up, ffn_w_down, moe_router, moe_w_gate, moe_w_up, moe_w_down, ple_norm, ple_w_gate, ple_w_proj):
    B, S, _ = x.shape
    depth = w_in.shape[0]
    T = B * S
    t = _tiles(T, S)
    ncp = S // NSA_CMP_STRIDE
    half = NSA_CMP_STRIDE * HD
    cols = jnp.asarray(np.maximum(_COLS, 0), jnp.int32)
    col_mask = jnp.asarray(_COLS >= 0)
    ones_row = jnp.ones((GROUP,), F32)

    h = x.reshape(T, D_MODEL)
    for i in range(depth):
        w_in_r = jnp.where(col_mask[None, :], jnp.take(w_in[i], cols, axis=1), 0.0).astype(BF16)
        gains = jnp.stack([
            jnp.tile(nsa_qk_gain[i, 0], HEADS) * (HD ** -0.5 * LOG2E),
            jnp.tile(diff_qk_gain[i, 0], 2 * HEADS) * (DIFF_QK ** -0.5 * LOG2E),
            jnp.tile(diff_qk_gain[i, 1], 2 * HEADS),
            jnp.tile(nsa_qk_gain[i, 2], HEADS),
            jnp.tile(nsa_qk_gain[i, 3], HEADS),
            ones_row, ones_row, ones_row])
        pos = nsa_cmp_pos[i].reshape(2, 2, half)
        w1 = nsa_cmp_w1[i].astype(BF16)
        w2p = jnp.pad(nsa_cmp_w2[i], ((0, 0), (0, 0), (0, LANE - HD))).astype(BF16)
        kc_gain = jnp.pad(nsa_qk_gain[i, 1], (0, LANE - HD)).reshape(1, LANE)
        w2pad = jnp.zeros((LANE, LANE), F32).at[:GLA_RANK].set(gla_w_gate2[i])
        diff_gain_col = jnp.broadcast_to(diff_norm[i][:, None], (HD, t["attn_q"]))

        u = _norm_matmul(h, norm_attn[i], w_in_r, t["proj_m"], t["proj_n"])
        nqt, dqt, dk, dvt, kvs, vst, kvw, vwt = _prep(u, gains, t["prep_m"])
        kv = u[:, _B128["kvcmp"] * LANE:(_B128["kvcmp"] + 1) * LANE]
        xk = kv[:, :HD].reshape(B, ncp, half)
        xv = kv[:, HD:].reshape(B, ncp, half)
        kc, vct = _compress(xk, xv, pos, w1, w2p, kc_gain)
        o_a = _nsa_attention(u, nqt, kc, vct, kvs, vst, kvw, vwt, B, S, t["nsa_q"], t["nsa_k"], t["nsa_q"])
        o_b = _diff_attention(dqt, dk, dvt, diff_lambda[i], diff_gain_col, B, S, t["attn_q"], t["attn_k"], i)
        o_c, o_d = _linear_mixers(u, w2pad, gla_b_gate[i].reshape(1, LANE),
                                  jnp.tile(gla_norm[i], HEADS).reshape(1, GROUP), hgrn_lb_logits,
                                  jnp.tile(hgrn_norm[i], HEADS).reshape(1, GROUP), B, S, t["lin_m"], i)
        h = _out_proj(h, (o_a, o_b, o_c, o_d), w_out[i].astype(BF16), t["row_m"])
        if i % 2 == 0:
            j = i // 2
            h = _ffn(h, norm_ffn[i], ffn_w_gate[j].astype(BF16), ffn_w_up[j].astype(BF16),
                     ffn_w_down[j].astype(BF16), t["ffn_m"], t["ffn_f"])
        else:
            j = i // 2
            router_pad = jnp.zeros((D_MODEL, LANE), F32).at[:, :N_EXPERTS].set(moe_router[j])
            c_bf, comb, rk, cnt = _router(h, norm_ffn[i], router_pad, MOE_CHUNK)
            h = _moe(h, c_bf, comb, rk, cnt, moe_w_gate[j].astype(BF16), moe_w_up[j].astype(BF16),
                     moe_w_down[j].astype(BF16), t["ffn_f"])
        h = _ple(h, p[i].reshape(T, PLE_DIM), ple_norm[i], ple_w_gate[i].astype(BF16),
                 ple_w_proj[i].astype(BF16), t["row_m"])
    return h.reshape(B, S, D_MODEL)
```

```python
import functools
import math

import numpy as np
import jax
import jax.numpy as jnp
from jax import lax
from jax.experimental import pallas as pl
from jax.experimental.pallas import tpu as pltpu

F32 = jnp.float32
BF16 = jnp.bfloat16

D_MODEL = 1024
HEADS = 4
HD = 64
GROUP = HEADS * HD
NSA_CMP_LEN = 32
NSA_CMP_STRIDE = 16
NSA_CMP_HIDDEN = 4 * HD
NSA_SEL_LEN = 64
NSA_TOPK = 16
NSA_WINDOW = 512
NSA_SUBTILES = 2
DIFF_QK = HD // 2
GLA_DK = HD // 2
GLA_RANK = 16
GLA_TAU = 16.0
CHUNK = 64
D_FF = 7 * D_MODEL // 2
N_EXPERTS = 8
PLE_DIM = 256
EPS = 1e-6
NEG = -1e30
BIG = 1e30
LOWEST = -3.0e38
LOG2E = 1.4426950408889634

VMEM_LIMIT = 52 * 1024 * 1024
LANE = 128

_SRC = dict(nsa_q=0, k_cmp=256, v_cmp=320, k_slc=384, v_slc=448, k_win=512, v_win=576, nsa_g=640,
            d_q=652, d_k=908, d_v=1164, g_q=1420, g_k=1548, g_v=1676, g_lr=1932, g_og=1948,
            r_q=2204, r_f=2460, r_i=2716, r_og=2972)
IN_COLS = 3228

_B256 = dict(nsa_q=0, d_q=1, d_k=2, d_v=3, g_v=4, g_og=5, r_q=6, r_f=7, r_i=8, r_og=9)
_B128 = dict(g_q=20, g_k=21, g_lr=22, kvcmp=23, kvslc=24, kvwin=25, nsa_g=26)
NC = 27 * 128


def _column_map():
    cols = -np.ones((NC,), np.int64)

    def put(dst, src, width):
        cols[dst:dst + width] = np.arange(src, src + width)

    for name in _B256:
        put(_B256[name] * 256, _SRC[name], 256)
    put(_B128["g_q"] * 128, _SRC["g_q"], 128)
    put(_B128["g_k"] * 128, _SRC["g_k"], 128)
    put(_B128["g_lr"] * 128, _SRC["g_lr"], GLA_RANK)
    put(_B128["kvcmp"] * 128, _SRC["k_cmp"], 128)
    put(_B128["kvslc"] * 128, _SRC["k_slc"], 128)
    put(_B128["kvwin"] * 128, _SRC["k_win"], 128)
    put(_B128["nsa_g"] * 128, _SRC["nsa_g"], 3 * HEADS)
    return cols


_COLS = _column_map()


def _dot(a, b):
    return jnp.dot(a, b, preferred_element_type=F32)


def _dot_nt(a, b):
    return lax.dot_general(a, b, (((1,), (1,)), ((), ())), preferred_element_type=F32)


def _dot_tn(a, b):
    return lax.dot_general(a, b, (((0,), (0,)), ((), ())), preferred_element_type=F32)


def _split2(x):
    hi = x.astype(BF16)
    lo = (x - hi.astype(F32)).astype(BF16)
    return hi, lo


def _split3(x):
    hi = x.astype(BF16)
    r = x - hi.astype(F32)
    mid = r.astype(BF16)
    lo = (r - mid.astype(F32)).astype(BF16)
    return hi, mid, lo


def _group_mean(x, ones_bf, group):
    hi, lo = _split2(x)
    return (_dot(hi, ones_bf) + _dot(lo, ones_bf)) * (1.0 / group)


def _group_rms(x, ones_bf, group):
    return x * lax.rsqrt(_group_mean(x * x, ones_bf, group) + EPS)


def _sigmoid(x):
    return 1.0 / (1.0 + jnp.exp(-x))


def _silu(x):
    return x * _sigmoid(x)


def _params(*sem):
    return pltpu.CompilerParams(dimension_semantics=sem, vmem_limit_bytes=VMEM_LIMIT)


def _block_ones(n, group):
    i = np.arange(n)
    return jnp.asarray((i[:, None] // group == i[None, :] // group).astype(np.float32), BF16)


def _norm_matmul_kernel(x_ref, g_ref, w_ref, o_ref, xn_ref):
    @pl.when(pl.program_id(1) == 0)
    def _():
        x = x_ref[...]
        y = x * lax.rsqrt(jnp.mean(x * x, axis=-1, keepdims=True) + EPS) * g_ref[...]
        xn_ref[...] = y.astype(BF16)

    o_ref[...] = _dot(xn_ref[...], w_ref[...])


def _norm_matmul(x, gain, w_bf, tm, tn):
    T, K = x.shape
    N = w_bf.shape[1]
    return pl.pallas_call(
        _norm_matmul_kernel,
        grid=(T // tm, N // tn),
        in_specs=[pl.BlockSpec((tm, K), lambda i, j: (i, 0)),
                  pl.BlockSpec((1, K), lambda i, j: (0, 0)),
                  pl.BlockSpec((K, tn), lambda i, j: (0, j))],
        out_specs=pl.BlockSpec((tm, tn), lambda i, j: (i, j)),
        out_shape=jax.ShapeDtypeStruct((T, N), F32),
        scratch_shapes=[pltpu.VMEM((tm, K), BF16)],
        compiler_params=_params("parallel", "arbitrary"),
        name="in_proj",
    )(x, gain.reshape(1, K), w_bf)


def _prep_kernel(nq_ref, dq_ref, dk_ref, dv_ref, kvs_ref, kvw_ref, gains_ref, ones64_ref, ones32_ref,
                 ones64h_ref, o_nq, o_dq, o_dk, o_dv, o_kvs, o_vs, o_kvw, o_vw):
    ones64 = ones64_ref[...]
    ones32 = ones32_ref[...]
    ones64h = ones64h_ref[...]
    o_nq[...] = (_group_rms(nq_ref[...], ones64, HD) * gains_ref[0:1, :]).T.astype(BF16)
    o_dq[...] = (_group_rms(dq_ref[...], ones32, DIFF_QK) * gains_ref[1:2, :]).T.astype(BF16)
    o_dk[...] = (_group_rms(dk_ref[...], ones32, DIFF_QK) * gains_ref[2:3, :]).astype(BF16)
    o_dv[...] = dv_ref[...].T.astype(BF16)
    lane = lax.broadcasted_iota(jnp.int32, kvs_ref.shape, 1)
    for kv_ref, gain, o_kv, o_v in ((kvs_ref, gains_ref[3:4, 0:LANE], o_kvs, o_vs),
                                    (kvw_ref, gains_ref[4:5, 0:LANE], o_kvw, o_vw)):
        x = kv_ref[...]
        o_kv[...] = jnp.where(lane < HD, _group_rms(x, ones64h, HD) * gain, x).astype(BF16)
        o_v[...] = x.T[HD:2 * HD, :].astype(BF16)


def _prep(u, gains, tm):
    T = u.shape[0]
    c256 = lambda name: pl.BlockSpec((tm, GROUP), lambda i, c=_B256[name]: (i, c))
    c128 = lambda name: pl.BlockSpec((tm, LANE), lambda i, c=_B128[name]: (i, c))
    const = lambda shape: pl.BlockSpec(shape, lambda i: (0, 0))
    rows = lambda w: pl.BlockSpec((tm, w), lambda i: (i, 0))
    colsT = lambda h: pl.BlockSpec((h, tm), lambda i: (0, i))
    sds = jax.ShapeDtypeStruct
    return pl.pallas_call(
        _prep_kernel,
        grid=(T // tm,),
        in_specs=[c256("nsa_q"), c256("d_q"), c256("d_k"), c256("d_v"), c128("kvslc"), c128("kvwin"),
                  const((8, GROUP)), const((GROUP, GROUP)), const((GROUP, GROUP)), const((LANE, LANE))],
        out_specs=[colsT(GROUP), colsT(GROUP), rows(GROUP), colsT(GROUP),
                   rows(LANE), colsT(HD), rows(LANE), colsT(HD)],
        out_shape=[sds((GROUP, T), BF16), sds((GROUP, T), BF16), sds((T, GROUP), BF16), sds((GROUP, T), BF16),
                   sds((T, LANE), BF16), sds((HD, T), BF16), sds((T, LANE), BF16), sds((HD, T), BF16)],
        compiler_params=_params("parallel"),
        name="attn_prep",
    )(u, u, u, u, u, u, gains, _block_ones(GROUP, HD), _block_ones(GROUP, DIFF_QK), _block_ones(LANE, HD))


def _compress_kernel(xk_ref, xv_ref, pos_ref, w1_ref, w2_ref, gain_ref, ones64h_ref, kc_ref, vc_ref):
    half = NSA_CMP_STRIDE * HD
    n_rows = xk_ref.shape[0]

    def compress(x, j):
        top = (x + pos_ref[j, 0:1, :]).astype(BF16)
        bot = (x + pos_ref[j, 1:2, :]).astype(BF16)
        a = _dot(top, w1_ref[j, 0:half, :])
        b = _dot(bot, w1_ref[j, half:2 * half, :])
        hidden = a + pltpu.roll(b, n_rows - 1, 0)
        return _dot(_silu(hidden).astype(BF16), w2_ref[j])

    kc = compress(xk_ref[...], 0)
    kc_ref[...] = _group_rms(kc, ones64h_ref[...], HD) * gain_ref[...]
    vc_ref[...] = compress(xv_ref[...], 1).T[0:HD, :].astype(BF16)


def _compress(xk, xv, pos, w1_bf, w2p_bf, gain_row):
    B, ncp, half = xk.shape
    sq = pl.Squeezed()
    full = lambda shape: pl.BlockSpec(shape, lambda b: (0,) * len(shape))
    return pl.pallas_call(
        _compress_kernel,
        grid=(B,),
        in_specs=[pl.BlockSpec((sq, ncp, half), lambda b: (b, 0, 0)),
                  pl.BlockSpec((sq, ncp, half), lambda b: (b, 0, 0)),
                  full((2, 2, half)), full((2, 2 * half, NSA_CMP_HIDDEN)), full((2, NSA_CMP_HIDDEN, LANE)),
                  full((1, LANE)), full((LANE, LANE))],
        out_specs=[pl.BlockSpec((sq, ncp, LANE), lambda b: (b, 0, 0)),
                   pl.BlockSpec((sq, HD, ncp), lambda b: (b, 0, 0))],
        out_shape=[jax.ShapeDtypeStruct((B, ncp, LANE), F32), jax.ShapeDtypeStruct((B, HD, ncp), BF16)],
        compiler_params=_params("parallel"),
        name="nsa_compress",
    )(xk, xv, pos, w1_bf, w2p_bf, gain_row, _block_ones(LANE, HD))


def _softmax_step_t(s, v_t, m_old, l_old, acc_ref):
    m_new = jnp.maximum(m_old, jnp.max(s, axis=0, keepdims=True))
    alpha = jnp.exp2(m_old - m_new)
    p = jnp.exp2(s - m_new)
    acc_ref[...] = alpha * acc_ref[...] + _dot(v_t, p.astype(BF16))
    return m_new, alpha * l_old + jnp.sum(p, axis=0, keepdims=True)


def _nsa_kernel(qt_ref, g_ref, kc_ref, vct_ref, kvs_ref, vst_ref, kvw_ref, vwt_ref, ovl_ref, exp_ref,
                o_ref, qs_ref, *acc_refs, tq, tk, tw, ksel):
    i = pl.program_id(1)
    t0 = i * tq
    ncp = kc_ref.shape[0]
    nsel = ovl_ref.shape[0]
    acc_s, acc_w = acc_refs[:HEADS], acc_refs[HEADS:]

    qt = qt_ref[...]
    qs_ref[HD:LANE, :] = jnp.zeros((LANE - HD, HEADS * tq), BF16)
    for h in range(HEADS):
        qs_ref[0:HD, h * tq:(h + 1) * tq] = qt[h * HD:(h + 1) * HD, :]
        acc_s[h][...] = jnp.zeros((HD, tq), F32)
        acc_w[h][...] = jnp.zeros((HD, tq), F32)

    kc_hi, kc_lo = _split2(kc_ref[...])
    n_idx = lax.broadcasted_iota(jnp.int32, (ncp, tq), 0)
    t_lane = t0 + lax.broadcasted_iota(jnp.int32, (ncp, tq), 1)
    ok = n_idx * NSA_CMP_STRIDE + (NSA_CMP_LEN - 1) <= t_lane
    cmp_scores = _dot(kc_hi, qs_ref[...]) + _dot(kc_lo, qs_ref[...])
    o_c = []
    psum = jnp.zeros((ncp, tq), F32)
    for h in range(HEADS):
        s = jnp.where(ok, cmp_scores[:, h * tq:(h + 1) * tq], NEG)
        e = jnp.exp2(s - jnp.max(s, axis=0, keepdims=True))
        p = jnp.where(ok, e / jnp.sum(e, axis=0, keepdims=True), 0.0)
        o_c.append(_dot(vct_ref[...], p.astype(BF16)))
        psum = psum + p

    p_hi, p_lo = _split2(psum)
    imp = _dot(ovl_ref[...], p_hi) + _dot(ovl_ref[...], p_lo)
    blk = lax.broadcasted_iota(jnp.int32, (nsel, tq), 0)
    t_col = t0 + lax.broadcasted_iota(jnp.int32, (nsel, tq), 1)
    cur = t_col // NSA_SEL_LEN
    forced = (blk == 0) | (blk == cur) | (blk == cur - 1)
    vals = jnp.where(forced, BIG, jnp.where(blk * NSA_SEL_LEN <= t_col, imp, NEG))
    blk_f = blk.astype(F32)
    sel = jnp.zeros((nsel, tq), F32)
    for _ in range(ksel):
        mx = jnp.max(vals, axis=0, keepdims=True)
        first = jnp.min(jnp.where(vals == mx, blk_f, float(nsel)), axis=0, keepdims=True)
        pick = blk_f == first
        sel = jnp.where(pick, 1.0, sel)
        vals = jnp.where(pick, LOWEST, vals)
    if nsel < LANE:
        sel = jnp.concatenate([sel, jnp.zeros((LANE - nsel, tq), F32)], axis=0)
    sel_bf = sel.astype(BF16)

    stats0 = tuple((jnp.full((1, tq), NEG, F32), jnp.zeros((1, tq), F32)) for _ in range(HEADS))

    def sweep(lo, hi, k_ref, vt_ref, accs, width, bias_fn):
        def body(kt, stats):
            k0 = pl.multiple_of(kt * width, width)
            k = k_ref[pl.ds(k0, width), :]
            scores = [_dot(k, qs_ref[:, h * tq:(h + 1) * tq]) for h in range(HEADS)]
            kpos = k0 + lax.broadcasted_iota(jnp.int32, (width, tq), 0)
            tpos = t0 + lax.broadcasted_iota(jnp.int32, (width, tq), 1)
            bias = bias_fn(k0, kpos, tpos)
            vt = vt_ref[:, pl.ds(k0, width)]
            return tuple(_softmax_step_t(scores[h] + bias, vt, stats[h][0], stats[h][1], accs[h])
                         for h in range(HEADS))

        return lax.fori_loop(lo, hi, body, stats0)

    def sel_bias(k0, kpos, tpos):
        chosen = _dot(exp_ref[pl.ds(k0, kpos.shape[0]), :], sel_bf)
        return jnp.where((chosen > 0.5) & (kpos <= tpos), 0.0, NEG)

    st_s = sweep(0, (t0 + tq - 1) // tk + 1, kvs_ref, vst_ref, acc_s, tk, sel_bias)

    def win_bias(k0, kpos, tpos):
        return jnp.where((kpos <= tpos) & (kpos > tpos - NSA_WINDOW), 0.0, NEG)

    st_w = sweep(jnp.maximum(t0 - NSA_WINDOW, 0) // tw, (t0 + tq - 1) // tw + 1,
                 kvw_ref, vwt_ref, acc_w, tw, win_bias)

    gates = _sigmoid(g_ref[...].T)
    outs = []
    for h in range(HEADS):
        outs.append(gates[3 * h:3 * h + 1, :] * o_c[h]
                    + gates[3 * h + 1:3 * h + 2, :] * (acc_s[h][...] / st_s[h][1])
                    + gates[3 * h + 2:3 * h + 3, :] * (acc_w[h][...] / st_w[h][1]))
    o_ref[...] = jnp.concatenate(outs, axis=0).T


def _nsa_attention(u, nqt, kc, vct, kvs, vst, kvw, vwt, B, S, tq, tk, tw):
    nq = S // tq
    ncp = S // NSA_CMP_STRIDE
    nsel = S // NSA_SEL_LEN
    ksel = min(NSA_TOPK, nsel)
    n_cmp = (S - NSA_CMP_LEN) // NSA_CMP_STRIDE + 1
    cmp_start = np.arange(ncp) * NSA_CMP_STRIDE
    sel_start = np.arange(nsel) * NSA_SEL_LEN
    overlap = ((cmp_start[:, None] <= sel_start[None, :] + NSA_SEL_LEN - 1)
               & (cmp_start[:, None] + NSA_CMP_LEN - 1 >= sel_start[None, :])
               & (np.arange(ncp)[:, None] < n_cmp))
    ovl_t = jnp.asarray(overlap.T.astype(np.float32), BF16)
    expand = (jnp.arange(S, dtype=jnp.int32)[:, None] // NSA_SEL_LEN
              == jnp.arange(LANE, dtype=jnp.int32)[None, :]).astype(BF16)
    sq = pl.Squeezed()
    seq_rows = pl.BlockSpec((S, LANE), lambda b, i: (b, 0))
    seq_cols = pl.BlockSpec((HD, S), lambda b, i: (0, b))
    kern = functools.partial(_nsa_kernel, tq=tq, tk=tk, tw=tw, ksel=ksel)
    return pl.pallas_call(
        kern,
        grid=(B, nq),
        in_specs=[pl.BlockSpec((GROUP, tq), lambda b, i: (0, b * nq + i)),
                  pl.BlockSpec((tq, LANE), lambda b, i, c=_B128["nsa_g"]: (b * nq + i, c)),
                  pl.BlockSpec((sq, ncp, LANE), lambda b, i: (b, 0, 0)),
                  pl.BlockSpec((sq, HD, ncp), lambda b, i: (b, 0, 0)),
                  seq_rows, seq_cols, seq_rows, seq_cols,
                  pl.BlockSpec((nsel, ncp), lambda b, i: (0, 0)),
                  pl.BlockSpec((S, LANE), lambda b, i: (0, 0))],
        out_specs=pl.BlockSpec((tq, GROUP), lambda b, i: (b * nq + i, 0)),
        out_shape=jax.ShapeDtypeStruct((B * S, GROUP), F32),
        scratch_shapes=[pltpu.VMEM((LANE, HEADS * tq), BF16)] + [pltpu.VMEM((HD, tq), F32)] * (2 * HEADS),
        compiler_params=_params("parallel", "arbitrary"),
        name="nsa_attention",
    )(nqt, u, kc, vct, kvs, vst, kvw, vwt, ovl_t, expand)


def _diff_kernel(lam_ref, qt_ref, k_ref, vt_ref, gain_ref, o_ref, qs_ref, *acc_refs, tq, tk, lam_init):
    i = pl.program_id(1)
    t0 = i * tq
    lanes = 2 * tq

    qt = qt_ref[...]
    row = lax.broadcasted_iota(jnp.int32, (GROUP, tq), 0)
    zero = jnp.zeros_like(qt)
    for h in range(HEADS):
        qs_ref[h] = jnp.concatenate([jnp.where(row // DIFF_QK == 2 * h, qt, zero),
                                     jnp.where(row // DIFF_QK == 2 * h + 1, qt, zero)], axis=1)
        acc_refs[h][...] = jnp.zeros((HD, lanes), F32)

    def step(k0, width, masked, stats):
        k = k_ref[pl.ds(k0, width), :]
        scores = [_dot(k, qs_ref[h]) for h in range(HEADS)]
        out = []
        for h in range(HEADS):
            s = scores[h]
            if masked:
                kpos = k0 + lax.broadcasted_iota(jnp.int32, (width, lanes), 0)
                tpos = t0 + lax.broadcasted_iota(jnp.int32, (width, lanes), 1) % tq
                s = jnp.where(kpos <= tpos, s, NEG)
            out.append(_softmax_step_t(s, vt_ref[h * HD:(h + 1) * HD, pl.ds(k0, width)],
                                       stats[h][0], stats[h][1], acc_refs[h]))
        return tuple(out)

    stats0 = tuple((jnp.full((1, lanes), NEG, F32), jnp.zeros((1, lanes), F32)) for _ in range(HEADS))
    n_wide = t0 // tk
    stats = lax.fori_loop(0, n_wide, lambda kt, c: step(pl.multiple_of(kt * tk, tk), tk, False, c), stats0)
    stats = lax.fori_loop(n_wide * (tk // tq), t0 // tq,
                          lambda kt, c: step(pl.multiple_of(kt * tq, tq), tq, False, c), stats)
    stats = step(pl.multiple_of(t0, tq), tq, True, stats)

    lam = lam_ref[...]
    lam_full = (jnp.exp(jnp.sum(lam[0:1] * lam[1:2], axis=1, keepdims=True))
                - jnp.exp(jnp.sum(lam[2:3] * lam[3:4], axis=1, keepdims=True)) + lam_init)
    outs = []
    for h in range(HEADS):
        r = acc_refs[h][...] / stats[h][1]
        d = r[:, 0:tq] - lam_full * r[:, tq:lanes]
        d = d * lax.rsqrt(jnp.mean(d * d, axis=0, keepdims=True) + EPS)
        outs.append(d * gain_ref[...] * (1.0 - lam_init))
    o_ref[...] = jnp.concatenate(outs, axis=0).T


def _diff_attention(dqt, dk, dvt, lam, gain_col, B, S, tq, tk, layer_idx):
    nq = S // tq
    lam_init = 0.8 - 0.6 * math.exp(-0.3 * layer_idx)
    kern = functools.partial(_diff_kernel, tq=tq, tk=tk, lam_init=lam_init)
    return pl.pallas_call(
        kern,
        grid=(B, nq),
        in_specs=[pl.BlockSpec((4, DIFF_QK), lambda b, i: (0, 0)),
                  pl.BlockSpec((GROUP, tq), lambda b, i: (0, b * nq + i)),
                  pl.BlockSpec((S, GROUP), lambda b, i: (b, 0)),
                  pl.BlockSpec((GROUP, S), lambda b, i: (0, b)),
                  pl.BlockSpec((HD, tq), lambda b, i: (0, 0))],
        out_specs=pl.BlockSpec((tq, GROUP), lambda b, i: (b * nq + i, 0)),
        out_shape=jax.ShapeDtypeStruct((B * S, GROUP), F32),
        scratch_shapes=[pltpu.VMEM((HEADS, GROUP, 2 * tq), BF16)] + [pltpu.VMEM((HD, 2 * tq), F32)] * HEADS,
        compiler_params=_params("parallel", "arbitrary"),
        name="diff_attention",
    )(lam, dqt, dk, dvt, gain_col)


_LEVELS = (32, 16, 8, 4, 2, 1)
LIN_GROUP = 4


def _stack_heads(x, lane, group, count):
    zero = jnp.zeros_like(x)
    return jnp.concatenate([jnp.where(lane // group == g, x, zero) for g in range(count)], axis=0)


def _unstack_heads(x4, lane, rows):
    out = jnp.where(lane // HD == 0, x4[0:rows], 0.0)
    for h in range(1, HEADS):
        out = out + jnp.where(lane // HD == h, x4[h * rows:(h + 1) * rows], 0.0)
    return out


def _linear_consts():
    r = np.arange(CHUNK)[:, None]
    t = np.arange(CHUNK)[None, :]
    tri = (t <= r).astype(np.float32)
    masks = [(r // (2 * s) == t // (2 * s)) for s in _LEVELS] + [r == t]
    mall = np.stack([np.tile(m.astype(np.float32), (1, HEADS)) for m in masks])
    return jnp.asarray(tri, BF16), jnp.asarray(mall, F32)


def _level_exponent(s, lg, b, row):
    if s == 1:
        return jnp.where((row & 1) != 0, 0.0, pltpu.roll(lg, CHUNK - 1, 0))
    if s == 2:
        nxt1 = pltpu.roll(lg, CHUNK - 1, 0)
        nxt2 = pltpu.roll(lg, CHUNK - 2, 0)
        r4 = row & 3
        return jnp.where(r4 == 0, nxt1 + nxt2, jnp.where(r4 == 1, nxt1, jnp.where(r4 == 2, 0.0, lg)))
    mids = [jnp.broadcast_to(b[m:m + 1, :], (2 * s, b.shape[1])) for m in range(s, CHUNK, 2 * s)]
    d = b - (jnp.concatenate(mids, axis=0) if len(mids) > 1 else mids[0])
    return jnp.where((row & s) != 0, d, -d)


def _linear_chunks(items, tri, mall_ref):
    bs = []
    for q, k, v, lg, state_ref, dk in items:
        hi, mid, lo = _split3(lg)
        bs.append(_dot(tri, hi) + _dot(tri, mid) + _dot(tri, lo))
    ats = []
    for (q, k, v, lg, state_ref, dk), b in zip(items, bs):
        dkh = HEADS * dk
        row = lax.broadcasted_iota(jnp.int32, (CHUNK, dkh), 0)
        lane_k = lax.broadcasted_iota(jnp.int32, (CHUNK, dkh), 1)
        a_t = mall_ref[len(_LEVELS)] * _dot_nt(k.astype(BF16), _stack_heads(q.astype(BF16), lane_k, dk, HEADS))
        for li, s in enumerate(_LEVELS):
            e = jnp.exp(_level_exponent(s, lg, b, row))
            upper = (row & s) != 0
            qt = jnp.where(upper, q * e, 0.0).astype(BF16)
            kt = jnp.where(upper, 0.0, k * e).astype(BF16)
            a_t = a_t + mall_ref[li] * _dot_nt(kt, _stack_heads(qt, lane_k, dk, HEADS))
        ats.append(a_t)
    lane_v = lax.broadcasted_iota(jnp.int32, (CHUNK, GROUP), 1)
    partial = []
    for (q, k, v, lg, state_ref, dk), b, a_t in zip(items, bs, ats):
        v_bf = v.astype(BF16)
        o_intra = _unstack_heads(_dot_tn(a_t.astype(BF16), v_bf), lane_v, CHUNK)
        e_b = jnp.exp(b)
        e_u = jnp.exp(b[CHUNK - 1:CHUNK, :] - b)
        kv = _dot_tn(v_bf, (k * e_u).astype(BF16))
        partial.append((o_intra, e_b, kv))
    outs = []
    for (q, k, v, lg, state_ref, dk), (o_intra, e_b, kv) in zip(items, partial):
        dkh = HEADS * dk
        st = state_ref[...]
        o_inter = _dot_nt((q * e_b).astype(BF16), st.astype(BF16))
        srow = lax.broadcasted_iota(jnp.int32, (GROUP, dkh), 0)
        scol = lax.broadcasted_iota(jnp.int32, (GROUP, dkh), 1)
        state_ref[...] = st * e_b[CHUNK - 1:CHUNK, :] + jnp.where(srow // HD == scol // dk, kv, 0.0)
        outs.append(o_inter + o_intra)
    return outs


def _linear_kernel(gq_ref, gk_ref, gv_ref, lr_ref, gog_ref, w2_ref, b_ref, ggain_ref,
                   rq_ref, rf_ref, ri_ref, rog_ref, lbl_ref, rgain_ref, tri_ref, mall_ref, ones64_ref,
                   og_ref, or_ref, gstate_ref, rstate_ref, *, tm, layer_idx):
    @pl.when(pl.program_id(1) == 0)
    def _():
        gstate_ref[...] = jnp.zeros_like(gstate_ref)
        rstate_ref[...] = jnp.zeros_like(rstate_ref)

    tri = tri_ref[...]
    ones64 = ones64_ref[...]
    w_hi, w_lo = _split2(w2_ref[...])
    logits = lbl_ref[...]
    ez = jnp.exp(logits - jnp.max(logits, axis=0, keepdims=True))
    probs = ez / jnp.sum(ez, axis=0, keepdims=True)
    lb = jnp.zeros((1, GROUP), F32)
    for j in range(1, layer_idx + 1):
        lb = lb + probs[j:j + 1]

    def body(c, carry):
        items, sinks = [], []
        for g in range(LIN_GROUP):
            rs = pl.ds(pl.multiple_of((c * LIN_GROUP + g) * CHUNK, CHUNK), CHUNK)
            lr_hi, lr_lo = _split2(lr_ref[rs, :])
            x = _dot(lr_hi, w_hi) + _dot(lr_lo, w_hi) + _dot(lr_hi, w_lo) + b_ref[...]
            lg = (jnp.minimum(x, 0.0) - jnp.log(1.0 + jnp.exp(-jnp.abs(x)))) * (1.0 / GLA_TAU)
            items.append((gq_ref[rs, :] * (GLA_DK ** -0.5), gk_ref[rs, :], gv_ref[rs, :], lg, gstate_ref, GLA_DK))
            sinks.append((og_ref, ggain_ref, gog_ref, rs))
            z = rf_ref[rs, :]
            f = lb + (1.0 - lb) * _sigmoid(z)
            items.append((rq_ref[rs, :], (1.0 - lb) * _sigmoid(-z), ri_ref[rs, :], jnp.log(f), rstate_ref, HD))
            sinks.append((or_ref, rgain_ref, rog_ref, rs))
        for o, (out_ref, gain_ref, gate_ref, rs) in zip(_linear_chunks(items, tri, mall_ref), sinks):
            out_ref[rs, :] = _group_rms(o, ones64, HD) * gain_ref[...] * _silu(gate_ref[rs, :])
        return carry

    lax.fori_loop(0, tm // (CHUNK * LIN_GROUP), body, 0)


def _linear_mixers(u, w2pad, b_gate, gla_gain, lb_logits, hgrn_gain, B, S, tm, layer_idx):
    nt = S // tm
    tri, mall = _linear_consts()
    ones64 = _block_ones(GROUP, HD)
    c256 = lambda name: pl.BlockSpec((tm, GROUP), lambda b, i, c=_B256[name]: (b * nt + i, c))
    c128 = lambda name: pl.BlockSpec((tm, LANE), lambda b, i, c=_B128[name]: (b * nt + i, c))
    full = lambda shape: pl.BlockSpec(shape, lambda b, i: (0,) * len(shape))
    out_spec = pl.BlockSpec((tm, GROUP), lambda b, i: (b * nt + i, 0))
    out_shape = jax.ShapeDtypeStruct((B * S, GROUP), F32)
    depth = lb_logits.shape[0]
    return pl.pallas_call(
        functools.partial(_linear_kernel, tm=tm, layer_idx=layer_idx),
        grid=(B, nt),
        in_specs=[c128("g_q"), c128("g_k"), c256("g_v"), c128("g_lr"), c256("g_og"),
                  full((LANE, LANE)), full((1, LANE)), full((1, GROUP)),
                  c256("r_q"), c256("r_f"), c256("r_i"), c256("r_og"), full((depth, GROUP)), full((1, GROUP)),
                  full(tri.shape), full(mall.shape), full((GROUP, GROUP))],
        out_specs=[out_spec, out_spec], out_shape=[out_shape, out_shape],
        scratch_shapes=[pltpu.VMEM((GROUP, HEADS * GLA_DK), F32), pltpu.VMEM((GROUP, GROUP), F32)],
        compiler_params=_params("parallel", "arbitrary"),
        name="linear_mixers",
    )(u, u, u, u, u, w2pad, b_gate, gla_gain, u, u, u, u, lb_logits, hgrn_gain, tri, mall, ones64)


def _mix_residual(h_ref, part_refs, wout_ref):
    x = h_ref[...]
    for j, r in enumerate(part_refs):
        x = x + _dot(r[...].astype(BF16), wout_ref[j * GROUP:(j + 1) * GROUP, :])
    return x


def _ffn_kernel(h_ref, a_ref, b_ref, c_ref, d_ref, wout_ref, g_ref, wg_ref, wu_ref, wd_ref, o_ref,
                xn_ref, acc_ref):
    f = pl.program_id(1)

    @pl.when(f == 0)
    def _():
        x = _mix_residual(h_ref, (a_ref, b_ref, c_ref, d_ref), wout_ref)
        y = x * lax.rsqrt(jnp.mean(x * x, axis=-1, keepdims=True) + EPS) * g_ref[...]
        xn_ref[...] = y.astype(BF16)
        acc_ref[...] = x

    xn = xn_ref[...]
    mid = _silu(_dot(xn, wg_ref[...])) * _dot(xn, wu_ref[...])
    acc_ref[...] += _dot(mid.astype(BF16), wd_ref[...])

    @pl.when(f == pl.num_programs(1) - 1)
    def _():
        o_ref[...] = acc_ref[...]


def _ffn(h, parts, w_out, gain, wg, wu, wd, tm, tf):
    T = h.shape[0]
    F = wg.shape[1]
    part = pl.BlockSpec((tm, GROUP), lambda i, f: (i, 0))
    return pl.pallas_call(
        _ffn_kernel,
        grid=(T // tm, F // tf),
        in_specs=[pl.BlockSpec((tm, D_MODEL), lambda i, f: (i, 0)), part, part, part, part,
                  pl.BlockSpec((D_MODEL, D_MODEL), lambda i, f: (0, 0)),
                  pl.BlockSpec((1, D_MODEL), lambda i, f: (0, 0)),
                  pl.BlockSpec((D_MODEL, tf), lambda i, f: (0, f)),
                  pl.BlockSpec((D_MODEL, tf), lambda i, f: (0, f)),
                  pl.BlockSpec((tf, D_MODEL), lambda i, f: (f, 0))],
        out_specs=pl.BlockSpec((tm, D_MODEL), lambda i, f: (i, 0)),
        out_shape=jax.ShapeDtypeStruct((T, D_MODEL), F32),
        scratch_shapes=[pltpu.VMEM((tm, D_MODEL), BF16), pltpu.VMEM((tm, D_MODEL), F32)],
        compiler_params=_params("parallel", "arbitrary"),
        name="ffn_swiglu",
    )(h, *parts, w_out, gain.reshape(1, D_MODEL), wg, wu, wd)


def _router_kernel(h_ref, a_ref, b_ref, c4_ref, d_ref, wout_ref, g_ref, r_ref, lower_ref,
                   h1_ref, c_ref, comb_ref, rk_ref, cnt_ref):
    x = _mix_residual(h_ref, (a_ref, b_ref, c4_ref, d_ref), wout_ref)
    h1_ref[...] = x
    y = x * lax.rsqrt(jnp.mean(x * x, axis=-1, keepdims=True) + EPS) * g_ref[...]
    c_ref[...] = y.astype(BF16)
    y_hi, y_lo = _split2(y)
    r_hi, r_lo = _split2(r_ref[...])
    logits = _dot(y_hi, r_hi) + _dot(y_lo, r_hi) + _dot(y_hi, r_lo)
    lane = lax.broadcasted_iota(jnp.int32, logits.shape, 1)
    lane_f = lane.astype(F32)
    logits = jnp.where(lane < N_EXPERTS, logits, LOWEST)
    m1 = jnp.max(logits, axis=1, keepdims=True)
    i1 = jnp.min(jnp.where(logits == m1, lane_f, float(LANE)), axis=1, keepdims=True)
    rest = jnp.where(lane_f == i1, LOWEST, logits)
    m2 = jnp.max(rest, axis=1, keepdims=True)
    i2 = jnp.min(jnp.where(rest == m2, lane_f, float(LANE)), axis=1, keepdims=True)
    e2 = jnp.exp(m2 - m1)
    w1 = 1.0 / (1.0 + e2)
    comb_ref[...] = jnp.where(lane_f == i1, w1, jnp.where(lane_f == i2, e2 * w1, 0.0))
    chosen = (lane_f == i1) | (lane_f == i2)
    sel = jnp.where(chosen, 1.0, 0.0)
    rank = _dot(lower_ref[...], sel.astype(BF16))
    rk_ref[...] = jnp.where(chosen, rank, -1.0).astype(jnp.int32)
    cnt_ref[...] = jnp.broadcast_to(jnp.sum(sel, axis=0, keepdims=True), cnt_ref.shape).astype(jnp.int32)


def _router(h, parts, w_out, gain, router_pad, tm):
    T = h.shape[0]
    i = np.arange(tm)
    lower = jnp.asarray((i[None, :] < i[:, None]).astype(np.float32), BF16)
    sds = jax.ShapeDtypeStruct
    rows = lambda w: pl.BlockSpec((tm, w), lambda i: (i, 0))
    return pl.pallas_call(
        _router_kernel,
        grid=(T // tm,),
        in_specs=[rows(D_MODEL), rows(GROUP), rows(GROUP), rows(GROUP), rows(GROUP),
                  pl.BlockSpec((D_MODEL, D_MODEL), lambda i: (0, 0)),
                  pl.BlockSpec((1, D_MODEL), lambda i: (0, 0)),
                  pl.BlockSpec((D_MODEL, LANE), lambda i: (0, 0)),
                  pl.BlockSpec((tm, tm), lambda i: (0, 0))],
        out_specs=[rows(D_MODEL), rows(D_MODEL), rows(LANE), rows(LANE), pl.BlockSpec((8, LANE), lambda i: (i, 0))],
        out_shape=[sds((T, D_MODEL), F32), sds((T, D_MODEL), BF16), sds((T, LANE), F32), sds((T, LANE), jnp.int32),
                   sds((T // tm * 8, LANE), jnp.int32)],
        compiler_params=_params("parallel"),
        name="moe_router",
    )(h, *parts, w_out, gain.reshape(1, D_MODEL), router_pad, lower)


MOE_CHUNK = 512
MOE_ROWS = 512
MOE_HALF = MOE_ROWS // 2
MOE_TILE = 1024
MOE_ALIGN = 16
MOE_PAD_SEGS = MOE_TILE // MOE_ROWS


def _moe_plan(cnt, T):
    nch = T // MOE_CHUNK
    n_ce = cnt.reshape(nch, 8, LANE)[:, 0, :N_EXPERTS]
    cap = (n_ce + MOE_ALIGN - 1) // MOE_ALIGN * MOE_ALIGN
    tot = jnp.sum(cap, axis=0)
    ptot = (tot + MOE_TILE - 1) // MOE_TILE * MOE_TILE
    start = jnp.cumsum(ptot) - ptot
    lo_ce = start[None, :] + jnp.cumsum(cap, axis=0) - cap
    pad_lo = (start + tot)[:, None] + MOE_ROWS * jnp.arange(MOE_PAD_SEGS, dtype=jnp.int32)[None, :]
    lo = jnp.concatenate([lo_ce.T, pad_lo], axis=1)
    n = jnp.concatenate([n_ce.T, jnp.full((N_EXPERTS, MOE_PAD_SEGS), MOE_ROWS, jnp.int32)], axis=1)
    n_tiles = _moe_tiles(T)
    tile_end = jnp.cumsum(ptot // MOE_TILE)
    j = jnp.arange(n_tiles, dtype=jnp.int32)
    tile_e = jnp.minimum(jnp.sum((tile_end[None, :] <= j[:, None]).astype(jnp.int32), axis=1), N_EXPERTS - 1)
    valid = (j < tile_end[-1]).astype(jnp.int32)
    n_flat = jnp.concatenate([n.reshape(-1), tile_end[-1:] * MOE_PAD_SEGS]).astype(jnp.int32)
    return lo.reshape(-1).astype(jnp.int32), n_flat, tile_e, valid


def _moe_tiles(T):
    nch = T // MOE_CHUNK
    rows = 2 * T + nch * N_EXPERTS * (MOE_ALIGN - 1) + N_EXPERTS * (MOE_TILE - 1)
    return -(-rows // MOE_TILE) + 1


def _moe_gather_kernel(lo_ref, n_ref, c_ref, rkt_ref, x_hbm, buf, sem, *, nch):
    e = pl.program_id(0)
    c = pl.program_id(1)
    n_steps = pl.num_programs(0) * pl.num_programs(1)
    step = e * pl.num_programs(1) + c
    slot = step % 2
    rk_row = rkt_ref[pl.ds(e, 1), :]
    rk_row = jnp.where(c < nch, rk_row, -1)
    row = lax.broadcasted_iota(jnp.int32, (MOE_HALF, MOE_CHUNK), 0)
    chunk = c_ref[...]
    onehot = jnp.where(rk_row == row, 1.0, 0.0).astype(BF16)
    buf[slot, 0:MOE_HALF, :] = _dot(onehot, chunk).astype(BF16)

    @pl.when(n_ref[step] > MOE_HALF)
    def _():
        onehot = jnp.where(rk_row == row + MOE_HALF, 1.0, 0.0).astype(BF16)
        buf[slot, MOE_HALF:MOE_ROWS, :] = _dot(onehot, chunk).astype(BF16)

    def copy(k, s, rows):
        dst = x_hbm.at[pl.ds(pl.multiple_of(lo_ref[k], MOE_ALIGN), rows)]
        return pltpu.make_async_copy(buf.at[s, 0:rows], dst, sem.at[s])

    def for_size(k, fn):
        @pl.when(n_ref[k] > MOE_HALF)
        def _():
            fn(MOE_ROWS)

        @pl.when(n_ref[k] <= MOE_HALF)
        def _():
            fn(MOE_HALF)

    @pl.when(step > 0)
    def _():
        for_size(step - 1, lambda rows: copy(step - 1, 1 - slot, rows).wait())

    for_size(step, lambda rows: copy(step, slot, rows).start())

    @pl.when(step == n_steps - 1)
    def _():
        for_size(step, lambda rows: copy(step, slot, rows).wait())
        buf[0] = jnp.zeros((MOE_ROWS, D_MODEL), BF16)

        def fill(k, carry):
            dst = x_hbm.at[pl.ds(pl.multiple_of(k * MOE_ROWS, MOE_ROWS), MOE_ROWS)]
            cp = pltpu.make_async_copy(buf.at[0], dst, sem.at[0])
            cp.start()
            cp.wait()
            return carry

        lax.fori_loop(n_ref[n_steps], x_hbm.shape[0] // MOE_ROWS, fill, 0)


def _moe_gather(c_bf, rkt, lo, n, n_tiles):
    T = c_bf.shape[0]
    nch = T // MOE_CHUNK
    last = nch - 1
    grid_spec = pltpu.PrefetchScalarGridSpec(
        num_scalar_prefetch=2,
        grid=(N_EXPERTS, nch + MOE_PAD_SEGS),
        in_specs=[pl.BlockSpec((MOE_CHUNK, D_MODEL), lambda e, c, lo, n: (jnp.minimum(c, last), 0)),
                  pl.BlockSpec((N_EXPERTS, MOE_CHUNK), lambda e, c, lo, n: (0, jnp.minimum(c, last)))],
        out_specs=pl.BlockSpec(memory_space=pl.ANY),
        scratch_shapes=[pltpu.VMEM((2, MOE_ROWS, D_MODEL), BF16), pltpu.SemaphoreType.DMA((2,))],
    )
    return pl.pallas_call(
        functools.partial(_moe_gather_kernel, nch=nch),
        grid_spec=grid_spec,
        out_shape=jax.ShapeDtypeStruct((n_tiles * MOE_TILE, D_MODEL), BF16),
        compiler_params=_params("arbitrary", "arbitrary"),
        name="moe_gather",
    )(lo, n, c_bf, rkt)


def _moe_ffn_kernel(te_ref, valid_ref, x_ref, wg_ref, wu_ref, wd_ref, y_ref, acc_ref):
    j = pl.program_id(0)
    f = pl.program_id(1)
    last = pl.num_programs(1) - 1

    @pl.when(valid_ref[j] == 1)
    def _():
        x = x_ref[...]
        mid = _silu(_dot(x, wg_ref[...])) * _dot(x, wu_ref[...])
        part = _dot(mid.astype(BF16), wd_ref[...])

        @pl.when(f == 0)
        def _():
            acc_ref[...] = part

        @pl.when(f > 0)
        def _():
            acc_ref[...] += part

        @pl.when(f == last)
        def _():
            y_ref[...] = acc_ref[...].astype(BF16)

    @pl.when((valid_ref[j] == 0) & (f == last))
    def _():
        y_ref[...] = jnp.zeros(y_ref.shape, BF16)


def _moe_ffn(x_sorted, tile_e, valid, wg, wu, wd, tf):
    n_tiles = x_sorted.shape[0] // MOE_TILE
    F = wg.shape[2]
    nf = F // tf
    sq = pl.Squeezed()
    fsel = lambda j, f, te, va: jnp.where(va[j] == 1, f, nf - 1)
    grid_spec = pltpu.PrefetchScalarGridSpec(
        num_scalar_prefetch=2,
        grid=(n_tiles, nf),
        in_specs=[pl.BlockSpec((MOE_TILE, D_MODEL), lambda j, f, te, va: (j, 0)),
                  pl.BlockSpec((sq, D_MODEL, tf), lambda j, f, te, va: (te[j], 0, fsel(j, f, te, va))),
                  pl.BlockSpec((sq, D_MODEL, tf), lambda j, f, te, va: (te[j], 0, fsel(j, f, te, va))),
                  pl.BlockSpec((sq, tf, D_MODEL), lambda j, f, te, va: (te[j], fsel(j, f, te, va), 0))],
        out_specs=pl.BlockSpec((MOE_TILE, D_MODEL), lambda j, f, te, va: (j, 0)),
        scratch_shapes=[pltpu.VMEM((MOE_TILE, D_MODEL), F32)],
    )
    return pl.pallas_call(
        _moe_ffn_kernel,
        grid_spec=grid_spec,
        out_shape=jax.ShapeDtypeStruct((n_tiles * MOE_TILE, D_MODEL), BF16),
        compiler_params=_params("parallel", "arbitrary"),
        name="moe_experts",
    )(tile_e, valid, x_sorted, wg, wu, wd)


def _ple_update(x, p, gain, w_gate, w_proj):
    y = x * lax.rsqrt(jnp.mean(x * x, axis=-1, keepdims=True) + EPS) * gain
    gate = _sigmoid(_dot(y.astype(BF16), w_gate))
    return x + _dot(p.astype(BF16), w_proj) * gate


def _moe_combine_kernel(lo_ref, n_ref, h_ref, rk_ref, comb_ref, p_ref, g_ref, wgate_ref, wproj_ref, y_hbm,
                        o_ref, ybuf, sem, *, nch):
    c = pl.program_id(0)
    seg = lambda e: e * (nch + MOE_PAD_SEGS) + c

    def copy(e, rows):
        src = y_hbm.at[pl.ds(pl.multiple_of(lo_ref[seg(e)], MOE_ALIGN), rows)]
        return pltpu.make_async_copy(src, ybuf.at[e, 0:rows], sem.at[e])

    def for_size(e, fn):
        @pl.when(n_ref[seg(e)] > MOE_HALF)
        def _():
            fn(MOE_ROWS)

        @pl.when(n_ref[seg(e)] <= MOE_HALF)
        def _():
            fn(MOE_HALF)

    for e in range(N_EXPERTS):
        for_size(e, lambda rows, e=e: copy(e, rows).start())
    o_ref[...] = h_ref[...]
    rk = rk_ref[...]
    comb = comb_ref[...]
    lane = lax.broadcasted_iota(jnp.int32, (MOE_CHUNK, MOE_HALF), 1)
    for e in range(N_EXPERTS):
        for_size(e, lambda rows, e=e: copy(e, rows).wait())
        rank_col = rk[:, e:e + 1]
        w_col = comb[:, e:e + 1]
        onehot = jnp.where(rank_col == lane, 1.0, 0.0).astype(BF16)
        o_ref[...] += w_col * _dot(onehot, ybuf[e, 0:MOE_HALF, :])

        @pl.when(n_ref[seg(e)] > MOE_HALF)
        def _():
            onehot = jnp.where(rank_col == lane + MOE_HALF, 1.0, 0.0).astype(BF16)
            o_ref[...] += w_col * _dot(onehot, ybuf[e, MOE_HALF:MOE_ROWS, :])

    o_ref[...] = _ple_update(o_ref[...], p_ref[...], g_ref[...], wgate_ref[...], wproj_ref[...])


def _moe_combine(h, rk_pad, comb, y_sorted, lo, n, p, ple_gain, ple_wg, ple_wp):
    T = h.shape[0]
    nch = T // MOE_CHUNK
    const = lambda shape: pl.BlockSpec(shape, lambda c, lo, n: (0, 0))
    grid_spec = pltpu.PrefetchScalarGridSpec(
        num_scalar_prefetch=2,
        grid=(nch,),
        in_specs=[pl.BlockSpec((MOE_CHUNK, D_MODEL), lambda c, lo, n: (c, 0)),
                  pl.BlockSpec((MOE_CHUNK, LANE), lambda c, lo, n: (c, 0)),
                  pl.BlockSpec((MOE_CHUNK, LANE), lambda c, lo, n: (c, 0)),
                  pl.BlockSpec((MOE_CHUNK, PLE_DIM), lambda c, lo, n: (c, 0)),
                  const((1, D_MODEL)), const((D_MODEL, D_MODEL)), const((PLE_DIM, D_MODEL)),
                  pl.BlockSpec(memory_space=pl.ANY)],
        out_specs=pl.BlockSpec((MOE_CHUNK, D_MODEL), lambda c, lo, n: (c, 0)),
        scratch_shapes=[pltpu.VMEM((N_EXPERTS, MOE_ROWS, D_MODEL), BF16), pltpu.SemaphoreType.DMA((N_EXPERTS,))],
    )
    return pl.pallas_call(
        functools.partial(_moe_combine_kernel, nch=nch),
        grid_spec=grid_spec,
        out_shape=jax.ShapeDtypeStruct((T, D_MODEL), F32),
        compiler_params=_params("arbitrary"),
        name="moe_combine",
    )(lo, n, h, rk_pad, comb, p, ple_gain.reshape(1, D_MODEL), ple_wg, ple_wp, y_sorted)


def _moe(h, c_bf, comb, rk, cnt, wg, wu, wd, tf, ple):
    T = h.shape[0]
    lo, n, tile_e, valid = _moe_plan(cnt, T)
    x_sorted = _moe_gather(c_bf, rk[:, :N_EXPERTS].T, lo, n, _moe_tiles(T))
    y_sorted = _moe_ffn(x_sorted, tile_e, valid, wg, wu, wd, tf)
    return _moe_combine(h, rk, comb, y_sorted, lo, n, *ple)


def _ple_kernel(h_ref, p_ref, g_ref, wg_ref, wp_ref, o_ref):
    o_ref[...] = _ple_update(h_ref[...], p_ref[...], g_ref[...], wg_ref[...], wp_ref[...])


def _ple(h, p, gain, wg, wp, tm):
    T = h.shape[0]
    return pl.pallas_call(
        _ple_kernel,
        grid=(T // tm,),
        in_specs=[pl.BlockSpec((tm, D_MODEL), lambda i: (i, 0)),
                  pl.BlockSpec((tm, PLE_DIM), lambda i: (i, 0)),
                  pl.BlockSpec((1, D_MODEL), lambda i: (0, 0)),
                  pl.BlockSpec((D_MODEL, D_MODEL), lambda i: (0, 0)),
                  pl.BlockSpec((PLE_DIM, D_MODEL), lambda i: (0, 0))],
        out_specs=pl.BlockSpec((tm, D_MODEL), lambda i: (i, 0)),
        out_shape=jax.ShapeDtypeStruct((T, D_MODEL), F32),
        compiler_params=_params("parallel"),
        name="ple_gate",
    )(h, p, gain.reshape(1, D_MODEL), wg, wp)


def _tiles(T, S):
    pick = lambda n, pref: max(t for t in pref if n % t == 0)
    return dict(
        proj_m=pick(T, (512, 256, 128)), proj_n=NC,
        prep_m=pick(T, (512, 256, 128)),
        attn_q=256, attn_k=pick(S, (512, 256)),
        nsa_q=pick(S, (512, 256)), nsa_k=pick(S, (512, 256)),
        lin_m=pick(S, (512, 256, 128, 64)),
        row_m=pick(T, (512, 256, 128)),
        ffn_m=pick(T, (1024, 512, 256, 128)), ffn_f=512,
    )


def kernel(x, p, norm_attn, w_in, w_out, nsa_cmp_pos, nsa_cmp_w1, nsa_cmp_w2, nsa_qk_gain, diff_qk_gain, diff_lambda, diff_norm, gla_w_gate2, gla_b_gate, gla_norm, hgrn_lb_logits, hgrn_norm, norm_ffn, ffn_w_gate, ffn_w_up, ffn_w_down, moe_router, moe_w_gate, moe_w_up, moe_w_down, ple_norm, ple_w_gate, ple_w_proj):
    B, S, _ = x.shape
    depth = w_in.shape[0]
    T = B * S
    t = _tiles(T, S)
    ncp = S // NSA_CMP_STRIDE
    half = NSA_CMP_STRIDE * HD
    cols = jnp.asarray(np.maximum(_COLS, 0), jnp.int32)
    col_mask = jnp.asarray(_COLS >= 0)
    ones_row = jnp.ones((GROUP,), F32)

    h = x.reshape(T, D_MODEL)
    for i in range(depth):
        w_in_r = jnp.where(col_mask[None, :], jnp.take(w_in[i], cols, axis=1), 0.0).astype(BF16)
        gains = jnp.stack([
            jnp.tile(nsa_qk_gain[i, 0], HEADS) * (HD ** -0.5 * LOG2E),
            jnp.tile(diff_qk_gain[i, 0], 2 * HEADS) * (DIFF_QK ** -0.5 * LOG2E),
            jnp.tile(diff_qk_gain[i, 1], 2 * HEADS),
            jnp.tile(nsa_qk_gain[i, 2], HEADS),
            jnp.tile(nsa_qk_gain[i, 3], HEADS),
            ones_row, ones_row, ones_row])
        pos = nsa_cmp_pos[i].reshape(2, 2, half)
        w1 = nsa_cmp_w1[i].astype(BF16)
        w2p = jnp.pad(nsa_cmp_w2[i], ((0, 0), (0, 0), (0, LANE - HD))).astype(BF16)
        kc_gain = jnp.pad(nsa_qk_gain[i, 1], (0, LANE - HD)).reshape(1, LANE)
        w2pad = jnp.zeros((LANE, LANE), F32).at[:GLA_RANK].set(gla_w_gate2[i])
        diff_gain_col = jnp.broadcast_to(diff_norm[i][:, None], (HD, t["attn_q"]))

        u = _norm_matmul(h, norm_attn[i], w_in_r, t["proj_m"], t["proj_n"])
        nqt, dqt, dk, dvt, kvs, vst, kvw, vwt = _prep(u, gains, t["prep_m"])
        kv = u[:, _B128["kvcmp"] * LANE:(_B128["kvcmp"] + 1) * LANE]
        xk = kv[:, :HD].reshape(B, ncp, half)
        xv = kv[:, HD:].reshape(B, ncp, half)
        kc, vct = _compress(xk, xv, pos, w1, w2p, kc_gain)
        o_a = _nsa_attention(u, nqt, kc, vct, kvs, vst, kvw, vwt, B, S, t["nsa_q"], t["nsa_k"], t["nsa_q"])
        o_b = _diff_attention(dqt, dk, dvt, diff_lambda[i], diff_gain_col, B, S, t["attn_q"], t["attn_k"], i)
        o_c, o_d = _linear_mixers(u, w2pad, gla_b_gate[i].reshape(1, LANE),
                                  jnp.tile(gla_norm[i], HEADS).reshape(1, GROUP), hgrn_lb_logits,
                                  jnp.tile(hgrn_norm[i], HEADS).reshape(1, GROUP), B, S, t["lin_m"], i)
        parts = (o_a, o_b, o_c, o_d)
        w_out_bf = w_out[i].astype(BF16)
        ple = (p[i].reshape(T, PLE_DIM), ple_norm[i], ple_w_gate[i].astype(BF16), ple_w_proj[i].astype(BF16))
        j = i // 2
        if i % 2 == 0:
            h = _ffn(h, parts, w_out_bf, norm_ffn[i], ffn_w_gate[j].astype(BF16), ffn_w_up[j].astype(BF16),
                     ffn_w_down[j].astype(BF16), t["ffn_m"], t["ffn_f"])
            h = _ple(h, *ple, t["row_m"])
        else:
            router_pad = jnp.zeros((D_MODEL, LANE), F32).at[:, :N_EXPERTS].set(moe_router[j])
            h, c_bf, comb, rk, cnt = _router(h, parts, w_out_bf, norm_ffn[i], router_pad, MOE_CHUNK)
            h = _moe(h, c_bf, comb, rk, cnt, moe_w_gate[j].astype(BF16), moe_w_up[j].astype(BF16),
                     moe_w_down[j].astype(BF16), t["ffn_f"], ple)
    return h.reshape(B, S, D_MODEL)
```

```python
import functools
import math

import numpy as np
import jax
import jax.numpy as jnp
from jax import lax
from jax.experimental import pallas as pl
from jax.experimental.pallas import tpu as pltpu

F32 = jnp.float32
BF16 = jnp.bfloat16

D_MODEL = 1024
HEADS = 4
HD = 64
GROUP = HEADS * HD
NSA_CMP_LEN = 32
NSA_CMP_STRIDE = 16
NSA_CMP_HIDDEN = 4 * HD
NSA_SEL_LEN = 64
NSA_TOPK = 16
NSA_WINDOW = 512
NSA_SUBTILES = 2
DIFF_QK = HD // 2
GLA_DK = HD // 2
GLA_RANK = 16
GLA_TAU = 16.0
CHUNK = 64
D_FF = 7 * D_MODEL // 2
N_EXPERTS = 8
PLE_DIM = 256
EPS = 1e-6
NEG = -1e30
BIG = 1e30
LOWEST = -3.0e38
LOG2E = 1.4426950408889634

VMEM_LIMIT = 52 * 1024 * 1024
LANE = 128

_SRC = dict(nsa_q=0, k_cmp=256, v_cmp=320, k_slc=384, v_slc=448, k_win=512, v_win=576, nsa_g=640,
            d_q=652, d_k=908, d_v=1164, g_q=1420, g_k=1548, g_v=1676, g_lr=1932, g_og=1948,
            r_q=2204, r_f=2460, r_i=2716, r_og=2972)
IN_COLS = 3228

_B256 = dict(nsa_q=0, d_q=1, d_k=2, d_v=3, g_v=4, g_og=5, r_q=6, r_f=7, r_i=8, r_og=9)
_B128 = dict(g_q=20, g_k=21, g_lr=22, kvcmp=23, kvslc=24, kvwin=25, nsa_g=26)
NC = 27 * 128


def _column_map():
    cols = -np.ones((NC,), np.int64)

    def put(dst, src, width):
        cols[dst:dst + width] = np.arange(src, src + width)

    for name in _B256:
        put(_B256[name] * 256, _SRC[name], 256)
    put(_B128["g_q"] * 128, _SRC["g_q"], 128)
    put(_B128["g_k"] * 128, _SRC["g_k"], 128)
    put(_B128["g_lr"] * 128, _SRC["g_lr"], GLA_RANK)
    put(_B128["kvcmp"] * 128, _SRC["k_cmp"], 128)
    put(_B128["kvslc"] * 128, _SRC["k_slc"], 128)
    put(_B128["kvwin"] * 128, _SRC["k_win"], 128)
    put(_B128["nsa_g"] * 128, _SRC["nsa_g"], 3 * HEADS)
    return cols


_COLS = _column_map()


def _dot(a, b):
    return jnp.dot(a, b, preferred_element_type=F32)


def _dot_nt(a, b):
    return lax.dot_general(a, b, (((1,), (1,)), ((), ())), preferred_element_type=F32)


def _dot_tn(a, b):
    return lax.dot_general(a, b, (((0,), (0,)), ((), ())), preferred_element_type=F32)


def _split2(x):
    hi = x.astype(BF16)
    lo = (x - hi.astype(F32)).astype(BF16)
    return hi, lo


def _split3(x):
    hi = x.astype(BF16)
    r = x - hi.astype(F32)
    mid = r.astype(BF16)
    lo = (r - mid.astype(F32)).astype(BF16)
    return hi, mid, lo


def _group_mean(x, ones_bf, group):
    hi, lo = _split2(x)
    return (_dot(hi, ones_bf) + _dot(lo, ones_bf)) * (1.0 / group)


def _group_rms(x, ones_bf, group):
    return x * lax.rsqrt(_group_mean(x * x, ones_bf, group) + EPS)


def _sigmoid(x):
    return 1.0 / (1.0 + jnp.exp(-x))


def _silu(x):
    return x * _sigmoid(x)


def _params(*sem):
    return pltpu.CompilerParams(dimension_semantics=sem, vmem_limit_bytes=VMEM_LIMIT)


def _block_ones(n, group):
    i = np.arange(n)
    return jnp.asarray((i[:, None] // group == i[None, :] // group).astype(np.float32), BF16)


def _norm_matmul_kernel(x_ref, g_ref, w_ref, o_ref, xn_ref):
    @pl.when(pl.program_id(1) == 0)
    def _():
        x = x_ref[...]
        y = x * lax.rsqrt(jnp.mean(x * x, axis=-1, keepdims=True) + EPS) * g_ref[...]
        xn_ref[...] = y.astype(BF16)

    o_ref[...] = _dot(xn_ref[...], w_ref[...])


def _norm_matmul(x, gain, w_bf, tm, tn):
    T, K = x.shape
    N = w_bf.shape[1]
    return pl.pallas_call(
        _norm_matmul_kernel,
        grid=(T // tm, N // tn),
        in_specs=[pl.BlockSpec((tm, K), lambda i, j: (i, 0)),
                  pl.BlockSpec((1, K), lambda i, j: (0, 0)),
                  pl.BlockSpec((K, tn), lambda i, j: (0, j))],
        out_specs=pl.BlockSpec((tm, tn), lambda i, j: (i, j)),
        out_shape=jax.ShapeDtypeStruct((T, N), F32),
        scratch_shapes=[pltpu.VMEM((tm, K), BF16)],
        compiler_params=_params("parallel", "arbitrary"),
        name="in_proj",
    )(x, gain.reshape(1, K), w_bf)


def _prep_kernel(nq_ref, dq_ref, dk_ref, dv_ref, kvs_ref, kvw_ref, gains_ref, ones64_ref, ones32_ref,
                 ones64h_ref, o_nq, o_dq, o_dk, o_dv, o_kvs, o_vs, o_kvw, o_vw):
    ones64 = ones64_ref[...]
    ones32 = ones32_ref[...]
    ones64h = ones64h_ref[...]
    o_nq[...] = (_group_rms(nq_ref[...], ones64, HD) * gains_ref[0:1, :]).T.astype(BF16)
    o_dq[...] = (_group_rms(dq_ref[...], ones32, DIFF_QK) * gains_ref[1:2, :]).T.astype(BF16)
    o_dk[...] = (_group_rms(dk_ref[...], ones32, DIFF_QK) * gains_ref[2:3, :]).astype(BF16)
    o_dv[...] = dv_ref[...].T.astype(BF16)
    lane = lax.broadcasted_iota(jnp.int32, kvs_ref.shape, 1)
    for kv_ref, gain, o_kv, o_v in ((kvs_ref, gains_ref[3:4, 0:LANE], o_kvs, o_vs),
                                    (kvw_ref, gains_ref[4:5, 0:LANE], o_kvw, o_vw)):
        x = kv_ref[...]
        o_kv[...] = jnp.where(lane < HD, _group_rms(x, ones64h, HD) * gain, x).astype(BF16)
        o_v[...] = x.T[HD:2 * HD, :].astype(BF16)


def _prep(u, gains, tm):
    T = u.shape[0]
    c256 = lambda name: pl.BlockSpec((tm, GROUP), lambda i, c=_B256[name]: (i, c))
    c128 = lambda name: pl.BlockSpec((tm, LANE), lambda i, c=_B128[name]: (i, c))
    const = lambda shape: pl.BlockSpec(shape, lambda i: (0, 0))
    rows = lambda w: pl.BlockSpec((tm, w), lambda i: (i, 0))
    colsT = lambda h: pl.BlockSpec((h, tm), lambda i: (0, i))
    sds = jax.ShapeDtypeStruct
    return pl.pallas_call(
        _prep_kernel,
        grid=(T // tm,),
        in_specs=[c256("nsa_q"), c256("d_q"), c256("d_k"), c256("d_v"), c128("kvslc"), c128("kvwin"),
                  const((8, GROUP)), const((GROUP, GROUP)), const((GROUP, GROUP)), const((LANE, LANE))],
        out_specs=[colsT(GROUP), colsT(GROUP), rows(GROUP), colsT(GROUP),
                   rows(LANE), colsT(HD), rows(LANE), colsT(HD)],
        out_shape=[sds((GROUP, T), BF16), sds((GROUP, T), BF16), sds((T, GROUP), BF16), sds((GROUP, T), BF16),
                   sds((T, LANE), BF16), sds((HD, T), BF16), sds((T, LANE), BF16), sds((HD, T), BF16)],
        compiler_params=_params("parallel"),
        name="attn_prep",
    )(u, u, u, u, u, u, gains, _block_ones(GROUP, HD), _block_ones(GROUP, DIFF_QK), _block_ones(LANE, HD))


def _compress_kernel(xk_ref, xv_ref, pos_ref, w1_ref, w2_ref, gain_ref, ones64h_ref, kc_ref, vc_ref):
    half = NSA_CMP_STRIDE * HD
    n_rows = xk_ref.shape[0]

    def compress(x, j):
        top = (x + pos_ref[j, 0:1, :]).astype(BF16)
        bot = (x + pos_ref[j, 1:2, :]).astype(BF16)
        a = _dot(top, w1_ref[j, 0:half, :])
        b = _dot(bot, w1_ref[j, half:2 * half, :])
        hidden = a + pltpu.roll(b, n_rows - 1, 0)
        return _dot(_silu(hidden).astype(BF16), w2_ref[j])

    kc = compress(xk_ref[...], 0)
    kc_ref[...] = _group_rms(kc, ones64h_ref[...], HD) * gain_ref[...]
    vc_ref[...] = compress(xv_ref[...], 1).T[0:HD, :].astype(BF16)


def _compress(xk, xv, pos, w1_bf, w2p_bf, gain_row):
    B, ncp, half = xk.shape
    sq = pl.Squeezed()
    full = lambda shape: pl.BlockSpec(shape, lambda b: (0,) * len(shape))
    return pl.pallas_call(
        _compress_kernel,
        grid=(B,),
        in_specs=[pl.BlockSpec((sq, ncp, half), lambda b: (b, 0, 0)),
                  pl.BlockSpec((sq, ncp, half), lambda b: (b, 0, 0)),
                  full((2, 2, half)), full((2, 2 * half, NSA_CMP_HIDDEN)), full((2, NSA_CMP_HIDDEN, LANE)),
                  full((1, LANE)), full((LANE, LANE))],
        out_specs=[pl.BlockSpec((sq, ncp, LANE), lambda b: (b, 0, 0)),
                   pl.BlockSpec((sq, HD, ncp), lambda b: (b, 0, 0))],
        out_shape=[jax.ShapeDtypeStruct((B, ncp, LANE), F32), jax.ShapeDtypeStruct((B, HD, ncp), BF16)],
        compiler_params=_params("parallel"),
        name="nsa_compress",
    )(xk, xv, pos, w1_bf, w2p_bf, gain_row, _block_ones(LANE, HD))


def _softmax_step_t(s, v_t, m_old, l_old, acc_ref):
    m_new = jnp.maximum(m_old, jnp.max(s, axis=0, keepdims=True))
    alpha = jnp.exp2(m_old - m_new)
    p = jnp.exp2(s - m_new)
    acc_ref[...] = alpha * acc_ref[...] + _dot(v_t, p.astype(BF16))
    return m_new, alpha * l_old + jnp.sum(p, axis=0, keepdims=True)


def _nsa_kernel(qt_ref, g_ref, kc_ref, vct_ref, kvs_ref, vst_ref, kvw_ref, vwt_ref, ovl_ref, exp_ref,
                o_ref, qs_ref, *acc_refs, tq, tk, tw, ksel):
    i = pl.program_id(1)
    t0 = i * tq
    ncp = kc_ref.shape[0]
    nsel = ovl_ref.shape[0]
    acc_s, acc_w = acc_refs[:HEADS], acc_refs[HEADS:]

    qt = qt_ref[...]
    qs_ref[HD:LANE, :] = jnp.zeros((LANE - HD, HEADS * tq), BF16)
    for h in range(HEADS):
        qs_ref[0:HD, h * tq:(h + 1) * tq] = qt[h * HD:(h + 1) * HD, :]
        acc_s[h][...] = jnp.zeros((HD, tq), F32)
        acc_w[h][...] = jnp.zeros((HD, tq), F32)

    kc_hi, kc_lo = _split2(kc_ref[...])
    n_idx = lax.broadcasted_iota(jnp.int32, (ncp, tq), 0)
    t_lane = t0 + lax.broadcasted_iota(jnp.int32, (ncp, tq), 1)
    ok = n_idx * NSA_CMP_STRIDE + (NSA_CMP_LEN - 1) <= t_lane
    cmp_scores = _dot(kc_hi, qs_ref[...]) + _dot(kc_lo, qs_ref[...])
    o_c = []
    psum = jnp.zeros((ncp, tq), F32)
    for h in range(HEADS):
        s = jnp.where(ok, cmp_scores[:, h * tq:(h + 1) * tq], NEG)
        e = jnp.exp2(s - jnp.max(s, axis=0, keepdims=True))
        p = jnp.where(ok, e / jnp.sum(e, axis=0, keepdims=True), 0.0)
        o_c.append(_dot(vct_ref[...], p.astype(BF16)))
        psum = psum + p

    p_hi, p_lo = _split2(psum)
    imp = _dot(ovl_ref[...], p_hi) + _dot(ovl_ref[...], p_lo)
    blk = lax.broadcasted_iota(jnp.int32, (nsel, tq), 0)
    t_col = t0 + lax.broadcasted_iota(jnp.int32, (nsel, tq), 1)
    cur = t_col // NSA_SEL_LEN
    forced = (blk == 0) | (blk == cur) | (blk == cur - 1)
    vals = jnp.where(forced, BIG, jnp.where(blk * NSA_SEL_LEN <= t_col, imp, NEG))
    blk_f = blk.astype(F32)
    sel = jnp.zeros((nsel, tq), F32)
    for _ in range(ksel):
        mx = jnp.max(vals, axis=0, keepdims=True)
        first = jnp.min(jnp.where(vals == mx, blk_f, float(nsel)), axis=0, keepdims=True)
        pick = blk_f == first
        sel = jnp.where(pick, 1.0, sel)
        vals = jnp.where(pick, LOWEST, vals)
    if nsel < LANE:
        sel = jnp.concatenate([sel, jnp.zeros((LANE - nsel, tq), F32)], axis=0)
    sel_bf = sel.astype(BF16)

    stats0 = tuple((jnp.full((1, tq), NEG, F32), jnp.zeros((1, tq), F32)) for _ in range(HEADS))

    def sweep(lo, hi, k_ref, vt_ref, accs, width, bias_fn):
        def body(kt, stats):
            k0 = pl.multiple_of(kt * width, width)
            k = k_ref[pl.ds(k0, width), :]
            scores = [_dot(k, qs_ref[:, h * tq:(h + 1) * tq]) for h in range(HEADS)]
            kpos = k0 + lax.broadcasted_iota(jnp.int32, (width, tq), 0)
            tpos = t0 + lax.broadcasted_iota(jnp.int32, (width, tq), 1)
            bias = bias_fn(k0, kpos, tpos)
            vt = vt_ref[:, pl.ds(k0, width)]
            return tuple(_softmax_step_t(scores[h] + bias, vt, stats[h][0], stats[h][1], accs[h])
                         for h in range(HEADS))

        return lax.fori_loop(lo, hi, body, stats0)

    def sel_bias(k0, kpos, tpos):
        chosen = _dot(exp_ref[pl.ds(k0, kpos.shape[0]), :], sel_bf)
        return jnp.where((chosen > 0.5) & (kpos <= tpos), 0.0, NEG)

    st_s = sweep(0, (t0 + tq - 1) // tk + 1, kvs_ref, vst_ref, acc_s, tk, sel_bias)

    def win_bias(k0, kpos, tpos):
        return jnp.where((kpos <= tpos) & (kpos > tpos - NSA_WINDOW), 0.0, NEG)

    st_w = sweep(jnp.maximum(t0 - NSA_WINDOW, 0) // tw, (t0 + tq - 1) // tw + 1,
                 kvw_ref, vwt_ref, acc_w, tw, win_bias)

    gates = _sigmoid(g_ref[...].T)
    outs = []
    for h in range(HEADS):
        outs.append(gates[3 * h:3 * h + 1, :] * o_c[h]
                    + gates[3 * h + 1:3 * h + 2, :] * (acc_s[h][...] / st_s[h][1])
                    + gates[3 * h + 2:3 * h + 3, :] * (acc_w[h][...] / st_w[h][1]))
    o_ref[...] = jnp.concatenate(outs, axis=0).T


def _nsa_attention(u, nqt, kc, vct, kvs, vst, kvw, vwt, B, S, tq, tk, tw):
    nq = S // tq
    ncp = S // NSA_CMP_STRIDE
    nsel = S // NSA_SEL_LEN
    ksel = min(NSA_TOPK, nsel)
    n_cmp = (S - NSA_CMP_LEN) // NSA_CMP_STRIDE + 1
    cmp_start = np.arange(ncp) * NSA_CMP_STRIDE
    sel_start = np.arange(nsel) * NSA_SEL_LEN
    overlap = ((cmp_start[:, None] <= sel_start[None, :] + NSA_SEL_LEN - 1)
               & (cmp_start[:, None] + NSA_CMP_LEN - 1 >= sel_start[None, :])
               & (np.arange(ncp)[:, None] < n_cmp))
    ovl_t = jnp.asarray(overlap.T.astype(np.float32), BF16)
    expand = (jnp.arange(S, dtype=jnp.int32)[:, None] // NSA_SEL_LEN
              == jnp.arange(LANE, dtype=jnp.int32)[None, :]).astype(BF16)
    sq = pl.Squeezed()
    seq_rows = pl.BlockSpec((S, LANE), lambda b, i: (b, 0))
    seq_cols = pl.BlockSpec((HD, S), lambda b, i: (0, b))
    kern = functools.partial(_nsa_kernel, tq=tq, tk=tk, tw=tw, ksel=ksel)
    return pl.pallas_call(
        kern,
        grid=(B, nq),
        in_specs=[pl.BlockSpec((GROUP, tq), lambda b, i: (0, b * nq + i)),
                  pl.BlockSpec((tq, LANE), lambda b, i, c=_B128["nsa_g"]: (b * nq + i, c)),
                  pl.BlockSpec((sq, ncp, LANE), lambda b, i: (b, 0, 0)),
                  pl.BlockSpec((sq, HD, ncp), lambda b, i: (b, 0, 0)),
                  seq_rows, seq_cols, seq_rows, seq_cols,
                  pl.BlockSpec((nsel, ncp), lambda b, i: (0, 0)),
                  pl.BlockSpec((S, LANE), lambda b, i: (0, 0))],
        out_specs=pl.BlockSpec((tq, GROUP), lambda b, i: (b * nq + i, 0)),
        out_shape=jax.ShapeDtypeStruct((B * S, GROUP), F32),
        scratch_shapes=[pltpu.VMEM((LANE, HEADS * tq), BF16)] + [pltpu.VMEM((HD, tq), F32)] * (2 * HEADS),
        compiler_params=_params("parallel", "arbitrary"),
        name="nsa_attention",
    )(nqt, u, kc, vct, kvs, vst, kvw, vwt, ovl_t, expand)


def _diff_kernel(lam_ref, qt_ref, k_ref, vt_ref, gain_ref, o_ref, qs_ref, *acc_refs, tq, tk, lam_init):
    i = pl.program_id(1)
    t0 = i * tq
    lanes = 2 * tq

    qt = qt_ref[...]
    row = lax.broadcasted_iota(jnp.int32, (GROUP, tq), 0)
    zero = jnp.zeros_like(qt)
    for h in range(HEADS):
        qs_ref[h] = jnp.concatenate([jnp.where(row // DIFF_QK == 2 * h, qt, zero),
                                     jnp.where(row // DIFF_QK == 2 * h + 1, qt, zero)], axis=1)
        acc_refs[h][...] = jnp.zeros((HD, lanes), F32)

    def step(k0, width, masked, stats):
        k = k_ref[pl.ds(k0, width), :]
        scores = [_dot(k, qs_ref[h]) for h in range(HEADS)]
        out = []
        for h in range(HEADS):
            s = scores[h]
            if masked:
                kpos = k0 + lax.broadcasted_iota(jnp.int32, (width, lanes), 0)
                tpos = t0 + lax.broadcasted_iota(jnp.int32, (width, lanes), 1) % tq
                s = jnp.where(kpos <= tpos, s, NEG)
            out.append(_softmax_step_t(s, vt_ref[h * HD:(h + 1) * HD, pl.ds(k0, width)],
                                       stats[h][0], stats[h][1], acc_refs[h]))
        return tuple(out)

    stats0 = tuple((jnp.full((1, lanes), NEG, F32), jnp.zeros((1, lanes), F32)) for _ in range(HEADS))
    n_wide = t0 // tk
    stats = lax.fori_loop(0, n_wide, lambda kt, c: step(pl.multiple_of(kt * tk, tk), tk, False, c), stats0)
    stats = lax.fori_loop(n_wide * (tk // tq), t0 // tq,
                          lambda kt, c: step(pl.multiple_of(kt * tq, tq), tq, False, c), stats)
    stats = step(pl.multiple_of(t0, tq), tq, True, stats)

    lam = lam_ref[...]
    lam_full = (jnp.exp(jnp.sum(lam[0:1] * lam[1:2], axis=1, keepdims=True))
                - jnp.exp(jnp.sum(lam[2:3] * lam[3:4], axis=1, keepdims=True)) + lam_init)
    outs = []
    for h in range(HEADS):
        r = acc_refs[h][...] / stats[h][1]
        d = r[:, 0:tq] - lam_full * r[:, tq:lanes]
        d = d * lax.rsqrt(jnp.mean(d * d, axis=0, keepdims=True) + EPS)
        outs.append(d * gain_ref[...] * (1.0 - lam_init))
    o_ref[...] = jnp.concatenate(outs, axis=0).T


def _diff_attention(dqt, dk, dvt, lam, gain_col, B, S, tq, tk, layer_idx):
    nq = S // tq
    lam_init = 0.8 - 0.6 * math.exp(-0.3 * layer_idx)
    kern = functools.partial(_diff_kernel, tq=tq, tk=tk, lam_init=lam_init)
    return pl.pallas_call(
        kern,
        grid=(B, nq),
        in_specs=[pl.BlockSpec((4, DIFF_QK), lambda b, i: (0, 0)),
                  pl.BlockSpec((GROUP, tq), lambda b, i: (0, b * nq + i)),
                  pl.BlockSpec((S, GROUP), lambda b, i: (b, 0)),
                  pl.BlockSpec((GROUP, S), lambda b, i: (0, b)),
                  pl.BlockSpec((HD, tq), lambda b, i: (0, 0))],
        out_specs=pl.BlockSpec((tq, GROUP), lambda b, i: (b * nq + i, 0)),
        out_shape=jax.ShapeDtypeStruct((B * S, GROUP), F32),
        scratch_shapes=[pltpu.VMEM((HEADS, GROUP, 2 * tq), BF16)] + [pltpu.VMEM((HD, 2 * tq), F32)] * HEADS,
        compiler_params=_params("parallel", "arbitrary"),
        name="diff_attention",
    )(lam, dqt, dk, dvt, gain_col)


_LEVELS = (32, 16, 8, 4, 2, 1)
LIN_GROUP = 4


def _stack_heads(x, lane, group, count):
    zero = jnp.zeros_like(x)
    return jnp.concatenate([jnp.where(lane // group == g, x, zero) for g in range(count)], axis=0)


def _unstack_heads(x4, lane, rows):
    out = jnp.where(lane // HD == 0, x4[0:rows], 0.0)
    for h in range(1, HEADS):
        out = out + jnp.where(lane // HD == h, x4[h * rows:(h + 1) * rows], 0.0)
    return out


def _linear_consts():
    r = np.arange(CHUNK)[:, None]
    t = np.arange(CHUNK)[None, :]
    tri = (t <= r).astype(np.float32)
    masks = [(r // (2 * s) == t // (2 * s)) for s in _LEVELS] + [r == t]
    mall = np.stack([np.tile(m.astype(np.float32), (1, HEADS)) for m in masks])
    return jnp.asarray(tri, BF16), jnp.asarray(mall, F32)


def _level_exponent(s, lg, b, row):
    if s == 1:
        return jnp.where((row & 1) != 0, 0.0, pltpu.roll(lg, CHUNK - 1, 0))
    if s == 2:
        nxt1 = pltpu.roll(lg, CHUNK - 1, 0)
        nxt2 = pltpu.roll(lg, CHUNK - 2, 0)
        r4 = row & 3
        return jnp.where(r4 == 0, nxt1 + nxt2, jnp.where(r4 == 1, nxt1, jnp.where(r4 == 2, 0.0, lg)))
    mids = [jnp.broadcast_to(b[m:m + 1, :], (2 * s, b.shape[1])) for m in range(s, CHUNK, 2 * s)]
    d = b - (jnp.concatenate(mids, axis=0) if len(mids) > 1 else mids[0])
    return jnp.where((row & s) != 0, d, -d)


def _linear_chunks(items, tri, mall_ref):
    bs = []
    for q, k, v, lg, state_ref, dk in items:
        hi, mid, lo = _split3(lg)
        bs.append(_dot(tri, hi) + _dot(tri, mid) + _dot(tri, lo))
    ats = []
    for (q, k, v, lg, state_ref, dk), b in zip(items, bs):
        dkh = HEADS * dk
        row = lax.broadcasted_iota(jnp.int32, (CHUNK, dkh), 0)
        lane_k = lax.broadcasted_iota(jnp.int32, (CHUNK, dkh), 1)
        a_t = mall_ref[len(_LEVELS)] * _dot_nt(k.astype(BF16), _stack_heads(q.astype(BF16), lane_k, dk, HEADS))
        for li, s in enumerate(_LEVELS):
            e = jnp.exp(_level_exponent(s, lg, b, row))
            upper = (row & s) != 0
            qt = jnp.where(upper, q * e, 0.0).astype(BF16)
            kt = jnp.where(upper, 0.0, k * e).astype(BF16)
            a_t = a_t + mall_ref[li] * _dot_nt(kt, _stack_heads(qt, lane_k, dk, HEADS))
        ats.append(a_t)
    lane_v = lax.broadcasted_iota(jnp.int32, (CHUNK, GROUP), 1)
    partial = []
    for (q, k, v, lg, state_ref, dk), b, a_t in zip(items, bs, ats):
        v_bf = v.astype(BF16)
        o_intra = _unstack_heads(_dot_tn(a_t.astype(BF16), v_bf), lane_v, CHUNK)
        e_b = jnp.exp(b)
        e_u = jnp.exp(b[CHUNK - 1:CHUNK, :] - b)
        kv = _dot_tn(v_bf, (k * e_u).astype(BF16))
        partial.append((o_intra, e_b, kv))
    outs = []
    for (q, k, v, lg, state_ref, dk), (o_intra, e_b, kv) in zip(items, partial):
        dkh = HEADS * dk
        st = state_ref[...]
        o_inter = _dot_nt((q * e_b).astype(BF16), st.astype(BF16))
        srow = lax.broadcasted_iota(jnp.int32, (GROUP, dkh), 0)
        scol = lax.broadcasted_iota(jnp.int32, (GROUP, dkh), 1)
        state_ref[...] = st * e_b[CHUNK - 1:CHUNK, :] + jnp.where(srow // HD == scol // dk, kv, 0.0)
        outs.append(o_inter + o_intra)
    return outs


def _linear_kernel(gq_ref, gk_ref, gv_ref, lr_ref, gog_ref, w2_ref, b_ref, ggain_ref,
                   rq_ref, rf_ref, ri_ref, rog_ref, lbl_ref, rgain_ref, tri_ref, mall_ref, ones64_ref,
                   og_ref, or_ref, gstate_ref, rstate_ref, *, tm, layer_idx):
    @pl.when(pl.program_id(1) == 0)
    def _():
        gstate_ref[...] = jnp.zeros_like(gstate_ref)
        rstate_ref[...] = jnp.zeros_like(rstate_ref)

    tri = tri_ref[...]
    ones64 = ones64_ref[...]
    w_hi, w_lo = _split2(w2_ref[...])
    logits = lbl_ref[...]
    ez = jnp.exp(logits - jnp.max(logits, axis=0, keepdims=True))
    probs = ez / jnp.sum(ez, axis=0, keepdims=True)
    lb = jnp.zeros((1, GROUP), F32)
    for j in range(1, layer_idx + 1):
        lb = lb + probs[j:j + 1]

    def body(c, carry):
        items, sinks = [], []
        for g in range(LIN_GROUP):
            rs = pl.ds(pl.multiple_of((c * LIN_GROUP + g) * CHUNK, CHUNK), CHUNK)
            lr_hi, lr_lo = _split2(lr_ref[rs, :])
            x = _dot(lr_hi, w_hi) + _dot(lr_lo, w_hi) + _dot(lr_hi, w_lo) + b_ref[...]
            lg = (jnp.minimum(x, 0.0) - jnp.log(1.0 + jnp.exp(-jnp.abs(x)))) * (1.0 / GLA_TAU)
            items.append((gq_ref[rs, :] * (GLA_DK ** -0.5), gk_ref[rs, :], gv_ref[rs, :], lg, gstate_ref, GLA_DK))
            sinks.append((og_ref, ggain_ref, gog_ref, rs))
            z = rf_ref[rs, :]
            f = lb + (1.0 - lb) * _sigmoid(z)
            items.append((rq_ref[rs, :], (1.0 - lb) * _sigmoid(-z), ri_ref[rs, :], jnp.log(f), rstate_ref, HD))
            sinks.append((or_ref, rgain_ref, rog_ref, rs))
        for o, (out_ref, gain_ref, gate_ref, rs) in zip(_linear_chunks(items, tri, mall_ref), sinks):
            out_ref[rs, :] = _group_rms(o, ones64, HD) * gain_ref[...] * _silu(gate_ref[rs, :])
        return carry

    lax.fori_loop(0, tm // (CHUNK * LIN_GROUP), body, 0)


def _linear_mixers(u, w2pad, b_gate, gla_gain, lb_logits, hgrn_gain, B, S, tm, layer_idx):
    nt = S // tm
    tri, mall = _linear_consts()
    ones64 = _block_ones(GROUP, HD)
    c256 = lambda name: pl.BlockSpec((tm, GROUP), lambda b, i, c=_B256[name]: (b * nt + i, c))
    c128 = lambda name: pl.BlockSpec((tm, LANE), lambda b, i, c=_B128[name]: (b * nt + i, c))
    full = lambda shape: pl.BlockSpec(shape, lambda b, i: (0,) * len(shape))
    out_spec = pl.BlockSpec((tm, GROUP), lambda b, i: (b * nt + i, 0))
    out_shape = jax.ShapeDtypeStruct((B * S, GROUP), F32)
    depth = lb_logits.shape[0]
    return pl.pallas_call(
        functools.partial(_linear_kernel, tm=tm, layer_idx=layer_idx),
        grid=(B, nt),
        in_specs=[c128("g_q"), c128("g_k"), c256("g_v"), c128("g_lr"), c256("g_og"),
                  full((LANE, LANE)), full((1, LANE)), full((1, GROUP)),
                  c256("r_q"), c256("r_f"), c256("r_i"), c256("r_og"), full((depth, GROUP)), full((1, GROUP)),
                  full(tri.shape), full(mall.shape), full((GROUP, GROUP))],
        out_specs=[out_spec, out_spec], out_shape=[out_shape, out_shape],
        scratch_shapes=[pltpu.VMEM((GROUP, HEADS * GLA_DK), F32), pltpu.VMEM((GROUP, GROUP), F32)],
        compiler_params=_params("parallel", "arbitrary"),
        name="linear_mixers",
    )(u, u, u, u, u, w2pad, b_gate, gla_gain, u, u, u, u, lb_logits, hgrn_gain, tri, mall, ones64)


def _mix_residual(h_ref, part_refs, wout_ref):
    x = h_ref[...]
    for j, r in enumerate(part_refs):
        x = x + _dot(r[...].astype(BF16), wout_ref[j * GROUP:(j + 1) * GROUP, :])
    return x


def _ffn_kernel(h_ref, a_ref, b_ref, c_ref, d_ref, wout_ref, g_ref, wg_ref, wu_ref, wd_ref, o_ref,
                xn_ref, acc_ref):
    f = pl.program_id(1)

    @pl.when(f == 0)
    def _():
        x = _mix_residual(h_ref, (a_ref, b_ref, c_ref, d_ref), wout_ref)
        y = x * lax.rsqrt(jnp.mean(x * x, axis=-1, keepdims=True) + EPS) * g_ref[...]
        xn_ref[...] = y.astype(BF16)
        acc_ref[...] = x

    xn = xn_ref[...]
    mid = _silu(_dot(xn, wg_ref[...])) * _dot(xn, wu_ref[...])
    acc_ref[...] += _dot(mid.astype(BF16), wd_ref[...])

    @pl.when(f == pl.num_programs(1) - 1)
    def _():
        o_ref[...] = acc_ref[...]


def _ffn(h, parts, w_out, gain, wg, wu, wd, tm, tf):
    T = h.shape[0]
    F = wg.shape[1]
    part = pl.BlockSpec((tm, GROUP), lambda i, f: (i, 0))
    return pl.pallas_call(
        _ffn_kernel,
        grid=(T // tm, F // tf),
        in_specs=[pl.BlockSpec((tm, D_MODEL), lambda i, f: (i, 0)), part, part, part, part,
                  pl.BlockSpec((D_MODEL, D_MODEL), lambda i, f: (0, 0)),
                  pl.BlockSpec((1, D_MODEL), lambda i, f: (0, 0)),
                  pl.BlockSpec((D_MODEL, tf), lambda i, f: (0, f)),
                  pl.BlockSpec((D_MODEL, tf), lambda i, f: (0, f)),
                  pl.BlockSpec((tf, D_MODEL), lambda i, f: (f, 0))],
        out_specs=pl.BlockSpec((tm, D_MODEL), lambda i, f: (i, 0)),
        out_shape=jax.ShapeDtypeStruct((T, D_MODEL), F32),
        scratch_shapes=[pltpu.VMEM((tm, D_MODEL), BF16), pltpu.VMEM((tm, D_MODEL), F32)],
        compiler_params=_params("parallel", "arbitrary"),
        name="ffn_swiglu",
    )(h, *parts, w_out, gain.reshape(1, D_MODEL), wg, wu, wd)


def _router_kernel(h_ref, a_ref, b_ref, c4_ref, d_ref, wout_ref, g_ref, r_ref, lower_ref,
                   h1_ref, c_ref, comb_ref, rk_ref, cnt_ref):
    x = _mix_residual(h_ref, (a_ref, b_ref, c4_ref, d_ref), wout_ref)
    h1_ref[...] = x
    y = x * lax.rsqrt(jnp.mean(x * x, axis=-1, keepdims=True) + EPS) * g_ref[...]
    c_ref[...] = y.astype(BF16)
    y_hi, y_lo = _split2(y)
    r_hi, r_lo = _split2(r_ref[...])
    logits = _dot(y_hi, r_hi) + _dot(y_lo, r_hi) + _dot(y_hi, r_lo)
    lane = lax.broadcasted_iota(jnp.int32, logits.shape, 1)
    lane_f = lane.astype(F32)
    logits = jnp.where(lane < N_EXPERTS, logits, LOWEST)
    m1 = jnp.max(logits, axis=1, keepdims=True)
    i1 = jnp.min(jnp.where(logits == m1, lane_f, float(LANE)), axis=1, keepdims=True)
    rest = jnp.where(lane_f == i1, LOWEST, logits)
    m2 = jnp.max(rest, axis=1, keepdims=True)
    i2 = jnp.min(jnp.where(rest == m2, lane_f, float(LANE)), axis=1, keepdims=True)
    e2 = jnp.exp(m2 - m1)
    w1 = 1.0 / (1.0 + e2)
    comb_ref[...] = jnp.where(lane_f == i1, w1, jnp.where(lane_f == i2, e2 * w1, 0.0))
    chosen = (lane_f == i1) | (lane_f == i2)
    sel = jnp.where(chosen, 1.0, 0.0)
    rank = _dot(lower_ref[...], sel.astype(BF16))
    rk_ref[...] = jnp.where(chosen, rank, -1.0).astype(jnp.int32)
    cnt_ref[...] = jnp.broadcast_to(jnp.sum(sel, axis=0, keepdims=True), cnt_ref.shape).astype(jnp.int32)


def _router(h, parts, w_out, gain, router_pad, tm):
    T = h.shape[0]
    i = np.arange(tm)
    lower = jnp.asarray((i[None, :] < i[:, None]).astype(np.float32), BF16)
    sds = jax.ShapeDtypeStruct
    rows = lambda w: pl.BlockSpec((tm, w), lambda i: (i, 0))
    return pl.pallas_call(
        _router_kernel,
        grid=(T // tm,),
        in_specs=[rows(D_MODEL), rows(GROUP), rows(GROUP), rows(GROUP), rows(GROUP),
                  pl.BlockSpec((D_MODEL, D_MODEL), lambda i: (0, 0)),
                  pl.BlockSpec((1, D_MODEL), lambda i: (0, 0)),
                  pl.BlockSpec((D_MODEL, LANE), lambda i: (0, 0)),
                  pl.BlockSpec((tm, tm), lambda i: (0, 0))],
        out_specs=[rows(D_MODEL), rows(D_MODEL), rows(LANE), rows(LANE), pl.BlockSpec((8, LANE), lambda i: (i, 0))],
        out_shape=[sds((T, D_MODEL), F32), sds((T, D_MODEL), BF16), sds((T, LANE), F32), sds((T, LANE), jnp.int32),
                   sds((T // tm * 8, LANE), jnp.int32)],
        compiler_params=_params("parallel"),
        name="moe_router",
    )(h, *parts, w_out, gain.reshape(1, D_MODEL), router_pad, lower)


MOE_CHUNK = 512
MOE_ROWS = 512
MOE_HALF = MOE_ROWS // 2
MOE_TILE = 1024
MOE_ALIGN = 16
MOE_PAD_SEGS = MOE_TILE // MOE_ROWS


def _moe_plan(cnt, T):
    nch = T // MOE_CHUNK
    n_ce = cnt.reshape(nch, 8, LANE)[:, 0, :N_EXPERTS]
    cap = (n_ce + MOE_ALIGN - 1) // MOE_ALIGN * MOE_ALIGN
    tot = jnp.sum(cap, axis=0)
    ptot = (tot + MOE_TILE - 1) // MOE_TILE * MOE_TILE
    start = jnp.cumsum(ptot) - ptot
    lo_ce = start[None, :] + jnp.cumsum(cap, axis=0) - cap
    pad_lo = (start + tot)[:, None] + MOE_ROWS * jnp.arange(MOE_PAD_SEGS, dtype=jnp.int32)[None, :]
    lo = jnp.concatenate([lo_ce.T, pad_lo], axis=1)
    n = jnp.concatenate([n_ce.T, jnp.full((N_EXPERTS, MOE_PAD_SEGS), MOE_ROWS, jnp.int32)], axis=1)
    n_tiles = _moe_tiles(T)
    tile_end = jnp.cumsum(ptot // MOE_TILE)
    j = jnp.arange(n_tiles, dtype=jnp.int32)
    tile_e = jnp.minimum(jnp.sum((tile_end[None, :] <= j[:, None]).astype(jnp.int32), axis=1), N_EXPERTS - 1)
    valid = (j < tile_end[-1]).astype(jnp.int32)
    n_flat = jnp.concatenate([n.reshape(-1), tile_end[-1:] * MOE_PAD_SEGS]).astype(jnp.int32)
    return lo.reshape(-1).astype(jnp.int32), n_flat, tile_e, valid


def _moe_tiles(T):
    nch = T // MOE_CHUNK
    rows = 2 * T + nch * N_EXPERTS * (MOE_ALIGN - 1) + N_EXPERTS * (MOE_TILE - 1)
    return -(-rows // MOE_TILE) + 1


def _moe_gather_kernel(lo_ref, n_ref, c_ref, rkt_ref, x_hbm, buf, sem, *, nch):
    e = pl.program_id(0)
    c = pl.program_id(1)
    n_steps = pl.num_programs(0) * pl.num_programs(1)
    step = e * pl.num_programs(1) + c
    slot = step % 2
    rk_row = rkt_ref[pl.ds(e, 1), :]
    rk_row = jnp.where(c < nch, rk_row, -1)
    row = lax.broadcasted_iota(jnp.int32, (MOE_HALF, MOE_CHUNK), 0)
    chunk = c_ref[...]
    onehot = jnp.where(rk_row == row, 1.0, 0.0).astype(BF16)
    buf[slot, 0:MOE_HALF, :] = _dot(onehot, chunk).astype(BF16)

    @pl.when(n_ref[step] > MOE_HALF)
    def _():
        onehot = jnp.where(rk_row == row + MOE_HALF, 1.0, 0.0).astype(BF16)
        buf[slot, MOE_HALF:MOE_ROWS, :] = _dot(onehot, chunk).astype(BF16)

    def copy(k, s, rows):
        dst = x_hbm.at[pl.ds(pl.multiple_of(lo_ref[k], MOE_ALIGN), rows)]
        return pltpu.make_async_copy(buf.at[s, 0:rows], dst, sem.at[s])

    def for_size(k, fn):
        @pl.when(n_ref[k] > MOE_HALF)
        def _():
            fn(MOE_ROWS)

        @pl.when(n_ref[k] <= MOE_HALF)
        def _():
            fn(MOE_HALF)

    @pl.when(step > 0)
    def _():
        for_size(step - 1, lambda rows: copy(step - 1, 1 - slot, rows).wait())

    for_size(step, lambda rows: copy(step, slot, rows).start())

    @pl.when(step == n_steps - 1)
    def _():
        for_size(step, lambda rows: copy(step, slot, rows).wait())
        buf[0] = jnp.zeros((MOE_ROWS, D_MODEL), BF16)

        def fill(k, carry):
            dst = x_hbm.at[pl.ds(pl.multiple_of(k * MOE_ROWS, MOE_ROWS), MOE_ROWS)]
            cp = pltpu.make_async_copy(buf.at[0], dst, sem.at[0])
            cp.start()
            cp.wait()
            return carry

        lax.fori_loop(n_ref[n_steps], x_hbm.shape[0] // MOE_ROWS, fill, 0)


def _moe_gather(c_bf, rkt, lo, n, n_tiles):
    T = c_bf.shape[0]
    nch = T // MOE_CHUNK
    last = nch - 1
    grid_spec = pltpu.PrefetchScalarGridSpec(
        num_scalar_prefetch=2,
        grid=(N_EXPERTS, nch + MOE_PAD_SEGS),
        in_specs=[pl.BlockSpec((MOE_CHUNK, D_MODEL), lambda e, c, lo, n: (jnp.minimum(c, last), 0)),
                  pl.BlockSpec((N_EXPERTS, MOE_CHUNK), lambda e, c, lo, n: (0, jnp.minimum(c, last)))],
        out_specs=pl.BlockSpec(memory_space=pl.ANY),
        scratch_shapes=[pltpu.VMEM((2, MOE_ROWS, D_MODEL), BF16), pltpu.SemaphoreType.DMA((2,))],
    )
    return pl.pallas_call(
        functools.partial(_moe_gather_kernel, nch=nch),
        grid_spec=grid_spec,
        out_shape=jax.ShapeDtypeStruct((n_tiles * MOE_TILE, D_MODEL), BF16),
        compiler_params=_params("arbitrary", "arbitrary"),
        name="moe_gather",
    )(lo, n, c_bf, rkt)


def _moe_ffn_kernel(te_ref, valid_ref, x_ref, wg_ref, wu_ref, wd_ref, y_ref, acc_ref):
    j = pl.program_id(0)
    f = pl.program_id(1)
    last = pl.num_programs(1) - 1

    @pl.when(f == 0)
    def _():
        acc_ref[...] = jnp.zeros(acc_ref.shape, F32)

    @pl.when(valid_ref[j] == 1)
    def _():
        x = x_ref[...]
        mid = _silu(_dot(x, wg_ref[...])) * _dot(x, wu_ref[...])
        acc_ref[...] += _dot(mid.astype(BF16), wd_ref[...])

    @pl.when(f == last)
    def _():
        y_ref[...] = acc_ref[...].astype(BF16)


def _moe_ffn(x_sorted, tile_e, valid, wg, wu, wd, tf):
    n_tiles = x_sorted.shape[0] // MOE_TILE
    F = wg.shape[2]
    nf = F // tf
    sq = pl.Squeezed()
    fsel = lambda j, f, te, va: jnp.where(va[j] == 1, f, nf - 1)
    grid_spec = pltpu.PrefetchScalarGridSpec(
        num_scalar_prefetch=2,
        grid=(n_tiles, nf),
        in_specs=[pl.BlockSpec((MOE_TILE, D_MODEL), lambda j, f, te, va: (j, 0)),
                  pl.BlockSpec((sq, D_MODEL, tf), lambda j, f, te, va: (te[j], 0, fsel(j, f, te, va))),
                  pl.BlockSpec((sq, D_MODEL, tf), lambda j, f, te, va: (te[j], 0, fsel(j, f, te, va))),
                  pl.BlockSpec((sq, tf, D_MODEL), lambda j, f, te, va: (te[j], fsel(j, f, te, va), 0))],
        out_specs=pl.BlockSpec((MOE_TILE, D_MODEL), lambda j, f, te, va: (j, 0)),
        scratch_shapes=[pltpu.VMEM((MOE_TILE, D_MODEL), F32)],
    )
    return pl.pallas_call(
        _moe_ffn_kernel,
        grid_spec=grid_spec,
        out_shape=jax.ShapeDtypeStruct((n_tiles * MOE_TILE, D_MODEL), BF16),
        compiler_params=_params("parallel", "arbitrary"),
        name="moe_experts",
    )(tile_e, valid, x_sorted, wg, wu, wd)


def _ple_update(x, p, gain, w_gate, w_proj):
    y = x * lax.rsqrt(jnp.mean(x * x, axis=-1, keepdims=True) + EPS) * gain
    gate = _sigmoid(_dot(y.astype(BF16), w_gate))
    return x + _dot(p.astype(BF16), w_proj) * gate


def _moe_combine_kernel(lo_ref, n_ref, h_ref, rk_ref, comb_ref, p_ref, g_ref, wgate_ref, wproj_ref, y_hbm,
                        o_ref, ybuf, sem, *, nch):
    c = pl.program_id(0)
    seg = lambda e: e * (nch + MOE_PAD_SEGS) + c

    def copy(e, rows):
        src = y_hbm.at[pl.ds(pl.multiple_of(lo_ref[seg(e)], MOE_ALIGN), rows)]
        return pltpu.make_async_copy(src, ybuf.at[e, 0:rows], sem.at[e])

    def for_size(e, fn):
        @pl.when(n_ref[seg(e)] > MOE_HALF)
        def _():
            fn(MOE_ROWS)

        @pl.when(n_ref[seg(e)] <= MOE_HALF)
        def _():
            fn(MOE_HALF)

    for e in range(N_EXPERTS):
        for_size(e, lambda rows, e=e: copy(e, rows).start())
    o_ref[...] = h_ref[...]
    rk = rk_ref[...]
    comb = comb_ref[...]
    lane = lax.broadcasted_iota(jnp.int32, (MOE_CHUNK, MOE_HALF), 1)
    for e in range(N_EXPERTS):
        for_size(e, lambda rows, e=e: copy(e, rows).wait())
        rank_col = rk[:, e:e + 1]
        w_col = comb[:, e:e + 1]
        onehot = jnp.where(rank_col == lane, 1.0, 0.0).astype(BF16)
        o_ref[...] += w_col * _dot(onehot, ybuf[e, 0:MOE_HALF, :])

        @pl.when(n_ref[seg(e)] > MOE_HALF)
        def _():
            onehot = jnp.where(rank_col == lane + MOE_HALF, 1.0, 0.0).astype(BF16)
            o_ref[...] += w_col * _dot(onehot, ybuf[e, MOE_HALF:MOE_ROWS, :])

    o_ref[...] = _ple_update(o_ref[...], p_ref[...], g_ref[...], wgate_ref[...], wproj_ref[...])


def _moe_combine(h, rk_pad, comb, y_sorted, lo, n, p, ple_gain, ple_wg, ple_wp):
    T = h.shape[0]
    nch = T // MOE_CHUNK
    const = lambda shape: pl.BlockSpec(shape, lambda c, lo, n: (0, 0))
    grid_spec = pltpu.PrefetchScalarGridSpec(
        num_scalar_prefetch=2,
        grid=(nch,),
        in_specs=[pl.BlockSpec((MOE_CHUNK, D_MODEL), lambda c, lo, n: (c, 0)),
                  pl.BlockSpec((MOE_CHUNK, LANE), lambda c, lo, n: (c, 0)),
                  pl.BlockSpec((MOE_CHUNK, LANE), lambda c, lo, n: (c, 0)),
                  pl.BlockSpec((MOE_CHUNK, PLE_DIM), lambda c, lo, n: (c, 0)),
                  const((1, D_MODEL)), const((D_MODEL, D_MODEL)), const((PLE_DIM, D_MODEL)),
                  pl.BlockSpec(memory_space=pl.ANY)],
        out_specs=pl.BlockSpec((MOE_CHUNK, D_MODEL), lambda c, lo, n: (c, 0)),
        scratch_shapes=[pltpu.VMEM((N_EXPERTS, MOE_ROWS, D_MODEL), BF16), pltpu.SemaphoreType.DMA((N_EXPERTS,))],
    )
    return pl.pallas_call(
        functools.partial(_moe_combine_kernel, nch=nch),
        grid_spec=grid_spec,
        out_shape=jax.ShapeDtypeStruct((T, D_MODEL), F32),
        compiler_params=_params("arbitrary"),
        name="moe_combine",
    )(lo, n, h, rk_pad, comb, p, ple_gain.reshape(1, D_MODEL), ple_wg, ple_wp, y_sorted)


def _moe(h, c_bf, comb, rk, cnt, wg, wu, wd, tf, ple):
    T = h.shape[0]
    lo, n, tile_e, valid = _moe_plan(cnt, T)
    x_sorted = _moe_gather(c_bf, rk[:, :N_EXPERTS].T, lo, n, _moe_tiles(T))
    y_sorted = _moe_ffn(x_sorted, tile_e, valid, wg, wu, wd, tf)
    return _moe_combine(h, rk, comb, y_sorted, lo, n, *ple)


def _ple_kernel(h_ref, p_ref, g_ref, wg_ref, wp_ref, o_ref):
    o_ref[...] = _ple_update(h_ref[...], p_ref[...], g_ref[...], wg_ref[...], wp_ref[...])


def _ple(h, p, gain, wg, wp, tm):
    T = h.shape[0]
    return pl.pallas_call(
        _ple_kernel,
        grid=(T // tm,),
        in_specs=[pl.BlockSpec((tm, D_MODEL), lambda i: (i, 0)),
                  pl.BlockSpec((tm, PLE_DIM), lambda i: (i, 0)),
                  pl.BlockSpec((1, D_MODEL), lambda i: (0, 0)),
                  pl.BlockSpec((D_MODEL, D_MODEL), lambda i: (0, 0)),
                  pl.BlockSpec((PLE_DIM, D_MODEL), lambda i: (0, 0))],
        out_specs=pl.BlockSpec((tm, D_MODEL), lambda i: (i, 0)),
        out_shape=jax.ShapeDtypeStruct((T, D_MODEL), F32),
        compiler_params=_params("parallel"),
        name="ple_gate",
    )(h, p, gain.reshape(1, D_MODEL), wg, wp)


def _tiles(T, S):
    pick = lambda n, pref: max(t for t in pref if n % t == 0)
    return dict(
        proj_m=pick(T, (512, 256, 128)), proj_n=NC,
        prep_m=pick(T, (512, 256, 128)),
        attn_q=256, attn_k=pick(S, (512, 256)),
        nsa_q=pick(S, (512, 256)), nsa_k=pick(S, (512, 256)),
        lin_m=pick(S, (512, 256, 128, 64)),
        row_m=pick(T, (512, 256, 128)),
        ffn_m=pick(T, (1024, 512, 256, 128)), ffn_f=512, moe_f=D_FF // 4,
    )


def kernel(x, p, norm_attn, w_in, w_out, nsa_cmp_pos, nsa_cmp_w1, nsa_cmp_w2, nsa_qk_gain, diff_qk_gain, diff_lambda, diff_norm, gla_w_gate2, gla_b_gate, gla_norm, hgrn_lb_logits, hgrn_norm, norm_ffn, ffn_w_gate, ffn_w_up, ffn_w_down, moe_router, moe_w_gate, moe_w_up, moe_w_down, ple_norm, ple_w_gate, ple_w_proj):
    B, S, _ = x.shape
    depth = w_in.shape[0]
    T = B * S
    t = _tiles(T, S)
    ncp = S // NSA_CMP_STRIDE
    half = NSA_CMP_STRIDE * HD
    cols = jnp.asarray(np.maximum(_COLS, 0), jnp.int32)
    col_mask = jnp.asarray(_COLS >= 0)
    ones_row = jnp.ones((GROUP,), F32)

    h = x.reshape(T, D_MODEL)
    for i in range(depth):
        w_in_r = jnp.where(col_mask[None, :], jnp.take(w_in[i], cols, axis=1), 0.0).astype(BF16)
        gains = jnp.stack([
            jnp.tile(nsa_qk_gain[i, 0], HEADS) * (HD ** -0.5 * LOG2E),
            jnp.tile(diff_qk_gain[i, 0], 2 * HEADS) * (DIFF_QK ** -0.5 * LOG2E),
            jnp.tile(diff_qk_gain[i, 1], 2 * HEADS),
            jnp.tile(nsa_qk_gain[i, 2], HEADS),
            jnp.tile(nsa_qk_gain[i, 3], HEADS),
            ones_row, ones_row, ones_row])
        pos = nsa_cmp_pos[i].reshape(2, 2, half)
        w1 = nsa_cmp_w1[i].astype(BF16)
        w2p = jnp.pad(nsa_cmp_w2[i], ((0, 0), (0, 0), (0, LANE - HD))).astype(BF16)
        kc_gain = jnp.pad(nsa_qk_gain[i, 1], (0, LANE - HD)).reshape(1, LANE)
        w2pad = jnp.zeros((LANE, LANE), F32).at[:GLA_RANK].set(gla_w_gate2[i])
        diff_gain_col = jnp.broadcast_to(diff_norm[i][:, None], (HD, t["attn_q"]))

        u = _norm_matmul(h, norm_attn[i], w_in_r, t["proj_m"], t["proj_n"])
        nqt, dqt, dk, dvt, kvs, vst, kvw, vwt = _prep(u, gains, t["prep_m"])
        kv = u[:, _B128["kvcmp"] * LANE:(_B128["kvcmp"] + 1) * LANE]
        xk = kv[:, :HD].reshape(B, ncp, half)
        xv = kv[:, HD:].reshape(B, ncp, half)
        kc, vct = _compress(xk, xv, pos, w1, w2p, kc_gain)
        o_a = _nsa_attention(u, nqt, kc, vct, kvs, vst, kvw, vwt, B, S, t["nsa_q"], t["nsa_k"], t["nsa_q"])
        o_b = _diff_attention(dqt, dk, dvt, diff_lambda[i], diff_gain_col, B, S, t["attn_q"], t["attn_k"], i)
        o_c, o_d = _linear_mixers(u, w2pad, gla_b_gate[i].reshape(1, LANE),
                                  jnp.tile(gla_norm[i], HEADS).reshape(1, GROUP), hgrn_lb_logits,
                                  jnp.tile(hgrn_norm[i], HEADS).reshape(1, GROUP), B, S, t["lin_m"], i)
        parts = (o_a, o_b, o_c, o_d)
        w_out_bf = w_out[i].astype(BF16)
        ple = (p[i].reshape(T, PLE_DIM), ple_norm[i], ple_w_gate[i].astype(BF16), ple_w_proj[i].astype(BF16))
        j = i // 2
        if i % 2 == 0:
            h = _ffn(h, parts, w_out_bf, norm_ffn[i], ffn_w_gate[j].astype(BF16), ffn_w_up[j].astype(BF16),
                     ffn_w_down[j].astype(BF16), t["ffn_m"], t["ffn_f"])
            h = _ple(h, *ple, t["row_m"])
        else:
            router_pad = jnp.zeros((D_MODEL, LANE), F32).at[:, :N_EXPERTS].set(moe_router[j])
            h, c_bf, comb, rk, cnt = _router(h, parts, w_out_bf, norm_ffn[i], router_pad, MOE_CHUNK)
            h = _moe(h, c_bf, comb, rk, cnt, moe_w_gate[j].astype(BF16), moe_w_up[j].astype(BF16),
                     moe_w_down[j].astype(BF16), t["moe_f"], ple)
    return h.reshape(B, S, D_MODEL)
```

```python
import functools
import math

import numpy as np
import jax
import jax.numpy as jnp
from jax import lax
from jax.experimental import pallas as pl
from jax.experimental.pallas import tpu as pltpu

F32 = jnp.float32
BF16 = jnp.bfloat16

D_MODEL = 1024
HEADS = 4
HD = 64
GROUP = HEADS * HD
NSA_CMP_LEN = 32
NSA_CMP_STRIDE = 16
NSA_CMP_HIDDEN = 4 * HD
NSA_SEL_LEN = 64
NSA_TOPK = 16
NSA_WINDOW = 512
NSA_SUBTILES = 2
DIFF_QK = HD // 2
GLA_DK = HD // 2
GLA_RANK = 16
GLA_TAU = 16.0
CHUNK = 64
D_FF = 7 * D_MODEL // 2
N_EXPERTS = 8
PLE_DIM = 256
EPS = 1e-6
NEG = -1e30
BIG = 1e30
LOWEST = -3.0e38
LOG2E = 1.4426950408889634

VMEM_LIMIT = 52 * 1024 * 1024
LANE = 128

_SRC = dict(nsa_q=0, k_cmp=256, v_cmp=320, k_slc=384, v_slc=448, k_win=512, v_win=576, nsa_g=640,
            d_q=652, d_k=908, d_v=1164, g_q=1420, g_k=1548, g_v=1676, g_lr=1932, g_og=1948,
            r_q=2204, r_f=2460, r_i=2716, r_og=2972)
IN_COLS = 3228

_B256 = dict(nsa_q=0, d_q=1, d_k=2, d_v=3, g_v=4, g_og=5, r_q=6, r_f=7, r_i=8, r_og=9)
_B128 = dict(g_q=20, g_k=21, g_lr=22, kvcmp=23, kvslc=24, kvwin=25, nsa_g=26)
NC = 27 * 128


def _column_map():
    cols = -np.ones((NC,), np.int64)

    def put(dst, src, width):
        cols[dst:dst + width] = np.arange(src, src + width)

    for name in _B256:
        put(_B256[name] * 256, _SRC[name], 256)
    put(_B128["g_q"] * 128, _SRC["g_q"], 128)
    put(_B128["g_k"] * 128, _SRC["g_k"], 128)
    put(_B128["g_lr"] * 128, _SRC["g_lr"], GLA_RANK)
    put(_B128["kvcmp"] * 128, _SRC["k_cmp"], 128)
    put(_B128["kvslc"] * 128, _SRC["k_slc"], 128)
    put(_B128["kvwin"] * 128, _SRC["k_win"], 128)
    put(_B128["nsa_g"] * 128, _SRC["nsa_g"], 3 * HEADS)
    return cols


_COLS = _column_map()


def _dot(a, b):
    return jnp.dot(a, b, preferred_element_type=F32)


def _dot_nt(a, b):
    return lax.dot_general(a, b, (((1,), (1,)), ((), ())), preferred_element_type=F32)


def _dot_tn(a, b):
    return lax.dot_general(a, b, (((0,), (0,)), ((), ())), preferred_element_type=F32)


def _split2(x):
    hi = x.astype(BF16)
    lo = (x - hi.astype(F32)).astype(BF16)
    return hi, lo


def _split3(x):
    hi = x.astype(BF16)
    r = x - hi.astype(F32)
    mid = r.astype(BF16)
    lo = (r - mid.astype(F32)).astype(BF16)
    return hi, mid, lo


def _group_mean(x, ones_bf, group):
    hi, lo = _split2(x)
    return (_dot(hi, ones_bf) + _dot(lo, ones_bf)) * (1.0 / group)


def _group_rms(x, ones_bf, group):
    return x * lax.rsqrt(_group_mean(x * x, ones_bf, group) + EPS)


def _sigmoid(x):
    return 1.0 / (1.0 + jnp.exp(-x))


def _silu(x):
    return x * _sigmoid(x)


def _params(*sem):
    return pltpu.CompilerParams(dimension_semantics=sem, vmem_limit_bytes=VMEM_LIMIT)


def _block_ones(n, group):
    i = np.arange(n)
    return jnp.asarray((i[:, None] // group == i[None, :] // group).astype(np.float32), BF16)


def _norm_matmul_kernel(x_ref, g_ref, w_ref, o_ref, xn_ref):
    @pl.when(pl.program_id(1) == 0)
    def _():
        x = x_ref[...]
        y = x * lax.rsqrt(jnp.mean(x * x, axis=-1, keepdims=True) + EPS) * g_ref[...]
        xn_ref[...] = y.astype(BF16)

    o_ref[...] = _dot(xn_ref[...], w_ref[...])


def _norm_matmul(x, gain, w_bf, tm, tn):
    T, K = x.shape
    N = w_bf.shape[1]
    return pl.pallas_call(
        _norm_matmul_kernel,
        grid=(T // tm, N // tn),
        in_specs=[pl.BlockSpec((tm, K), lambda i, j: (i, 0)),
                  pl.BlockSpec((1, K), lambda i, j: (0, 0)),
                  pl.BlockSpec((K, tn), lambda i, j: (0, j))],
        out_specs=pl.BlockSpec((tm, tn), lambda i, j: (i, j)),
        out_shape=jax.ShapeDtypeStruct((T, N), F32),
        scratch_shapes=[pltpu.VMEM((tm, K), BF16)],
        compiler_params=_params("parallel", "arbitrary"),
        name="in_proj",
    )(x, gain.reshape(1, K), w_bf)


def _prep_kernel(nq_ref, dq_ref, dk_ref, dv_ref, kvs_ref, kvw_ref, gains_ref, ones64_ref, ones32_ref,
                 ones64h_ref, o_nq, o_dq, o_dk, o_dv, o_kvs, o_vs, o_kvw, o_vw):
    ones64 = ones64_ref[...]
    ones32 = ones32_ref[...]
    ones64h = ones64h_ref[...]
    o_nq[...] = (_group_rms(nq_ref[...], ones64, HD) * gains_ref[0:1, :]).T.astype(BF16)
    o_dq[...] = (_group_rms(dq_ref[...], ones32, DIFF_QK) * gains_ref[1:2, :]).T.astype(BF16)
    o_dk[...] = (_group_rms(dk_ref[...], ones32, DIFF_QK) * gains_ref[2:3, :]).astype(BF16)
    o_dv[...] = dv_ref[...].T.astype(BF16)
    lane = lax.broadcasted_iota(jnp.int32, kvs_ref.shape, 1)
    for kv_ref, gain, o_kv, o_v in ((kvs_ref, gains_ref[3:4, 0:LANE], o_kvs, o_vs),
                                    (kvw_ref, gains_ref[4:5, 0:LANE], o_kvw, o_vw)):
        x = kv_ref[...]
        o_kv[...] = jnp.where(lane < HD, _group_rms(x, ones64h, HD) * gain, x).astype(BF16)
        o_v[...] = x.T[HD:2 * HD, :].astype(BF16)


def _prep(u, gains, tm):
    T = u.shape[0]
    c256 = lambda name: pl.BlockSpec((tm, GROUP), lambda i, c=_B256[name]: (i, c))
    c128 = lambda name: pl.BlockSpec((tm, LANE), lambda i, c=_B128[name]: (i, c))
    const = lambda shape: pl.BlockSpec(shape, lambda i: (0, 0))
    rows = lambda w: pl.BlockSpec((tm, w), lambda i: (i, 0))
    colsT = lambda h: pl.BlockSpec((h, tm), lambda i: (0, i))
    sds = jax.ShapeDtypeStruct
    return pl.pallas_call(
        _prep_kernel,
        grid=(T // tm,),
        in_specs=[c256("nsa_q"), c256("d_q"), c256("d_k"), c256("d_v"), c128("kvslc"), c128("kvwin"),
                  const((8, GROUP)), const((GROUP, GROUP)), const((GROUP, GROUP)), const((LANE, LANE))],
        out_specs=[colsT(GROUP), colsT(GROUP), rows(GROUP), colsT(GROUP),
                   rows(LANE), colsT(HD), rows(LANE), colsT(HD)],
        out_shape=[sds((GROUP, T), BF16), sds((GROUP, T), BF16), sds((T, GROUP), BF16), sds((GROUP, T), BF16),
                   sds((T, LANE), BF16), sds((HD, T), BF16), sds((T, LANE), BF16), sds((HD, T), BF16)],
        compiler_params=_params("parallel"),
        name="attn_prep",
    )(u, u, u, u, u, u, gains, _block_ones(GROUP, HD), _block_ones(GROUP, DIFF_QK), _block_ones(LANE, HD))


def _compress_kernel(xk_ref, xv_ref, pos_ref, w1_ref, w2_ref, gain_ref, ones64h_ref, kc_ref, vc_ref):
    half = NSA_CMP_STRIDE * HD
    n_rows = xk_ref.shape[0]

    def compress(x, j):
        top = (x + pos_ref[j, 0:1, :]).astype(BF16)
        bot = (x + pos_ref[j, 1:2, :]).astype(BF16)
        a = _dot(top, w1_ref[j, 0:half, :])
        b = _dot(bot, w1_ref[j, half:2 * half, :])
        hidden = a + pltpu.roll(b, n_rows - 1, 0)
        return _dot(_silu(hidden).astype(BF16), w2_ref[j])

    kc = compress(xk_ref[...], 0)
    kc_ref[...] = _group_rms(kc, ones64h_ref[...], HD) * gain_ref[...]
    vc_ref[...] = compress(xv_ref[...], 1).T[0:HD, :].astype(BF16)


def _compress(xk, xv, pos, w1_bf, w2p_bf, gain_row):
    B, ncp, half = xk.shape
    sq = pl.Squeezed()
    full = lambda shape: pl.BlockSpec(shape, lambda b: (0,) * len(shape))
    return pl.pallas_call(
        _compress_kernel,
        grid=(B,),
        in_specs=[pl.BlockSpec((sq, ncp, half), lambda b: (b, 0, 0)),
                  pl.BlockSpec((sq, ncp, half), lambda b: (b, 0, 0)),
                  full((2, 2, half)), full((2, 2 * half, NSA_CMP_HIDDEN)), full((2, NSA_CMP_HIDDEN, LANE)),
                  full((1, LANE)), full((LANE, LANE))],
        out_specs=[pl.BlockSpec((sq, ncp, LANE), lambda b: (b, 0, 0)),
                   pl.BlockSpec((sq, HD, ncp), lambda b: (b, 0, 0))],
        out_shape=[jax.ShapeDtypeStruct((B, ncp, LANE), F32), jax.ShapeDtypeStruct((B, HD, ncp), BF16)],
        compiler_params=_params("parallel"),
        name="nsa_compress",
    )(xk, xv, pos, w1_bf, w2p_bf, gain_row, _block_ones(LANE, HD))


def _softmax_step_t(s, v_t, m_old, l_old, acc_ref):
    m_new = jnp.maximum(m_old, jnp.max(s, axis=0, keepdims=True))
    alpha = jnp.exp2(m_old - m_new)
    p = jnp.exp2(s - m_new)
    acc_ref[...] = alpha * acc_ref[...] + _dot(v_t, p.astype(BF16))
    return m_new, alpha * l_old + jnp.sum(p, axis=0, keepdims=True)


def _pipelined_sweep(lo, hi, qk, process, process_last, stats, sa_ref, sb_ref):
    def pair(jj, st):
        t = lo + 2 * jj
        qk(t + 1, sb_ref)
        st = process(t, sa_ref, st)
        qk(t + 2, sa_ref)
        return process(t + 1, sb_ref, st)

    def two_left(st):
        qk(hi - 1, sb_ref)
        st = process(hi - 2, sa_ref, st)
        return process_last(hi - 1, sb_ref, st)

    n = hi - lo
    qk(lo, sa_ref)
    stats = lax.fori_loop(0, (n - 1) // 2, pair, stats)
    return lax.cond(n % 2 == 0, two_left, lambda st: process_last(hi - 1, sa_ref, st), stats)


def _nsa_kernel(qt_ref, g_ref, kc_ref, vct_ref, kvs_ref, vst_ref, kvw_ref, vwt_ref, ovl_ref, exp_ref,
                o_ref, qs_ref, sa_ref, sb_ref, *acc_refs, tq, tk, tw, ksel):
    i = pl.program_id(1)
    t0 = i * tq
    ncp = kc_ref.shape[0]
    nsel = ovl_ref.shape[0]
    acc_s, acc_w = acc_refs[:HEADS], acc_refs[HEADS:]

    qt = qt_ref[...]
    qs_ref[HD:LANE, :] = jnp.zeros((LANE - HD, HEADS * tq), BF16)
    for h in range(HEADS):
        qs_ref[0:HD, h * tq:(h + 1) * tq] = qt[h * HD:(h + 1) * HD, :]
        acc_s[h][...] = jnp.zeros((HD, tq), F32)
        acc_w[h][...] = jnp.zeros((HD, tq), F32)

    kc_hi, kc_lo = _split2(kc_ref[...])
    n_idx = lax.broadcasted_iota(jnp.int32, (ncp, tq), 0)
    t_lane = t0 + lax.broadcasted_iota(jnp.int32, (ncp, tq), 1)
    ok = n_idx * NSA_CMP_STRIDE + (NSA_CMP_LEN - 1) <= t_lane
    cmp_scores = _dot(kc_hi, qs_ref[...]) + _dot(kc_lo, qs_ref[...])
    o_c = []
    psum = jnp.zeros((ncp, tq), F32)
    for h in range(HEADS):
        s = jnp.where(ok, cmp_scores[:, h * tq:(h + 1) * tq], NEG)
        e = jnp.exp2(s - jnp.max(s, axis=0, keepdims=True))
        p = jnp.where(ok, e / jnp.sum(e, axis=0, keepdims=True), 0.0)
        o_c.append(_dot(vct_ref[...], p.astype(BF16)))
        psum = psum + p

    p_hi, p_lo = _split2(psum)
    imp = _dot(ovl_ref[...], p_hi) + _dot(ovl_ref[...], p_lo)
    blk = lax.broadcasted_iota(jnp.int32, (nsel, tq), 0)
    t_col = t0 + lax.broadcasted_iota(jnp.int32, (nsel, tq), 1)
    cur = t_col // NSA_SEL_LEN
    forced = (blk == 0) | (blk == cur) | (blk == cur - 1)
    vals = jnp.where(forced, BIG, jnp.where(blk * NSA_SEL_LEN <= t_col, imp, NEG))
    blk_f = blk.astype(F32)
    sel = jnp.zeros((nsel, tq), F32)
    for _ in range(ksel):
        mx = jnp.max(vals, axis=0, keepdims=True)
        first = jnp.min(jnp.where(vals == mx, blk_f, float(nsel)), axis=0, keepdims=True)
        pick = blk_f == first
        sel = jnp.where(pick, 1.0, sel)
        vals = jnp.where(pick, LOWEST, vals)
    if nsel < LANE:
        sel = jnp.concatenate([sel, jnp.zeros((LANE - nsel, tq), F32)], axis=0)
    sel_bf = sel.astype(BF16)

    stats0 = tuple((jnp.full((1, tq), NEG, F32), jnp.zeros((1, tq), F32)) for _ in range(HEADS))

    def sweep(lo, hi, k_ref, vt_ref, accs, width, bias_fn):
        def qk(kt, dst):
            k = k_ref[pl.ds(pl.multiple_of(kt * width, width), width), :]
            for h in range(HEADS):
                dst[h, 0:width, :] = _dot(k, qs_ref[:, h * tq:(h + 1) * tq])

        def process(kt, src, stats):
            k0 = pl.multiple_of(kt * width, width)
            kpos = k0 + lax.broadcasted_iota(jnp.int32, (width, tq), 0)
            tpos = t0 + lax.broadcasted_iota(jnp.int32, (width, tq), 1)
            bias = bias_fn(k0, kpos, tpos)
            vt = vt_ref[:, pl.ds(k0, width)]
            return tuple(_softmax_step_t(src[h, 0:width, :] + bias, vt, stats[h][0], stats[h][1], accs[h])
                         for h in range(HEADS))

        return _pipelined_sweep(lo, hi, qk, process, process, stats0, sa_ref, sb_ref)

    def sel_bias(k0, kpos, tpos):
        chosen = _dot(exp_ref[pl.ds(k0, kpos.shape[0]), :], sel_bf)
        return jnp.where((chosen > 0.5) & (kpos <= tpos), 0.0, NEG)

    st_s = sweep(0, (t0 + tq - 1) // tk + 1, kvs_ref, vst_ref, acc_s, tk, sel_bias)

    def win_bias(k0, kpos, tpos):
        return jnp.where((kpos <= tpos) & (kpos > tpos - NSA_WINDOW), 0.0, NEG)

    st_w = sweep(jnp.maximum(t0 - NSA_WINDOW, 0) // tw, (t0 + tq - 1) // tw + 1,
                 kvw_ref, vwt_ref, acc_w, tw, win_bias)

    gates = _sigmoid(g_ref[...].T)
    outs = []
    for h in range(HEADS):
        outs.append(gates[3 * h:3 * h + 1, :] * o_c[h]
                    + gates[3 * h + 1:3 * h + 2, :] * (acc_s[h][...] / st_s[h][1])
                    + gates[3 * h + 2:3 * h + 3, :] * (acc_w[h][...] / st_w[h][1]))
    o_ref[...] = jnp.concatenate(outs, axis=0).T


def _nsa_attention(u, nqt, kc, vct, kvs, vst, kvw, vwt, B, S, tq, tk, tw):
    nq = S // tq
    ncp = S // NSA_CMP_STRIDE
    nsel = S // NSA_SEL_LEN
    ksel = min(NSA_TOPK, nsel)
    n_cmp = (S - NSA_CMP_LEN) // NSA_CMP_STRIDE + 1
    cmp_start = np.arange(ncp) * NSA_CMP_STRIDE
    sel_start = np.arange(nsel) * NSA_SEL_LEN
    overlap = ((cmp_start[:, None] <= sel_start[None, :] + NSA_SEL_LEN - 1)
               & (cmp_start[:, None] + NSA_CMP_LEN - 1 >= sel_start[None, :])
               & (np.arange(ncp)[:, None] < n_cmp))
    ovl_t = jnp.asarray(overlap.T.astype(np.float32), BF16)
    expand = (jnp.arange(S, dtype=jnp.int32)[:, None] // NSA_SEL_LEN
              == jnp.arange(LANE, dtype=jnp.int32)[None, :]).astype(BF16)
    sq = pl.Squeezed()
    seq_rows = pl.BlockSpec((S, LANE), lambda b, i: (b, 0))
    seq_cols = pl.BlockSpec((HD, S), lambda b, i: (0, b))
    kern = functools.partial(_nsa_kernel, tq=tq, tk=tk, tw=tw, ksel=ksel)
    return pl.pallas_call(
        kern,
        grid=(B, nq),
        in_specs=[pl.BlockSpec((GROUP, tq), lambda b, i: (0, b * nq + i)),
                  pl.BlockSpec((tq, LANE), lambda b, i, c=_B128["nsa_g"]: (b * nq + i, c)),
                  pl.BlockSpec((sq, ncp, LANE), lambda b, i: (b, 0, 0)),
                  pl.BlockSpec((sq, HD, ncp), lambda b, i: (b, 0, 0)),
                  seq_rows, seq_cols, seq_rows, seq_cols,
                  pl.BlockSpec((nsel, ncp), lambda b, i: (0, 0)),
                  pl.BlockSpec((S, LANE), lambda b, i: (0, 0))],
        out_specs=pl.BlockSpec((tq, GROUP), lambda b, i: (b * nq + i, 0)),
        out_shape=jax.ShapeDtypeStruct((B * S, GROUP), F32),
        scratch_shapes=([pltpu.VMEM((LANE, HEADS * tq), BF16)]
                        + [pltpu.VMEM((HEADS, max(tk, tw), tq), F32)] * 2
                        + [pltpu.VMEM((HD, tq), F32)] * (2 * HEADS)),
        compiler_params=_params("parallel", "arbitrary"),
        name="nsa_attention",
    )(nqt, u, kc, vct, kvs, vst, kvw, vwt, ovl_t, expand)


def _diff_kernel(lam_ref, qt_ref, k_ref, vt_ref, gain_ref, o_ref, qs_ref, sa_ref, sb_ref, *acc_refs,
                 tq, lam_init):
    i = pl.program_id(1)
    t0 = i * tq
    lanes = 2 * tq

    qt = qt_ref[...]
    row = lax.broadcasted_iota(jnp.int32, (GROUP, tq), 0)
    zero = jnp.zeros_like(qt)
    for h in range(HEADS):
        qs_ref[h] = jnp.concatenate([jnp.where(row // DIFF_QK == 2 * h, qt, zero),
                                     jnp.where(row // DIFF_QK == 2 * h + 1, qt, zero)], axis=1)
        acc_refs[h][...] = jnp.zeros((HD, lanes), F32)

    def qk(kt, dst):
        k = k_ref[pl.ds(pl.multiple_of(kt * tq, tq), tq), :]
        for h in range(HEADS):
            dst[h] = _dot(k, qs_ref[h])

    def process(kt, src, masked, stats):
        k0 = pl.multiple_of(kt * tq, tq)
        out = []
        for h in range(HEADS):
            s = src[h]
            if masked:
                kpos = k0 + lax.broadcasted_iota(jnp.int32, (tq, lanes), 0)
                tpos = t0 + lax.broadcasted_iota(jnp.int32, (tq, lanes), 1) % tq
                s = jnp.where(kpos <= tpos, s, NEG)
            out.append(_softmax_step_t(s, vt_ref[h * HD:(h + 1) * HD, pl.ds(k0, tq)],
                                       stats[h][0], stats[h][1], acc_refs[h]))
        return tuple(out)

    stats0 = tuple((jnp.full((1, lanes), NEG, F32), jnp.zeros((1, lanes), F32)) for _ in range(HEADS))
    stats = _pipelined_sweep(0, i + 1, qk, lambda kt, src, st: process(kt, src, False, st),
                             lambda kt, src, st: process(kt, src, True, st), stats0, sa_ref, sb_ref)

    lam = lam_ref[...]
    lam_full = (jnp.exp(jnp.sum(lam[0:1] * lam[1:2], axis=1, keepdims=True))
                - jnp.exp(jnp.sum(lam[2:3] * lam[3:4], axis=1, keepdims=True)) + lam_init)
    outs = []
    for h in range(HEADS):
        r = acc_refs[h][...] / stats[h][1]
        d = r[:, 0:tq] - lam_full * r[:, tq:lanes]
        d = d * lax.rsqrt(jnp.mean(d * d, axis=0, keepdims=True) + EPS)
        outs.append(d * gain_ref[...] * (1.0 - lam_init))
    o_ref[...] = jnp.concatenate(outs, axis=0).T


def _diff_attention(dqt, dk, dvt, lam, gain_col, B, S, tq, layer_idx):
    nq = S // tq
    lam_init = 0.8 - 0.6 * math.exp(-0.3 * layer_idx)
    kern = functools.partial(_diff_kernel, tq=tq, lam_init=lam_init)
    scores = pltpu.VMEM((HEADS, tq, 2 * tq), F32)
    return pl.pallas_call(
        kern,
        grid=(B, nq),
        in_specs=[pl.BlockSpec((4, DIFF_QK), lambda b, i: (0, 0)),
                  pl.BlockSpec((GROUP, tq), lambda b, i: (0, b * nq + i)),
                  pl.BlockSpec((S, GROUP), lambda b, i: (b, 0)),
                  pl.BlockSpec((GROUP, S), lambda b, i: (0, b)),
                  pl.BlockSpec((HD, tq), lambda b, i: (0, 0))],
        out_specs=pl.BlockSpec((tq, GROUP), lambda b, i: (b * nq + i, 0)),
        out_shape=jax.ShapeDtypeStruct((B * S, GROUP), F32),
        scratch_shapes=([pltpu.VMEM((HEADS, GROUP, 2 * tq), BF16), scores, scores]
                        + [pltpu.VMEM((HD, 2 * tq), F32)] * HEADS),
        compiler_params=_params("parallel", "arbitrary"),
        name="diff_attention",
    )(lam, dqt, dk, dvt, gain_col)


_LEVELS = (32, 16, 8, 4, 2, 1)
LIN_GROUP = 4


def _stack_heads(x, lane, group, count):
    zero = jnp.zeros_like(x)
    return jnp.concatenate([jnp.where(lane // group == g, x, zero) for g in range(count)], axis=0)


def _unstack_heads(x4, lane, rows):
    out = jnp.where(lane // HD == 0, x4[0:rows], 0.0)
    for h in range(1, HEADS):
        out = out + jnp.where(lane // HD == h, x4[h * rows:(h + 1) * rows], 0.0)
    return out


def _linear_consts():
    r = np.arange(CHUNK)[:, None]
    t = np.arange(CHUNK)[None, :]
    tri = (t <= r).astype(np.float32)
    masks = [(r // (2 * s) == t // (2 * s)) for s in _LEVELS] + [r == t]
    mall = np.stack([np.tile(m.astype(np.float32), (1, HEADS)) for m in masks])
    return jnp.asarray(tri, BF16), jnp.asarray(mall, F32)


def _level_exponent(s, lg, b, row):
    if s == 1:
        return jnp.where((row & 1) != 0, 0.0, pltpu.roll(lg, CHUNK - 1, 0))
    if s == 2:
        nxt1 = pltpu.roll(lg, CHUNK - 1, 0)
        nxt2 = pltpu.roll(lg, CHUNK - 2, 0)
        r4 = row & 3
        return jnp.where(r4 == 0, nxt1 + nxt2, jnp.where(r4 == 1, nxt1, jnp.where(r4 == 2, 0.0, lg)))
    mids = [jnp.broadcast_to(b[m:m + 1, :], (2 * s, b.shape[1])) for m in range(s, CHUNK, 2 * s)]
    d = b - (jnp.concatenate(mids, axis=0) if len(mids) > 1 else mids[0])
    return jnp.where((row & s) != 0, d, -d)


def _linear_chunks(items, tri, mall_ref):
    bs = []
    for q, k, v, lg, state_ref, dk in items:
        hi, mid, lo = _split3(lg)
        bs.append(_dot(tri, hi) + _dot(tri, mid) + _dot(tri, lo))
    ats = []
    for (q, k, v, lg, state_ref, dk), b in zip(items, bs):
        dkh = HEADS * dk
        row = lax.broadcasted_iota(jnp.int32, (CHUNK, dkh), 0)
        lane_k = lax.broadcasted_iota(jnp.int32, (CHUNK, dkh), 1)
        a_t = mall_ref[len(_LEVELS)] * _dot_nt(k.astype(BF16), _stack_heads(q.astype(BF16), lane_k, dk, HEADS))
        for li, s in enumerate(_LEVELS):
            e = jnp.exp(_level_exponent(s, lg, b, row))
            upper = (row & s) != 0
            qt = jnp.where(upper, q * e, 0.0).astype(BF16)
            kt = jnp.where(upper, 0.0, k * e).astype(BF16)
            a_t = a_t + mall_ref[li] * _dot_nt(kt, _stack_heads(qt, lane_k, dk, HEADS))
        ats.append(a_t)
    lane_v = lax.broadcasted_iota(jnp.int32, (CHUNK, GROUP), 1)
    partial = []
    for (q, k, v, lg, state_ref, dk), b, a_t in zip(items, bs, ats):
        v_bf = v.astype(BF16)
        o_intra = _unstack_heads(_dot_tn(a_t.astype(BF16), v_bf), lane_v, CHUNK)
        e_b = jnp.exp(b)
        e_u = jnp.exp(b[CHUNK - 1:CHUNK, :] - b)
        kv = _dot_tn(v_bf, (k * e_u).astype(BF16))
        partial.append((o_intra, e_b, kv))
    outs = []
    for (q, k, v, lg, state_ref, dk), (o_intra, e_b, kv) in zip(items, partial):
        dkh = HEADS * dk
        st = state_ref[...]
        o_inter = _dot_nt((q * e_b).astype(BF16), st.astype(BF16))
        srow = lax.broadcasted_iota(jnp.int32, (GROUP, dkh), 0)
        scol = lax.broadcasted_iota(jnp.int32, (GROUP, dkh), 1)
        state_ref[...] = st * e_b[CHUNK - 1:CHUNK, :] + jnp.where(srow // HD == scol // dk, kv, 0.0)
        outs.append(o_inter + o_intra)
    return outs


def _linear_kernel(gq_ref, gk_ref, gv_ref, lr_ref, gog_ref, w2_ref, b_ref, ggain_ref,
                   rq_ref, rf_ref, ri_ref, rog_ref, lbl_ref, rgain_ref, tri_ref, mall_ref, ones64_ref,
                   og_ref, or_ref, gstate_ref, rstate_ref, *, tm, layer_idx):
    @pl.when(pl.program_id(1) == 0)
    def _():
        gstate_ref[...] = jnp.zeros_like(gstate_ref)
        rstate_ref[...] = jnp.zeros_like(rstate_ref)

    tri = tri_ref[...]
    ones64 = ones64_ref[...]
    w_hi, w_lo = _split2(w2_ref[...])
    logits = lbl_ref[...]
    ez = jnp.exp(logits - jnp.max(logits, axis=0, keepdims=True))
    probs = ez / jnp.sum(ez, axis=0, keepdims=True)
    lb = jnp.zeros((1, GROUP), F32)
    for j in range(1, layer_idx + 1):
        lb = lb + probs[j:j + 1]

    def body(c, carry):
        items, sinks = [], []
        for g in range(LIN_GROUP):
            rs = pl.ds(pl.multiple_of((c * LIN_GROUP + g) * CHUNK, CHUNK), CHUNK)
            lr_hi, lr_lo = _split2(lr_ref[rs, :])
            x = _dot(lr_hi, w_hi) + _dot(lr_lo, w_hi) + _dot(lr_hi, w_lo) + b_ref[...]
            lg = (jnp.minimum(x, 0.0) - jnp.log(1.0 + jnp.exp(-jnp.abs(x)))) * (1.0 / GLA_TAU)
            items.append((gq_ref[rs, :] * (GLA_DK ** -0.5), gk_ref[rs, :], gv_ref[rs, :], lg, gstate_ref, GLA_DK))
            sinks.append((og_ref, ggain_ref, gog_ref, rs))
            z = rf_ref[rs, :]
            f = lb + (1.0 - lb) * _sigmoid(z)
            items.append((rq_ref[rs, :], (1.0 - lb) * _sigmoid(-z), ri_ref[rs, :], jnp.log(f), rstate_ref, HD))
            sinks.append((or_ref, rgain_ref, rog_ref, rs))
        for o, (out_ref, gain_ref, gate_ref, rs) in zip(_linear_chunks(items, tri, mall_ref), sinks):
            out_ref[rs, :] = _group_rms(o, ones64, HD) * gain_ref[...] * _silu(gate_ref[rs, :])
        return carry

    lax.fori_loop(0, tm // (CHUNK * LIN_GROUP), body, 0)


def _linear_mixers(u, w2pad, b_gate, gla_gain, lb_logits, hgrn_gain, B, S, tm, layer_idx):
    nt = S // tm
    tri, mall = _linear_consts()
    ones64 = _block_ones(GROUP, HD)
    c256 = lambda name: pl.BlockSpec((tm, GROUP), lambda b, i, c=_B256[name]: (b * nt + i, c))
    c128 = lambda name: pl.BlockSpec((tm, LANE), lambda b, i, c=_B128[name]: (b * nt + i, c))
    full = lambda shape: pl.BlockSpec(shape, lambda b, i: (0,) * len(shape))
    out_spec = pl.BlockSpec((tm, GROUP), lambda b, i: (b * nt + i, 0))
    out_shape = jax.ShapeDtypeStruct((B * S, GROUP), F32)
    depth = lb_logits.shape[0]
    return pl.pallas_call(
        functools.partial(_linear_kernel, tm=tm, layer_idx=layer_idx),
        grid=(B, nt),
        in_specs=[c128("g_q"), c128("g_k"), c256("g_v"), c128("g_lr"), c256("g_og"),
                  full((LANE, LANE)), full((1, LANE)), full((1, GROUP)),
                  c256("r_q"), c256("r_f"), c256("r_i"), c256("r_og"), full((depth, GROUP)), full((1, GROUP)),
                  full(tri.shape), full(mall.shape), full((GROUP, GROUP))],
        out_specs=[out_spec, out_spec], out_shape=[out_shape, out_shape],
        scratch_shapes=[pltpu.VMEM((GROUP, HEADS * GLA_DK), F32), pltpu.VMEM((GROUP, GROUP), F32)],
        compiler_params=_params("parallel", "arbitrary"),
        name="linear_mixers",
    )(u, u, u, u, u, w2pad, b_gate, gla_gain, u, u, u, u, lb_logits, hgrn_gain, tri, mall, ones64)


def _mix_residual(h_ref, part_refs, wout_ref):
    x = h_ref[...]
    for j, r in enumerate(part_refs):
        x = x + _dot(r[...].astype(BF16), wout_ref[j * GROUP:(j + 1) * GROUP, :])
    return x


def _ffn_kernel(h_ref, a_ref, b_ref, c_ref, d_ref, wout_ref, g_ref, wg_ref, wu_ref, wd_ref, o_ref,
                xn_ref, acc_ref):
    f = pl.program_id(1)

    @pl.when(f == 0)
    def _():
        x = _mix_residual(h_ref, (a_ref, b_ref, c_ref, d_ref), wout_ref)
        y = x * lax.rsqrt(jnp.mean(x * x, axis=-1, keepdims=True) + EPS) * g_ref[...]
        xn_ref[...] = y.astype(BF16)
        acc_ref[...] = x

    xn = xn_ref[...]
    mid = _silu(_dot(xn, wg_ref[...])) * _dot(xn, wu_ref[...])
    acc_ref[...] += _dot(mid.astype(BF16), wd_ref[...])

    @pl.when(f == pl.num_programs(1) - 1)
    def _():
        o_ref[...] = acc_ref[...]


def _ffn(h, parts, w_out, gain, wg, wu, wd, tm, tf):
    T = h.shape[0]
    F = wg.shape[1]
    part = pl.BlockSpec((tm, GROUP), lambda i, f: (i, 0))
    return pl.pallas_call(
        _ffn_kernel,
        grid=(T // tm, F // tf),
        in_specs=[pl.BlockSpec((tm, D_MODEL), lambda i, f: (i, 0)), part, part, part, part,
                  pl.BlockSpec((D_MODEL, D_MODEL), lambda i, f: (0, 0)),
                  pl.BlockSpec((1, D_MODEL), lambda i, f: (0, 0)),
                  pl.BlockSpec((D_MODEL, tf), lambda i, f: (0, f)),
                  pl.BlockSpec((D_MODEL, tf), lambda i, f: (0, f)),
                  pl.BlockSpec((tf, D_MODEL), lambda i, f: (f, 0))],
        out_specs=pl.BlockSpec((tm, D_MODEL), lambda i, f: (i, 0)),
        out_shape=jax.ShapeDtypeStruct((T, D_MODEL), F32),
        scratch_shapes=[pltpu.VMEM((tm, D_MODEL), BF16), pltpu.VMEM((tm, D_MODEL), F32)],
        compiler_params=_params("parallel", "arbitrary"),
        name="ffn_swiglu",
    )(h, *parts, w_out, gain.reshape(1, D_MODEL), wg, wu, wd)


def _router_kernel(h_ref, a_ref, b_ref, c4_ref, d_ref, wout_ref, g_ref, r_ref, lower_ref,
                   h1_ref, c_ref, comb_ref, rk_ref, cnt_ref):
    x = _mix_residual(h_ref, (a_ref, b_ref, c4_ref, d_ref), wout_ref)
    h1_ref[...] = x
    y = x * lax.rsqrt(jnp.mean(x * x, axis=-1, keepdims=True) + EPS) * g_ref[...]
    c_ref[...] = y.astype(BF16)
    y_hi, y_lo = _split2(y)
    r_hi, r_lo = _split2(r_ref[...])
    logits = _dot(y_hi, r_hi) + _dot(y_lo, r_hi) + _dot(y_hi, r_lo)
    lane = lax.broadcasted_iota(jnp.int32, logits.shape, 1)
    lane_f = lane.astype(F32)
    logits = jnp.where(lane < N_EXPERTS, logits, LOWEST)
    m1 = jnp.max(logits, axis=1, keepdims=True)
    i1 = jnp.min(jnp.where(logits == m1, lane_f, float(LANE)), axis=1, keepdims=True)
    rest = jnp.where(lane_f == i1, LOWEST, logits)
    m2 = jnp.max(rest, axis=1, keepdims=True)
    i2 = jnp.min(jnp.where(rest == m2, lane_f, float(LANE)), axis=1, keepdims=True)
    e2 = jnp.exp(m2 - m1)
    w1 = 1.0 / (1.0 + e2)
    comb_ref[...] = jnp.where(lane_f == i1, w1, jnp.where(lane_f == i2, e2 * w1, 0.0))
    chosen = (lane_f == i1) | (lane_f == i2)
    sel = jnp.where(chosen, 1.0, 0.0)
    rank = _dot(lower_ref[...], sel.astype(BF16))
    rk_ref[...] = jnp.where(chosen, rank, -1.0).astype(jnp.int32)
    cnt_ref[...] = jnp.broadcast_to(jnp.sum(sel, axis=0, keepdims=True), cnt_ref.shape).astype(jnp.int32)


def _router(h, parts, w_out, gain, router_pad, tm):
    T = h.shape[0]
    i = np.arange(tm)
    lower = jnp.asarray((i[None, :] < i[:, None]).astype(np.float32), BF16)
    sds = jax.ShapeDtypeStruct
    rows = lambda w: pl.BlockSpec((tm, w), lambda i: (i, 0))
    return pl.pallas_call(
        _router_kernel,
        grid=(T // tm,),
        in_specs=[rows(D_MODEL), rows(GROUP), rows(GROUP), rows(GROUP), rows(GROUP),
                  pl.BlockSpec((D_MODEL, D_MODEL), lambda i: (0, 0)),
                  pl.BlockSpec((1, D_MODEL), lambda i: (0, 0)),
                  pl.BlockSpec((D_MODEL, LANE), lambda i: (0, 0)),
                  pl.BlockSpec((tm, tm), lambda i: (0, 0))],
        out_specs=[rows(D_MODEL), rows(D_MODEL), rows(LANE), rows(LANE), pl.BlockSpec((8, LANE), lambda i: (i, 0))],
        out_shape=[sds((T, D_MODEL), F32), sds((T, D_MODEL), BF16), sds((T, LANE), F32), sds((T, LANE), jnp.int32),
                   sds((T // tm * 8, LANE), jnp.int32)],
        compiler_params=_params("parallel"),
        name="moe_router",
    )(h, *parts, w_out, gain.reshape(1, D_MODEL), router_pad, lower)


MOE_CHUNK = 512
MOE_ROWS = 512
MOE_HALF = MOE_ROWS // 2
MOE_TILE = 1024
MOE_ALIGN = 16
MOE_PAD_SEGS = MOE_TILE // MOE_ROWS


def _moe_plan(cnt, T):
    nch = T // MOE_CHUNK
    n_ce = cnt.reshape(nch, 8, LANE)[:, 0, :N_EXPERTS]
    cap = (n_ce + MOE_ALIGN - 1) // MOE_ALIGN * MOE_ALIGN
    tot = jnp.sum(cap, axis=0)
    ptot = (tot + MOE_TILE - 1) // MOE_TILE * MOE_TILE
    start = jnp.cumsum(ptot) - ptot
    lo_ce = start[None, :] + jnp.cumsum(cap, axis=0) - cap
    pad_lo = (start + tot)[:, None] + MOE_ROWS * jnp.arange(MOE_PAD_SEGS, dtype=jnp.int32)[None, :]
    lo = jnp.concatenate([lo_ce.T, pad_lo], axis=1)
    n = jnp.concatenate([n_ce.T, jnp.full((N_EXPERTS, MOE_PAD_SEGS), MOE_ROWS, jnp.int32)], axis=1)
    n_tiles = _moe_tiles(T)
    tile_end = jnp.cumsum(ptot // MOE_TILE)
    j = jnp.arange(n_tiles, dtype=jnp.int32)
    tile_e = jnp.minimum(jnp.sum((tile_end[None, :] <= j[:, None]).astype(jnp.int32), axis=1), N_EXPERTS - 1)
    valid = (j < tile_end[-1]).astype(jnp.int32)
    n_flat = jnp.concatenate([n.reshape(-1), tile_end[-1:] * MOE_PAD_SEGS]).astype(jnp.int32)
    return lo.reshape(-1).astype(jnp.int32), n_flat, tile_e, valid


def _moe_tiles(T):
    nch = T // MOE_CHUNK
    rows = 2 * T + nch * N_EXPERTS * (MOE_ALIGN - 1) + N_EXPERTS * (MOE_TILE - 1)
    return -(-rows // MOE_TILE) + 1


def _moe_gather_kernel(lo_ref, n_ref, c_ref, rkt_ref, x_hbm, buf, sem, *, nch):
    e = pl.program_id(0)
    c = pl.program_id(1)
    n_steps = pl.num_programs(0) * pl.num_programs(1)
    step = e * pl.num_programs(1) + c
    slot = step % 2
    rk_row = rkt_ref[pl.ds(e, 1), :]
    rk_row = jnp.where(c < nch, rk_row, -1)
    row = lax.broadcasted_iota(jnp.int32, (MOE_HALF, MOE_CHUNK), 0)
    chunk = c_ref[...]
    onehot = jnp.where(rk_row == row, 1.0, 0.0).astype(BF16)
    buf[slot, 0:MOE_HALF, :] = _dot(onehot, chunk).astype(BF16)

    @pl.when(n_ref[step] > MOE_HALF)
    def _():
        onehot = jnp.where(rk_row == row + MOE_HALF, 1.0, 0.0).astype(BF16)
        buf[slot, MOE_HALF:MOE_ROWS, :] = _dot(onehot, chunk).astype(BF16)

    def copy(k, s, rows):
        dst = x_hbm.at[pl.ds(pl.multiple_of(lo_ref[k], MOE_ALIGN), rows)]
        return pltpu.make_async_copy(buf.at[s, 0:rows], dst, sem.at[s])

    def for_size(k, fn):
        @pl.when(n_ref[k] > MOE_HALF)
        def _():
            fn(MOE_ROWS)

        @pl.when(n_ref[k] <= MOE_HALF)
        def _():
            fn(MOE_HALF)

    @pl.when(step > 0)
    def _():
        for_size(step - 1, lambda rows: copy(step - 1, 1 - slot, rows).wait())

    for_size(step, lambda rows: copy(step, slot, rows).start())

    @pl.when(step == n_steps - 1)
    def _():
        for_size(step, lambda rows: copy(step, slot, rows).wait())
        buf[0] = jnp.zeros((MOE_ROWS, D_MODEL), BF16)

        def fill(k, carry):
            dst = x_hbm.at[pl.ds(pl.multiple_of(k * MOE_ROWS, MOE_ROWS), MOE_ROWS)]
            cp = pltpu.make_async_copy(buf.at[0], dst, sem.at[0])
            cp.start()
            cp.wait()
            return carry

        lax.fori_loop(n_ref[n_steps], x_hbm.shape[0] // MOE_ROWS, fill, 0)


def _moe_gather(c_bf, rkt, lo, n, n_tiles):
    T = c_bf.shape[0]
    nch = T // MOE_CHUNK
    last = nch - 1
    grid_spec = pltpu.PrefetchScalarGridSpec(
        num_scalar_prefetch=2,
        grid=(N_EXPERTS, nch + MOE_PAD_SEGS),
        in_specs=[pl.BlockSpec((MOE_CHUNK, D_MODEL), lambda e, c, lo, n: (jnp.minimum(c, last), 0)),
                  pl.BlockSpec((N_EXPERTS, MOE_CHUNK), lambda e, c, lo, n: (0, jnp.minimum(c, last)))],
        out_specs=pl.BlockSpec(memory_space=pl.ANY),
        scratch_shapes=[pltpu.VMEM((2, MOE_ROWS, D_MODEL), BF16), pltpu.SemaphoreType.DMA((2,))],
    )
    return pl.pallas_call(
        functools.partial(_moe_gather_kernel, nch=nch),
        grid_spec=grid_spec,
        out_shape=jax.ShapeDtypeStruct((n_tiles * MOE_TILE, D_MODEL), BF16),
        compiler_params=_params("arbitrary", "arbitrary"),
        name="moe_gather",
    )(lo, n, c_bf, rkt)


def _moe_ffn_kernel(te_ref, valid_ref, x_ref, wg_ref, wu_ref, wd_ref, y_ref, acc_ref):
    j = pl.program_id(0)
    f = pl.program_id(1)
    last = pl.num_programs(1) - 1

    @pl.when(f == 0)
    def _():
        acc_ref[...] = jnp.zeros(acc_ref.shape, F32)

    @pl.when(valid_ref[j] == 1)
    def _():
        x = x_ref[...]
        mid = _silu(_dot(x, wg_ref[...])) * _dot(x, wu_ref[...])
        acc_ref[...] += _dot(mid.astype(BF16), wd_ref[...])

    @pl.when(f == last)
    def _():
        y_ref[...] = acc_ref[...].astype(BF16)


def _moe_ffn(x_sorted, tile_e, valid, wg, wu, wd, tf):
    n_tiles = x_sorted.shape[0] // MOE_TILE
    F = wg.shape[2]
    nf = F // tf
    sq = pl.Squeezed()
    fsel = lambda j, f, te, va: jnp.where(va[j] == 1, f, nf - 1)
    grid_spec = pltpu.PrefetchScalarGridSpec(
        num_scalar_prefetch=2,
        grid=(n_tiles, nf),
        in_specs=[pl.BlockSpec((MOE_TILE, D_MODEL), lambda j, f, te, va: (j, 0)),
                  pl.BlockSpec((sq, D_MODEL, tf), lambda j, f, te, va: (te[j], 0, fsel(j, f, te, va))),
                  pl.BlockSpec((sq, D_MODEL, tf), lambda j, f, te, va: (te[j], 0, fsel(j, f, te, va))),
                  pl.BlockSpec((sq, tf, D_MODEL), lambda j, f, te, va: (te[j], fsel(j, f, te, va), 0))],
        out_specs=pl.BlockSpec((MOE_TILE, D_MODEL), lambda j, f, te, va: (j, 0)),
        scratch_shapes=[pltpu.VMEM((MOE_TILE, D_MODEL), F32)],
    )
    return pl.pallas_call(
        _moe_ffn_kernel,
        grid_spec=grid_spec,
        out_shape=jax.ShapeDtypeStruct((n_tiles * MOE_TILE, D_MODEL), BF16),
        compiler_params=_params("parallel", "arbitrary"),
        name="moe_experts",
    )(tile_e, valid, x_sorted, wg, wu, wd)


def _ple_update(x, p, gain, w_gate, w_proj):
    y = x * lax.rsqrt(jnp.mean(x * x, axis=-1, keepdims=True) + EPS) * gain
    gate = _sigmoid(_dot(y.astype(BF16), w_gate))
    return x + _dot(p.astype(BF16), w_proj) * gate


def _moe_combine_kernel(lo_ref, n_ref, h_ref, rk_ref, comb_ref, p_ref, g_ref, wgate_ref, wproj_ref, y_hbm,
                        o_ref, ybuf, sem, *, nch):
    c = pl.program_id(0)
    seg = lambda e: e * (nch + MOE_PAD_SEGS) + c

    def copy(e, rows):
        src = y_hbm.at[pl.ds(pl.multiple_of(lo_ref[seg(e)], MOE_ALIGN), rows)]
        return pltpu.make_async_copy(src, ybuf.at[e, 0:rows], sem.at[e])

    def for_size(e, fn):
        @pl.when(n_ref[seg(e)] > MOE_HALF)
        def _():
            fn(MOE_ROWS)

        @pl.when(n_ref[seg(e)] <= MOE_HALF)
        def _():
            fn(MOE_HALF)

    for e in range(N_EXPERTS):
        for_size(e, lambda rows, e=e: copy(e, rows).start())
    o_ref[...] = h_ref[...]
    rk = rk_ref[...]
    comb = comb_ref[...]
    lane = lax.broadcasted_iota(jnp.int32, (MOE_CHUNK, MOE_HALF), 1)
    for e in range(N_EXPERTS):
        for_size(e, lambda rows, e=e: copy(e, rows).wait())
        rank_col = rk[:, e:e + 1]
        w_col = comb[:, e:e + 1]
        onehot = jnp.where(rank_col == lane, 1.0, 0.0).astype(BF16)
        o_ref[...] += w_col * _dot(onehot, ybuf[e, 0:MOE_HALF, :])

        @pl.when(n_ref[seg(e)] > MOE_HALF)
        def _():
            onehot = jnp.where(rank_col == lane + MOE_HALF, 1.0, 0.0).astype(BF16)
            o_ref[...] += w_col * _dot(onehot, ybuf[e, MOE_HALF:MOE_ROWS, :])

    o_ref[...] = _ple_update(o_ref[...], p_ref[...], g_ref[...], wgate_ref[...], wproj_ref[...])


def _moe_combine(h, rk_pad, comb, y_sorted, lo, n, p, ple_gain, ple_wg, ple_wp):
    T = h.shape[0]
    nch = T // MOE_CHUNK
    const = lambda shape: pl.BlockSpec(shape, lambda c, lo, n: (0, 0))
    grid_spec = pltpu.PrefetchScalarGridSpec(
        num_scalar_prefetch=2,
        grid=(nch,),
        in_specs=[pl.BlockSpec((MOE_CHUNK, D_MODEL), lambda c, lo, n: (c, 0)),
                  pl.BlockSpec((MOE_CHUNK, LANE), lambda c, lo, n: (c, 0)),
                  pl.BlockSpec((MOE_CHUNK, LANE), lambda c, lo, n: (c, 0)),
                  pl.BlockSpec((MOE_CHUNK, PLE_DIM), lambda c, lo, n: (c, 0)),
                  const((1, D_MODEL)), const((D_MODEL, D_MODEL)), const((PLE_DIM, D_MODEL)),
                  pl.BlockSpec(memory_space=pl.ANY)],
        out_specs=pl.BlockSpec((MOE_CHUNK, D_MODEL), lambda c, lo, n: (c, 0)),
        scratch_shapes=[pltpu.VMEM((N_EXPERTS, MOE_ROWS, D_MODEL), BF16), pltpu.SemaphoreType.DMA((N_EXPERTS,))],
    )
    return pl.pallas_call(
        functools.partial(_moe_combine_kernel, nch=nch),
        grid_spec=grid_spec,
        out_shape=jax.ShapeDtypeStruct((T, D_MODEL), F32),
        compiler_params=_params("arbitrary"),
        name="moe_combine",
    )(lo, n, h, rk_pad, comb, p, ple_gain.reshape(1, D_MODEL), ple_wg, ple_wp, y_sorted)


def _moe(h, c_bf, comb, rk, cnt, wg, wu, wd, tf, ple):
    T = h.shape[0]
    lo, n, tile_e, valid = _moe_plan(cnt, T)
    x_sorted = _moe_gather(c_bf, rk[:, :N_EXPERTS].T, lo, n, _moe_tiles(T))
    y_sorted = _moe_ffn(x_sorted, tile_e, valid, wg, wu, wd, tf)
    return _moe_combine(h, rk, comb, y_sorted, lo, n, *ple)


def _ple_kernel(h_ref, p_ref, g_ref, wg_ref, wp_ref, o_ref):
    o_ref[...] = _ple_update(h_ref[...], p_ref[...], g_ref[...], wg_ref[...], wp_ref[...])


def _ple(h, p, gain, wg, wp, tm):
    T = h.shape[0]
    return pl.pallas_call(
        _ple_kernel,
        grid=(T // tm,),
        in_specs=[pl.BlockSpec((tm, D_MODEL), lambda i: (i, 0)),
                  pl.BlockSpec((tm, PLE_DIM), lambda i: (i, 0)),
                  pl.BlockSpec((1, D_MODEL), lambda i: (0, 0)),
                  pl.BlockSpec((D_MODEL, D_MODEL), lambda i: (0, 0)),
                  pl.BlockSpec((PLE_DIM, D_MODEL), lambda i: (0, 0))],
        out_specs=pl.BlockSpec((tm, D_MODEL), lambda i: (i, 0)),
        out_shape=jax.ShapeDtypeStruct((T, D_MODEL), F32),
        compiler_params=_params("parallel"),
        name="ple_gate",
    )(h, p, gain.reshape(1, D_MODEL), wg, wp)


def _tiles(T, S):
    pick = lambda n, pref: max(t for t in pref if n % t == 0)
    return dict(
        proj_m=pick(T, (512, 256, 128)), proj_n=NC,
        prep_m=pick(T, (512, 256, 128)),
        attn_q=256, attn_k=pick(S, (512, 256)),
        nsa_q=pick(S, (512, 256)), nsa_k=pick(S, (512, 256)),
        lin_m=pick(S, (512, 256, 128, 64)),
        row_m=pick(T, (512, 256, 128)),
        ffn_m=pick(T, (1024, 512, 256, 128)), ffn_f=512, moe_f=D_FF // 4,
    )


def kernel(x, p, norm_attn, w_in, w_out, nsa_cmp_pos, nsa_cmp_w1, nsa_cmp_w2, nsa_qk_gain, diff_qk_gain, diff_lambda, diff_norm, gla_w_gate2, gla_b_gate, gla_norm, hgrn_lb_logits, hgrn_norm, norm_ffn, ffn_w_gate, ffn_w_up, ffn_w_down, moe_router, moe_w_gate, moe_w_up, moe_w_down, ple_norm, ple_w_gate, ple_w_proj):
    B, S, _ = x.shape
    depth = w_in.shape[0]
    T = B * S
    t = _tiles(T, S)
    ncp = S // NSA_CMP_STRIDE
    half = NSA_CMP_STRIDE * HD
    cols = jnp.asarray(np.maximum(_COLS, 0), jnp.int32)
    col_mask = jnp.asarray(_COLS >= 0)
    ones_row = jnp.ones((GROUP,), F32)

    h = x.reshape(T, D_MODEL)
    for i in range(depth):
        w_in_r = jnp.where(col_mask[None, :], jnp.take(w_in[i], cols, axis=1), 0.0).astype(BF16)
        gains = jnp.stack([
            jnp.tile(nsa_qk_gain[i, 0], HEADS) * (HD ** -0.5 * LOG2E),
            jnp.tile(diff_qk_gain[i, 0], 2 * HEADS) * (DIFF_QK ** -0.5 * LOG2E),
            jnp.tile(diff_qk_gain[i, 1], 2 * HEADS),
            jnp.tile(nsa_qk_gain[i, 2], HEADS),
            jnp.tile(nsa_qk_gain[i, 3], HEADS),
            ones_row, ones_row, ones_row])
        pos = nsa_cmp_pos[i].reshape(2, 2, half)
        w1 = nsa_cmp_w1[i].astype(BF16)
        w2p = jnp.pad(nsa_cmp_w2[i], ((0, 0), (0, 0), (0, LANE - HD))).astype(BF16)
        kc_gain = jnp.pad(nsa_qk_gain[i, 1], (0, LANE - HD)).reshape(1, LANE)
        w2pad = jnp.zeros((LANE, LANE), F32).at[:GLA_RANK].set(gla_w_gate2[i])
        diff_gain_col = jnp.broadcast_to(diff_norm[i][:, None], (HD, t["attn_q"]))

        u = _norm_matmul(h, norm_attn[i], w_in_r, t["proj_m"], t["proj_n"])
        nqt, dqt, dk, dvt, kvs, vst, kvw, vwt = _prep(u, gains, t["prep_m"])
        kv = u[:, _B128["kvcmp"] * LANE:(_B128["kvcmp"] + 1) * LANE]
        xk = kv[:, :HD].reshape(B, ncp, half)
        xv = kv[:, HD:].reshape(B, ncp, half)
        kc, vct = _compress(xk, xv, pos, w1, w2p, kc_gain)
        o_a = _nsa_attention(u, nqt, kc, vct, kvs, vst, kvw, vwt, B, S, t["nsa_q"], t["nsa_k"], t["nsa_q"])
        o_b = _diff_attention(dqt, dk, dvt, diff_lambda[i], diff_gain_col, B, S, t["attn_q"], i)
        o_c, o_d = _linear_mixers(u, w2pad, gla_b_gate[i].reshape(1, LANE),
                                  jnp.tile(gla_norm[i], HEADS).reshape(1, GROUP), hgrn_lb_logits,
                                  jnp.tile(hgrn_norm[i], HEADS).reshape(1, GROUP), B, S, t["lin_m"], i)
        parts = (o_a, o_b, o_c, o_d)
        w_out_bf = w_out[i].astype(BF16)
        ple = (p[i].reshape(T, PLE_DIM), ple_norm[i], ple_w_gate[i].astype(BF16), ple_w_proj[i].astype(BF16))
        j = i // 2
        if i % 2 == 0:
            h = _ffn(h, parts, w_out_bf, norm_ffn[i], ffn_w_gate[j].astype(BF16), ffn_w_up[j].astype(BF16),
                     ffn_w_down[j].astype(BF16), t["ffn_m"], t["ffn_f"])
            h = _ple(h, *ple, t["row_m"])
        else:
            router_pad = jnp.zeros((D_MODEL, LANE), F32).at[:, :N_EXPERTS].set(moe_router[j])
            h, c_bf, comb, rk, cnt = _router(h, parts, w_out_bf, norm_ffn[i], router_pad, MOE_CHUNK)
            h = _moe(h, c_bf, comb, rk, cnt, moe_w_gate[j].astype(BF16), moe_w_up[j].astype(BF16),
                     moe_w_down[j].astype(BF16), t["moe_f"], ple)
    return h.reshape(B, S, D_MODEL)
```

```python
import functools
import math

import numpy as np
import jax
import jax.numpy as jnp
from jax import lax
from jax.experimental import pallas as pl
from jax.experimental.pallas import tpu as pltpu

F32 = jnp.float32
BF16 = jnp.bfloat16

D_MODEL = 1024
HEADS = 4
HD = 64
GROUP = HEADS * HD
NSA_CMP_LEN = 32
NSA_CMP_STRIDE = 16
NSA_CMP_HIDDEN = 4 * HD
NSA_SEL_LEN = 64
NSA_TOPK = 16
NSA_WINDOW = 512
NSA_SUBTILES = 2
DIFF_QK = HD // 2
GLA_DK = HD // 2
GLA_RANK = 16
GLA_TAU = 16.0
CHUNK = 64
D_FF = 7 * D_MODEL // 2
N_EXPERTS = 8
PLE_DIM = 256
EPS = 1e-6
NEG = -1e30
BIG = 1e30
LOWEST = -3.0e38
LOG2E = 1.4426950408889634

VMEM_LIMIT = 52 * 1024 * 1024
LANE = 128

_SRC = dict(nsa_q=0, k_cmp=256, v_cmp=320, k_slc=384, v_slc=448, k_win=512, v_win=576, nsa_g=640,
            d_q=652, d_k=908, d_v=1164, g_q=1420, g_k=1548, g_v=1676, g_lr=1932, g_og=1948,
            r_q=2204, r_f=2460, r_i=2716, r_og=2972)
IN_COLS = 3228

_B256 = dict(nsa_q=0, d_q=1, d_k=2, d_v=3, g_v=4, g_og=5, r_q=6, r_f=7, r_i=8, r_og=9)
_B128 = dict(g_q=20, g_k=21, g_lr=22, kvcmp=23, kvslc=24, kvwin=25, nsa_g=26)
NC = 27 * 128


def _column_map():
    cols = -np.ones((NC,), np.int64)

    def put(dst, src, width):
        cols[dst:dst + width] = np.arange(src, src + width)

    for name in _B256:
        put(_B256[name] * 256, _SRC[name], 256)
    put(_B128["g_q"] * 128, _SRC["g_q"], 128)
    put(_B128["g_k"] * 128, _SRC["g_k"], 128)
    put(_B128["g_lr"] * 128, _SRC["g_lr"], GLA_RANK)
    put(_B128["kvcmp"] * 128, _SRC["k_cmp"], 128)
    put(_B128["kvslc"] * 128, _SRC["k_slc"], 128)
    put(_B128["kvwin"] * 128, _SRC["k_win"], 128)
    put(_B128["nsa_g"] * 128, _SRC["nsa_g"], 3 * HEADS)
    return cols


_COLS = _column_map()


def _dot(a, b):
    return jnp.dot(a, b, preferred_element_type=F32)


def _dot_nt(a, b):
    return lax.dot_general(a, b, (((1,), (1,)), ((), ())), preferred_element_type=F32)


def _dot_tn(a, b):
    return lax.dot_general(a, b, (((0,), (0,)), ((), ())), preferred_element_type=F32)


def _split2(x):
    hi = x.astype(BF16)
    lo = (x - hi.astype(F32)).astype(BF16)
    return hi, lo


def _split3(x):
    hi = x.astype(BF16)
    r = x - hi.astype(F32)
    mid = r.astype(BF16)
    lo = (r - mid.astype(F32)).astype(BF16)
    return hi, mid, lo


def _group_mean(x, ones_bf, group):
    hi, lo = _split2(x)
    return (_dot(hi, ones_bf) + _dot(lo, ones_bf)) * (1.0 / group)


def _group_rms(x, ones_bf, group):
    return x * lax.rsqrt(_group_mean(x * x, ones_bf, group) + EPS)


def _sigmoid(x):
    return 1.0 / (1.0 + jnp.exp(-x))


def _silu(x):
    return x * _sigmoid(x)


def _params(*sem):
    return pltpu.CompilerParams(dimension_semantics=sem, vmem_limit_bytes=VMEM_LIMIT)


def _block_ones(n, group):
    i = np.arange(n)
    return jnp.asarray((i[:, None] // group == i[None, :] // group).astype(np.float32), BF16)


def _norm_matmul_kernel(x_ref, g_ref, w_ref, o_ref, xn_ref):
    @pl.when(pl.program_id(1) == 0)
    def _():
        x = x_ref[...]
        y = x * lax.rsqrt(jnp.mean(x * x, axis=-1, keepdims=True) + EPS) * g_ref[...]
        xn_ref[...] = y.astype(BF16)

    o_ref[...] = _dot(xn_ref[...], w_ref[...])


def _norm_matmul(x, gain, w_bf, tm, tn):
    T, K = x.shape
    N = w_bf.shape[1]
    return pl.pallas_call(
        _norm_matmul_kernel,
        grid=(T // tm, N // tn),
        in_specs=[pl.BlockSpec((tm, K), lambda i, j: (i, 0)),
                  pl.BlockSpec((1, K), lambda i, j: (0, 0)),
                  pl.BlockSpec((K, tn), lambda i, j: (0, j))],
        out_specs=pl.BlockSpec((tm, tn), lambda i, j: (i, j)),
        out_shape=jax.ShapeDtypeStruct((T, N), F32),
        scratch_shapes=[pltpu.VMEM((tm, K), BF16)],
        compiler_params=_params("parallel", "arbitrary"),
        name="in_proj",
    )(x, gain.reshape(1, K), w_bf)


def _prep_kernel(nq_ref, dq_ref, dk_ref, dv_ref, kvs_ref, kvw_ref, gains_ref, ones64_ref, ones32_ref,
                 ones64h_ref, o_nq, o_dq, o_dk, o_dv, o_kvs, o_vs, o_kvw, o_vw):
    ones64 = ones64_ref[...]
    ones32 = ones32_ref[...]
    ones64h = ones64h_ref[...]
    o_nq[...] = (_group_rms(nq_ref[...], ones64, HD) * gains_ref[0:1, :]).T.astype(BF16)
    o_dq[...] = (_group_rms(dq_ref[...], ones32, DIFF_QK) * gains_ref[1:2, :]).T.astype(BF16)
    o_dk[...] = (_group_rms(dk_ref[...], ones32, DIFF_QK) * gains_ref[2:3, :]).astype(BF16)
    o_dv[...] = dv_ref[...].T.astype(BF16)
    lane = lax.broadcasted_iota(jnp.int32, kvs_ref.shape, 1)
    for kv_ref, gain, o_kv, o_v in ((kvs_ref, gains_ref[3:4, 0:LANE], o_kvs, o_vs),
                                    (kvw_ref, gains_ref[4:5, 0:LANE], o_kvw, o_vw)):
        x = kv_ref[...]
        o_kv[...] = jnp.where(lane < HD, _group_rms(x, ones64h, HD) * gain, x).astype(BF16)
        o_v[...] = x.T[HD:2 * HD, :].astype(BF16)


def _prep(u, gains, tm):
    T = u.shape[0]
    c256 = lambda name: pl.BlockSpec((tm, GROUP), lambda i, c=_B256[name]: (i, c))
    c128 = lambda name: pl.BlockSpec((tm, LANE), lambda i, c=_B128[name]: (i, c))
    const = lambda shape: pl.BlockSpec(shape, lambda i: (0, 0))
    rows = lambda w: pl.BlockSpec((tm, w), lambda i: (i, 0))
    colsT = lambda h: pl.BlockSpec((h, tm), lambda i: (0, i))
    sds = jax.ShapeDtypeStruct
    return pl.pallas_call(
        _prep_kernel,
        grid=(T // tm,),
        in_specs=[c256("nsa_q"), c256("d_q"), c256("d_k"), c256("d_v"), c128("kvslc"), c128("kvwin"),
                  const((8, GROUP)), const((GROUP, GROUP)), const((GROUP, GROUP)), const((LANE, LANE))],
        out_specs=[colsT(GROUP), colsT(GROUP), rows(GROUP), colsT(GROUP),
                   rows(LANE), colsT(HD), rows(LANE), colsT(HD)],
        out_shape=[sds((GROUP, T), BF16), sds((GROUP, T), BF16), sds((T, GROUP), BF16), sds((GROUP, T), BF16),
                   sds((T, LANE), BF16), sds((HD, T), BF16), sds((T, LANE), BF16), sds((HD, T), BF16)],
        compiler_params=_params("parallel"),
        name="attn_prep",
    )(u, u, u, u, u, u, gains, _block_ones(GROUP, HD), _block_ones(GROUP, DIFF_QK), _block_ones(LANE, HD))


def _compress_kernel(kv_ref, pos_ref, w1_ref, w2_ref, gain_ref, ones64h_ref, kc_ref, vc_ref):
    ncp = kc_ref.shape[0]
    hid = NSA_CMP_HIDDEN
    top = jnp.zeros((ncp, 2 * hid), F32)
    bot = jnp.zeros((ncp, 2 * hid), F32)
    for l in range(NSA_CMP_STRIDE):
        x = kv_ref[pl.ds(l, ncp, stride=NSA_CMP_STRIDE), :]
        top = top + _dot((x + pos_ref[0, l:l + 1, :]).astype(BF16), w1_ref[0, l])
        bot = bot + _dot((x + pos_ref[1, l:l + 1, :]).astype(BF16), w1_ref[1, l])
    hidden = top + pltpu.roll(bot, ncp - 1, 0)
    act = _silu(hidden).astype(BF16)
    kc = _dot(act[:, 0:hid], w2_ref[0])
    kc_ref[...] = _group_rms(kc, ones64h_ref[...], HD) * gain_ref[...]
    vc_ref[...] = _dot(act[:, hid:2 * hid], w2_ref[1]).T[0:HD, :].astype(BF16)


def _compress(u, pos, w1_bd, w2p_bf, gain_row, B, S):
    ncp = S // NSA_CMP_STRIDE
    sq = pl.Squeezed()
    full = lambda shape: pl.BlockSpec(shape, lambda b: (0,) * len(shape))
    return pl.pallas_call(
        _compress_kernel,
        grid=(B,),
        in_specs=[pl.BlockSpec((S, LANE), lambda b, c=_B128["kvcmp"]: (b, c)),
                  full((2, NSA_CMP_STRIDE, LANE)), full((2, NSA_CMP_STRIDE, LANE, 2 * NSA_CMP_HIDDEN)),
                  full((2, NSA_CMP_HIDDEN, LANE)), full((1, LANE)), full((LANE, LANE))],
        out_specs=[pl.BlockSpec((sq, ncp, LANE), lambda b: (b, 0, 0)),
                   pl.BlockSpec((sq, HD, ncp), lambda b: (b, 0, 0))],
        out_shape=[jax.ShapeDtypeStruct((B, ncp, LANE), F32), jax.ShapeDtypeStruct((B, HD, ncp), BF16)],
        compiler_params=_params("parallel"),
        name="nsa_compress",
    )(u, pos, w1_bd, w2p_bf, gain_row, _block_ones(LANE, HD))


def _softmax_step_t(s, v_t, m_old, l_old, acc_ref):
    m_new = jnp.maximum(m_old, jnp.max(s, axis=0, keepdims=True))
    alpha = jnp.exp2(m_old - m_new)
    p = jnp.exp2(s - m_new)
    acc_ref[...] = alpha * acc_ref[...] + _dot(v_t, p.astype(BF16))
    return m_new, alpha * l_old + jnp.sum(p, axis=0, keepdims=True)


def _pipelined_sweep(lo, hi, qk, process, process_last, stats, sa_ref, sb_ref):
    def pair(jj, st):
        t = lo + 2 * jj
        qk(t + 1, sb_ref)
        st = process(t, sa_ref, st)
        qk(t + 2, sa_ref)
        return process(t + 1, sb_ref, st)

    def two_left(st):
        qk(hi - 1, sb_ref)
        st = process(hi - 2, sa_ref, st)
        return process_last(hi - 1, sb_ref, st)

    n = hi - lo
    qk(lo, sa_ref)
    stats = lax.fori_loop(0, (n - 1) // 2, pair, stats)
    return lax.cond(n % 2 == 0, two_left, lambda st: process_last(hi - 1, sa_ref, st), stats)


def _nsa_kernel(qt_ref, g_ref, kc_ref, vct_ref, kvs_ref, vst_ref, kvw_ref, vwt_ref, ovl_ref, exp_ref,
                o_ref, qs_ref, sa_ref, sb_ref, *acc_refs, tq, tk, tw, ksel):
    i = pl.program_id(1)
    t0 = i * tq
    ncp = kc_ref.shape[0]
    nsel = ovl_ref.shape[0]
    acc_s, acc_w = acc_refs[:HEADS], acc_refs[HEADS:]

    qt = qt_ref[...]
    qs_ref[HD:LANE, :] = jnp.zeros((LANE - HD, HEADS * tq), BF16)
    for h in range(HEADS):
        qs_ref[0:HD, h * tq:(h + 1) * tq] = qt[h * HD:(h + 1) * HD, :]
        acc_s[h][...] = jnp.zeros((HD, tq), F32)
        acc_w[h][...] = jnp.zeros((HD, tq), F32)

    kc_hi, kc_lo = _split2(kc_ref[...])
    n_idx = lax.broadcasted_iota(jnp.int32, (ncp, tq), 0)
    t_lane = t0 + lax.broadcasted_iota(jnp.int32, (ncp, tq), 1)
    ok = n_idx * NSA_CMP_STRIDE + (NSA_CMP_LEN - 1) <= t_lane
    cmp_scores = _dot(kc_hi, qs_ref[...]) + _dot(kc_lo, qs_ref[...])
    o_c = []
    psum = jnp.zeros((ncp, tq), F32)
    for h in range(HEADS):
        s = jnp.where(ok, cmp_scores[:, h * tq:(h + 1) * tq], NEG)
        e = jnp.exp2(s - jnp.max(s, axis=0, keepdims=True))
        p = jnp.where(ok, e / jnp.sum(e, axis=0, keepdims=True), 0.0)
        o_c.append(_dot(vct_ref[...], p.astype(BF16)))
        psum = psum + p

    p_hi, p_lo = _split2(psum)
    imp = _dot(ovl_ref[...], p_hi) + _dot(ovl_ref[...], p_lo)
    blk = lax.broadcasted_iota(jnp.int32, (nsel, tq), 0)
    t_col = t0 + lax.broadcasted_iota(jnp.int32, (nsel, tq), 1)
    cur = t_col // NSA_SEL_LEN
    forced = (blk == 0) | (blk == cur) | (blk == cur - 1)
    vals = jnp.where(forced, BIG, jnp.where(blk * NSA_SEL_LEN <= t_col, imp, NEG))
    blk_f = blk.astype(F32)
    sel = jnp.zeros((nsel, tq), F32)
    for _ in range(ksel):
        mx = jnp.max(vals, axis=0, keepdims=True)
        first = jnp.min(jnp.where(vals == mx, blk_f, float(nsel)), axis=0, keepdims=True)
        pick = blk_f == first
        sel = jnp.where(pick, 1.0, sel)
        vals = jnp.where(pick, LOWEST, vals)
    if nsel < LANE:
        sel = jnp.concatenate([sel, jnp.zeros((LANE - nsel, tq), F32)], axis=0)
    sel_bf = sel.astype(BF16)

    stats0 = tuple((jnp.full((1, tq), NEG, F32), jnp.zeros((1, tq), F32)) for _ in range(HEADS))

    def sweep(lo, hi, k_ref, vt_ref, accs, width, bias_fn):
        def qk(kt, dst):
            k = k_ref[pl.ds(pl.multiple_of(kt * width, width), width), :]
            for h in range(HEADS):
                dst[h, 0:width, :] = _dot(k, qs_ref[:, h * tq:(h + 1) * tq])

        def process(kt, src, stats):
            k0 = pl.multiple_of(kt * width, width)
            kpos = k0 + lax.broadcasted_iota(jnp.int32, (width, tq), 0)
            tpos = t0 + lax.broadcasted_iota(jnp.int32, (width, tq), 1)
            bias = bias_fn(k0, kpos, tpos)
            vt = vt_ref[:, pl.ds(k0, width)]
            return tuple(_softmax_step_t(src[h, 0:width, :] + bias, vt, stats[h][0], stats[h][1], accs[h])
                         for h in range(HEADS))

        return _pipelined_sweep(lo, hi, qk, process, process, stats0, sa_ref, sb_ref)

    def sel_bias(k0, kpos, tpos):
        chosen = _dot(exp_ref[pl.ds(k0, kpos.shape[0]), :], sel_bf)
        return jnp.where((chosen > 0.5) & (kpos <= tpos), 0.0, NEG)

    st_s = sweep(0, (t0 + tq - 1) // tk + 1, kvs_ref, vst_ref, acc_s, tk, sel_bias)

    def win_bias(k0, kpos, tpos):
        return jnp.where((kpos <= tpos) & (kpos > tpos - NSA_WINDOW), 0.0, NEG)

    st_w = sweep(jnp.maximum(t0 - NSA_WINDOW, 0) // tw, (t0 + tq - 1) // tw + 1,
                 kvw_ref, vwt_ref, acc_w, tw, win_bias)

    gates = _sigmoid(g_ref[...].T)
    outs = []
    for h in range(HEADS):
        outs.append(gates[3 * h:3 * h + 1, :] * o_c[h]
                    + gates[3 * h + 1:3 * h + 2, :] * (acc_s[h][...] / st_s[h][1])
                    + gates[3 * h + 2:3 * h + 3, :] * (acc_w[h][...] / st_w[h][1]))
    o_ref[...] = jnp.concatenate(outs, axis=0).T


def _nsa_attention(u, nqt, kc, vct, kvs, vst, kvw, vwt, B, S, tq, tk, tw):
    nq = S // tq
    ncp = S // NSA_CMP_STRIDE
    nsel = S // NSA_SEL_LEN
    ksel = min(NSA_TOPK, nsel)
    n_cmp = (S - NSA_CMP_LEN) // NSA_CMP_STRIDE + 1
    cmp_start = np.arange(ncp) * NSA_CMP_STRIDE
    sel_start = np.arange(nsel) * NSA_SEL_LEN
    overlap = ((cmp_start[:, None] <= sel_start[None, :] + NSA_SEL_LEN - 1)
               & (cmp_start[:, None] + NSA_CMP_LEN - 1 >= sel_start[None, :])
               & (np.arange(ncp)[:, None] < n_cmp))
    ovl_t = jnp.asarray(overlap.T.astype(np.float32), BF16)
    expand = (jnp.arange(S, dtype=jnp.int32)[:, None] // NSA_SEL_LEN
              == jnp.arange(LANE, dtype=jnp.int32)[None, :]).astype(BF16)
    sq = pl.Squeezed()
    seq_rows = pl.BlockSpec((S, LANE), lambda b, i: (b, 0))
    seq_cols = pl.BlockSpec((HD, S), lambda b, i: (0, b))
    kern = functools.partial(_nsa_kernel, tq=tq, tk=tk, tw=tw, ksel=ksel)
    return pl.pallas_call(
        kern,
        grid=(B, nq),
        in_specs=[pl.BlockSpec((GROUP, tq), lambda b, i: (0, b * nq + i)),
                  pl.BlockSpec((tq, LANE), lambda b, i, c=_B128["nsa_g"]: (b * nq + i, c)),
                  pl.BlockSpec((sq, ncp, LANE), lambda b, i: (b, 0, 0)),
                  pl.BlockSpec((sq, HD, ncp), lambda b, i: (b, 0, 0)),
                  seq_rows, seq_cols, seq_rows, seq_cols,
                  pl.BlockSpec((nsel, ncp), lambda b, i: (0, 0)),
                  pl.BlockSpec((S, LANE), lambda b, i: (0, 0))],
        out_specs=pl.BlockSpec((tq, GROUP), lambda b, i: (b * nq + i, 0)),
        out_shape=jax.ShapeDtypeStruct((B * S, GROUP), F32),
        scratch_shapes=([pltpu.VMEM((LANE, HEADS * tq), BF16)]
                        + [pltpu.VMEM((HEADS, max(tk, tw), tq), F32)] * 2
                        + [pltpu.VMEM((HD, tq), F32)] * (2 * HEADS)),
        compiler_params=_params("parallel", "arbitrary"),
        name="nsa_attention",
    )(nqt, u, kc, vct, kvs, vst, kvw, vwt, ovl_t, expand)


def _diff_kernel(lam_ref, qt_ref, k_ref, vt_ref, gain_ref, o_ref, qs_ref, sa_ref, sb_ref, *acc_refs,
                 tq, lam_init):
    i = pl.program_id(1)
    t0 = i * tq
    lanes = 2 * tq

    qt = qt_ref[...]
    row = lax.broadcasted_iota(jnp.int32, (GROUP, tq), 0)
    zero = jnp.zeros_like(qt)
    for h in range(HEADS):
        qs_ref[h] = jnp.concatenate([jnp.where(row // DIFF_QK == 2 * h, qt, zero),
                                     jnp.where(row // DIFF_QK == 2 * h + 1, qt, zero)], axis=1)
        acc_refs[h][...] = jnp.zeros((HD, lanes), F32)

    def qk(kt, dst):
        k = k_ref[pl.ds(pl.multiple_of(kt * tq, tq), tq), :]
        for h in range(HEADS):
            dst[h] = _dot(k, qs_ref[h])

    def process(kt, src, masked, stats):
        k0 = pl.multiple_of(kt * tq, tq)
        out = []
        for h in range(HEADS):
            s = src[h]
            if masked:
                kpos = k0 + lax.broadcasted_iota(jnp.int32, (tq, lanes), 0)
                tpos = t0 + lax.broadcasted_iota(jnp.int32, (tq, lanes), 1) % tq
                s = jnp.where(kpos <= tpos, s, NEG)
            out.append(_softmax_step_t(s, vt_ref[h * HD:(h + 1) * HD, pl.ds(k0, tq)],
                                       stats[h][0], stats[h][1], acc_refs[h]))
        return tuple(out)

    stats0 = tuple((jnp.full((1, lanes), NEG, F32), jnp.zeros((1, lanes), F32)) for _ in range(HEADS))
    stats = _pipelined_sweep(0, i + 1, qk, lambda kt, src, st: process(kt, src, False, st),
                             lambda kt, src, st: process(kt, src, True, st), stats0, sa_ref, sb_ref)

    lam = lam_ref[...]
    lam_full = (jnp.exp(jnp.sum(lam[0:1] * lam[1:2], axis=1, keepdims=True))
                - jnp.exp(jnp.sum(lam[2:3] * lam[3:4], axis=1, keepdims=True)) + lam_init)
    outs = []
    for h in range(HEADS):
        r = acc_refs[h][...] / stats[h][1]
        d = r[:, 0:tq] - lam_full * r[:, tq:lanes]
        d = d * lax.rsqrt(jnp.mean(d * d, axis=0, keepdims=True) + EPS)
        outs.append(d * gain_ref[...] * (1.0 - lam_init))
    o_ref[...] = jnp.concatenate(outs, axis=0).T


def _diff_attention(dqt, dk, dvt, lam, gain_col, B, S, tq, layer_idx):
    nq = S // tq
    lam_init = 0.8 - 0.6 * math.exp(-0.3 * layer_idx)
    kern = functools.partial(_diff_kernel, tq=tq, lam_init=lam_init)
    scores = pltpu.VMEM((HEADS, tq, 2 * tq), F32)
    return pl.pallas_call(
        kern,
        grid=(B, nq),
        in_specs=[pl.BlockSpec((4, DIFF_QK), lambda b, i: (0, 0)),
                  pl.BlockSpec((GROUP, tq), lambda b, i: (0, b * nq + i)),
                  pl.BlockSpec((S, GROUP), lambda b, i: (b, 0)),
                  pl.BlockSpec((GROUP, S), lambda b, i: (0, b)),
                  pl.BlockSpec((HD, tq), lambda b, i: (0, 0))],
        out_specs=pl.BlockSpec((tq, GROUP), lambda b, i: (b * nq + i, 0)),
        out_shape=jax.ShapeDtypeStruct((B * S, GROUP), F32),
        scratch_shapes=([pltpu.VMEM((HEADS, GROUP, 2 * tq), BF16), scores, scores]
                        + [pltpu.VMEM((HD, 2 * tq), F32)] * HEADS),
        compiler_params=_params("parallel", "arbitrary"),
        name="diff_attention",
    )(lam, dqt, dk, dvt, gain_col)


_LEVELS = (32, 16, 8, 4, 2, 1)
LIN_GROUP = 4


def _stack_heads(x, lane, group, count):
    zero = jnp.zeros_like(x)
    return jnp.concatenate([jnp.where(lane // group == g, x, zero) for g in range(count)], axis=0)


def _unstack_heads(x4, lane, rows):
    out = jnp.where(lane // HD == 0, x4[0:rows], 0.0)
    for h in range(1, HEADS):
        out = out + jnp.where(lane // HD == h, x4[h * rows:(h + 1) * rows], 0.0)
    return out


def _linear_consts():
    r = np.arange(CHUNK)[:, None]
    t = np.arange(CHUNK)[None, :]
    tri = (t <= r).astype(np.float32)
    masks = [(r // (2 * s) == t // (2 * s)) for s in _LEVELS] + [r == t]
    mall = np.stack([np.tile(m.astype(np.float32), (1, HEADS)) for m in masks])
    return jnp.asarray(tri, BF16), jnp.asarray(mall, F32)


def _level_exponent(s, lg, b, row):
    if s == 1:
        return jnp.where((row & 1) != 0, 0.0, pltpu.roll(lg, CHUNK - 1, 0))
    if s == 2:
        nxt1 = pltpu.roll(lg, CHUNK - 1, 0)
        nxt2 = pltpu.roll(lg, CHUNK - 2, 0)
        r4 = row & 3
        return jnp.where(r4 == 0, nxt1 + nxt2, jnp.where(r4 == 1, nxt1, jnp.where(r4 == 2, 0.0, lg)))
    mids = [jnp.broadcast_to(b[m:m + 1, :], (2 * s, b.shape[1])) for m in range(s, CHUNK, 2 * s)]
    d = b - (jnp.concatenate(mids, axis=0) if len(mids) > 1 else mids[0])
    return jnp.where((row & s) != 0, d, -d)


def _linear_chunks(items, tri, mall_ref):
    bs = []
    for q, k, v, lg, state_ref, dk in items:
        hi, mid, lo = _split3(lg)
        bs.append(_dot(tri, hi) + _dot(tri, mid) + _dot(tri, lo))
    ats = []
    for (q, k, v, lg, state_ref, dk), b in zip(items, bs):
        dkh = HEADS * dk
        row = lax.broadcasted_iota(jnp.int32, (CHUNK, dkh), 0)
        lane_k = lax.broadcasted_iota(jnp.int32, (CHUNK, dkh), 1)
        a_t = mall_ref[len(_LEVELS)] * _dot_nt(k.astype(BF16), _stack_heads(q.astype(BF16), lane_k, dk, HEADS))
        for li, s in enumerate(_LEVELS):
            e = jnp.exp(_level_exponent(s, lg, b, row))
            upper = (row & s) != 0
            qt = jnp.where(upper, q * e, 0.0).astype(BF16)
            kt = jnp.where(upper, 0.0, k * e).astype(BF16)
            a_t = a_t + mall_ref[li] * _dot_nt(kt, _stack_heads(qt, lane_k, dk, HEADS))
        ats.append(a_t)
    lane_v = lax.broadcasted_iota(jnp.int32, (CHUNK, GROUP), 1)
    partial = []
    for (q, k, v, lg, state_ref, dk), b, a_t in zip(items, bs, ats):
        v_bf = v.astype(BF16)
        o_intra = _unstack_heads(_dot_tn(a_t.astype(BF16), v_bf), lane_v, CHUNK)
        e_b = jnp.exp(b)
        e_u = jnp.exp(b[CHUNK - 1:CHUNK, :] - b)
        kv = _dot_tn(v_bf, (k * e_u).astype(BF16))
        partial.append((o_intra, e_b, kv))
    outs = []
    for (q, k, v, lg, state_ref, dk), (o_intra, e_b, kv) in zip(items, partial):
        dkh = HEADS * dk
        st = state_ref[...]
        o_inter = _dot_nt((q * e_b).astype(BF16), st.astype(BF16))
        srow = lax.broadcasted_iota(jnp.int32, (GROUP, dkh), 0)
        scol = lax.broadcasted_iota(jnp.int32, (GROUP, dkh), 1)
        state_ref[...] = st * e_b[CHUNK - 1:CHUNK, :] + jnp.where(srow // HD == scol // dk, kv, 0.0)
        outs.append(o_inter + o_intra)
    return outs


def _linear_kernel(gq_ref, gk_ref, gv_ref, lr_ref, gog_ref, w2_ref, b_ref, ggain_ref,
                   rq_ref, rf_ref, ri_ref, rog_ref, lbl_ref, rgain_ref, tri_ref, mall_ref, ones64_ref,
                   og_ref, or_ref, gstate_ref, rstate_ref, *, tm, layer_idx):
    @pl.when(pl.program_id(1) == 0)
    def _():
        gstate_ref[...] = jnp.zeros_like(gstate_ref)
        rstate_ref[...] = jnp.zeros_like(rstate_ref)

    tri = tri_ref[...]
    ones64 = ones64_ref[...]
    w_hi, w_lo = _split2(w2_ref[...])
    logits = lbl_ref[...]
    ez = jnp.exp(logits - jnp.max(logits, axis=0, keepdims=True))
    probs = ez / jnp.sum(ez, axis=0, keepdims=True)
    lb = jnp.zeros((1, GROUP), F32)
    for j in range(1, layer_idx + 1):
        lb = lb + probs[j:j + 1]

    def body(c, carry):
        items, sinks = [], []
        for g in range(LIN_GROUP):
            rs = pl.ds(pl.multiple_of((c * LIN_GROUP + g) * CHUNK, CHUNK), CHUNK)
            lr_hi, lr_lo = _split2(lr_ref[rs, :])
            x = _dot(lr_hi, w_hi) + _dot(lr_lo, w_hi) + _dot(lr_hi, w_lo) + b_ref[...]
            lg = (jnp.minimum(x, 0.0) - jnp.log(1.0 + jnp.exp(-jnp.abs(x)))) * (1.0 / GLA_TAU)
            items.append((gq_ref[rs, :] * (GLA_DK ** -0.5), gk_ref[rs, :], gv_ref[rs, :], lg, gstate_ref, GLA_DK))
            sinks.append((og_ref, ggain_ref, gog_ref, rs))
            z = rf_ref[rs, :]
            f = lb + (1.0 - lb) * _sigmoid(z)
            items.append((rq_ref[rs, :], (1.0 - lb) * _sigmoid(-z), ri_ref[rs, :], jnp.log(f), rstate_ref, HD))
            sinks.append((or_ref, rgain_ref, rog_ref, rs))
        for o, (out_ref, gain_ref, gate_ref, rs) in zip(_linear_chunks(items, tri, mall_ref), sinks):
            out_ref[rs, :] = _group_rms(o, ones64, HD) * gain_ref[...] * _silu(gate_ref[rs, :])
        return carry

    lax.fori_loop(0, tm // (CHUNK * LIN_GROUP), body, 0)


def _linear_mixers(u, w2pad, b_gate, gla_gain, lb_logits, hgrn_gain, B, S, tm, layer_idx):
    nt = S // tm
    tri, mall = _linear_consts()
    ones64 = _block_ones(GROUP, HD)
    c256 = lambda name: pl.BlockSpec((tm, GROUP), lambda b, i, c=_B256[name]: (b * nt + i, c))
    c128 = lambda name: pl.BlockSpec((tm, LANE), lambda b, i, c=_B128[name]: (b * nt + i, c))
    full = lambda shape: pl.BlockSpec(shape, lambda b, i: (0,) * len(shape))
    out_spec = pl.BlockSpec((tm, GROUP), lambda b, i: (b * nt + i, 0))
    out_shape = jax.ShapeDtypeStruct((B * S, GROUP), F32)
    depth = lb_logits.shape[0]
    return pl.pallas_call(
        functools.partial(_linear_kernel, tm=tm, layer_idx=layer_idx),
        grid=(B, nt),
        in_specs=[c128("g_q"), c128("g_k"), c256("g_v"), c128("g_lr"), c256("g_og"),
                  full((LANE, LANE)), full((1, LANE)), full((1, GROUP)),
                  c256("r_q"), c256("r_f"), c256("r_i"), c256("r_og"), full((depth, GROUP)), full((1, GROUP)),
                  full(tri.shape), full(mall.shape), full((GROUP, GROUP))],
        out_specs=[out_spec, out_spec], out_shape=[out_shape, out_shape],
        scratch_shapes=[pltpu.VMEM((GROUP, HEADS * GLA_DK), F32), pltpu.VMEM((GROUP, GROUP), F32)],
        compiler_params=_params("parallel", "arbitrary"),
        name="linear_mixers",
    )(u, u, u, u, u, w2pad, b_gate, gla_gain, u, u, u, u, lb_logits, hgrn_gain, tri, mall, ones64)


def _mix_residual(h_ref, part_refs, wout_ref):
    x = h_ref[...]
    for j, r in enumerate(part_refs):
        x = x + _dot(r[...].astype(BF16), wout_ref[j * GROUP:(j + 1) * GROUP, :])
    return x


def _ffn_kernel(h_ref, a_ref, b_ref, c_ref, d_ref, wout_ref, g_ref, wg_ref, wu_ref, wd_ref, o_ref,
                xn_ref, acc_ref):
    f = pl.program_id(1)

    @pl.when(f == 0)
    def _():
        x = _mix_residual(h_ref, (a_ref, b_ref, c_ref, d_ref), wout_ref)
        y = x * lax.rsqrt(jnp.mean(x * x, axis=-1, keepdims=True) + EPS) * g_ref[...]
        xn_ref[...] = y.astype(BF16)
        acc_ref[...] = x

    xn = xn_ref[...]
    mid = _silu(_dot(xn, wg_ref[...])) * _dot(xn, wu_ref[...])
    acc_ref[...] += _dot(mid.astype(BF16), wd_ref[...])

    @pl.when(f == pl.num_programs(1) - 1)
    def _():
        o_ref[...] = acc_ref[...]


def _ffn(h, parts, w_out, gain, wg, wu, wd, tm, tf):
    T = h.shape[0]
    F = wg.shape[1]
    part = pl.BlockSpec((tm, GROUP), lambda i, f: (i, 0))
    return pl.pallas_call(
        _ffn_kernel,
        grid=(T // tm, F // tf),
        in_specs=[pl.BlockSpec((tm, D_MODEL), lambda i, f: (i, 0)), part, part, part, part,
                  pl.BlockSpec((D_MODEL, D_MODEL), lambda i, f: (0, 0)),
                  pl.BlockSpec((1, D_MODEL), lambda i, f: (0, 0)),
                  pl.BlockSpec((D_MODEL, tf), lambda i, f: (0, f)),
                  pl.BlockSpec((D_MODEL, tf), lambda i, f: (0, f)),
                  pl.BlockSpec((tf, D_MODEL), lambda i, f: (f, 0))],
        out_specs=pl.BlockSpec((tm, D_MODEL), lambda i, f: (i, 0)),
        out_shape=jax.ShapeDtypeStruct((T, D_MODEL), F32),
        scratch_shapes=[pltpu.VMEM((tm, D_MODEL), BF16), pltpu.VMEM((tm, D_MODEL), F32)],
        compiler_params=_params("parallel", "arbitrary"),
        name="ffn_swiglu",
    )(h, *parts, w_out, gain.reshape(1, D_MODEL), wg, wu, wd)


def _router_kernel(h_ref, a_ref, b_ref, c4_ref, d_ref, wout_ref, g_ref, r_ref, lower_ref,
                   h1_ref, c_ref, comb_ref, rk_ref, cnt_ref):
    x = _mix_residual(h_ref, (a_ref, b_ref, c4_ref, d_ref), wout_ref)
    h1_ref[...] = x
    y = x * lax.rsqrt(jnp.mean(x * x, axis=-1, keepdims=True) + EPS) * g_ref[...]
    c_ref[...] = y.astype(BF16)
    y_hi, y_lo = _split2(y)
    r_hi, r_lo = _split2(r_ref[...])
    logits = _dot(y_hi, r_hi) + _dot(y_lo, r_hi) + _dot(y_hi, r_lo)
    lane = lax.broadcasted_iota(jnp.int32, logits.shape, 1)
    lane_f = lane.astype(F32)
    logits = jnp.where(lane < N_EXPERTS, logits, LOWEST)
    m1 = jnp.max(logits, axis=1, keepdims=True)
    i1 = jnp.min(jnp.where(logits == m1, lane_f, float(LANE)), axis=1, keepdims=True)
    rest = jnp.where(lane_f == i1, LOWEST, logits)
    m2 = jnp.max(rest, axis=1, keepdims=True)
    i2 = jnp.min(jnp.where(rest == m2, lane_f, float(LANE)), axis=1, keepdims=True)
    e2 = jnp.exp(m2 - m1)
    w1 = 1.0 / (1.0 + e2)
    comb_ref[...] = jnp.where(lane_f == i1, w1, jnp.where(lane_f == i2, e2 * w1, 0.0))
    chosen = (lane_f == i1) | (lane_f == i2)
    sel = jnp.where(chosen, 1.0, 0.0)
    rank = _dot(lower_ref[...], sel.astype(BF16))
    rk_ref[...] = jnp.where(chosen, rank, -1.0).astype(jnp.int32)
    cnt_ref[...] = jnp.broadcast_to(jnp.sum(sel, axis=0, keepdims=True), cnt_ref.shape).astype(jnp.int32)


def _router(h, parts, w_out, gain, router_pad, tm):
    T = h.shape[0]
    i = np.arange(tm)
    lower = jnp.asarray((i[None, :] < i[:, None]).astype(np.float32), BF16)
    sds = jax.ShapeDtypeStruct
    rows = lambda w: pl.BlockSpec((tm, w), lambda i: (i, 0))
    return pl.pallas_call(
        _router_kernel,
        grid=(T // tm,),
        in_specs=[rows(D_MODEL), rows(GROUP), rows(GROUP), rows(GROUP), rows(GROUP),
                  pl.BlockSpec((D_MODEL, D_MODEL), lambda i: (0, 0)),
                  pl.BlockSpec((1, D_MODEL), lambda i: (0, 0)),
                  pl.BlockSpec((D_MODEL, LANE), lambda i: (0, 0)),
                  pl.BlockSpec((tm, tm), lambda i: (0, 0))],
        out_specs=[rows(D_MODEL), rows(D_MODEL), rows(LANE), rows(LANE), pl.BlockSpec((8, LANE), lambda i: (i, 0))],
        out_shape=[sds((T, D_MODEL), F32), sds((T, D_MODEL), BF16), sds((T, LANE), F32), sds((T, LANE), jnp.int32),
                   sds((T // tm * 8, LANE), jnp.int32)],
        compiler_params=_params("parallel"),
        name="moe_router",
    )(h, *parts, w_out, gain.reshape(1, D_MODEL), router_pad, lower)


MOE_CHUNK = 512
MOE_ROWS = 512
MOE_HALF = MOE_ROWS // 2
MOE_TILE = 1024
MOE_ALIGN = 16
MOE_PAD_SEGS = MOE_TILE // MOE_ROWS


def _moe_plan(cnt, T):
    nch = T // MOE_CHUNK
    n_ce = cnt.reshape(nch, 8, LANE)[:, 0, :N_EXPERTS]
    cap = (n_ce + MOE_ALIGN - 1) // MOE_ALIGN * MOE_ALIGN
    tot = jnp.sum(cap, axis=0)
    ptot = (tot + MOE_TILE - 1) // MOE_TILE * MOE_TILE
    start = jnp.cumsum(ptot) - ptot
    lo_ce = start[None, :] + jnp.cumsum(cap, axis=0) - cap
    pad_lo = (start + tot)[:, None] + MOE_ROWS * jnp.arange(MOE_PAD_SEGS, dtype=jnp.int32)[None, :]
    lo = jnp.concatenate([lo_ce.T, pad_lo], axis=1)
    n = jnp.concatenate([n_ce.T, jnp.full((N_EXPERTS, MOE_PAD_SEGS), MOE_ROWS, jnp.int32)], axis=1)
    n_tiles = _moe_tiles(T)
    tile_end = jnp.cumsum(ptot // MOE_TILE)
    j = jnp.arange(n_tiles, dtype=jnp.int32)
    tile_e = jnp.minimum(jnp.sum((tile_end[None, :] <= j[:, None]).astype(jnp.int32), axis=1), N_EXPERTS - 1)
    valid = (j < tile_end[-1]).astype(jnp.int32)
    n_flat = jnp.concatenate([n.reshape(-1), tile_end[-1:] * MOE_PAD_SEGS]).astype(jnp.int32)
    return lo.reshape(-1).astype(jnp.int32), n_flat, tile_e, valid


def _moe_tiles(T):
    nch = T // MOE_CHUNK
    rows = 2 * T + nch * N_EXPERTS * (MOE_ALIGN - 1) + N_EXPERTS * (MOE_TILE - 1)
    return -(-rows // MOE_TILE) + 1


def _moe_gather_kernel(lo_ref, n_ref, c_ref, rkt_ref, x_hbm, buf, sem, *, nch):
    e = pl.program_id(0)
    c = pl.program_id(1)
    n_steps = pl.num_programs(0) * pl.num_programs(1)
    step = e * pl.num_programs(1) + c
    slot = step % 2
    rk_row = rkt_ref[pl.ds(e, 1), :]
    rk_row = jnp.where(c < nch, rk_row, -1)
    row = lax.broadcasted_iota(jnp.int32, (MOE_HALF, MOE_CHUNK), 0)
    chunk = c_ref[...]
    onehot = jnp.where(rk_row == row, 1.0, 0.0).astype(BF16)
    buf[slot, 0:MOE_HALF, :] = _dot(onehot, chunk).astype(BF16)

    @pl.when(n_ref[step] > MOE_HALF)
    def _():
        onehot = jnp.where(rk_row == row + MOE_HALF, 1.0, 0.0).astype(BF16)
        buf[slot, MOE_HALF:MOE_ROWS, :] = _dot(onehot, chunk).astype(BF16)

    def copy(k, s, rows):
        dst = x_hbm.at[pl.ds(pl.multiple_of(lo_ref[k], MOE_ALIGN), rows)]
        return pltpu.make_async_copy(buf.at[s, 0:rows], dst, sem.at[s])

    def for_size(k, fn):
        @pl.when(n_ref[k] > MOE_HALF)
        def _():
            fn(MOE_ROWS)

        @pl.when(n_ref[k] <= MOE_HALF)
        def _():
            fn(MOE_HALF)

    @pl.when(step > 0)
    def _():
        for_size(step - 1, lambda rows: copy(step - 1, 1 - slot, rows).wait())

    for_size(step, lambda rows: copy(step, slot, rows).start())

    @pl.when(step == n_steps - 1)
    def _():
        for_size(step, lambda rows: copy(step, slot, rows).wait())
        buf[0] = jnp.zeros((MOE_ROWS, D_MODEL), BF16)

        def fill(k, carry):
            dst = x_hbm.at[pl.ds(pl.multiple_of(k * MOE_ROWS, MOE_ROWS), MOE_ROWS)]
            cp = pltpu.make_async_copy(buf.at[0], dst, sem.at[0])
            cp.start()
            cp.wait()
            return carry

        lax.fori_loop(n_ref[n_steps], x_hbm.shape[0] // MOE_ROWS, fill, 0)


def _moe_gather(c_bf, rkt, lo, n, n_tiles):
    T = c_bf.shape[0]
    nch = T // MOE_CHUNK
    last = nch - 1
    grid_spec = pltpu.PrefetchScalarGridSpec(
        num_scalar_prefetch=2,
        grid=(N_EXPERTS, nch + MOE_PAD_SEGS),
        in_specs=[pl.BlockSpec((MOE_CHUNK, D_MODEL), lambda e, c, lo, n: (jnp.minimum(c, last), 0)),
                  pl.BlockSpec((N_EXPERTS, MOE_CHUNK), lambda e, c, lo, n: (0, jnp.minimum(c, last)))],
        out_specs=pl.BlockSpec(memory_space=pl.ANY),
        scratch_shapes=[pltpu.VMEM((2, MOE_ROWS, D_MODEL), BF16), pltpu.SemaphoreType.DMA((2,))],
    )
    return pl.pallas_call(
        functools.partial(_moe_gather_kernel, nch=nch),
        grid_spec=grid_spec,
        out_shape=jax.ShapeDtypeStruct((n_tiles * MOE_TILE, D_MODEL), BF16),
        compiler_params=_params("arbitrary", "arbitrary"),
        name="moe_gather",
    )(lo, n, c_bf, rkt)


def _moe_ffn_kernel(te_ref, valid_ref, x_ref, wg_ref, wu_ref, wd_ref, y_ref, acc_ref):
    j = pl.program_id(0)
    f = pl.program_id(1)
    last = pl.num_programs(1) - 1

    @pl.when(f == 0)
    def _():
        acc_ref[...] = jnp.zeros(acc_ref.shape, F32)

    @pl.when(valid_ref[j] == 1)
    def _():
        x = x_ref[...]
        mid = _silu(_dot(x, wg_ref[...])) * _dot(x, wu_ref[...])
        acc_ref[...] += _dot(mid.astype(BF16), wd_ref[...])

    @pl.when(f == last)
    def _():
        y_ref[...] = acc_ref[...].astype(BF16)


def _moe_ffn(x_sorted, tile_e, valid, wg, wu, wd, tf):
    n_tiles = x_sorted.shape[0] // MOE_TILE
    F = wg.shape[2]
    nf = F // tf
    sq = pl.Squeezed()
    fsel = lambda j, f, te, va: jnp.where(va[j] == 1, f, nf - 1)
    grid_spec = pltpu.PrefetchScalarGridSpec(
        num_scalar_prefetch=2,
        grid=(n_tiles, nf),
        in_specs=[pl.BlockSpec((MOE_TILE, D_MODEL), lambda j, f, te, va: (j, 0)),
                  pl.BlockSpec((sq, D_MODEL, tf), lambda j, f, te, va: (te[j], 0, fsel(j, f, te, va))),
                  pl.BlockSpec((sq, D_MODEL, tf), lambda j, f, te, va: (te[j], 0, fsel(j, f, te, va))),
                  pl.BlockSpec((sq, tf, D_MODEL), lambda j, f, te, va: (te[j], fsel(j, f, te, va), 0))],
        out_specs=pl.BlockSpec((MOE_TILE, D_MODEL), lambda j, f, te, va: (j, 0)),
        scratch_shapes=[pltpu.VMEM((MOE_TILE, D_MODEL), F32)],
    )
    return pl.pallas_call(
        _moe_ffn_kernel,
        grid_spec=grid_spec,
        out_shape=jax.ShapeDtypeStruct((n_tiles * MOE_TILE, D_MODEL), BF16),
        compiler_params=_params("parallel", "arbitrary"),
        name="moe_experts",
    )(tile_e, valid, x_sorted, wg, wu, wd)


def _ple_update(x, p, gain, w_gate, w_proj):
    y = x * lax.rsqrt(jnp.mean(x * x, axis=-1, keepdims=True) + EPS) * gain
    gate = _sigmoid(_dot(y.astype(BF16), w_gate))
    return x + _dot(p.astype(BF16), w_proj) * gate


def _moe_combine_kernel(lo_ref, n_ref, h_ref, rk_ref, comb_ref, p_ref, g_ref, wgate_ref, wproj_ref, y_hbm,
                        o_ref, ybuf, sem, *, nch):
    c = pl.program_id(0)
    seg = lambda e: e * (nch + MOE_PAD_SEGS) + c

    def copy(e, rows):
        src = y_hbm.at[pl.ds(pl.multiple_of(lo_ref[seg(e)], MOE_ALIGN), rows)]
        return pltpu.make_async_copy(src, ybuf.at[e, 0:rows], sem.at[e])

    def for_size(e, fn):
        @pl.when(n_ref[seg(e)] > MOE_HALF)
        def _():
            fn(MOE_ROWS)

        @pl.when(n_ref[seg(e)] <= MOE_HALF)
        def _():
            fn(MOE_HALF)

    for e in range(N_EXPERTS):
        for_size(e, lambda rows, e=e: copy(e, rows).start())
    o_ref[...] = h_ref[...]
    rk = rk_ref[...]
    comb = comb_ref[...]
    lane = lax.broadcasted_iota(jnp.int32, (MOE_CHUNK, MOE_HALF), 1)
    for e in range(N_EXPERTS):
        for_size(e, lambda rows, e=e: copy(e, rows).wait())
        rank_col = rk[:, e:e + 1]
        w_col = comb[:, e:e + 1]
        onehot = jnp.where(rank_col == lane, 1.0, 0.0).astype(BF16)
        o_ref[...] += w_col * _dot(onehot, ybuf[e, 0:MOE_HALF, :])

        @pl.when(n_ref[seg(e)] > MOE_HALF)
        def _():
            onehot = jnp.where(rank_col == lane + MOE_HALF, 1.0, 0.0).astype(BF16)
            o_ref[...] += w_col * _dot(onehot, ybuf[e, MOE_HALF:MOE_ROWS, :])

    o_ref[...] = _ple_update(o_ref[...], p_ref[...], g_ref[...], wgate_ref[...], wproj_ref[...])


def _moe_combine(h, rk_pad, comb, y_sorted, lo, n, p, layer, ple_gain, ple_wg, ple_wp):
    T = h.shape[0]
    nch = T // MOE_CHUNK
    const = lambda shape: pl.BlockSpec(shape, lambda c, lo, n: (0, 0))
    grid_spec = pltpu.PrefetchScalarGridSpec(
        num_scalar_prefetch=2,
        grid=(nch,),
        in_specs=[pl.BlockSpec((MOE_CHUNK, D_MODEL), lambda c, lo, n: (c, 0)),
                  pl.BlockSpec((MOE_CHUNK, LANE), lambda c, lo, n: (c, 0)),
                  pl.BlockSpec((MOE_CHUNK, LANE), lambda c, lo, n: (c, 0)),
                  pl.BlockSpec((pl.Squeezed(), MOE_CHUNK, PLE_DIM), lambda c, lo, n: (layer, c, 0)),
                  const((1, D_MODEL)), const((D_MODEL, D_MODEL)), const((PLE_DIM, D_MODEL)),
                  pl.BlockSpec(memory_space=pl.ANY)],
        out_specs=pl.BlockSpec((MOE_CHUNK, D_MODEL), lambda c, lo, n: (c, 0)),
        scratch_shapes=[pltpu.VMEM((N_EXPERTS, MOE_ROWS, D_MODEL), BF16), pltpu.SemaphoreType.DMA((N_EXPERTS,))],
    )
    return pl.pallas_call(
        functools.partial(_moe_combine_kernel, nch=nch),
        grid_spec=grid_spec,
        out_shape=jax.ShapeDtypeStruct((T, D_MODEL), F32),
        compiler_params=_params("arbitrary"),
        name="moe_combine",
    )(lo, n, h, rk_pad, comb, p, ple_gain.reshape(1, D_MODEL), ple_wg, ple_wp, y_sorted)


def _moe(h, c_bf, comb, rk, cnt, wg, wu, wd, tf, ple):
    T = h.shape[0]
    lo, n, tile_e, valid = _moe_plan(cnt, T)
    x_sorted = _moe_gather(c_bf, rk[:, :N_EXPERTS].T, lo, n, _moe_tiles(T))
    y_sorted = _moe_ffn(x_sorted, tile_e, valid, wg, wu, wd, tf)
    return _moe_combine(h, rk, comb, y_sorted, lo, n, *ple)


def _ple_kernel(h_ref, p_ref, g_ref, wg_ref, wp_ref, o_ref):
    o_ref[...] = _ple_update(h_ref[...], p_ref[...], g_ref[...], wg_ref[...], wp_ref[...])


def _ple(h, p, layer, gain, wg, wp, tm):
    T = h.shape[0]
    return pl.pallas_call(
        _ple_kernel,
        grid=(T // tm,),
        in_specs=[pl.BlockSpec((tm, D_MODEL), lambda i: (i, 0)),
                  pl.BlockSpec((pl.Squeezed(), tm, PLE_DIM), lambda i: (layer, i, 0)),
                  pl.BlockSpec((1, D_MODEL), lambda i: (0, 0)),
                  pl.BlockSpec((D_MODEL, D_MODEL), lambda i: (0, 0)),
                  pl.BlockSpec((PLE_DIM, D_MODEL), lambda i: (0, 0))],
        out_specs=pl.BlockSpec((tm, D_MODEL), lambda i: (i, 0)),
        out_shape=jax.ShapeDtypeStruct((T, D_MODEL), F32),
        compiler_params=_params("parallel"),
        name="ple_gate",
    )(h, p, gain.reshape(1, D_MODEL), wg, wp)


def _tiles(T, S):
    pick = lambda n, pref: max(t for t in pref if n % t == 0)
    return dict(
        proj_m=pick(T, (512, 256, 128)), proj_n=NC,
        prep_m=pick(T, (512, 256, 128)),
        attn_q=256, attn_k=pick(S, (512, 256)),
        nsa_q=pick(S, (512, 256)), nsa_k=pick(S, (512, 256)),
        lin_m=pick(S, (512, 256, 128, 64)),
        row_m=pick(T, (512, 256, 128)),
        ffn_m=pick(T, (1024, 512, 256, 128)), ffn_f=512, moe_f=D_FF // 4,
    )


def kernel(x, p, norm_attn, w_in, w_out, nsa_cmp_pos, nsa_cmp_w1, nsa_cmp_w2, nsa_qk_gain, diff_qk_gain, diff_lambda, diff_norm, gla_w_gate2, gla_b_gate, gla_norm, hgrn_lb_logits, hgrn_norm, norm_ffn, ffn_w_gate, ffn_w_up, ffn_w_down, moe_router, moe_w_gate, moe_w_up, moe_w_down, ple_norm, ple_w_gate, ple_w_proj):
    B, S, _ = x.shape
    depth = w_in.shape[0]
    T = B * S
    t = _tiles(T, S)
    cols =jnp.asarray(np.maximum(_COLS, 0), jnp.int32)
    col_mask = jnp.asarray(_COLS >= 0)
    ones_row = jnp.ones((GROUP,), F32)

    h = x.reshape(T, D_MODEL)
    for i in range(depth):
        w_in_r = jnp.where(col_mask[None, :], jnp.take(w_in[i], cols, axis=1), 0.0).astype(BF16)
        gains = jnp.stack([
            jnp.tile(nsa_qk_gain[i, 0], HEADS) * (HD ** -0.5 * LOG2E),
            jnp.tile(diff_qk_gain[i, 0], 2 * HEADS) * (DIFF_QK ** -0.5 * LOG2E),
            jnp.tile(diff_qk_gain[i, 1], 2 * HEADS),
            jnp.tile(nsa_qk_gain[i, 2], HEADS),
            jnp.tile(nsa_qk_gain[i, 3], HEADS),
            ones_row, ones_row, ones_row])
        pos = jnp.transpose(nsa_cmp_pos[i], (1, 0, 2)).reshape(2, NSA_CMP_STRIDE, LANE)
        w1r = nsa_cmp_w1[i].reshape(2, NSA_CMP_LEN, HD, NSA_CMP_HIDDEN)
        zeros = jnp.zeros_like(w1r[0])
        w1_bd = jnp.concatenate([jnp.concatenate([w1r[0], zeros], axis=-1),
                                 jnp.concatenate([zeros, w1r[1]], axis=-1)], axis=1)
        w1_bd = w1_bd.reshape(2, NSA_CMP_STRIDE, LANE, 2 * NSA_CMP_HIDDEN).astype(BF16)
        w2p = jnp.pad(nsa_cmp_w2[i], ((0, 0), (0, 0), (0, LANE - HD))).astype(BF16)
        kc_gain = jnp.pad(nsa_qk_gain[i, 1], (0, LANE - HD)).reshape(1, LANE)
        w2pad = jnp.zeros((LANE, LANE), F32).at[:GLA_RANK].set(gla_w_gate2[i])
        diff_gain_col = jnp.broadcast_to(diff_norm[i][:, None], (HD, t["attn_q"]))

        u = _norm_matmul(h, norm_attn[i], w_in_r, t["proj_m"], t["proj_n"])
        nqt, dqt, dk, dvt, kvs, vst, kvw, vwt = _prep(u, gains, t["prep_m"])
        kc, vct = _compress(u, pos, w1_bd, w2p, kc_gain, B, S)
        o_a = _nsa_attention(u, nqt, kc, vct, kvs, vst, kvw, vwt, B, S, t["nsa_q"], t["nsa_k"], t["nsa_q"])
        o_b = _diff_attention(dqt, dk, dvt, diff_lambda[i], diff_gain_col, B, S, t["attn_q"], i)
        o_c, o_d = _linear_mixers(u, w2pad, gla_b_gate[i].reshape(1, LANE),
                                  jnp.tile(gla_norm[i], HEADS).reshape(1, GROUP), hgrn_lb_logits,
                                  jnp.tile(hgrn_norm[i], HEADS).reshape(1, GROUP), B, S, t["lin_m"], i)
        parts = (o_a, o_b, o_c, o_d)
        w_out_bf = w_out[i].astype(BF16)
        ple = (p.reshape(depth, T, PLE_DIM), i, ple_norm[i], ple_w_gate[i].astype(BF16), ple_w_proj[i].astype(BF16))
        j = i // 2
        if i % 2 == 0:
            h = _ffn(h, parts, w_out_bf, norm_ffn[i], ffn_w_gate[j].astype(BF16), ffn_w_up[j].astype(BF16),
                     ffn_w_down[j].astype(BF16), t["ffn_m"], t["ffn_f"])
            h = _ple(h, *ple, t["row_m"])
        else:
            router_pad = jnp.zeros((D_MODEL, LANE), F32).at[:, :N_EXPERTS].set(moe_router[j])
            h, c_bf, comb, rk, cnt = _router(h, parts, w_out_bf, norm_ffn[i], router_pad, MOE_CHUNK)
            h = _moe(h, c_bf, comb, rk, cnt, moe_w_gate[j].astype(BF16), moe_w_up[j].astype(BF16),
                     moe_w_down[j].astype(BF16), t["moe_f"], ple)
    return h.reshape(B, S, D_MODEL)
```

```python
import functools
import math

import numpy as np
import jax
import jax.numpy as jnp
from jax import lax
from jax.experimental import pallas as pl
from jax.experimental.pallas import tpu as pltpu

F32 = jnp.float32
BF16 = jnp.bfloat16

D_MODEL = 1024
HEADS = 4
HD = 64
GROUP = HEADS * HD
NSA_CMP_LEN = 32
NSA_CMP_STRIDE = 16
NSA_CMP_HIDDEN = 4 * HD
NSA_SEL_LEN = 64
NSA_TOPK = 16
NSA_WINDOW = 512
NSA_SUBTILES = 2
DIFF_QK = HD // 2
GLA_DK = HD // 2
GLA_RANK = 16
GLA_TAU = 16.0
CHUNK = 64
D_FF = 7 * D_MODEL // 2
N_EXPERTS = 8
PLE_DIM = 256
EPS = 1e-6
NEG = -1e30
BIG = 1e30
LOWEST = -3.0e38
MASK_BIG = 2.0 ** 100
LOG2E = 1.4426950408889634

VMEM_LIMIT = 52 * 1024 * 1024
LANE = 128

_SRC = dict(nsa_q=0, k_cmp=256, v_cmp=320, k_slc=384, v_slc=448, k_win=512, v_win=576, nsa_g=640,
            d_q=652, d_k=908, d_v=1164, g_q=1420, g_k=1548, g_v=1676, g_lr=1932, g_og=1948,
            r_q=2204, r_f=2460, r_i=2716, r_og=2972)
IN_COLS = 3228

_B256 = dict(nsa_q=0, d_q=1, d_k=2, d_v=3, g_v=4, g_og=5, r_q=6, r_f=7, r_i=8, r_og=9)
_B128 = dict(g_q=20, g_k=21, g_lr=22, kvcmp=23, kvslc=24, kvwin=25, nsa_g=26)
NC = 27 * 128


def _column_map():
    cols = -np.ones((NC,), np.int64)

    def put(dst, src, width):
        cols[dst:dst + width] = np.arange(src, src + width)

    for name in _B256:
        put(_B256[name] * 256, _SRC[name], 256)
    put(_B128["g_q"] * 128, _SRC["g_q"], 128)
    put(_B128["g_k"] * 128, _SRC["g_k"], 128)
    put(_B128["g_lr"] * 128, _SRC["g_lr"], GLA_RANK)
    put(_B128["kvcmp"] * 128, _SRC["k_cmp"], 128)
    put(_B128["kvslc"] * 128, _SRC["k_slc"], 128)
    put(_B128["kvwin"] * 128, _SRC["k_win"], 128)
    put(_B128["nsa_g"] * 128, _SRC["nsa_g"], 3 * HEADS)
    return cols


_COLS = _column_map()


def _dot(a, b):
    return jnp.dot(a, b, preferred_element_type=F32)


def _dot_nt(a, b):
    return lax.dot_general(a, b, (((1,), (1,)), ((), ())), preferred_element_type=F32)


def _dot_tn(a, b):
    return lax.dot_general(a, b, (((0,), (0,)), ((), ())), preferred_element_type=F32)


def _split2(x):
    hi = x.astype(BF16)
    lo = (x - hi.astype(F32)).astype(BF16)
    return hi, lo


def _split3(x):
    hi = x.astype(BF16)
    r = x - hi.astype(F32)
    mid = r.astype(BF16)
    lo = (r - mid.astype(F32)).astype(BF16)
    return hi, mid, lo


def _group_mean(x, ones_bf, group):
    hi, lo = _split2(x)
    return (_dot(hi, ones_bf) + _dot(lo, ones_bf)) * (1.0 / group)


def _group_rms(x, ones_bf, group):
    return x * lax.rsqrt(_group_mean(x * x, ones_bf, group) + EPS)


def _sigmoid(x):
    return 1.0 / (1.0 + jnp.exp(-x))


def _silu(x):
    return x * _sigmoid(x)


def _params(*sem):
    return pltpu.CompilerParams(dimension_semantics=sem, vmem_limit_bytes=VMEM_LIMIT)


def _block_ones(n, group):
    i = np.arange(n)
    return jnp.asarray((i[:, None] // group == i[None, :] // group).astype(np.float32), BF16)


def _norm_matmul_kernel(x_ref, g_ref, w_ref, o_ref, xn_ref):
    @pl.when(pl.program_id(1) == 0)
    def _():
        x = x_ref[...]
        y = x * lax.rsqrt(jnp.mean(x * x, axis=-1, keepdims=True) + EPS) * g_ref[...]
        xn_ref[...] = y.astype(BF16)

    o_ref[...] = _dot(xn_ref[...], w_ref[...])


def _norm_matmul(x, gain, w_bf, tm, tn):
    T, K = x.shape
    N = w_bf.shape[1]
    return pl.pallas_call(
        _norm_matmul_kernel,
        grid=(T // tm, N // tn),
        in_specs=[pl.BlockSpec((tm, K), lambda i, j: (i, 0)),
                  pl.BlockSpec((1, K), lambda i, j: (0, 0)),
                  pl.BlockSpec((K, tn), lambda i, j: (0, j))],
        out_specs=pl.BlockSpec((tm, tn), lambda i, j: (i, j)),
        out_shape=jax.ShapeDtypeStruct((T, N), F32),
        scratch_shapes=[pltpu.VMEM((tm, K), BF16)],
        compiler_params=_params("parallel", "arbitrary"),
        name="in_proj",
    )(x, gain.reshape(1, K), w_bf)


def _prep_kernel(nq_ref, dq_ref, dk_ref, dv_ref, kvs_ref, kvw_ref, gains_ref, ones64_ref, ones32_ref,
                 ones64h_ref, o_nq, o_dq, o_dk, o_dv, o_kvs, o_vs, o_kvw, o_vw):
    ones64 = ones64_ref[...]
    ones32 = ones32_ref[...]
    ones64h = ones64h_ref[...]
    o_nq[...] = (_group_rms(nq_ref[...], ones64, HD) * gains_ref[0:1, :]).T.astype(BF16)
    o_dq[...] = (_group_rms(dq_ref[...], ones32, DIFF_QK) * gains_ref[1:2, :]).T.astype(BF16)
    o_dk[...] = (_group_rms(dk_ref[...], ones32, DIFF_QK) * gains_ref[2:3, :]).astype(BF16)
    o_dv[...] = dv_ref[...].T.astype(BF16)
    lane = lax.broadcasted_iota(jnp.int32, kvs_ref.shape, 1)
    for kv_ref, gain, o_kv, o_v in ((kvs_ref, gains_ref[3:4, 0:LANE], o_kvs, o_vs),
                                    (kvw_ref, gains_ref[4:5, 0:LANE], o_kvw, o_vw)):
        x = kv_ref[...]
        o_kv[...] = jnp.where(lane < HD, _group_rms(x, ones64h, HD) * gain, x).astype(BF16)
        o_v[...] = x.T[HD:2 * HD, :].astype(BF16)


def _prep(u, gains, tm):
    T = u.shape[0]
    c256 = lambda name: pl.BlockSpec((tm, GROUP), lambda i, c=_B256[name]: (i, c))
    c128 = lambda name: pl.BlockSpec((tm, LANE), lambda i, c=_B128[name]: (i, c))
    const = lambda shape: pl.BlockSpec(shape, lambda i: (0, 0))
    rows = lambda w: pl.BlockSpec((tm, w), lambda i: (i, 0))
    colsT = lambda h: pl.BlockSpec((h, tm), lambda i: (0, i))
    sds = jax.ShapeDtypeStruct
    return pl.pallas_call(
        _prep_kernel,
        grid=(T // tm,),
        in_specs=[c256("nsa_q"), c256("d_q"), c256("d_k"), c256("d_v"), c128("kvslc"), c128("kvwin"),
                  const((8, GROUP)), const((GROUP, GROUP)), const((GROUP, GROUP)), const((LANE, LANE))],
        out_specs=[colsT(GROUP), colsT(GROUP), rows(GROUP), colsT(GROUP),
                   rows(LANE), colsT(HD), rows(LANE), colsT(HD)],
        out_shape=[sds((GROUP, T), BF16), sds((GROUP, T), BF16), sds((T, GROUP), BF16), sds((GROUP, T), BF16),
                   sds((T, LANE), BF16), sds((HD, T), BF16), sds((T, LANE), BF16), sds((HD, T), BF16)],
        compiler_params=_params("parallel"),
        name="attn_prep",
    )(u, u, u, u, u, u, gains, _block_ones(GROUP, HD), _block_ones(GROUP, DIFF_QK), _block_ones(LANE, HD))


def _compress_kernel(kv_ref, pos_ref, w1_ref, w2_ref, gain_ref, ones64h_ref, kc_ref, vc_ref):
    ncp = kc_ref.shape[0]
    hid = NSA_CMP_HIDDEN
    top = jnp.zeros((ncp, 2 * hid), F32)
    bot = jnp.zeros((ncp, 2 * hid), F32)
    for l in range(NSA_CMP_STRIDE):
        x = kv_ref[pl.ds(l, ncp, stride=NSA_CMP_STRIDE), :]
        top = top + _dot((x + pos_ref[0, l:l + 1, :]).astype(BF16), w1_ref[0, l])
        bot = bot + _dot((x + pos_ref[1, l:l + 1, :]).astype(BF16), w1_ref[1, l])
    hidden = top + pltpu.roll(bot, ncp - 1, 0)
    act = _silu(hidden).astype(BF16)
    kc = _dot(act[:, 0:hid], w2_ref[0])
    kc_ref[...] = _group_rms(kc, ones64h_ref[...], HD) * gain_ref[...]
    vc_ref[...] = _dot(act[:, hid:2 * hid], w2_ref[1]).T[0:HD, :].astype(BF16)


def _compress(u, pos, w1_bd, w2p_bf, gain_row, B, S):
    ncp = S // NSA_CMP_STRIDE
    sq = pl.Squeezed()
    full = lambda shape: pl.BlockSpec(shape, lambda b: (0,) * len(shape))
    return pl.pallas_call(
        _compress_kernel,
        grid=(B,),
        in_specs=[pl.BlockSpec((S, LANE), lambda b, c=_B128["kvcmp"]: (b, c)),
                  full((2, NSA_CMP_STRIDE, LANE)), full((2, NSA_CMP_STRIDE, LANE, 2 * NSA_CMP_HIDDEN)),
                  full((2, NSA_CMP_HIDDEN, LANE)), full((1, LANE)), full((LANE, LANE))],
        out_specs=[pl.BlockSpec((sq, ncp, LANE), lambda b: (b, 0, 0)),
                   pl.BlockSpec((sq, HD, ncp), lambda b: (b, 0, 0))],
        out_shape=[jax.ShapeDtypeStruct((B, ncp, LANE), F32), jax.ShapeDtypeStruct((B, HD, ncp), BF16)],
        compiler_params=_params("parallel"),
        name="nsa_compress",
    )(u, pos, w1_bd, w2p_bf, gain_row, _block_ones(LANE, HD))


def _softmax_step_t(s, v_t, m_old, l_old, acc_ref):
    m_new = jnp.maximum(m_old, jnp.max(s, axis=0, keepdims=True))
    alpha = jnp.exp2(m_old - m_new)
    p = jnp.exp2(s - m_new)
    acc_ref[...] = alpha * acc_ref[...] + _dot(v_t, p.astype(BF16))
    return m_new, alpha * l_old + jnp.sum(p, axis=0, keepdims=True)


def _pipelined_sweep(lo, hi, qk, process, process_last, stats, sa_ref, sb_ref):
    def pair(jj, st):
        t = lo + 2 * jj
        qk(t + 1, sb_ref)
        st = process(t, sa_ref, st)
        qk(t + 2, sa_ref)
        return process(t + 1, sb_ref, st)

    def two_left(st):
        qk(hi - 1, sb_ref)
        st = process(hi - 2, sa_ref, st)
        return process_last(hi - 1, sb_ref, st)

    n = hi - lo
    qk(lo, sa_ref)
    stats = lax.fori_loop(0, (n - 1) // 2, pair, stats)
    return lax.cond(n % 2 == 0, two_left, lambda st: process_last(hi - 1, sa_ref, st), stats)


def _nsa_kernel(qt_ref, g_ref, kc_ref, vct_ref, kvs_ref, vst_ref, kvw_ref, vwt_ref, ovl_ref, exp_ref,
                o_ref, qs_ref, sa_ref, sb_ref, *acc_refs, tq, tk, tw, ksel):
    i = pl.program_id(1)
    t0 = i * tq
    ncp = kc_ref.shape[0]
    nsel = ovl_ref.shape[0]
    acc_s, acc_w = acc_refs[:HEADS], acc_refs[HEADS:]

    qt = qt_ref[...]
    qs_ref[HD:LANE, :] = jnp.zeros((LANE - HD, HEADS * tq), BF16)
    for h in range(HEADS):
        qs_ref[0:HD, h * tq:(h + 1) * tq] = qt[h * HD:(h + 1) * HD, :]
        acc_s[h][...] = jnp.zeros((HD, tq), F32)
        acc_w[h][...] = jnp.zeros((HD, tq), F32)

    kc_hi, kc_lo = _split2(kc_ref[...])
    n_idx = lax.broadcasted_iota(jnp.int32, (ncp, tq), 0)
    t_lane = t0 + lax.broadcasted_iota(jnp.int32, (ncp, tq), 1)
    ok = n_idx * NSA_CMP_STRIDE + (NSA_CMP_LEN - 1) <= t_lane
    cmp_scores = _dot(kc_hi, qs_ref[...]) + _dot(kc_lo, qs_ref[...])
    o_c = []
    psum = jnp.zeros((ncp, tq), F32)
    for h in range(HEADS):
        s = jnp.where(ok, cmp_scores[:, h * tq:(h + 1) * tq], NEG)
        e = jnp.exp2(s - jnp.max(s, axis=0, keepdims=True))
        p = jnp.where(ok, e / jnp.sum(e, axis=0, keepdims=True), 0.0)
        o_c.append(_dot(vct_ref[...], p.astype(BF16)))
        psum = psum + p

    p_hi, p_lo = _split2(psum)
    imp = _dot(ovl_ref[...], p_hi) + _dot(ovl_ref[...], p_lo)
    blk = lax.broadcasted_iota(jnp.int32, (nsel, tq), 0)
    t_col = t0 + lax.broadcasted_iota(jnp.int32, (nsel, tq), 1)
    cur = t_col // NSA_SEL_LEN
    forced = (blk == 0) | (blk == cur) | (blk == cur - 1)
    vals = jnp.where(forced, BIG, jnp.where(blk * NSA_SEL_LEN <= t_col, imp, NEG))
    blk_f = blk.astype(F32)
    sel = jnp.zeros((nsel, tq), F32)
    for _ in range(ksel):
        mx = jnp.max(vals, axis=0, keepdims=True)
        first = jnp.min(jnp.where(vals == mx, blk_f, float(nsel)), axis=0, keepdims=True)
        pick = blk_f == first
        sel = jnp.where(pick, 1.0, sel)
        vals = jnp.where(pick, LOWEST, vals)
    if nsel < LANE:
        sel = jnp.concatenate([sel, jnp.zeros((LANE - nsel, tq), F32)], axis=0)
    sel_bf = sel.astype(BF16)

    stats0 = tuple((jnp.full((1, tq), NEG, F32), jnp.zeros((1, tq), F32)) for _ in range(HEADS))

    def sweep(lo, hi, k_ref, vt_ref, accs, width, bias_fn, last_bias_fn):
        def qk(kt, dst):
            k = k_ref[pl.ds(pl.multiple_of(kt * width, width), width), :]
            for h in range(HEADS):
                dst[h, 0:width, :] = _dot(k, qs_ref[:, h * tq:(h + 1) * tq])

        def process(kt, src, stats, fn):
            k0 = pl.multiple_of(kt * width, width)
            kpos = k0 + lax.broadcasted_iota(jnp.int32, (width, tq), 0)
            tpos = t0 + lax.broadcasted_iota(jnp.int32, (width, tq), 1)
            bias = fn(k0, kpos, tpos)
            vt = vt_ref[:, pl.ds(k0, width)]
            return tuple(_softmax_step_t(src[h, 0:width, :] + bias, vt, stats[h][0], stats[h][1], accs[h])
                         for h in range(HEADS))

        return _pipelined_sweep(lo, hi, qk, functools.partial(process, fn=bias_fn),
                                functools.partial(process, fn=last_bias_fn), stats0, sa_ref, sb_ref)

    def sel_bias(k0, kpos, tpos):
        chosen = _dot(exp_ref[pl.ds(k0, kpos.shape[0]), :], sel_bf)
        return chosen - MASK_BIG

    def sel_bias_diag(k0, kpos, tpos):
        chosen = _dot(exp_ref[pl.ds(k0, kpos.shape[0]), :], sel_bf)
        return jnp.where(kpos <= tpos, chosen - MASK_BIG, NEG)

    st_s = sweep(0, i + 1, kvs_ref, vst_ref, acc_s, tk, sel_bias, sel_bias_diag)

    def win_bias(k0, kpos, tpos):
        return jnp.where((kpos <= tpos) & (kpos > tpos - NSA_WINDOW), 0.0, NEG)

    st_w = sweep(jnp.maximum(t0 - NSA_WINDOW, 0) // tw, (t0 + tq - 1) // tw + 1,
                 kvw_ref, vwt_ref, acc_w, tw, win_bias, win_bias)

    gates = _sigmoid(g_ref[...].T)
    outs = []
    for h in range(HEADS):
        outs.append(gates[3 * h:3 * h + 1, :] * o_c[h]
                    + gates[3 * h + 1:3 * h + 2, :] * (acc_s[h][...] / st_s[h][1])
                    + gates[3 * h + 2:3 * h + 3, :] * (acc_w[h][...] / st_w[h][1]))
    o_ref[...] = jnp.concatenate(outs, axis=0).T


def _nsa_attention(u, nqt, kc, vct, kvs, vst, kvw, vwt, B, S, tq, tk, tw):
    nq = S // tq
    ncp = S // NSA_CMP_STRIDE
    nsel = S // NSA_SEL_LEN
    ksel = min(NSA_TOPK, nsel)
    n_cmp = (S - NSA_CMP_LEN) // NSA_CMP_STRIDE + 1
    cmp_start = np.arange(ncp) * NSA_CMP_STRIDE
    sel_start = np.arange(nsel) * NSA_SEL_LEN
    overlap = ((cmp_start[:, None] <= sel_start[None, :] + NSA_SEL_LEN - 1)
               & (cmp_start[:, None] + NSA_CMP_LEN - 1 >= sel_start[None, :])
               & (np.arange(ncp)[:, None] < n_cmp))
    ovl_t = jnp.asarray(overlap.T.astype(np.float32), BF16)
    assert tq == tk, "the selected sweep treats every key tile before the query tile's own as fully visible"
    expand = jnp.where(jnp.arange(S, dtype=jnp.int32)[:, None] // NSA_SEL_LEN
                       == jnp.arange(LANE, dtype=jnp.int32)[None, :], MASK_BIG, 0.0).astype(BF16)
    sq = pl.Squeezed()
    seq_rows = pl.BlockSpec((S, LANE), lambda b, i: (b, 0))
    seq_cols = pl.BlockSpec((HD, S), lambda b, i: (0, b))
    kern = functools.partial(_nsa_kernel, tq=tq, tk=tk, tw=tw, ksel=ksel)
    return pl.pallas_call(
        kern,
        grid=(B, nq),
        in_specs=[pl.BlockSpec((GROUP, tq), lambda b, i: (0, b * nq + i)),
                  pl.BlockSpec((tq, LANE), lambda b, i, c=_B128["nsa_g"]: (b * nq + i, c)),
                  pl.BlockSpec((sq, ncp, LANE), lambda b, i: (b, 0, 0)),
                  pl.BlockSpec((sq, HD, ncp), lambda b, i: (b, 0, 0)),
                  seq_rows, seq_cols, seq_rows, seq_cols,
                  pl.BlockSpec((nsel, ncp), lambda b, i: (0, 0)),
                  pl.BlockSpec((S, LANE), lambda b, i: (0, 0))],
        out_specs=pl.BlockSpec((tq, GROUP), lambda b, i: (b * nq + i, 0)),
        out_shape=jax.ShapeDtypeStruct((B * S, GROUP), F32),
        scratch_shapes=([pltpu.VMEM((LANE, HEADS * tq), BF16)]
                        + [pltpu.VMEM((HEADS, max(tk, tw), tq), F32)] * 2
                        + [pltpu.VMEM((HD, tq), F32)] * (2 * HEADS)),
        compiler_params=_params("parallel", "arbitrary"),
        name="nsa_attention",
    )(nqt, u, kc, vct, kvs, vst, kvw, vwt, ovl_t, expand)


def _diff_kernel(lam_ref, qt_ref, k_ref, vt_ref, gain_ref, o_ref, qs_ref, sa_ref, sb_ref, *acc_refs,
                 tq, lam_init):
    i = pl.program_id(1)
    t0 = i * tq
    lanes = 2 * tq

    qt = qt_ref[...]
    row = lax.broadcasted_iota(jnp.int32, (GROUP, tq), 0)
    zero = jnp.zeros_like(qt)
    for h in range(HEADS):
        qs_ref[h] = jnp.concatenate([jnp.where(row // DIFF_QK == 2 * h, qt, zero),
                                     jnp.where(row // DIFF_QK == 2 * h + 1, qt, zero)], axis=1)
        acc_refs[h][...] = jnp.zeros((HD, lanes), F32)

    def qk(kt, dst):
        k = k_ref[pl.ds(pl.multiple_of(kt * tq, tq), tq), :]
        for h in range(HEADS):
            dst[h] = _dot(k, qs_ref[h])

    def process(kt, src, masked, stats):
        k0 = pl.multiple_of(kt * tq, tq)
        out = []
        for h in range(HEADS):
            s = src[h]
            if masked:
                kpos = k0 + lax.broadcasted_iota(jnp.int32, (tq, lanes), 0)
                tpos = t0 + lax.broadcasted_iota(jnp.int32, (tq, lanes), 1) % tq
                s = jnp.where(kpos <= tpos, s, NEG)
            out.append(_softmax_step_t(s, vt_ref[h * HD:(h + 1) * HD, pl.ds(k0, tq)],
                                       stats[h][0], stats[h][1], acc_refs[h]))
        return tuple(out)

    stats0 = tuple((jnp.full((1, lanes), NEG, F32), jnp.zeros((1, lanes), F32)) for _ in range(HEADS))
    stats = _pipelined_sweep(0, i + 1, qk, lambda kt, src, st: process(kt, src, False, st),
                             lambda kt, src, st: process(kt, src, True, st), stats0, sa_ref, sb_ref)

    lam = lam_ref[...]
    lam_full = (jnp.exp(jnp.sum(lam[0:1] * lam[1:2], axis=1, keepdims=True))
                - jnp.exp(jnp.sum(lam[2:3] * lam[3:4], axis=1, keepdims=True)) + lam_init)
    outs = []
    for h in range(HEADS):
        r = acc_refs[h][...] / stats[h][1]
        d = r[:, 0:tq] - lam_full * r[:, tq:lanes]
        d = d * lax.rsqrt(jnp.mean(d * d, axis=0, keepdims=True) + EPS)
        outs.append(d * gain_ref[...] * (1.0 - lam_init))
    o_ref[...] = jnp.concatenate(outs, axis=0).T


def _diff_attention(dqt, dk, dvt, lam, gain_col, B, S, tq, layer_idx):
    nq = S // tq
    lam_init = 0.8 - 0.6 * math.exp(-0.3 * layer_idx)
    kern = functools.partial(_diff_kernel, tq=tq, lam_init=lam_init)
    scores = pltpu.VMEM((HEADS, tq, 2 * tq), F32)
    return pl.pallas_call(
        kern,
        grid=(B, nq),
        in_specs=[pl.BlockSpec((4, DIFF_QK), lambda b, i: (0, 0)),
                  pl.BlockSpec((GROUP, tq), lambda b, i: (0, b * nq + i)),
                  pl.BlockSpec((S, GROUP), lambda b, i: (b, 0)),
                  pl.BlockSpec((GROUP, S), lambda b, i: (0, b)),
                  pl.BlockSpec((HD, tq), lambda b, i: (0, 0))],
        out_specs=pl.BlockSpec((tq, GROUP), lambda b, i: (b * nq + i, 0)),
        out_shape=jax.ShapeDtypeStruct((B * S, GROUP), F32),
        scratch_shapes=([pltpu.VMEM((HEADS, GROUP, 2 * tq), BF16), scores, scores]
                        + [pltpu.VMEM((HD, 2 * tq), F32)] * HEADS),
        compiler_params=_params("parallel", "arbitrary"),
        name="diff_attention",
    )(lam, dqt, dk, dvt, gain_col)


_LEVELS = (32, 16, 8, 4, 2, 1)
LIN_GROUP = 8


def _stack_heads(x, lane, group, count):
    zero = jnp.zeros_like(x)
    return jnp.concatenate([jnp.where(lane // group == g, x, zero) for g in range(count)], axis=0)


def _unstack_heads(x4, lane, rows):
    out = jnp.where(lane // HD == 0, x4[0:rows], 0.0)
    for h in range(1, HEADS):
        out = out + jnp.where(lane // HD == h, x4[h * rows:(h + 1) * rows], 0.0)
    return out


def _linear_consts():
    r = np.arange(CHUNK)[:, None]
    t = np.arange(CHUNK)[None, :]
    tri = (t <= r).astype(np.float32)
    masks = [(r // (2 * s) == t // (2 * s)) for s in _LEVELS] + [r == t]
    mall = np.stack([np.tile(m.astype(np.float32), (1, HEADS)) for m in masks])
    return jnp.asarray(tri, BF16), jnp.asarray(mall, F32)


def _level_exponent(s, lg, b, row):
    if s == 1:
        return jnp.where((row & 1) != 0, 0.0, pltpu.roll(lg, CHUNK - 1, 0))
    if s == 2:
        nxt1 = pltpu.roll(lg, CHUNK - 1, 0)
        nxt2 = pltpu.roll(lg, CHUNK - 2, 0)
        r4 = row & 3
        return jnp.where(r4 == 0, nxt1 + nxt2, jnp.where(r4 == 1, nxt1, jnp.where(r4 == 2, 0.0, lg)))
    mids = [jnp.broadcast_to(b[m:m + 1, :], (2 * s, b.shape[1])) for m in range(s, CHUNK, 2 * s)]
    d = b - (jnp.concatenate(mids, axis=0) if len(mids) > 1 else mids[0])
    return jnp.where((row & s) != 0, d, -d)


def _linear_chunks(items, tri, mall_ref):
    bs = []
    for q, k, v, lg, state_ref, dk in items:
        hi, mid, lo = _split3(lg)
        bs.append(_dot(tri, hi) + _dot(tri, mid) + _dot(tri, lo))
    ats = []
    for (q, k, v, lg, state_ref, dk), b in zip(items, bs):
        dkh = HEADS * dk
        row = lax.broadcasted_iota(jnp.int32, (CHUNK, dkh), 0)
        lane_k = lax.broadcasted_iota(jnp.int32, (CHUNK, dkh), 1)
        a_t = mall_ref[len(_LEVELS)] * _dot_nt(k.astype(BF16), _stack_heads(q.astype(BF16), lane_k, dk, HEADS))
        for li, s in enumerate(_LEVELS):
            e = jnp.exp(_level_exponent(s, lg, b, row))
            upper = (row & s) != 0
            qt = jnp.where(upper, q * e, 0.0).astype(BF16)
            kt = jnp.where(upper, 0.0, k * e).astype(BF16)
            a_t = a_t + mall_ref[li] * _dot_nt(kt, _stack_heads(qt, lane_k, dk, HEADS))
        ats.append(a_t)
    lane_v = lax.broadcasted_iota(jnp.int32, (CHUNK, GROUP), 1)
    partial = []
    for (q, k, v, lg, state_ref, dk), b, a_t in zip(items, bs, ats):
        v_bf = v.astype(BF16)
        o_intra = _unstack_heads(_dot_tn(a_t.astype(BF16), v_bf), lane_v, CHUNK)
        e_b = jnp.exp(b)
        e_u = jnp.exp(b[CHUNK - 1:CHUNK, :] - b)
        kv = _dot_tn(v_bf, (k * e_u).astype(BF16))
        partial.append((o_intra, e_b, kv))
    outs = []
    for (q, k, v, lg, state_ref, dk), (o_intra, e_b, kv) in zip(items, partial):
        dkh = HEADS * dk
        st = state_ref[...]
        o_inter = _dot_nt((q * e_b).astype(BF16), st.astype(BF16))
        srow = lax.broadcasted_iota(jnp.int32, (GROUP, dkh), 0)
        scol = lax.broadcasted_iota(jnp.int32, (GROUP, dkh), 1)
        state_ref[...] = st * e_b[CHUNK - 1:CHUNK, :] + jnp.where(srow // HD == scol // dk, kv, 0.0)
        outs.append(o_inter + o_intra)
    return outs


def _linear_kernel(gq_ref, gk_ref, gv_ref, lr_ref, gog_ref, w2_ref, b_ref, ggain_ref,
                   rq_ref, rf_ref, ri_ref, rog_ref, lbl_ref, rgain_ref, tri_ref, mall_ref, ones64_ref,
                   og_ref, or_ref, gstate_ref, rstate_ref, *, tm, layer_idx):
    @pl.when(pl.program_id(1) == 0)
    def _():
        gstate_ref[...] = jnp.zeros_like(gstate_ref)
        rstate_ref[...] = jnp.zeros_like(rstate_ref)

    tri = tri_ref[...]
    ones64 = ones64_ref[...]
    w_hi, w_lo = _split2(w2_ref[...])
    logits = lbl_ref[...]
    ez = jnp.exp(logits - jnp.max(logits, axis=0, keepdims=True))
    probs = ez / jnp.sum(ez, axis=0, keepdims=True)
    lb = jnp.zeros((1, GROUP), F32)
    for j in range(1, layer_idx + 1):
        lb = lb + probs[j:j + 1]

    group = math.gcd(LIN_GROUP, tm // CHUNK)

    def body(c, carry):
        items, sinks = [], []
        for g in range(group):
            rs = pl.ds(pl.multiple_of((c * group + g) * CHUNK, CHUNK), CHUNK)
            lr_hi, lr_lo = _split2(lr_ref[rs, :])
            x = _dot(lr_hi, w_hi) + _dot(lr_lo, w_hi) + _dot(lr_hi, w_lo) + b_ref[...]
            lg = (jnp.minimum(x, 0.0) - jnp.log(1.0 + jnp.exp(-jnp.abs(x)))) * (1.0 / GLA_TAU)
            items.append((gq_ref[rs, :] * (GLA_DK ** -0.5), gk_ref[rs, :], gv_ref[rs, :], lg, gstate_ref, GLA_DK))
            sinks.append((og_ref, ggain_ref, gog_ref, rs))
            z = rf_ref[rs, :]
            f = lb + (1.0 - lb) * _sigmoid(z)
            items.append((rq_ref[rs, :], (1.0 - lb) * _sigmoid(-z), ri_ref[rs, :], jnp.log(f), rstate_ref, HD))
            sinks.append((or_ref, rgain_ref, rog_ref, rs))
        for o, (out_ref, gain_ref, gate_ref, rs) in zip(_linear_chunks(items, tri, mall_ref), sinks):
            out_ref[rs, :] = _group_rms(o, ones64, HD) * gain_ref[...] * _silu(gate_ref[rs, :])
        return carry

    lax.fori_loop(0, tm // (CHUNK * group), body, 0)


def _linear_mixers(u, w2pad, b_gate, gla_gain, lb_logits, hgrn_gain, B, S, tm, layer_idx):
    nt = S // tm
    tri, mall = _linear_consts()
    ones64 = _block_ones(GROUP, HD)
    c256 = lambda name: pl.BlockSpec((tm, GROUP), lambda b, i, c=_B256[name]: (b * nt + i, c))
    c128 = lambda name: pl.BlockSpec((tm, LANE), lambda b, i, c=_B128[name]: (b * nt + i, c))
    full = lambda shape: pl.BlockSpec(shape, lambda b, i: (0,) * len(shape))
    out_spec = pl.BlockSpec((tm, GROUP), lambda b, i: (b * nt + i, 0))
    out_shape = jax.ShapeDtypeStruct((B * S, GROUP), F32)
    depth = lb_logits.shape[0]
    return pl.pallas_call(
        functools.partial(_linear_kernel, tm=tm, layer_idx=layer_idx),
        grid=(B, nt),
        in_specs=[c128("g_q"), c128("g_k"), c256("g_v"), c128("g_lr"), c256("g_og"),
                  full((LANE, LANE)), full((1, LANE)), full((1, GROUP)),
                  c256("r_q"), c256("r_f"), c256("r_i"), c256("r_og"), full((depth, GROUP)), full((1, GROUP)),
                  full(tri.shape), full(mall.shape), full((GROUP, GROUP))],
        out_specs=[out_spec, out_spec], out_shape=[out_shape, out_shape],
        scratch_shapes=[pltpu.VMEM((GROUP, HEADS * GLA_DK), F32), pltpu.VMEM((GROUP, GROUP), F32)],
        compiler_params=_params("parallel", "arbitrary"),
        name="linear_mixers",
    )(u, u, u, u, u, w2pad, b_gate, gla_gain, u, u, u, u, lb_logits, hgrn_gain, tri, mall, ones64)


def _mix_residual(h_ref, part_refs, wout_ref):
    x = h_ref[...]
    for j, r in enumerate(part_refs):
        x = x + _dot(r[...].astype(BF16), wout_ref[j * GROUP:(j + 1) * GROUP, :])
    return x


def _ffn_kernel(h_ref, a_ref, b_ref, c_ref, d_ref, wout_ref, g_ref, wg_ref, wu_ref, wd_ref, o_ref,
                xn_ref, acc_ref):
    f = pl.program_id(1)

    @pl.when(f == 0)
    def _():
        x = _mix_residual(h_ref, (a_ref, b_ref, c_ref, d_ref), wout_ref)
        y = x * lax.rsqrt(jnp.mean(x * x, axis=-1, keepdims=True) + EPS) * g_ref[...]
        xn_ref[...] = y.astype(BF16)
        acc_ref[...] = x

    xn = xn_ref[...]
    mid = _silu(_dot(xn, wg_ref[...])) * _dot(xn, wu_ref[...])
    acc_ref[...] += _dot(mid.astype(BF16), wd_ref[...])

    @pl.when(f == pl.num_programs(1) - 1)
    def _():
        o_ref[...] = acc_ref[...]


def _ffn(h, parts, w_out, gain, wg, wu, wd, tm, tf):
    T = h.shape[0]
    F = wg.shape[1]
    part = pl.BlockSpec((tm, GROUP), lambda i, f: (i, 0))
    return pl.pallas_call(
        _ffn_kernel,
        grid=(T // tm, F // tf),
        in_specs=[pl.BlockSpec((tm, D_MODEL), lambda i, f: (i, 0)), part, part, part, part,
                  pl.BlockSpec((D_MODEL, D_MODEL), lambda i, f: (0, 0)),
                  pl.BlockSpec((1, D_MODEL), lambda i, f: (0, 0)),
                  pl.BlockSpec((D_MODEL, tf), lambda i, f: (0, f)),
                  pl.BlockSpec((D_MODEL, tf), lambda i, f: (0, f)),
                  pl.BlockSpec((tf, D_MODEL), lambda i, f: (f, 0))],
        out_specs=pl.BlockSpec((tm, D_MODEL), lambda i, f: (i, 0)),
        out_shape=jax.ShapeDtypeStruct((T, D_MODEL), F32),
        scratch_shapes=[pltpu.VMEM((tm, D_MODEL), BF16), pltpu.VMEM((tm, D_MODEL), F32)],
        compiler_params=_params("parallel", "arbitrary"),
        name="ffn_swiglu",
    )(h, *parts, w_out, gain.reshape(1, D_MODEL), wg, wu, wd)


def _router_kernel(h_ref, a_ref, b_ref, c4_ref, d_ref, wout_ref, g_ref, r_ref, lower_ref,
                   h1_ref, c_ref, comb_ref, rk_ref, cnt_ref):
    x = _mix_residual(h_ref, (a_ref, b_ref, c4_ref, d_ref), wout_ref)
    h1_ref[...] = x
    y = x * lax.rsqrt(jnp.mean(x * x, axis=-1, keepdims=True) + EPS) * g_ref[...]
    c_ref[...] = y.astype(BF16)
    y_hi, y_lo = _split2(y)
    r_hi, r_lo = _split2(r_ref[...])
    logits = _dot(y_hi, r_hi) + _dot(y_lo, r_hi) + _dot(y_hi, r_lo)
    lane = lax.broadcasted_iota(jnp.int32, logits.shape, 1)
    lane_f = lane.astype(F32)
    logits = jnp.where(lane < N_EXPERTS, logits, LOWEST)
    m1 = jnp.max(logits, axis=1, keepdims=True)
    i1 = jnp.min(jnp.where(logits == m1, lane_f, float(LANE)), axis=1, keepdims=True)
    rest = jnp.where(lane_f == i1, LOWEST, logits)
    m2 = jnp.max(rest, axis=1, keepdims=True)
    i2 = jnp.min(jnp.where(rest == m2, lane_f, float(LANE)), axis=1, keepdims=True)
    e2 = jnp.exp(m2 - m1)
    w1 = 1.0 / (1.0 + e2)
    comb_ref[...] = jnp.where(lane_f == i1, w1, jnp.where(lane_f == i2, e2 * w1, 0.0))
    chosen = (lane_f == i1) | (lane_f == i2)
    sel = jnp.where(chosen, 1.0, 0.0)
    rank = _dot(lower_ref[...], sel.astype(BF16))
    rk_ref[...] = jnp.where(chosen, rank, -1.0).astype(jnp.int32)
    cnt_ref[...] = jnp.broadcast_to(jnp.sum(sel, axis=0, keepdims=True), cnt_ref.shape).astype(jnp.int32)


def _router(h, parts, w_out, gain, router_pad, tm):
    T = h.shape[0]
    i = np.arange(tm)
    lower = jnp.asarray((i[None, :] < i[:, None]).astype(np.float32), BF16)
    sds = jax.ShapeDtypeStruct
    rows = lambda w: pl.BlockSpec((tm, w), lambda i: (i, 0))
    return pl.pallas_call(
        _router_kernel,
        grid=(T // tm,),
        in_specs=[rows(D_MODEL), rows(GROUP), rows(GROUP), rows(GROUP), rows(GROUP),
                  pl.BlockSpec((D_MODEL, D_MODEL), lambda i: (0, 0)),
                  pl.BlockSpec((1, D_MODEL), lambda i: (0, 0)),
                  pl.BlockSpec((D_MODEL, LANE), lambda i: (0, 0)),
                  pl.BlockSpec((tm, tm), lambda i: (0, 0))],
        out_specs=[rows(D_MODEL), rows(D_MODEL), rows(LANE), rows(LANE), pl.BlockSpec((8, LANE), lambda i: (i, 0))],
        out_shape=[sds((T, D_MODEL), F32), sds((T, D_MODEL), BF16), sds((T, LANE), F32), sds((T, LANE), jnp.int32),
                   sds((T // tm * 8, LANE), jnp.int32)],
        compiler_params=_params("parallel"),
        name="moe_router",
    )(h, *parts, w_out, gain.reshape(1, D_MODEL), router_pad, lower)


MOE_CHUNK = 512
MOE_ROWS = 512
MOE_HALF = MOE_ROWS // 2
MOE_TILE = 1024
MOE_ALIGN = 16
MOE_PAD_SEGS = MOE_TILE // MOE_ROWS


def _moe_plan(cnt, T):
    nch = T // MOE_CHUNK
    n_ce = cnt.reshape(nch, 8, LANE)[:, 0, :N_EXPERTS]
    cap = (n_ce + MOE_ALIGN - 1) // MOE_ALIGN * MOE_ALIGN
    tot = jnp.sum(cap, axis=0)
    ptot = (tot + MOE_TILE - 1) // MOE_TILE * MOE_TILE
    start = jnp.cumsum(ptot) - ptot
    lo_ce = start[None, :] + jnp.cumsum(cap, axis=0) - cap
    pad_lo = (start + tot)[:, None] + MOE_ROWS * jnp.arange(MOE_PAD_SEGS, dtype=jnp.int32)[None, :]
    lo = jnp.concatenate([lo_ce.T, pad_lo], axis=1)
    n = jnp.concatenate([n_ce.T, jnp.full((N_EXPERTS, MOE_PAD_SEGS), MOE_ROWS, jnp.int32)], axis=1)
    n_tiles = _moe_tiles(T)
    tile_end = jnp.cumsum(ptot // MOE_TILE)
    j = jnp.arange(n_tiles, dtype=jnp.int32)
    tile_e = jnp.minimum(jnp.sum((tile_end[None, :] <= j[:, None]).astype(jnp.int32), axis=1), N_EXPERTS - 1)
    valid = (j < tile_end[-1]).astype(jnp.int32)
    n_flat = jnp.concatenate([n.reshape(-1), tile_end[-1:] * MOE_PAD_SEGS]).astype(jnp.int32)
    return lo.reshape(-1).astype(jnp.int32), n_flat, tile_e, valid


def _moe_tiles(T):
    nch = T // MOE_CHUNK
    rows = 2 * T + nch * N_EXPERTS * (MOE_ALIGN - 1) + N_EXPERTS * (MOE_TILE - 1)
    return -(-rows // MOE_TILE) + 1


def _moe_gather_kernel(lo_ref, n_ref, c_ref, rkt_ref, x_hbm, buf, sem, *, nch):
    e = pl.program_id(0)
    c = pl.program_id(1)
    n_steps = pl.num_programs(0) * pl.num_programs(1)
    step = e * pl.num_programs(1) + c
    slot = step % 2
    rk_row = rkt_ref[pl.ds(e, 1), :]
    rk_row = jnp.where(c < nch, rk_row, -1)
    row = lax.broadcasted_iota(jnp.int32, (MOE_HALF, MOE_CHUNK), 0)
    chunk = c_ref[...]
    onehot = jnp.where(rk_row == row, 1.0, 0.0).astype(BF16)
    buf[slot, 0:MOE_HALF, :] = _dot(onehot, chunk).astype(BF16)

    @pl.when(n_ref[step] > MOE_HALF)
    def _():
        onehot = jnp.where(rk_row == row + MOE_HALF, 1.0, 0.0).astype(BF16)
        buf[slot, MOE_HALF:MOE_ROWS, :] = _dot(onehot, chunk).astype(BF16)

    def copy(k, s, rows):
        dst = x_hbm.at[pl.ds(pl.multiple_of(lo_ref[k], MOE_ALIGN), rows)]
        return pltpu.make_async_copy(buf.at[s, 0:rows], dst, sem.at[s])

    def for_size(k, fn):
        @pl.when(n_ref[k] > MOE_HALF)
        def _():
            fn(MOE_ROWS)

        @pl.when(n_ref[k] <= MOE_HALF)
        def _():
            fn(MOE_HALF)

    @pl.when(step > 0)
    def _():
        for_size(step - 1, lambda rows: copy(step - 1, 1 - slot, rows).wait())

    for_size(step, lambda rows: copy(step, slot, rows).start())

    @pl.when(step == n_steps - 1)
    def _():
        for_size(step, lambda rows: copy(step, slot, rows).wait())
        buf[0] = jnp.zeros((MOE_ROWS, D_MODEL), BF16)

        def fill(k, carry):
            dst = x_hbm.at[pl.ds(pl.multiple_of(k * MOE_ROWS, MOE_ROWS), MOE_ROWS)]
            cp = pltpu.make_async_copy(buf.at[0], dst, sem.at[0])
            cp.start()
            cp.wait()
            return carry

        lax.fori_loop(n_ref[n_steps], x_hbm.shape[0] // MOE_ROWS, fill, 0)


def _moe_gather(c_bf, rkt, lo, n, n_tiles):
    T = c_bf.shape[0]
    nch = T // MOE_CHUNK
    last = nch - 1
    grid_spec = pltpu.PrefetchScalarGridSpec(
        num_scalar_prefetch=2,
        grid=(N_EXPERTS, nch + MOE_PAD_SEGS),
        in_specs=[pl.BlockSpec((MOE_CHUNK, D_MODEL), lambda e, c, lo, n: (jnp.minimum(c, last), 0)),
                  pl.BlockSpec((N_EXPERTS, MOE_CHUNK), lambda e, c, lo, n: (0, jnp.minimum(c, last)))],
        out_specs=pl.BlockSpec(memory_space=pl.ANY),
        scratch_shapes=[pltpu.VMEM((2, MOE_ROWS, D_MODEL), BF16), pltpu.SemaphoreType.DMA((2,))],
    )
    return pl.pallas_call(
        functools.partial(_moe_gather_kernel, nch=nch),
        grid_spec=grid_spec,
        out_shape=jax.ShapeDtypeStruct((n_tiles * MOE_TILE, D_MODEL), BF16),
        compiler_params=_params("arbitrary", "arbitrary"),
        name="moe_gather",
    )(lo, n, c_bf, rkt)


def _moe_ffn_kernel(te_ref, valid_ref, x_ref, wg_ref, wu_ref, wd_ref, y_ref, acc_ref):
    j = pl.program_id(0)
    f = pl.program_id(1)
    last = pl.num_programs(1) - 1

    @pl.when(f == 0)
    def _():
        acc_ref[...] = jnp.zeros(acc_ref.shape, F32)

    @pl.when(valid_ref[j] == 1)
    def _():
        x = x_ref[...]
        mid = _silu(_dot(x, wg_ref[...])) * _dot(x, wu_ref[...])
        acc_ref[...] += _dot(mid.astype(BF16), wd_ref[...])

    @pl.when(f == last)
    def _():
        y_ref[...] = acc_ref[...].astype(BF16)


def _moe_ffn(x_sorted, tile_e, valid, wg, wu, wd, tf):
    n_tiles = x_sorted.shape[0] // MOE_TILE
    F = wg.shape[2]
    nf = F // tf
    sq = pl.Squeezed()
    fsel = lambda j, f, te, va: jnp.where(va[j] == 1, f, nf - 1)
    grid_spec = pltpu.PrefetchScalarGridSpec(
        num_scalar_prefetch=2,
        grid=(n_tiles, nf),
        in_specs=[pl.BlockSpec((MOE_TILE, D_MODEL), lambda j, f, te, va: (j, 0)),
                  pl.BlockSpec((sq, D_MODEL, tf), lambda j, f, te, va: (te[j], 0, fsel(j, f, te, va))),
                  pl.BlockSpec((sq, D_MODEL, tf), lambda j, f, te, va: (te[j], 0, fsel(j, f, te, va))),
                  pl.BlockSpec((sq, tf, D_MODEL), lambda j, f, te, va: (te[j], fsel(j, f, te, va), 0))],
        out_specs=pl.BlockSpec((MOE_TILE, D_MODEL), lambda j, f, te, va: (j, 0)),
        scratch_shapes=[pltpu.VMEM((MOE_TILE, D_MODEL), F32)],
    )
    return pl.pallas_call(
        _moe_ffn_kernel,
        grid_spec=grid_spec,
        out_shape=jax.ShapeDtypeStruct((n_tiles * MOE_TILE, D_MODEL), BF16),
        compiler_params=_params("parallel", "arbitrary"),
        name="moe_experts",
    )(tile_e, valid, x_sorted, wg, wu, wd)


def _ple_update(x, p, gain, w_gate, w_proj):
    y = x * lax.rsqrt(jnp.mean(x * x, axis=-1, keepdims=True) + EPS) * gain
    gate = _sigmoid(_dot(y.astype(BF16), w_gate))
    return x + _dot(p.astype(BF16), w_proj) * gate


def _moe_combine_kernel(lo_ref, n_ref, h_ref, rk_ref, comb_ref, p_ref, g_ref, wgate_ref, wproj_ref, y_hbm,
                        o_ref, ybuf, sem, *, nch):
    c = pl.program_id(0)
    seg = lambda e: e * (nch + MOE_PAD_SEGS) + c

    def copy(e, rows):
        src = y_hbm.at[pl.ds(pl.multiple_of(lo_ref[seg(e)], MOE_ALIGN), rows)]
        return pltpu.make_async_copy(src, ybuf.at[e, 0:rows], sem.at[e])

    def for_size(e, fn):
        @pl.when(n_ref[seg(e)] > MOE_HALF)
        def _():
            fn(MOE_ROWS)

        @pl.when(n_ref[seg(e)] <= MOE_HALF)
        def _():
            fn(MOE_HALF)

    for e in range(N_EXPERTS):
        for_size(e, lambda rows, e=e: copy(e, rows).start())
    o_ref[...] = h_ref[...]
    rk = rk_ref[...]
    comb = comb_ref[...]
    lane = lax.broadcasted_iota(jnp.int32, (MOE_CHUNK, MOE_HALF), 1)
    for e in range(N_EXPERTS):
        for_size(e, lambda rows, e=e: copy(e, rows).wait())
        rank_col = rk[:, e:e + 1]
        w_col = comb[:, e:e + 1]
        onehot = jnp.where(rank_col == lane, 1.0, 0.0).astype(BF16)
        o_ref[...] += w_col * _dot(onehot, ybuf[e, 0:MOE_HALF, :])

        @pl.when(n_ref[seg(e)] > MOE_HALF)
        def _():
            onehot = jnp.where(rank_col == lane + MOE_HALF, 1.0, 0.0).astype(BF16)
            o_ref[...] += w_col * _dot(onehot, ybuf[e, MOE_HALF:MOE_ROWS, :])

    o_ref[...] = _ple_update(o_ref[...], p_ref[...], g_ref[...], wgate_ref[...], wproj_ref[...])


def _moe_combine(h, rk_pad, comb, y_sorted, lo, n, p, layer, ple_gain, ple_wg, ple_wp):
    T = h.shape[0]
    nch = T // MOE_CHUNK
    const = lambda shape: pl.BlockSpec(shape, lambda c, lo, n: (0, 0))
    grid_spec = pltpu.PrefetchScalarGridSpec(
        num_scalar_prefetch=2,
        grid=(nch,),
        in_specs=[pl.BlockSpec((MOE_CHUNK, D_MODEL), lambda c, lo, n: (c, 0)),
                  pl.BlockSpec((MOE_CHUNK, LANE), lambda c, lo, n: (c, 0)),
                  pl.BlockSpec((MOE_CHUNK, LANE), lambda c, lo, n: (c, 0)),
                  pl.BlockSpec((pl.Squeezed(), MOE_CHUNK, PLE_DIM), lambda c, lo, n: (layer, c, 0)),
                  const((1, D_MODEL)), const((D_MODEL, D_MODEL)), const((PLE_DIM, D_MODEL)),
                  pl.BlockSpec(memory_space=pl.ANY)],
        out_specs=pl.BlockSpec((MOE_CHUNK, D_MODEL), lambda c, lo, n: (c, 0)),
        scratch_shapes=[pltpu.VMEM((N_EXPERTS, MOE_ROWS, D_MODEL), BF16), pltpu.SemaphoreType.DMA((N_EXPERTS,))],
    )
    return pl.pallas_call(
        functools.partial(_moe_combine_kernel, nch=nch),
        grid_spec=grid_spec,
        out_shape=jax.ShapeDtypeStruct((T, D_MODEL), F32),
        compiler_params=_params("arbitrary"),
        name="moe_combine",
    )(lo, n, h, rk_pad, comb, p, ple_gain.reshape(1, D_MODEL), ple_wg, ple_wp, y_sorted)


def _moe(h, c_bf, comb, rk, cnt, wg, wu, wd, tf, ple):
    T = h.shape[0]
    lo, n, tile_e, valid = _moe_plan(cnt, T)
    x_sorted = _moe_gather(c_bf, rk[:, :N_EXPERTS].T, lo, n, _moe_tiles(T))
    y_sorted = _moe_ffn(x_sorted, tile_e, valid, wg, wu, wd, tf)
    return _moe_combine(h, rk, comb, y_sorted, lo, n, *ple)


def _ple_kernel(h_ref, p_ref, g_ref, wg_ref, wp_ref, o_ref):
    o_ref[...] = _ple_update(h_ref[...], p_ref[...], g_ref[...], wg_ref[...], wp_ref[...])


def _ple(h, p, layer, gain, wg, wp, tm):
    T = h.shape[0]
    return pl.pallas_call(
        _ple_kernel,
        grid=(T // tm,),
        in_specs=[pl.BlockSpec((tm, D_MODEL), lambda i: (i, 0)),
                  pl.BlockSpec((pl.Squeezed(), tm, PLE_DIM), lambda i: (layer, i, 0)),
                  pl.BlockSpec((1, D_MODEL), lambda i: (0, 0)),
                  pl.BlockSpec((D_MODEL, D_MODEL), lambda i: (0, 0)),
                  pl.BlockSpec((PLE_DIM, D_MODEL), lambda i: (0, 0))],
        out_specs=pl.BlockSpec((tm, D_MODEL), lambda i: (i, 0)),
        out_shape=jax.ShapeDtypeStruct((T, D_MODEL), F32),
        compiler_params=_params("parallel"),
        name="ple_gate",
    )(h, p, gain.reshape(1, D_MODEL), wg, wp)


def _tiles(T, S):
    pick = lambda n, pref: max(t for t in pref if n % t == 0)
    return dict(
        proj_m=pick(T, (512, 256, 128)), proj_n=NC,
        prep_m=pick(T, (512, 256, 128)),
        attn_q=256, attn_k=pick(S, (512, 256)),
        nsa_q=pick(S, (512, 256)), nsa_k=pick(S, (512, 256)),
        lin_m=pick(S, (512, 256, 128, 64)),
        row_m=pick(T, (512, 256, 128)),
        ffn_m=pick(T, (1024, 512, 256, 128)), ffn_f=512, moe_f=D_FF // 4,
    )


def kernel(x, p, norm_attn, w_in, w_out, nsa_cmp_pos, nsa_cmp_w1, nsa_cmp_w2, nsa_qk_gain, diff_qk_gain, diff_lambda, diff_norm, gla_w_gate2, gla_b_gate, gla_norm, hgrn_lb_logits, hgrn_norm, norm_ffn, ffn_w_gate, ffn_w_up, ffn_w_down, moe_router, moe_w_gate, moe_w_up, moe_w_down, ple_norm, ple_w_gate, ple_w_proj):
    B, S, _ = x.shape
    depth = w_in.shape[0]
    T = B * S
    t = _tiles(T, S)
    cols =jnp.asarray(np.maximum(_COLS, 0), jnp.int32)
    col_mask = jnp.asarray(_COLS >= 0)
    ones_row = jnp.ones((GROUP,), F32)

    h = x.reshape(T, D_MODEL)
    for i in range(depth):
        w_in_r = jnp.where(col_mask[None, :], jnp.take(w_in[i], cols, axis=1), 0.0).astype(BF16)
        gains = jnp.stack([
            jnp.tile(nsa_qk_gain[i, 0], HEADS) * (HD ** -0.5 * LOG2E),
            jnp.tile(diff_qk_gain[i, 0], 2 * HEADS) * (DIFF_QK ** -0.5 * LOG2E),
            jnp.tile(diff_qk_gain[i, 1], 2 * HEADS),
            jnp.tile(nsa_qk_gain[i, 2], HEADS),
            jnp.tile(nsa_qk_gain[i, 3], HEADS),
            ones_row, ones_row, ones_row])
        pos = jnp.transpose(nsa_cmp_pos[i], (1, 0, 2)).reshape(2, NSA_CMP_STRIDE, LANE)
        w1r = nsa_cmp_w1[i].reshape(2, NSA_CMP_LEN, HD, NSA_CMP_HIDDEN)
        zeros = jnp.zeros_like(w1r[0])
        w1_bd = jnp.concatenate([jnp.concatenate([w1r[0], zeros], axis=-1),
                                 jnp.concatenate([zeros, w1r[1]], axis=-1)], axis=1)
        w1_bd = w1_bd.reshape(2, NSA_CMP_STRIDE, LANE, 2 * NSA_CMP_HIDDEN).astype(BF16)
        w2p = jnp.pad(nsa_cmp_w2[i], ((0, 0), (0, 0), (0, LANE - HD))).astype(BF16)
        kc_gain = jnp.pad(nsa_qk_gain[i, 1], (0, LANE - HD)).reshape(1, LANE)
        w2pad = jnp.zeros((LANE, LANE), F32).at[:GLA_RANK].set(gla_w_gate2[i])
        diff_gain_col = jnp.broadcast_to(diff_norm[i][:, None], (HD, t["attn_q"]))

        u = _norm_matmul(h, norm_attn[i], w_in_r, t["proj_m"], t["proj_n"])
        nqt, dqt, dk, dvt, kvs, vst, kvw, vwt = _prep(u, gains, t["prep_m"])
        kc, vct = _compress(u, pos, w1_bd, w2p, kc_gain, B, S)
        o_a = _nsa_attention(u, nqt, kc, vct, kvs, vst, kvw, vwt, B, S, t["nsa_q"], t["nsa_k"], t["nsa_q"])
        o_b = _diff_attention(dqt, dk, dvt, diff_lambda[i], diff_gain_col, B, S, t["attn_q"], i)
        o_c, o_d = _linear_mixers(u, w2pad, gla_b_gate[i].reshape(1, LANE),
                                  jnp.tile(gla_norm[i], HEADS).reshape(1, GROUP), hgrn_lb_logits,
                                  jnp.tile(hgrn_norm[i], HEADS).reshape(1, GROUP), B, S, t["lin_m"], i)
        parts = (o_a, o_b, o_c, o_d)
        w_out_bf = w_out[i].astype(BF16)
        ple = (p.reshape(depth, T, PLE_DIM), i, ple_norm[i], ple_w_gate[i].astype(BF16), ple_w_proj[i].astype(BF16))
        j = i // 2
        if i % 2 == 0:
            h = _ffn(h, parts, w_out_bf, norm_ffn[i], ffn_w_gate[j].astype(BF16), ffn_w_up[j].astype(BF16),
                     ffn_w_down[j].astype(BF16), t["ffn_m"], t["ffn_f"])
            h = _ple(h, *ple, t["row_m"])
        else:
            router_pad = jnp.zeros((D_MODEL, LANE), F32).at[:, :N_EXPERTS].set(moe_router[j])
            h, c_bf, comb, rk, cnt = _router(h, parts, w_out_bf, norm_ffn[i], router_pad, MOE_CHUNK)
            h = _moe(h, c_bf, comb, rk, cnt, moe_w_gate[j].astype(BF16), moe_w_up[j].astype(BF16),
                     moe_w_down[j].astype(BF16), t["moe_f"], ple)
    return h.reshape(B, S, D_MODEL)
```

```python
import functools
import math

import numpy as np
import jax
import jax.numpy as jnp
from jax import lax
from jax.experimental import pallas as pl
from jax.experimental.pallas import tpu as pltpu

F32 = jnp.float32
BF16 = jnp.bfloat16

D_MODEL = 1024
HEADS = 4
HD = 64
GROUP = HEADS * HD
NSA_CMP_LEN = 32
NSA_CMP_STRIDE = 16
NSA_CMP_HIDDEN = 4 * HD
NSA_SEL_LEN = 64
NSA_TOPK = 16
NSA_WINDOW = 512
NSA_SUBTILES = 2
DIFF_QK = HD // 2
GLA_DK = HD // 2
GLA_RANK = 16
GLA_TAU = 16.0
CHUNK = 64
D_FF = 7 * D_MODEL // 2
N_EXPERTS = 8
PLE_DIM = 256
EPS = 1e-6
NEG = -1e30
BIG = 1e30
LOWEST = -3.0e38
MASK_BIG = 2.0 ** 100
LOG2E = 1.4426950408889634

VMEM_LIMIT = 52 * 1024 * 1024
LANE = 128

_SRC = dict(nsa_q=0, k_cmp=256, v_cmp=320, k_slc=384, v_slc=448, k_win=512, v_win=576, nsa_g=640,
            d_q=652, d_k=908, d_v=1164, g_q=1420, g_k=1548, g_v=1676, g_lr=1932, g_og=1948,
            r_q=2204, r_f=2460, r_i=2716, r_og=2972)
IN_COLS = 3228

_B256 = dict(nsa_q=0, d_q=1, d_k=2, d_v=3, g_v=4, g_og=5, r_q=6, r_i=7, r_og=8)
_B128 = dict(g_q=18, g_k=19, kvslc=20, kvwin=21)
NC16 = 22 * 128
_F256 = dict(r_f=0)
_F128 = dict(g_lr=2, nsa_g=3, kvcmp=4)
NC32 = 5 * 128
NC = NC16 + NC32


def _column_map():
    cols = -np.ones((NC,), np.int64)

    def put(dst, src, width):
        cols[dst:dst + width] = np.arange(src, src + width)

    for name in _B256:
        put(_B256[name] * 256, _SRC[name], 256)
    put(_B128["g_q"] * 128, _SRC["g_q"], 128)
    put(_B128["g_k"] * 128, _SRC["g_k"], 128)
    put(_B128["kvslc"] * 128, _SRC["k_slc"], 128)
    put(_B128["kvwin"] * 128, _SRC["k_win"], 128)
    put(NC16 + _F256["r_f"] * 256, _SRC["r_f"], 256)
    put(NC16 + _F128["g_lr"] * 128, _SRC["g_lr"], GLA_RANK)
    put(NC16 + _F128["nsa_g"] * 128, _SRC["nsa_g"], 3 * HEADS)
    put(NC16 + _F128["kvcmp"] * 128, _SRC["k_cmp"], 128)
    return cols


_COLS = _column_map()


def _dot(a, b):
    return jnp.dot(a, b, preferred_element_type=F32)


def _dot_nt(a, b):
    return lax.dot_general(a, b, (((1,), (1,)), ((), ())), preferred_element_type=F32)


def _dot_tn(a, b):
    return lax.dot_general(a, b, (((0,), (0,)), ((), ())), preferred_element_type=F32)


def _split2(x):
    hi = x.astype(BF16)
    lo = (x - hi.astype(F32)).astype(BF16)
    return hi, lo


def _split3(x):
    hi = x.astype(BF16)
    r = x - hi.astype(F32)
    mid = r.astype(BF16)
    lo = (r - mid.astype(F32)).astype(BF16)
    return hi, mid, lo


def _group_mean(x, ones_bf, group):
    hi, lo = _split2(x)
    return (_dot(hi, ones_bf) + _dot(lo, ones_bf)) * (1.0 / group)


def _group_rms(x, ones_bf, group):
    return x * lax.rsqrt(_group_mean(x * x, ones_bf, group) + EPS)


def _sigmoid(x):
    return 1.0 / (1.0 + jnp.exp(-x))


def _silu(x):
    return x * _sigmoid(x)


def _params(*sem):
    return pltpu.CompilerParams(dimension_semantics=sem, vmem_limit_bytes=VMEM_LIMIT)


def _block_ones(n, group):
    i = np.arange(n)
    return jnp.asarray((i[:, None] // group == i[None, :] // group).astype(np.float32), BF16)


def _norm_matmul_kernel(x_ref, g_ref, w_ref, o16_ref, o32_ref):
    x = x_ref[...]
    y = x * lax.rsqrt(jnp.mean(x * x, axis=-1, keepdims=True) + EPS) * g_ref[...]
    u = _dot(y.astype(BF16), w_ref[...])
    o16_ref[...] = u[:, 0:NC16].astype(BF16)
    o32_ref[...] = u[:, NC16:NC]


def _norm_matmul(x, gain, w_bf, tm):
    T, K = x.shape
    return pl.pallas_call(
        _norm_matmul_kernel,
        grid=(T // tm,),
        in_specs=[pl.BlockSpec((tm, K), lambda i: (i, 0)),
                  pl.BlockSpec((1, K), lambda i: (0, 0)),
                  pl.BlockSpec((K, NC), lambda i: (0, 0))],
        out_specs=[pl.BlockSpec((tm, NC16), lambda i: (i, 0)), pl.BlockSpec((tm, NC32), lambda i: (i, 0))],
        out_shape=[jax.ShapeDtypeStruct((T, NC16), BF16), jax.ShapeDtypeStruct((T, NC32), F32)],
        compiler_params=_params("parallel"),
        name="in_proj",
    )(x, gain.reshape(1, K), w_bf)


def _prep_kernel(nq_ref, dq_ref, dk_ref, dv_ref, kvs_ref, kvw_ref, gains_ref, ones64_ref, ones32_ref,
                 ones64h_ref, o_nq, o_dq, o_dk, o_dv, o_kvs, o_vs, o_kvw, o_vw):
    ones64 = ones64_ref[...]
    ones32 = ones32_ref[...]
    ones64h = ones64h_ref[...]
    f32 = lambda ref: ref[...].astype(F32)
    o_nq[...] = (_group_rms(f32(nq_ref), ones64, HD) * gains_ref[0:1, :]).T.astype(BF16)
    o_dq[...] = (_group_rms(f32(dq_ref), ones32, DIFF_QK) * gains_ref[1:2, :]).T.astype(BF16)
    o_dk[...] = (_group_rms(f32(dk_ref), ones32, DIFF_QK) * gains_ref[2:3, :]).astype(BF16)
    o_dv[...] = f32(dv_ref).T.astype(BF16)
    lane = lax.broadcasted_iota(jnp.int32, kvs_ref.shape, 1)
    for kv_ref, gain, o_kv, o_v in ((kvs_ref, gains_ref[3:4, 0:LANE], o_kvs, o_vs),
                                    (kvw_ref, gains_ref[4:5, 0:LANE], o_kvw, o_vw)):
        x = f32(kv_ref)
        o_kv[...] = jnp.where(lane < HD, _group_rms(x, ones64h, HD) * gain, x).astype(BF16)
        o_v[...] = x.T[HD:2 * HD, :].astype(BF16)


def _prep(u, gains, tm):
    T = u.shape[0]
    c256 = lambda name: pl.BlockSpec((tm, GROUP), lambda i, c=_B256[name]: (i, c))
    c128 = lambda name: pl.BlockSpec((tm, LANE), lambda i, c=_B128[name]: (i, c))
    const = lambda shape: pl.BlockSpec(shape, lambda i: (0, 0))
    rows = lambda w: pl.BlockSpec((tm, w), lambda i: (i, 0))
    colsT = lambda h: pl.BlockSpec((h, tm), lambda i: (0, i))
    sds = jax.ShapeDtypeStruct
    return pl.pallas_call(
        _prep_kernel,
        grid=(T // tm,),
        in_specs=[c256("nsa_q"), c256("d_q"), c256("d_k"), c256("d_v"), c128("kvslc"), c128("kvwin"),
                  const((8, GROUP)), const((GROUP, GROUP)), const((GROUP, GROUP)), const((LANE, LANE))],
        out_specs=[colsT(GROUP), colsT(GROUP), rows(GROUP), colsT(GROUP),
                   rows(LANE), colsT(HD), rows(LANE), colsT(HD)],
        out_shape=[sds((GROUP, T), BF16), sds((GROUP, T), BF16), sds((T, GROUP), BF16), sds((GROUP, T), BF16),
                   sds((T, LANE), BF16), sds((HD, T), BF16), sds((T, LANE), BF16), sds((HD, T), BF16)],
        compiler_params=_params("parallel"),
        name="attn_prep",
    )(u, u, u, u, u, u, gains, _block_ones(GROUP, HD), _block_ones(GROUP, DIFF_QK), _block_ones(LANE, HD))


def _compress_kernel(kv_ref, pos_ref, w1_ref, w2_ref, gain_ref, ones64h_ref, kc_ref, vc_ref):
    ncp = kc_ref.shape[0]
    hid = NSA_CMP_HIDDEN
    top = jnp.zeros((ncp, 2 * hid), F32)
    bot = jnp.zeros((ncp, 2 * hid), F32)
    for l in range(NSA_CMP_STRIDE):
        x = kv_ref[pl.ds(l, ncp, stride=NSA_CMP_STRIDE), :]
        top = top + _dot((x + pos_ref[0, l:l + 1, :]).astype(BF16), w1_ref[0, l])
        bot = bot + _dot((x + pos_ref[1, l:l + 1, :]).astype(BF16), w1_ref[1, l])
    hidden = top + pltpu.roll(bot, ncp - 1, 0)
    act = _silu(hidden).astype(BF16)
    kc = _dot(act[:, 0:hid], w2_ref[0])
    kc_ref[...] = _group_rms(kc, ones64h_ref[...], HD) * gain_ref[...]
    vc_ref[...] = _dot(act[:, hid:2 * hid], w2_ref[1]).T[0:HD, :].astype(BF16)


def _compress(u, pos, w1_bd, w2p_bf, gain_row, B, S):
    ncp = S // NSA_CMP_STRIDE
    sq = pl.Squeezed()
    full = lambda shape: pl.BlockSpec(shape, lambda b: (0,) * len(shape))
    return pl.pallas_call(
        _compress_kernel,
        grid=(B,),
        in_specs=[pl.BlockSpec((S, LANE), lambda b, c=_F128["kvcmp"]: (b, c)),
                  full((2, NSA_CMP_STRIDE, LANE)), full((2, NSA_CMP_STRIDE, LANE, 2 * NSA_CMP_HIDDEN)),
                  full((2, NSA_CMP_HIDDEN, LANE)), full((1, LANE)), full((LANE, LANE))],
        out_specs=[pl.BlockSpec((sq, ncp, LANE), lambda b: (b, 0, 0)),
                   pl.BlockSpec((sq, HD, ncp), lambda b: (b, 0, 0))],
        out_shape=[jax.ShapeDtypeStruct((B, ncp, LANE), F32), jax.ShapeDtypeStruct((B, HD, ncp), BF16)],
        compiler_params=_params("parallel"),
        name="nsa_compress",
    )(u, pos, w1_bd, w2p_bf, gain_row, _block_ones(LANE, HD))


def _softmax_step_t(s, v_t, m_old, l_old, acc_ref):
    m_new = jnp.maximum(m_old, jnp.max(s, axis=0, keepdims=True))
    alpha = jnp.exp2(m_old - m_new)
    p = jnp.exp2(s - m_new)
    acc_ref[...] = alpha * acc_ref[...] + _dot(v_t, p.astype(BF16))
    return m_new, alpha * l_old + jnp.sum(p, axis=0, keepdims=True)


def _pipelined_sweep(lo, hi, qk, process, process_last, stats, sa_ref, sb_ref):
    def pair(jj, st):
        t = lo + 2 * jj
        qk(t + 1, sb_ref)
        st = process(t, sa_ref, st)
        qk(t + 2, sa_ref)
        return process(t + 1, sb_ref, st)

    def two_left(st):
        qk(hi - 1, sb_ref)
        st = process(hi - 2, sa_ref, st)
        return process_last(hi - 1, sb_ref, st)

    n = hi - lo
    qk(lo, sa_ref)
    stats = lax.fori_loop(0, (n - 1) // 2, pair, stats)
    return lax.cond(n % 2 == 0, two_left, lambda st: process_last(hi - 1, sa_ref, st), stats)


def _nsa_kernel(qt_ref, g_ref, kc_ref, vct_ref, kvs_ref, vst_ref, kvw_ref, vwt_ref, ovl_ref, exp_ref,
                o_ref, qs_ref, sa_ref, sb_ref, *acc_refs, tq, tk, tw, ksel):
    i = pl.program_id(1)
    t0 = i * tq
    ncp = kc_ref.shape[0]
    nsel = ovl_ref.shape[0]
    acc_s, acc_w = acc_refs[:HEADS], acc_refs[HEADS:]

    qt = qt_ref[...]
    qs_ref[HD:LANE, :] = jnp.zeros((LANE - HD, HEADS * tq), BF16)
    for h in range(HEADS):
        qs_ref[0:HD, h * tq:(h + 1) * tq] = qt[h * HD:(h + 1) * HD, :]
        acc_s[h][...] = jnp.zeros((HD, tq), F32)
        acc_w[h][...] = jnp.zeros((HD, tq), F32)

    kc_hi, kc_lo = _split2(kc_ref[...])
    n_idx = lax.broadcasted_iota(jnp.int32, (ncp, tq), 0)
    t_lane = t0 + lax.broadcasted_iota(jnp.int32, (ncp, tq), 1)
    ok = n_idx * NSA_CMP_STRIDE + (NSA_CMP_LEN - 1) <= t_lane
    cmp_scores = _dot(kc_hi, qs_ref[...]) + _dot(kc_lo, qs_ref[...])
    o_c = []
    psum = jnp.zeros((ncp, tq), F32)
    for h in range(HEADS):
        s = jnp.where(ok, cmp_scores[:, h * tq:(h + 1) * tq], NEG)
        e = jnp.exp2(s - jnp.max(s, axis=0, keepdims=True))
        p = jnp.where(ok, e / jnp.sum(e, axis=0, keepdims=True), 0.0)
        o_c.append(_dot(vct_ref[...], p.astype(BF16)))
        psum = psum + p

    p_hi, p_lo = _split2(psum)
    imp = _dot(ovl_ref[...], p_hi) + _dot(ovl_ref[...], p_lo)
    blk = lax.broadcasted_iota(jnp.int32, (nsel, tq), 0)
    t_col = t0 + lax.broadcasted_iota(jnp.int32, (nsel, tq), 1)
    cur = t_col // NSA_SEL_LEN
    forced = (blk == 0) | (blk == cur) | (blk == cur - 1)
    vals = jnp.where(forced, BIG, jnp.where(blk * NSA_SEL_LEN <= t_col, imp, NEG))
    blk_f = blk.astype(F32)
    sel = jnp.zeros((nsel, tq), F32)
    for _ in range(ksel):
        mx = jnp.max(vals, axis=0, keepdims=True)
        first = jnp.min(jnp.where(vals == mx, blk_f, float(nsel)), axis=0, keepdims=True)
        pick = blk_f == first
        sel = jnp.where(pick, 1.0, sel)
        vals = jnp.where(pick, LOWEST, vals)
    if nsel < LANE:
        sel = jnp.concatenate([sel, jnp.zeros((LANE - nsel, tq), F32)], axis=0)
    sel_bf = sel.astype(BF16)

    stats0 = tuple((jnp.full((1, tq), NEG, F32), jnp.zeros((1, tq), F32)) for _ in range(HEADS))

    def sweep(lo, hi, k_ref, vt_ref, accs, width, bias_fn, last_bias_fn):
        def qk(kt, dst):
            k = k_ref[pl.ds(pl.multiple_of(kt * width, width), width), :]
            for h in range(HEADS):
                dst[h, 0:width, :] = _dot(k, qs_ref[:, h * tq:(h + 1) * tq])

        def process(kt, src, stats, fn):
            k0 = pl.multiple_of(kt * width, width)
            kpos = k0 + lax.broadcasted_iota(jnp.int32, (width, tq), 0)
            tpos = t0 + lax.broadcasted_iota(jnp.int32, (width, tq), 1)
            bias = fn(k0, kpos, tpos)
            vt = vt_ref[:, pl.ds(k0, width)]
            return tuple(_softmax_step_t(src[h, 0:width, :] + bias, vt, stats[h][0], stats[h][1], accs[h])
                         for h in range(HEADS))

        return _pipelined_sweep(lo, hi, qk, functools.partial(process, fn=bias_fn),
                                functools.partial(process, fn=last_bias_fn), stats0, sa_ref, sb_ref)

    def sel_bias(k0, kpos, tpos):
        chosen = _dot(exp_ref[pl.ds(k0, kpos.shape[0]), :], sel_bf)
        return chosen - MASK_BIG

    def sel_bias_diag(k0, kpos, tpos):
        chosen = _dot(exp_ref[pl.ds(k0, kpos.shape[0]), :], sel_bf)
        return jnp.where(kpos <= tpos, chosen - MASK_BIG, NEG)

    st_s = sweep(0, i + 1, kvs_ref, vst_ref, acc_s, tk, sel_bias, sel_bias_diag)

    def win_bias(k0, kpos, tpos):
        return jnp.where((kpos <= tpos) & (kpos > tpos - NSA_WINDOW), 0.0, NEG)

    st_w = sweep(jnp.maximum(t0 - NSA_WINDOW, 0) // tw, (t0 + tq - 1) // tw + 1,
                 kvw_ref, vwt_ref, acc_w, tw, win_bias, win_bias)

    gates = _sigmoid(g_ref[...].T)
    outs = []
    for h in range(HEADS):
        outs.append(gates[3 * h:3 * h + 1, :] * o_c[h]
                    + gates[3 * h + 1:3 * h + 2, :] * (acc_s[h][...] / st_s[h][1])
                    + gates[3 * h + 2:3 * h + 3, :] * (acc_w[h][...] / st_w[h][1]))
    o_ref[...] = jnp.concatenate(outs, axis=0).T


def _nsa_attention(u, nqt, kc, vct, kvs, vst, kvw, vwt, B, S, tq, tk, tw):
    nq = S // tq
    ncp = S // NSA_CMP_STRIDE
    nsel = S // NSA_SEL_LEN
    ksel = min(NSA_TOPK, nsel)
    n_cmp = (S - NSA_CMP_LEN) // NSA_CMP_STRIDE + 1
    cmp_start = np.arange(ncp) * NSA_CMP_STRIDE
    sel_start = np.arange(nsel) * NSA_SEL_LEN
    overlap = ((cmp_start[:, None] <= sel_start[None, :] + NSA_SEL_LEN - 1)
               & (cmp_start[:, None] + NSA_CMP_LEN - 1 >= sel_start[None, :])
               & (np.arange(ncp)[:, None] < n_cmp))
    ovl_t = jnp.asarray(overlap.T.astype(np.float32), BF16)
    assert tq == tk, "the selected sweep treats every key tile before the query tile's own as fully visible"
    expand = jnp.where(jnp.arange(S, dtype=jnp.int32)[:, None] // NSA_SEL_LEN
                       == jnp.arange(LANE, dtype=jnp.int32)[None, :], MASK_BIG, 0.0).astype(BF16)
    sq = pl.Squeezed()
    seq_rows = pl.BlockSpec((S, LANE), lambda b, i: (b, 0))
    seq_cols = pl.BlockSpec((HD, S), lambda b, i: (0, b))
    kern = functools.partial(_nsa_kernel, tq=tq, tk=tk, tw=tw, ksel=ksel)
    return pl.pallas_call(
        kern,
        grid=(B, nq),
        in_specs=[pl.BlockSpec((GROUP, tq), lambda b, i: (0, b * nq + i)),
                  pl.BlockSpec((tq, LANE), lambda b, i, c=_F128["nsa_g"]: (b * nq + i, c)),
                  pl.BlockSpec((sq, ncp, LANE), lambda b, i: (b, 0, 0)),
                  pl.BlockSpec((sq, HD, ncp), lambda b, i: (b, 0, 0)),
                  seq_rows, seq_cols, seq_rows, seq_cols,
                  pl.BlockSpec((nsel, ncp), lambda b, i: (0, 0)),
                  pl.BlockSpec((S, LANE), lambda b, i: (0, 0))],
        out_specs=pl.BlockSpec((tq, GROUP), lambda b, i: (b * nq + i, 0)),
        out_shape=jax.ShapeDtypeStruct((B * S, GROUP), F32),
        scratch_shapes=([pltpu.VMEM((LANE, HEADS * tq), BF16)]
                        + [pltpu.VMEM((HEADS, max(tk, tw), tq), F32)] * 2
                        + [pltpu.VMEM((HD, tq), F32)] * (2 * HEADS)),
        compiler_params=_params("parallel", "arbitrary"),
        name="nsa_attention",
    )(nqt, u, kc, vct, kvs, vst, kvw, vwt, ovl_t, expand)


def _diff_kernel(lam_ref, qt_ref, k_ref, vt_ref, gain_ref, o_ref, qs_ref, sa_ref, sb_ref, *acc_refs,
                 tq, lam_init):
    i = pl.program_id(1)
    t0 = i * tq
    lanes = 2 * tq

    qt = qt_ref[...]
    row = lax.broadcasted_iota(jnp.int32, (GROUP, tq), 0)
    zero = jnp.zeros_like(qt)
    for h in range(HEADS):
        qs_ref[h] = jnp.concatenate([jnp.where(row // DIFF_QK == 2 * h, qt, zero),
                                     jnp.where(row // DIFF_QK == 2 * h + 1, qt, zero)], axis=1)
        acc_refs[h][...] = jnp.zeros((HD, lanes), F32)

    def qk(kt, dst):
        k = k_ref[pl.ds(pl.multiple_of(kt * tq, tq), tq), :]
        for h in range(HEADS):
            dst[h] = _dot(k, qs_ref[h])

    def process(kt, src, masked, stats):
        k0 = pl.multiple_of(kt * tq, tq)
        out = []
        for h in range(HEADS):
            s = src[h]
            if masked:
                kpos = k0 + lax.broadcasted_iota(jnp.int32, (tq, lanes), 0)
                tpos = t0 + lax.broadcasted_iota(jnp.int32, (tq, lanes), 1) % tq
                s = jnp.where(kpos <= tpos, s, NEG)
            out.append(_softmax_step_t(s, vt_ref[h * HD:(h + 1) * HD, pl.ds(k0, tq)],
                                       stats[h][0], stats[h][1], acc_refs[h]))
        return tuple(out)

    stats0 = tuple((jnp.full((1, lanes), NEG, F32), jnp.zeros((1, lanes), F32)) for _ in range(HEADS))
    stats = _pipelined_sweep(0, i + 1, qk, lambda kt, src, st: process(kt, src, False, st),
                             lambda kt, src, st: process(kt, src, True, st), stats0, sa_ref, sb_ref)

    lam = lam_ref[...]
    lam_full = (jnp.exp(jnp.sum(lam[0:1] * lam[1:2], axis=1, keepdims=True))
                - jnp.exp(jnp.sum(lam[2:3] * lam[3:4], axis=1, keepdims=True)) + lam_init)
    outs = []
    for h in range(HEADS):
        r = acc_refs[h][...] / stats[h][1]
        d = r[:, 0:tq] - lam_full * r[:, tq:lanes]
        d = d * lax.rsqrt(jnp.mean(d * d, axis=0, keepdims=True) + EPS)
        outs.append(d * gain_ref[...] * (1.0 - lam_init))
    o_ref[...] = jnp.concatenate(outs, axis=0).T


def _diff_attention(dqt, dk, dvt, lam, gain_col, B, S, tq, layer_idx):
    nq = S // tq
    lam_init = 0.8 - 0.6 * math.exp(-0.3 * layer_idx)
    kern = functools.partial(_diff_kernel, tq=tq, lam_init=lam_init)
    scores = pltpu.VMEM((HEADS, tq, 2 * tq), F32)
    return pl.pallas_call(
        kern,
        grid=(B, nq),
        in_specs=[pl.BlockSpec((4, DIFF_QK), lambda b, i: (0, 0)),
                  pl.BlockSpec((GROUP, tq), lambda b, i: (0, b * nq + i)),
                  pl.BlockSpec((S, GROUP), lambda b, i: (b, 0)),
                  pl.BlockSpec((GROUP, S), lambda b, i: (0, b)),
                  pl.BlockSpec((HD, tq), lambda b, i: (0, 0))],
        out_specs=pl.BlockSpec((tq, GROUP), lambda b, i: (b * nq + i, 0)),
        out_shape=jax.ShapeDtypeStruct((B * S, GROUP), F32),
        scratch_shapes=([pltpu.VMEM((HEADS, GROUP, 2 * tq), BF16), scores, scores]
                        + [pltpu.VMEM((HD, 2 * tq), F32)] * HEADS),
        compiler_params=_params("parallel", "arbitrary"),
        name="diff_attention",
    )(lam, dqt, dk, dvt, gain_col)


_LEVELS = (32, 16, 8, 4, 2, 1)
LIN_GROUP = 8


def _stack_heads(x, lane, group, count):
    zero = jnp.zeros_like(x)
    return jnp.concatenate([jnp.where(lane // group == g, x, zero) for g in range(count)], axis=0)


def _unstack_heads(x4, lane, rows):
    out = jnp.where(lane // HD == 0, x4[0:rows], 0.0)
    for h in range(1, HEADS):
        out = out + jnp.where(lane // HD == h, x4[h * rows:(h + 1) * rows], 0.0)
    return out


def _linear_consts():
    r = np.arange(CHUNK)[:, None]
    t = np.arange(CHUNK)[None, :]
    tri = (t <= r).astype(np.float32)
    masks = [(r // (2 * s) == t // (2 * s)) for s in _LEVELS] + [r == t]
    mall = np.stack([np.tile(m.astype(np.float32), (1, HEADS)) for m in masks])
    return jnp.asarray(tri, BF16), jnp.asarray(mall, F32)


def _level_exponent(s, lg, b, row):
    if s == 1:
        return jnp.where((row & 1) != 0, 0.0, pltpu.roll(lg, CHUNK - 1, 0))
    if s == 2:
        nxt1 = pltpu.roll(lg, CHUNK - 1, 0)
        nxt2 = pltpu.roll(lg, CHUNK - 2, 0)
        r4 = row & 3
        return jnp.where(r4 == 0, nxt1 + nxt2, jnp.where(r4 == 1, nxt1, jnp.where(r4 == 2, 0.0, lg)))
    mids = [jnp.broadcast_to(b[m:m + 1, :], (2 * s, b.shape[1])) for m in range(s, CHUNK, 2 * s)]
    d = b - (jnp.concatenate(mids, axis=0) if len(mids) > 1 else mids[0])
    return jnp.where((row & s) != 0, d, -d)


def _linear_chunks(items, tri, mall_ref):
    bs = []
    for q, k, v, lg, state_ref, dk in items:
        hi, mid, lo = _split3(lg)
        bs.append(_dot(tri, hi) + _dot(tri, mid) + _dot(tri, lo))
    ats = []
    for (q, k, v, lg, state_ref, dk), b in zip(items, bs):
        dkh = HEADS * dk
        row = lax.broadcasted_iota(jnp.int32, (CHUNK, dkh), 0)
        lane_k = lax.broadcasted_iota(jnp.int32, (CHUNK, dkh), 1)
        a_t = mall_ref[len(_LEVELS)] * _dot_nt(k.astype(BF16), _stack_heads(q.astype(BF16), lane_k, dk, HEADS))
        for li, s in enumerate(_LEVELS):
            e = jnp.exp(_level_exponent(s, lg, b, row))
            upper = (row & s) != 0
            qt = jnp.where(upper, q * e, 0.0).astype(BF16)
            kt = jnp.where(upper, 0.0, k * e).astype(BF16)
            a_t = a_t + mall_ref[li] * _dot_nt(kt, _stack_heads(qt, lane_k, dk, HEADS))
        ats.append(a_t)
    lane_v = lax.broadcasted_iota(jnp.int32, (CHUNK, GROUP), 1)
    partial = []
    for (q, k, v, lg, state_ref, dk), b, a_t in zip(items, bs, ats):
        v_bf = v.astype(BF16)
        o_intra = _unstack_heads(_dot_tn(a_t.astype(BF16), v_bf), lane_v, CHUNK)
        e_b = jnp.exp(b)
        e_u = jnp.exp(b[CHUNK - 1:CHUNK, :] - b)
        kv = _dot_tn(v_bf, (k * e_u).astype(BF16))
        partial.append((o_intra, e_b, kv))
    outs = []
    for (q, k, v, lg, state_ref, dk), (o_intra, e_b, kv) in zip(items, partial):
        dkh = HEADS * dk
        st = state_ref[...]
        o_inter = _dot_nt((q * e_b).astype(BF16), st.astype(BF16))
        srow = lax.broadcasted_iota(jnp.int32, (GROUP, dkh), 0)
        scol = lax.broadcasted_iota(jnp.int32, (GROUP, dkh), 1)
        state_ref[...] = st * e_b[CHUNK - 1:CHUNK, :] + jnp.where(srow // HD == scol // dk, kv, 0.0)
        outs.append(o_inter + o_intra)
    return outs


def _linear_kernel(gq_ref, gk_ref, gv_ref, lr_ref, gog_ref, w2_ref, b_ref, ggain_ref,
                   rq_ref, rf_ref, ri_ref, rog_ref, lbl_ref, rgain_ref, tri_ref, mall_ref, ones64_ref,
                   og_ref, or_ref, gstate_ref, rstate_ref, *, tm, layer_idx):
    @pl.when(pl.program_id(1) == 0)
    def _():
        gstate_ref[...] = jnp.zeros_like(gstate_ref)
        rstate_ref[...] = jnp.zeros_like(rstate_ref)

    tri = tri_ref[...]
    ones64 = ones64_ref[...]
    w_hi, w_lo = _split2(w2_ref[...])
    logits = lbl_ref[...]
    ez = jnp.exp(logits - jnp.max(logits, axis=0, keepdims=True))
    probs = ez / jnp.sum(ez, axis=0, keepdims=True)
    lb = jnp.zeros((1, GROUP), F32)
    for j in range(1, layer_idx + 1):
        lb = lb + probs[j:j + 1]

    group = math.gcd(LIN_GROUP, tm // CHUNK)
    f32 = lambda ref, rs: ref[rs, :].astype(F32)

    def body(c, carry):
        items, sinks = [], []
        for g in range(group):
            rs = pl.ds(pl.multiple_of((c * group + g) * CHUNK, CHUNK), CHUNK)
            lr_hi, lr_lo = _split2(lr_ref[rs, :])
            x = _dot(lr_hi, w_hi) + _dot(lr_lo, w_hi) + _dot(lr_hi, w_lo) + b_ref[...]
            lg = (jnp.minimum(x, 0.0) - jnp.log(1.0 + jnp.exp(-jnp.abs(x)))) * (1.0 / GLA_TAU)
            items.append((f32(gq_ref, rs) * (GLA_DK ** -0.5), f32(gk_ref, rs), f32(gv_ref, rs), lg,
                          gstate_ref, GLA_DK))
            sinks.append((og_ref, ggain_ref, gog_ref, rs))
            z = rf_ref[rs, :]
            f = lb + (1.0 - lb) * _sigmoid(z)
            items.append((f32(rq_ref, rs), (1.0 - lb) * _sigmoid(-z), f32(ri_ref, rs), jnp.log(f), rstate_ref, HD))
            sinks.append((or_ref, rgain_ref, rog_ref, rs))
        for o, (out_ref, gain_ref, gate_ref, rs) in zip(_linear_chunks(items, tri, mall_ref), sinks):
            out_ref[rs, :] = _group_rms(o, ones64, HD) * gain_ref[...] * _silu(f32(gate_ref, rs))
        return carry

    lax.fori_loop(0, tm // (CHUNK * group), body, 0)


def _linear_mixers(u, u32, w2pad, b_gate, gla_gain, lb_logits, hgrn_gain, B, S, tm, layer_idx):
    nt = S // tm
    tri, mall = _linear_consts()
    ones64 = _block_ones(GROUP, HD)
    c256 = lambda name: pl.BlockSpec((tm, GROUP), lambda b, i, c=_B256[name]: (b * nt + i, c))
    c128 = lambda name: pl.BlockSpec((tm, LANE), lambda b, i, c=_B128[name]: (b * nt + i, c))
    f_lr = pl.BlockSpec((tm, LANE), lambda b, i, c=_F128["g_lr"]: (b * nt + i, c))
    f_rf = pl.BlockSpec((tm, GROUP), lambda b, i, c=_F256["r_f"]: (b * nt + i, c))
    full = lambda shape: pl.BlockSpec(shape, lambda b, i: (0,) * len(shape))
    out_spec = pl.BlockSpec((tm, GROUP), lambda b, i: (b * nt + i, 0))
    out_shape = jax.ShapeDtypeStruct((B * S, GROUP), F32)
    depth = lb_logits.shape[0]
    return pl.pallas_call(
        functools.partial(_linear_kernel, tm=tm, layer_idx=layer_idx),
        grid=(B, nt),
        in_specs=[c128("g_q"), c128("g_k"), c256("g_v"), f_lr, c256("g_og"),
                  full((LANE, LANE)), full((1, LANE)), full((1, GROUP)),
                  c256("r_q"), f_rf, c256("r_i"), c256("r_og"), full((depth, GROUP)), full((1, GROUP)),
                  full(tri.shape), full(mall.shape), full((GROUP, GROUP))],
        out_specs=[out_spec, out_spec], out_shape=[out_shape, out_shape],
        scratch_shapes=[pltpu.VMEM((GROUP, HEADS * GLA_DK), F32), pltpu.VMEM((GROUP, GROUP), F32)],
        compiler_params=_params("parallel", "arbitrary"),
        name="linear_mixers",
    )(u, u, u, u32, u, w2pad, b_gate, gla_gain, u, u32, u, u, lb_logits, hgrn_gain, tri, mall, ones64)


def _mix_residual(h_ref, part_refs, wout_ref):
    x = h_ref[...]
    for j, r in enumerate(part_refs):
        x = x + _dot(r[...].astype(BF16), wout_ref[j * GROUP:(j + 1) * GROUP, :])
    return x


def _ffn_kernel(h_ref, a_ref, b_ref, c_ref, d_ref, wout_ref, g_ref, wg_ref, wu_ref, wd_ref, o_ref,
                xn_ref, acc_ref):
    f = pl.program_id(1)

    @pl.when(f == 0)
    def _():
        x = _mix_residual(h_ref, (a_ref, b_ref, c_ref, d_ref), wout_ref)
        y = x * lax.rsqrt(jnp.mean(x * x, axis=-1, keepdims=True) + EPS) * g_ref[...]
        xn_ref[...] = y.astype(BF16)
        acc_ref[...] = x

    xn = xn_ref[...]
    mid = _silu(_dot(xn, wg_ref[...])) * _dot(xn, wu_ref[...])
    acc_ref[...] += _dot(mid.astype(BF16), wd_ref[...])

    @pl.when(f == pl.num_programs(1) - 1)
    def _():
        o_ref[...] = acc_ref[...]


def _ffn(h, parts, w_out, gain, wg, wu, wd, tm, tf):
    T = h.shape[0]
    F = wg.shape[1]
    part = pl.BlockSpec((tm, GROUP), lambda i, f: (i, 0))
    return pl.pallas_call(
        _ffn_kernel,
        grid=(T // tm, F // tf),
        in_specs=[pl.BlockSpec((tm, D_MODEL), lambda i, f: (i, 0)), part, part, part, part,
                  pl.BlockSpec((D_MODEL, D_MODEL), lambda i, f: (0, 0)),
                  pl.BlockSpec((1, D_MODEL), lambda i, f: (0, 0)),
                  pl.BlockSpec((D_MODEL, tf), lambda i, f: (0, f)),
                  pl.BlockSpec((D_MODEL, tf), lambda i, f: (0, f)),
                  pl.BlockSpec((tf, D_MODEL), lambda i, f: (f, 0))],
        out_specs=pl.BlockSpec((tm, D_MODEL), lambda i, f: (i, 0)),
        out_shape=jax.ShapeDtypeStruct((T, D_MODEL), F32),
        scratch_shapes=[pltpu.VMEM((tm, D_MODEL), BF16), pltpu.VMEM((tm, D_MODEL), F32)],
        compiler_params=_params("parallel", "arbitrary"),
        name="ffn_swiglu",
    )(h, *parts, w_out, gain.reshape(1, D_MODEL), wg, wu, wd)


def _router_kernel(h_ref, a_ref, b_ref, c4_ref, d_ref, wout_ref, g_ref, r_ref, lower_ref,
                   h1_ref, c_ref, comb_ref, rk_ref, cnt_ref):
    x = _mix_residual(h_ref, (a_ref, b_ref, c4_ref, d_ref), wout_ref)
    h1_ref[...] = x
    y = x * lax.rsqrt(jnp.mean(x * x, axis=-1, keepdims=True) + EPS) * g_ref[...]
    c_ref[...] = y.astype(BF16)
    y_hi, y_lo = _split2(y)
    r_hi, r_lo = _split2(r_ref[...])
    logits = _dot(y_hi, r_hi) + _dot(y_lo, r_hi) + _dot(y_hi, r_lo)
    lane = lax.broadcasted_iota(jnp.int32, logits.shape, 1)
    lane_f = lane.astype(F32)
    logits = jnp.where(lane < N_EXPERTS, logits, LOWEST)
    m1 = jnp.max(logits, axis=1, keepdims=True)
    i1 = jnp.min(jnp.where(logits == m1, lane_f, float(LANE)), axis=1, keepdims=True)
    rest = jnp.where(lane_f == i1, LOWEST, logits)
    m2 = jnp.max(rest, axis=1, keepdims=True)
    i2 = jnp.min(jnp.where(rest == m2, lane_f, float(LANE)), axis=1, keepdims=True)
    e2 = jnp.exp(m2 - m1)
    w1 = 1.0 / (1.0 + e2)
    comb_ref[...] = jnp.where(lane_f == i1, w1, jnp.where(lane_f == i2, e2 * w1, 0.0))
    chosen = (lane_f == i1) | (lane_f == i2)
    sel = jnp.where(chosen, 1.0, 0.0)
    rank = _dot(lower_ref[...], sel.astype(BF16))
    rk_ref[...] = jnp.where(chosen, rank, -1.0).astype(jnp.int32)
    cnt_ref[...] = jnp.broadcast_to(jnp.sum(sel, axis=0, keepdims=True), cnt_ref.shape).astype(jnp.int32)


def _router(h, parts, w_out, gain, router_pad, tm):
    T = h.shape[0]
    i = np.arange(tm)
    lower = jnp.asarray((i[None, :] < i[:, None]).astype(np.float32), BF16)
    sds = jax.ShapeDtypeStruct
    rows = lambda w: pl.BlockSpec((tm, w), lambda i: (i, 0))
    return pl.pallas_call(
        _router_kernel,
        grid=(T // tm,),
        in_specs=[rows(D_MODEL), rows(GROUP), rows(GROUP), rows(GROUP), rows(GROUP),
                  pl.BlockSpec((D_MODEL, D_MODEL), lambda i: (0, 0)),
                  pl.BlockSpec((1, D_MODEL), lambda i: (0, 0)),
                  pl.BlockSpec((D_MODEL, LANE), lambda i: (0, 0)),
                  pl.BlockSpec((tm, tm), lambda i: (0, 0))],
        out_specs=[rows(D_MODEL), rows(D_MODEL), rows(LANE), rows(LANE), pl.BlockSpec((8, LANE), lambda i: (i, 0))],
        out_shape=[sds((T, D_MODEL), F32), sds((T, D_MODEL), BF16), sds((T, LANE), F32), sds((T, LANE), jnp.int32),
                   sds((T // tm * 8, LANE), jnp.int32)],
        compiler_params=_params("parallel"),
        name="moe_router",
    )(h, *parts, w_out, gain.reshape(1, D_MODEL), router_pad, lower)


MOE_CHUNK = 512
MOE_ROWS = 512
MOE_HALF = MOE_ROWS // 2
MOE_TILE = 1024
MOE_ALIGN = 16
MOE_PAD_SEGS = MOE_TILE // MOE_ROWS


def _moe_plan(cnt, T):
    nch = T // MOE_CHUNK
    n_ce = cnt.reshape(nch, 8, LANE)[:, 0, :N_EXPERTS]
    cap = (n_ce + MOE_ALIGN - 1) // MOE_ALIGN * MOE_ALIGN
    tot = jnp.sum(cap, axis=0)
    ptot = (tot + MOE_TILE - 1) // MOE_TILE * MOE_TILE
    start = jnp.cumsum(ptot) - ptot
    lo_ce = start[None, :] + jnp.cumsum(cap, axis=0) - cap
    pad_lo = (start + tot)[:, None] + MOE_ROWS * jnp.arange(MOE_PAD_SEGS, dtype=jnp.int32)[None, :]
    lo = jnp.concatenate([lo_ce.T, pad_lo], axis=1)
    n = jnp.concatenate([n_ce.T, jnp.full((N_EXPERTS, MOE_PAD_SEGS), MOE_ROWS, jnp.int32)], axis=1)
    n_tiles = _moe_tiles(T)
    tile_end = jnp.cumsum(ptot // MOE_TILE)
    j = jnp.arange(n_tiles, dtype=jnp.int32)
    tile_e = jnp.minimum(jnp.sum((tile_end[None, :] <= j[:, None]).astype(jnp.int32), axis=1), N_EXPERTS - 1)
    valid = (j < tile_end[-1]).astype(jnp.int32)
    n_flat = jnp.concatenate([n.reshape(-1), tile_end[-1:] * MOE_PAD_SEGS]).astype(jnp.int32)
    return lo.reshape(-1).astype(jnp.int32), n_flat, tile_e, valid


def _moe_tiles(T):
    nch = T // MOE_CHUNK
    rows = 2 * T + nch * N_EXPERTS * (MOE_ALIGN - 1) + N_EXPERTS * (MOE_TILE - 1)
    return -(-rows // MOE_TILE) + 1


def _moe_gather_kernel(lo_ref, n_ref, c_ref, rkt_ref, x_hbm, buf, sem, *, nch):
    e = pl.program_id(0)
    c = pl.program_id(1)
    n_steps = pl.num_programs(0) * pl.num_programs(1)
    step = e * pl.num_programs(1) + c
    slot = step % 2
    rk_row = rkt_ref[pl.ds(e, 1), :]
    rk_row = jnp.where(c < nch, rk_row, -1)
    row = lax.broadcasted_iota(jnp.int32, (MOE_HALF, MOE_CHUNK), 0)
    chunk = c_ref[...]
    onehot = jnp.where(rk_row == row, 1.0, 0.0).astype(BF16)
    buf[slot, 0:MOE_HALF, :] = _dot(onehot, chunk).astype(BF16)

    @pl.when(n_ref[step] > MOE_HALF)
    def _():
        onehot = jnp.where(rk_row == row + MOE_HALF, 1.0, 0.0).astype(BF16)
        buf[slot, MOE_HALF:MOE_ROWS, :] = _dot(onehot, chunk).astype(BF16)

    def copy(k, s, rows):
        dst = x_hbm.at[pl.ds(pl.multiple_of(lo_ref[k], MOE_ALIGN), rows)]
        return pltpu.make_async_copy(buf.at[s, 0:rows], dst, sem.at[s])

    def for_size(k, fn):
        @pl.when(n_ref[k] > MOE_HALF)
        def _():
            fn(MOE_ROWS)

        @pl.when(n_ref[k] <= MOE_HALF)
        def _():
            fn(MOE_HALF)

    @pl.when(step > 0)
    def _():
        for_size(step - 1, lambda rows: copy(step - 1, 1 - slot, rows).wait())

    for_size(step, lambda rows: copy(step, slot, rows).start())

    @pl.when(step == n_steps - 1)
    def _():
        for_size(step, lambda rows: copy(step, slot, rows).wait())
        buf[0] = jnp.zeros((MOE_ROWS, D_MODEL), BF16)

        def fill(k, carry):
            dst = x_hbm.at[pl.ds(pl.multiple_of(k * MOE_ROWS, MOE_ROWS), MOE_ROWS)]
            cp = pltpu.make_async_copy(buf.at[0], dst, sem.at[0])
            cp.start()
            cp.wait()
            return carry

        lax.fori_loop(n_ref[n_steps], x_hbm.shape[0] // MOE_ROWS, fill, 0)


def _moe_gather(c_bf, rkt, lo, n, n_tiles):
    T = c_bf.shape[0]
    nch = T // MOE_CHUNK
    last = nch - 1
    grid_spec = pltpu.PrefetchScalarGridSpec(
        num_scalar_prefetch=2,
        grid=(N_EXPERTS, nch + MOE_PAD_SEGS),
        in_specs=[pl.BlockSpec((MOE_CHUNK, D_MODEL), lambda e, c, lo, n: (jnp.minimum(c, last), 0)),
                  pl.BlockSpec((N_EXPERTS, MOE_CHUNK), lambda e, c, lo, n: (0, jnp.minimum(c, last)))],
        out_specs=pl.BlockSpec(memory_space=pl.ANY),
        scratch_shapes=[pltpu.VMEM((2, MOE_ROWS, D_MODEL), BF16), pltpu.SemaphoreType.DMA((2,))],
    )
    return pl.pallas_call(
        functools.partial(_moe_gather_kernel, nch=nch),
        grid_spec=grid_spec,
        out_shape=jax.ShapeDtypeStruct((n_tiles * MOE_TILE, D_MODEL), BF16),
        compiler_params=_params("arbitrary", "arbitrary"),
        name="moe_gather",
    )(lo, n, c_bf, rkt)


def _moe_ffn_kernel(te_ref, valid_ref, x_ref, wg_ref, wu_ref, wd_ref, y_ref, acc_ref):
    j = pl.program_id(0)
    f = pl.program_id(1)
    last = pl.num_programs(1) - 1

    @pl.when(f == 0)
    def _():
        acc_ref[...] = jnp.zeros(acc_ref.shape, F32)

    @pl.when(valid_ref[j] == 1)
    def _():
        x = x_ref[...]
        mid = _silu(_dot(x, wg_ref[...])) * _dot(x, wu_ref[...])
        acc_ref[...] += _dot(mid.astype(BF16), wd_ref[...])

    @pl.when(f == last)
    def _():
        y_ref[...] = acc_ref[...].astype(BF16)


def _moe_ffn(x_sorted, tile_e, valid, wg, wu, wd, tf):
    n_tiles = x_sorted.shape[0] // MOE_TILE
    F = wg.shape[2]
    nf = F // tf
    sq = pl.Squeezed()
    fsel = lambda j, f, te, va: jnp.where(va[j] == 1, f, nf - 1)
    grid_spec = pltpu.PrefetchScalarGridSpec(
        num_scalar_prefetch=2,
        grid=(n_tiles, nf),
        in_specs=[pl.BlockSpec((MOE_TILE, D_MODEL), lambda j, f, te, va: (j, 0)),
                  pl.BlockSpec((sq, D_MODEL, tf), lambda j, f, te, va: (te[j], 0, fsel(j, f, te, va))),
                  pl.BlockSpec((sq, D_MODEL, tf), lambda j, f, te, va: (te[j], 0, fsel(j, f, te, va))),
                  pl.BlockSpec((sq, tf, D_MODEL), lambda j, f, te, va: (te[j], fsel(j, f, te, va), 0))],
        out_specs=pl.BlockSpec((MOE_TILE, D_MODEL), lambda j, f, te, va: (j, 0)),
        scratch_shapes=[pltpu.VMEM((MOE_TILE, D_MODEL), F32)],
    )
    return pl.pallas_call(
        _moe_ffn_kernel,
        grid_spec=grid_spec,
        out_shape=jax.ShapeDtypeStruct((n_tiles * MOE_TILE, D_MODEL), BF16),
        compiler_params=_params("parallel", "arbitrary"),
        name="moe_experts",
    )(tile_e, valid, x_sorted, wg, wu, wd)


def _ple_update(x, p, gain, w_gate, w_proj):
    y = x * lax.rsqrt(jnp.mean(x * x, axis=-1, keepdims=True) + EPS) * gain
    gate = _sigmoid(_dot(y.astype(BF16), w_gate))
    return x + _dot(p.astype(BF16), w_proj) * gate


def _moe_combine_kernel(lo_ref, n_ref, h_ref, rk_ref, comb_ref, p_ref, g_ref, wgate_ref, wproj_ref, y_hbm,
                        o_ref, ybuf, sem, *, nch):
    c = pl.program_id(0)
    seg = lambda e: e * (nch + MOE_PAD_SEGS) + c

    def copy(e, rows):
        src = y_hbm.at[pl.ds(pl.multiple_of(lo_ref[seg(e)], MOE_ALIGN), rows)]
        return pltpu.make_async_copy(src, ybuf.at[e, 0:rows], sem.at[e])

    def for_size(e, fn):
        @pl.when(n_ref[seg(e)] > MOE_HALF)
        def _():
            fn(MOE_ROWS)

        @pl.when(n_ref[seg(e)] <= MOE_HALF)
        def _():
            fn(MOE_HALF)

    for e in range(N_EXPERTS):
        for_size(e, lambda rows, e=e: copy(e, rows).start())
    o_ref[...] = h_ref[...]
    rk = rk_ref[...]
    comb = comb_ref[...]
    lane = lax.broadcasted_iota(jnp.int32, (MOE_CHUNK, MOE_HALF), 1)
    for e in range(N_EXPERTS):
        for_size(e, lambda rows, e=e: copy(e, rows).wait())
        rank_col = rk[:, e:e + 1]
        w_col = comb[:, e:e + 1]
        onehot = jnp.where(rank_col == lane, 1.0, 0.0).astype(BF16)
        o_ref[...] += w_col * _dot(onehot, ybuf[e, 0:MOE_HALF, :])

        @pl.when(n_ref[seg(e)] > MOE_HALF)
        def _():
            onehot = jnp.where(rank_col == lane + MOE_HALF, 1.0, 0.0).astype(BF16)
            o_ref[...] += w_col * _dot(onehot, ybuf[e, MOE_HALF:MOE_ROWS, :])

    o_ref[...] = _ple_update(o_ref[...], p_ref[...], g_ref[...], wgate_ref[...], wproj_ref[...])


def _moe_combine(h, rk_pad, comb, y_sorted, lo, n, p, layer, ple_gain, ple_wg, ple_wp):
    T = h.shape[0]
    nch = T // MOE_CHUNK
    const = lambda shape: pl.BlockSpec(shape, lambda c, lo, n: (0, 0))
    grid_spec = pltpu.PrefetchScalarGridSpec(
        num_scalar_prefetch=2,
        grid=(nch,),
        in_specs=[pl.BlockSpec((MOE_CHUNK, D_MODEL), lambda c, lo, n: (c, 0)),
                  pl.BlockSpec((MOE_CHUNK, LANE), lambda c, lo, n: (c, 0)),
                  pl.BlockSpec((MOE_CHUNK, LANE), lambda c, lo, n: (c, 0)),
                  pl.BlockSpec((pl.Squeezed(), MOE_CHUNK, PLE_DIM), lambda c, lo, n: (layer, c, 0)),
                  const((1, D_MODEL)), const((D_MODEL, D_MODEL)), const((PLE_DIM, D_MODEL)),
                  pl.BlockSpec(memory_space=pl.ANY)],
        out_specs=pl.BlockSpec((MOE_CHUNK, D_MODEL), lambda c, lo, n: (c, 0)),
        scratch_shapes=[pltpu.VMEM((N_EXPERTS, MOE_ROWS, D_MODEL), BF16), pltpu.SemaphoreType.DMA((N_EXPERTS,))],
    )
    return pl.pallas_call(
        functools.partial(_moe_combine_kernel, nch=nch),
        grid_spec=grid_spec,
        out_shape=jax.ShapeDtypeStruct((T, D_MODEL), F32),
        compiler_params=_params("arbitrary"),
        name="moe_combine",
    )(lo, n, h, rk_pad, comb, p, ple_gain.reshape(1, D_MODEL), ple_wg, ple_wp, y_sorted)


def _moe(h, c_bf, comb, rk, cnt, wg, wu, wd, tf, ple):
    T = h.shape[0]
    lo, n, tile_e, valid = _moe_plan(cnt, T)
    x_sorted = _moe_gather(c_bf, rk[:, :N_EXPERTS].T, lo, n, _moe_tiles(T))
    y_sorted = _moe_ffn(x_sorted, tile_e, valid, wg, wu, wd, tf)
    return _moe_combine(h, rk, comb, y_sorted, lo, n, *ple)


def _ple_kernel(h_ref, p_ref, g_ref, wg_ref, wp_ref, o_ref):
    o_ref[...] = _ple_update(h_ref[...], p_ref[...], g_ref[...], wg_ref[...], wp_ref[...])


def _ple(h, p, layer, gain, wg, wp, tm):
    T = h.shape[0]
    return pl.pallas_call(
        _ple_kernel,
        grid=(T // tm,),
        in_specs=[pl.BlockSpec((tm, D_MODEL), lambda i: (i, 0)),
                  pl.BlockSpec((pl.Squeezed(), tm, PLE_DIM), lambda i: (layer, i, 0)),
                  pl.BlockSpec((1, D_MODEL), lambda i: (0, 0)),
                  pl.BlockSpec((D_MODEL, D_MODEL), lambda i: (0, 0)),
                  pl.BlockSpec((PLE_DIM, D_MODEL), lambda i: (0, 0))],
        out_specs=pl.BlockSpec((tm, D_MODEL), lambda i: (i, 0)),
        out_shape=jax.ShapeDtypeStruct((T, D_MODEL), F32),
        compiler_params=_params("parallel"),
        name="ple_gate",
    )(h, p, gain.reshape(1, D_MODEL), wg, wp)


def _tiles(T, S):
    pick = lambda n, pref: max(t for t in pref if n % t == 0)
    return dict(
        proj_m=pick(T, (512, 256, 128)),
        prep_m=pick(T, (512, 256, 128)),
        attn_q=256, attn_k=pick(S, (512, 256)),
        nsa_q=pick(S, (512, 256)), nsa_k=pick(S, (512, 256)),
        lin_m=pick(S, (512, 256, 128, 64)),
        row_m=pick(T, (512, 256, 128)),
        ffn_m=pick(T, (1024, 512, 256, 128)), ffn_f=512, moe_f=D_FF // 4,
    )


def kernel(x, p, norm_attn, w_in, w_out, nsa_cmp_pos, nsa_cmp_w1, nsa_cmp_w2, nsa_qk_gain, diff_qk_gain, diff_lambda, diff_norm, gla_w_gate2, gla_b_gate, gla_norm, hgrn_lb_logits, hgrn_norm, norm_ffn, ffn_w_gate, ffn_w_up, ffn_w_down, moe_router, moe_w_gate, moe_w_up, moe_w_down, ple_norm, ple_w_gate, ple_w_proj):
    B, S, _ = x.shape
    depth = w_in.shape[0]
    T = B * S
    t = _tiles(T, S)
    cols =jnp.asarray(np.maximum(_COLS, 0), jnp.int32)
    col_mask = jnp.asarray(_COLS >= 0)
    ones_row = jnp.ones((GROUP,), F32)

    h = x.reshape(T, D_MODEL)
    for i in range(depth):
        w_in_r = jnp.where(col_mask[None, :], jnp.take(w_in[i], cols, axis=1), 0.0).astype(BF16)
        gains = jnp.stack([
            jnp.tile(nsa_qk_gain[i, 0], HEADS) * (HD ** -0.5 * LOG2E),
            jnp.tile(diff_qk_gain[i, 0], 2 * HEADS) * (DIFF_QK ** -0.5 * LOG2E),
            jnp.tile(diff_qk_gain[i, 1], 2 * HEADS),
            jnp.tile(nsa_qk_gain[i, 2], HEADS),
            jnp.tile(nsa_qk_gain[i, 3], HEADS),
            ones_row, ones_row, ones_row])
        pos = jnp.transpose(nsa_cmp_pos[i], (1, 0, 2)).reshape(2, NSA_CMP_STRIDE, LANE)
        w1r = nsa_cmp_w1[i].reshape(2, NSA_CMP_LEN, HD, NSA_CMP_HIDDEN)
        zeros = jnp.zeros_like(w1r[0])
        w1_bd = jnp.concatenate([jnp.concatenate([w1r[0], zeros], axis=-1),
                                 jnp.concatenate([zeros, w1r[1]], axis=-1)], axis=1)
        w1_bd = w1_bd.reshape(2, NSA_CMP_STRIDE, LANE, 2 * NSA_CMP_HIDDEN).astype(BF16)
        w2p = jnp.pad(nsa_cmp_w2[i], ((0, 0), (0, 0), (0, LANE - HD))).astype(BF16)
        kc_gain = jnp.pad(nsa_qk_gain[i, 1], (0, LANE - HD)).reshape(1, LANE)
        w2pad = jnp.zeros((LANE, LANE), F32).at[:GLA_RANK].set(gla_w_gate2[i])
        diff_gain_col = jnp.broadcast_to(diff_norm[i][:, None], (HD, t["attn_q"]))

        u, u32 = _norm_matmul(h, norm_attn[i], w_in_r, t["proj_m"])
        nqt, dqt, dk, dvt, kvs, vst, kvw, vwt = _prep(u, gains, t["prep_m"])
        kc, vct = _compress(u32, pos, w1_bd, w2p, kc_gain, B, S)
        o_a = _nsa_attention(u32, nqt, kc, vct, kvs, vst, kvw, vwt, B, S, t["nsa_q"], t["nsa_k"], t["nsa_q"])
        o_b = _diff_attention(dqt, dk, dvt, diff_lambda[i], diff_gain_col, B, S, t["attn_q"], i)
        o_c, o_d = _linear_mixers(u, u32, w2pad, gla_b_gate[i].reshape(1, LANE),
                                  jnp.tile(gla_norm[i], HEADS).reshape(1, GROUP), hgrn_lb_logits,
                                  jnp.tile(hgrn_norm[i], HEADS).reshape(1, GROUP), B, S, t["lin_m"], i)
        parts = (o_a, o_b, o_c, o_d)
        w_out_bf = w_out[i].astype(BF16)
        ple = (p.reshape(depth, T, PLE_DIM), i, ple_norm[i], ple_w_gate[i].astype(BF16), ple_w_proj[i].astype(BF16))
        j = i // 2
        if i % 2 == 0:
            h = _ffn(h, parts, w_out_bf, norm_ffn[i], ffn_w_gate[j].astype(BF16), ffn_w_up[j].astype(BF16),
                     ffn_w_down[j].astype(BF16), t["ffn_m"], t["ffn_f"])
            h = _ple(h, *ple, t["row_m"])
        else:
            router_pad = jnp.zeros((D_MODEL, LANE), F32).at[:, :N_EXPERTS].set(moe_router[j])
            h, c_bf, comb, rk, cnt = _router(h, parts, w_out_bf, norm_ffn[i], router_pad, MOE_CHUNK)
            h = _moe(h, c_bf, comb, rk, cnt, moe_w_gate[j].astype(BF16), moe_w_up[j].astype(BF16),
                     moe_w_down[j].astype(BF16), t["moe_f"], ple)
    return h.reshape(B, S, D_MODEL)
```

```python
import functools
import math

import numpy as np
import jax
import jax.numpy as jnp
from jax import lax
from jax.experimental import pallas as pl
from jax.experimental.pallas import tpu as pltpu

F32 = jnp.float32
BF16 = jnp.bfloat16

D_MODEL = 1024
HEADS = 4
HD = 64
GROUP = HEADS * HD
NSA_CMP_LEN = 32
NSA_CMP_STRIDE = 16
NSA_CMP_HIDDEN = 4 * HD
NSA_SEL_LEN = 64
NSA_TOPK = 16
NSA_WINDOW = 512
DIFF_QK = HD // 2
GLA_DK = HD // 2
GLA_RANK = 16
GLA_TAU = 16.0
CHUNK = 64
D_FF = 7 * D_MODEL // 2
N_EXPERTS = 8
PLE_DIM = 256
EPS = 1e-6
NEG = -1e30
BIG = 1e30
LOWEST = -3.0e38
MASK_BIG = 2.0 ** 100
LOG2E = 1.4426950408889634

VMEM_LIMIT = 52 * 1024 * 1024
LANE = 128
SUBLANES = 8

_SRC = dict(nsa_q=0, k_cmp=256, v_cmp=320, k_slc=384, v_slc=448, k_win=512, v_win=576, nsa_g=640,
            d_q=652, d_k=908, d_v=1164, g_q=1420, g_k=1548, g_v=1676, g_lr=1932, g_og=1948,
            r_q=2204, r_f=2460, r_i=2716, r_og=2972)

_B256 = dict(nsa_q=0, d_q=1, d_k=2, d_v=3, g_v=4, g_og=5, r_q=6, r_i=7, r_og=8)
_B128 = dict(g_q=18, g_k=19, kvslc=20, kvwin=21)
NC16 = 22 * 128
_F256 = dict(r_f=0)
_F128 = dict(g_lr=2, nsa_g=3, kvcmp=4)
NC32 = 5 * 128
NC = NC16 + NC32


def _column_map():
    cols = -np.ones((NC,), np.int64)

    def put(dst, src, width):
        cols[dst:dst + width] = np.arange(src, src + width)

    for name in _B256:
        put(_B256[name] * 256, _SRC[name], 256)
    put(_B128["g_q"] * 128, _SRC["g_q"], 128)
    put(_B128["g_k"] * 128, _SRC["g_k"], 128)
    put(_B128["kvslc"] * 128, _SRC["k_slc"], 128)
    put(_B128["kvwin"] * 128, _SRC["k_win"], 128)
    put(NC16 + _F256["r_f"] * 256, _SRC["r_f"], 256)
    put(NC16 + _F128["g_lr"] * 128, _SRC["g_lr"], GLA_RANK)
    put(NC16 + _F128["nsa_g"] * 128, _SRC["nsa_g"], 3 * HEADS)
    put(NC16 + _F128["kvcmp"] * 128, _SRC["k_cmp"], 128)
    return cols


_COLS = _column_map()


def _dot(a, b):
    return jnp.dot(a, b, preferred_element_type=F32)


def _dot_nt(a, b):
    return lax.dot_general(a, b, (((1,), (1,)), ((), ())), preferred_element_type=F32)


def _dot_tn(a, b):
    return lax.dot_general(a, b, (((0,), (0,)), ((), ())), preferred_element_type=F32)


def _split2(x):
    hi = x.astype(BF16)
    lo = (x - hi.astype(F32)).astype(BF16)
    return hi, lo


def _split3(x):
    hi = x.astype(BF16)
    r = x - hi.astype(F32)
    mid = r.astype(BF16)
    lo = (r - mid.astype(F32)).astype(BF16)
    return hi, mid, lo


def _group_mean(x, ones_bf, group):
    hi, lo = _split2(x)
    return (_dot(hi, ones_bf) + _dot(lo, ones_bf)) * (1.0 / group)


def _group_rms(x, ones_bf, group):
    return x * lax.rsqrt(_group_mean(x * x, ones_bf, group) + EPS)


def _sigmoid(x):
    return 1.0 / (1.0 + jnp.exp(-x))


def _silu(x):
    return x * _sigmoid(x)


def _params(*sem):
    return pltpu.CompilerParams(dimension_semantics=sem, vmem_limit_bytes=VMEM_LIMIT)


def _block_ones(n, group):
    i = np.arange(n)
    return jnp.asarray((i[:, None] // group == i[None, :] // group).astype(np.float32), BF16)


def _norm_matmul_kernel(x_ref, g_ref, w_ref, o16_ref, o32_ref):
    x = x_ref[...]
    y = x * lax.rsqrt(jnp.mean(x * x, axis=-1, keepdims=True) + EPS) * g_ref[...]
    u = _dot(y.astype(BF16), w_ref[...])
    o16_ref[...] = u[:, 0:NC16].astype(BF16)
    o32_ref[...] = u[:, NC16:NC]


def _norm_matmul(x, gain, w_bf, tm):
    T, K = x.shape
    return pl.pallas_call(
        _norm_matmul_kernel,
        grid=(T // tm,),
        in_specs=[pl.BlockSpec((tm, K), lambda i: (i, 0)),
                  pl.BlockSpec((1, K), lambda i: (0, 0)),
                  pl.BlockSpec((K, NC), lambda i: (0, 0))],
        out_specs=[pl.BlockSpec((tm, NC16), lambda i: (i, 0)), pl.BlockSpec((tm, NC32), lambda i: (i, 0))],
        out_shape=[jax.ShapeDtypeStruct((T, NC16), BF16), jax.ShapeDtypeStruct((T, NC32), F32)],
        compiler_params=_params("parallel"),
        name="in_proj",
    )(x, gain.reshape(1, K), w_bf)


def _prep_kernel(nq_ref, dq_ref, dk_ref, dv_ref, kvs_ref, kvw_ref, gains_ref, ones64_ref, ones32_ref,
                 ones64h_ref, o_nq, o_dq, o_dk, o_dv, o_kvs, o_vs, o_kvw, o_vw):
    ones64 = ones64_ref[...]
    ones32 = ones32_ref[...]
    ones64h = ones64h_ref[...]
    f32 = lambda ref: ref[...].astype(F32)
    o_nq[...] = (_group_rms(f32(nq_ref), ones64, HD) * gains_ref[0:1, :]).T.astype(BF16)
    o_dq[...] = (_group_rms(f32(dq_ref), ones32, DIFF_QK) * gains_ref[1:2, :]).T.astype(BF16)
    o_dk[...] = (_group_rms(f32(dk_ref), ones32, DIFF_QK) * gains_ref[2:3, :]).astype(BF16)
    o_dv[...] = f32(dv_ref).T.astype(BF16)
    lane = lax.broadcasted_iota(jnp.int32, kvs_ref.shape, 1)
    for kv_ref, gain, o_kv, o_v in ((kvs_ref, gains_ref[3:4, 0:LANE], o_kvs, o_vs),
                                    (kvw_ref, gains_ref[4:5, 0:LANE], o_kvw, o_vw)):
        x = f32(kv_ref)
        o_kv[...] = jnp.where(lane < HD, _group_rms(x, ones64h, HD) * gain, x).astype(BF16)
        o_v[...] = x.T[HD:2 * HD, :].astype(BF16)


def _prep(u, gains, tm):
    T = u.shape[0]
    c256 = lambda name: pl.BlockSpec((tm, GROUP), lambda i, c=_B256[name]: (i, c))
    c128 = lambda name: pl.BlockSpec((tm, LANE), lambda i, c=_B128[name]: (i, c))
    const = lambda shape: pl.BlockSpec(shape, lambda i: (0, 0))
    rows = lambda w: pl.BlockSpec((tm, w), lambda i: (i, 0))
    colsT = lambda h: pl.BlockSpec((h, tm), lambda i: (0, i))
    sds = jax.ShapeDtypeStruct
    return pl.pallas_call(
        _prep_kernel,
        grid=(T // tm,),
        in_specs=[c256("nsa_q"), c256("d_q"), c256("d_k"), c256("d_v"), c128("kvslc"), c128("kvwin"),
                  const((8, GROUP)), const((GROUP, GROUP)), const((GROUP, GROUP)), const((LANE, LANE))],
        out_specs=[colsT(GROUP), colsT(GROUP), rows(GROUP), colsT(GROUP),
                   rows(LANE), colsT(HD), rows(LANE), colsT(HD)],
        out_shape=[sds((GROUP, T), BF16), sds((GROUP, T), BF16), sds((T, GROUP), BF16), sds((GROUP, T), BF16),
                   sds((T, LANE), BF16), sds((HD, T), BF16), sds((T, LANE), BF16), sds((HD, T), BF16)],
        compiler_params=_params("parallel"),
        name="attn_prep",
    )(u, u, u, u, u, u, gains, _block_ones(GROUP, HD), _block_ones(GROUP, DIFF_QK), _block_ones(LANE, HD))


def _compress_kernel(kv_ref, pos_ref, w1_ref, w2_ref, gain_ref, ones64h_ref, kc_ref, vc_ref):
    ncp = kc_ref.shape[0]
    hid = NSA_CMP_HIDDEN
    top = jnp.zeros((ncp, 2 * hid), F32)
    bot = jnp.zeros((ncp, 2 * hid), F32)
    for l in range(NSA_CMP_STRIDE):
        x = kv_ref[pl.ds(l, ncp, stride=NSA_CMP_STRIDE), :]
        top = top + _dot((x + pos_ref[0, l:l + 1, :]).astype(BF16), w1_ref[0, l])
        bot = bot + _dot((x + pos_ref[1, l:l + 1, :]).astype(BF16), w1_ref[1, l])
    hidden = top + pltpu.roll(bot, ncp - 1, 0)
    act = _silu(hidden).astype(BF16)
    kc = _dot(act[:, 0:hid], w2_ref[0])
    kc_ref[...] = _group_rms(kc, ones64h_ref[...], HD) * gain_ref[...]
    vc_ref[...] = _dot(act[:, hid:2 * hid], w2_ref[1]).T[0:HD, :].astype(BF16)


def _compress(u, pos, w1_bd, w2p_bf, gain_row, B, S):
    ncp = S // NSA_CMP_STRIDE
    sq = pl.Squeezed()
    full = lambda shape: pl.BlockSpec(shape, lambda b: (0,) * len(shape))
    return pl.pallas_call(
        _compress_kernel,
        grid=(B,),
        in_specs=[pl.BlockSpec((S, LANE), lambda b, c=_F128["kvcmp"]: (b, c)),
                  full((2, NSA_CMP_STRIDE, LANE)), full((2, NSA_CMP_STRIDE, LANE, 2 * NSA_CMP_HIDDEN)),
                  full((2, NSA_CMP_HIDDEN, LANE)), full((1, LANE)), full((LANE, LANE))],
        out_specs=[pl.BlockSpec((sq, ncp, LANE), lambda b: (b, 0, 0)),
                   pl.BlockSpec((sq, HD, ncp), lambda b: (b, 0, 0))],
        out_shape=[jax.ShapeDtypeStruct((B, ncp, LANE), F32), jax.ShapeDtypeStruct((B, HD, ncp), BF16)],
        compiler_params=_params("parallel"),
        name="nsa_compress",
    )(u, pos, w1_bd, w2p_bf, gain_row, _block_ones(LANE, HD))


def _softmax_step_t(s, v_t, m_old, l_old, acc_ref):
    m_new = jnp.maximum(m_old, jnp.max(s, axis=0, keepdims=True))
    alpha = jnp.exp2(m_old - m_new)
    p = jnp.exp2(s - m_new)
    acc_ref[...] = alpha * acc_ref[...] + _dot(v_t, p.astype(BF16))
    return m_new, alpha * l_old + jnp.sum(p, axis=0, keepdims=True)


def _pipelined_sweep(lo, hi, qk, process, process_last, stats, sa_ref, sb_ref):
    def pair(jj, st):
        t = lo + 2 * jj
        qk(t + 1, sb_ref)
        st = process(t, sa_ref, st)
        qk(t + 2, sa_ref)
        return process(t + 1, sb_ref, st)

    def two_left(st):
        qk(hi - 1, sb_ref)
        st = process(hi - 2, sa_ref, st)
        return process_last(hi - 1, sb_ref, st)

    n = hi - lo
    qk(lo, sa_ref)
    stats = lax.fori_loop(0, (n - 1) // 2, pair, stats)
    return lax.cond(n % 2 == 0, two_left, lambda st: process_last(hi - 1, sa_ref, st), stats)


def _nsa_kernel(qt_ref, g_ref, kc_ref, vct_ref, kvs_ref, vst_ref, kvw_ref, vwt_ref, ovl_ref, exp_ref,
                o_ref, qs_ref, sa_ref, sb_ref, *acc_refs, tq, tk, tw, ksel):
    i = pl.program_id(1)
    t0 = i * tq
    ncp = kc_ref.shape[0]
    nsel = ovl_ref.shape[0]
    acc_s, acc_w = acc_refs[:HEADS], acc_refs[HEADS:]

    qt = qt_ref[...]
    qs_ref[HD:LANE, :] = jnp.zeros((LANE - HD, HEADS * tq), BF16)
    for h in range(HEADS):
        qs_ref[0:HD, h * tq:(h + 1) * tq] = qt[h * HD:(h + 1) * HD, :]
        acc_s[h][...] = jnp.zeros((HD, tq), F32)
        acc_w[h][...] = jnp.zeros((HD, tq), F32)

    kc_hi, kc_lo = _split2(kc_ref[...])
    n_idx = lax.broadcasted_iota(jnp.int32, (ncp, tq), 0)
    t_lane = t0 + lax.broadcasted_iota(jnp.int32, (ncp, tq), 1)
    ok = n_idx * NSA_CMP_STRIDE + (NSA_CMP_LEN - 1) <= t_lane
    cmp_scores = _dot(kc_hi, qs_ref[...]) + _dot(kc_lo, qs_ref[...])
    o_c = []
    psum = jnp.zeros((ncp, tq), F32)
    for h in range(HEADS):
        s = jnp.where(ok, cmp_scores[:, h * tq:(h + 1) * tq], NEG)
        e = jnp.exp2(s - jnp.max(s, axis=0, keepdims=True))
        p = jnp.where(ok, e / jnp.sum(e, axis=0, keepdims=True), 0.0)
        o_c.append(_dot(vct_ref[...], p.astype(BF16)))
        psum = psum + p

    p_hi, p_lo = _split2(psum)
    imp = _dot(ovl_ref[...], p_hi) + _dot(ovl_ref[...], p_lo)
    blk = lax.broadcasted_iota(jnp.int32, (nsel, tq), 0)
    t_col = t0 + lax.broadcasted_iota(jnp.int32, (nsel, tq), 1)
    cur = t_col // NSA_SEL_LEN
    forced = (blk == 0) | (blk == cur) | (blk == cur - 1)
    vals = jnp.where(forced, BIG, jnp.where(blk * NSA_SEL_LEN <= t_col, imp, NEG))
    blk_f = blk.astype(F32)
    sel = jnp.zeros((nsel, tq), F32)
    for _ in range(ksel):
        mx = jnp.max(vals, axis=0, keepdims=True)
        first = jnp.min(jnp.where(vals == mx, blk_f, float(nsel)), axis=0, keepdims=True)
        pick = blk_f == first
        sel = jnp.where(pick, 1.0, sel)
        vals = jnp.where(pick, LOWEST, vals)
    if nsel < LANE:
        sel = jnp.concatenate([sel, jnp.zeros((LANE - nsel, tq), F32)], axis=0)
    sel_bf = sel.astype(BF16)

    stats0 = tuple((jnp.full((1, tq), NEG, F32), jnp.zeros((1, tq), F32)) for _ in range(HEADS))

    def sweep(lo, hi, k_ref, vt_ref, accs, width, bias_fn, last_bias_fn):
        def qk(kt, dst):
            k = k_ref[pl.ds(pl.multiple_of(kt * width, width), width), :]
            for h in range(HEADS):
                dst[h, 0:width, :] = _dot(k, qs_ref[:, h * tq:(h + 1) * tq])

        def process(kt, src, stats, fn):
            k0 = pl.multiple_of(kt * width, width)
            kpos = k0 + lax.broadcasted_iota(jnp.int32, (width, tq), 0)
            tpos = t0 + lax.broadcasted_iota(jnp.int32, (width, tq), 1)
            bias = fn(k0, kpos, tpos)
            vt = vt_ref[:, pl.ds(k0, width)]
            return tuple(_softmax_step_t(src[h, 0:width, :] + bias, vt, stats[h][0], stats[h][1], accs[h])
                         for h in range(HEADS))

        return _pipelined_sweep(lo, hi, qk, functools.partial(process, fn=bias_fn),
                                functools.partial(process, fn=last_bias_fn), stats0, sa_ref, sb_ref)

    def sel_bias(k0, kpos, tpos):
        chosen = _dot(exp_ref[pl.ds(k0, kpos.shape[0]), :], sel_bf)
        return chosen - MASK_BIG

    def sel_bias_diag(k0, kpos, tpos):
        chosen = _dot(exp_ref[pl.ds(k0, kpos.shape[0]), :], sel_bf)
        return jnp.where(kpos <= tpos, chosen - MASK_BIG, NEG)

    st_s = sweep(0, i + 1, kvs_ref, vst_ref, acc_s, tk, sel_bias, sel_bias_diag)

    def win_bias(k0, kpos, tpos):
        return jnp.where((kpos <= tpos) & (kpos > tpos - NSA_WINDOW), 0.0, NEG)

    st_w = sweep(jnp.maximum(t0 - NSA_WINDOW, 0) // tw, (t0 + tq - 1) // tw + 1,
                 kvw_ref, vwt_ref, acc_w, tw, win_bias, win_bias)

    gates = _sigmoid(g_ref[...].T)
    outs = []
    for h in range(HEADS):
        outs.append(gates[3 * h:3 * h + 1, :] * o_c[h]
                    + gates[3 * h + 1:3 * h + 2, :] * (acc_s[h][...] / st_s[h][1])
                    + gates[3 * h + 2:3 * h + 3, :] * (acc_w[h][...] / st_w[h][1]))
    o_ref[...] = jnp.concatenate(outs, axis=0).T


def _nsa_attention(u, nqt, kc, vct, kvs, vst, kvw, vwt, B, S, tq, tk, tw):
    nq = S // tq
    ncp = S // NSA_CMP_STRIDE
    nsel = S // NSA_SEL_LEN
    ksel = min(NSA_TOPK, nsel)
    n_cmp = (S - NSA_CMP_LEN) // NSA_CMP_STRIDE + 1
    cmp_start = np.arange(ncp) * NSA_CMP_STRIDE
    sel_start = np.arange(nsel) * NSA_SEL_LEN
    overlap = ((cmp_start[:, None] <= sel_start[None, :] + NSA_SEL_LEN - 1)
               & (cmp_start[:, None] + NSA_CMP_LEN - 1 >= sel_start[None, :])
               & (np.arange(ncp)[:, None] < n_cmp))
    ovl_t = jnp.asarray(overlap.T.astype(np.float32), BF16)
    assert tq == tk, "the selected sweep treats every key tile before the query tile's own as fully visible"
    expand = jnp.where(jnp.arange(S, dtype=jnp.int32)[:, None] // NSA_SEL_LEN
                       == jnp.arange(LANE, dtype=jnp.int32)[None, :], MASK_BIG, 0.0).astype(BF16)
    sq = pl.Squeezed()
    seq_rows = pl.BlockSpec((S, LANE), lambda b, i: (b, 0))
    seq_cols = pl.BlockSpec((HD, S), lambda b, i: (0, b))
    kern = functools.partial(_nsa_kernel, tq=tq, tk=tk, tw=tw, ksel=ksel)
    return pl.pallas_call(
        kern,
        grid=(B, nq),
        in_specs=[pl.BlockSpec((GROUP, tq), lambda b, i: (0, b * nq + i)),
                  pl.BlockSpec((tq, LANE), lambda b, i, c=_F128["nsa_g"]: (b * nq + i, c)),
                  pl.BlockSpec((sq, ncp, LANE), lambda b, i: (b, 0, 0)),
                  pl.BlockSpec((sq, HD, ncp), lambda b, i: (b, 0, 0)),
                  seq_rows, seq_cols, seq_rows, seq_cols,
                  pl.BlockSpec((nsel, ncp), lambda b, i: (0, 0)),
                  pl.BlockSpec((S, LANE), lambda b, i: (0, 0))],
        out_specs=pl.BlockSpec((tq, GROUP), lambda b, i: (b * nq + i, 0)),
        out_shape=jax.ShapeDtypeStruct((B * S, GROUP), F32),
        scratch_shapes=([pltpu.VMEM((LANE, HEADS * tq), BF16)]
                        + [pltpu.VMEM((HEADS, max(tk, tw), tq), F32)] * 2
                        + [pltpu.VMEM((HD, tq), F32)] * (2 * HEADS)),
        compiler_params=_params("parallel", "arbitrary"),
        name="nsa_attention",
    )(nqt, u, kc, vct, kvs, vst, kvw, vwt, ovl_t, expand)


def _diff_kernel(lam_ref, qt_ref, k_ref, vt_ref, gain_ref, o_ref, qs_ref, sa_ref, sb_ref, *acc_refs,
                 tq, lam_init):
    i = pl.program_id(1)
    t0 = i * tq
    lanes = 2 * tq

    qt = qt_ref[...]
    row = lax.broadcasted_iota(jnp.int32, (GROUP, tq), 0)
    zero = jnp.zeros_like(qt)
    for h in range(HEADS):
        qs_ref[h] = jnp.concatenate([jnp.where(row // DIFF_QK == 2 * h, qt, zero),
                                     jnp.where(row // DIFF_QK == 2 * h + 1, qt, zero)], axis=1)
        acc_refs[h][...] = jnp.zeros((HD, lanes), F32)

    def qk(kt, dst):
        k = k_ref[pl.ds(pl.multiple_of(kt * tq, tq), tq), :]
        for h in range(HEADS):
            dst[h] = _dot(k, qs_ref[h])

    def process(kt, src, masked, stats):
        k0 = pl.multiple_of(kt * tq, tq)
        out = []
        for h in range(HEADS):
            s = src[h]
            if masked:
                kpos = k0 + lax.broadcasted_iota(jnp.int32, (tq, lanes), 0)
                tpos = t0 + lax.broadcasted_iota(jnp.int32, (tq, lanes), 1) % tq
                s = jnp.where(kpos <= tpos, s, NEG)
            out.append(_softmax_step_t(s, vt_ref[h * HD:(h + 1) * HD, pl.ds(k0, tq)],
                                       stats[h][0], stats[h][1], acc_refs[h]))
        return tuple(out)

    stats0 = tuple((jnp.full((1, lanes), NEG, F32), jnp.zeros((1, lanes), F32)) for _ in range(HEADS))
    stats = _pipelined_sweep(0, i + 1, qk, lambda kt, src, st: process(kt, src, False, st),
                             lambda kt, src, st: process(kt, src, True, st), stats0, sa_ref, sb_ref)

    lam = lam_ref[...]
    lam_full = (jnp.exp(jnp.sum(lam[0:1] * lam[1:2], axis=1, keepdims=True))
                - jnp.exp(jnp.sum(lam[2:3] * lam[3:4], axis=1, keepdims=True)) + lam_init)
    outs = []
    for h in range(HEADS):
        r = acc_refs[h][...] / stats[h][1]
        d = r[:, 0:tq] - lam_full * r[:, tq:lanes]
        d = d * lax.rsqrt(jnp.mean(d * d, axis=0, keepdims=True) + EPS)
        outs.append(d * gain_ref[...] * (1.0 - lam_init))
    o_ref[...] = jnp.concatenate(outs, axis=0).T


def _diff_attention(dqt, dk, dvt, lam, gain_col, B, S, tq, layer_idx):
    nq = S // tq
    lam_init = 0.8 - 0.6 * math.exp(-0.3 * layer_idx)
    kern = functools.partial(_diff_kernel, tq=tq, lam_init=lam_init)
    scores = pltpu.VMEM((HEADS, tq, 2 * tq), F32)
    return pl.pallas_call(
        kern,
        grid=(B, nq),
        in_specs=[pl.BlockSpec((4, DIFF_QK), lambda b, i: (0, 0)),
                  pl.BlockSpec((GROUP, tq), lambda b, i: (0, b * nq + i)),
                  pl.BlockSpec((S, GROUP), lambda b, i: (b, 0)),
                  pl.BlockSpec((GROUP, S), lambda b, i: (0, b)),
                  pl.BlockSpec((HD, tq), lambda b, i: (0, 0))],
        out_specs=pl.BlockSpec((tq, GROUP), lambda b, i: (b * nq + i, 0)),
        out_shape=jax.ShapeDtypeStruct((B * S, GROUP), F32),
        scratch_shapes=([pltpu.VMEM((HEADS, GROUP, 2 * tq), BF16), scores, scores]
                        + [pltpu.VMEM((HD, 2 * tq), F32)] * HEADS),
        compiler_params=_params("parallel", "arbitrary"),
        name="diff_attention",
    )(lam, dqt, dk, dvt, gain_col)


_LEVELS = (32, 16, 8, 4, 2, 1)
LIN_GROUP = 8


def _stack_heads(x, lane, group, count):
    zero = jnp.zeros_like(x)
    return jnp.concatenate([jnp.where(lane // group == g, x, zero) for g in range(count)], axis=0)


def _unstack_heads(x4, lane, rows):
    out = jnp.where(lane // HD == 0, x4[0:rows], 0.0)
    for h in range(1, HEADS):
        out = out + jnp.where(lane // HD == h, x4[h * rows:(h + 1) * rows], 0.0)
    return out


def _linear_consts():
    r = np.arange(CHUNK)[:, None]
    t = np.arange(CHUNK)[None, :]
    tri = (t <= r).astype(np.float32)
    masks = [(r // (2 * s) == t // (2 * s)) for s in _LEVELS] + [r == t]
    mall = np.stack([np.tile(m.astype(np.float32), (1, HEADS)) for m in masks])
    return jnp.asarray(tri, BF16), jnp.asarray(mall, F32)


def _level_exponent(s, lg, b, row):
    if s == 1:
        return jnp.where((row & 1) != 0, 0.0, pltpu.roll(lg, CHUNK - 1, 0))
    if s == 2:
        nxt1 = pltpu.roll(lg, CHUNK - 1, 0)
        nxt2 = pltpu.roll(lg, CHUNK - 2, 0)
        r4 = row & 3
        return jnp.where(r4 == 0, nxt1 + nxt2, jnp.where(r4 == 1, nxt1, jnp.where(r4 == 2, 0.0, lg)))
    mids = [jnp.broadcast_to(b[m:m + 1, :], (2 * s, b.shape[1])) for m in range(s, CHUNK, 2 * s)]
    d = b - (jnp.concatenate(mids, axis=0) if len(mids) > 1 else mids[0])
    return jnp.where((row & s) != 0, d, -d)


def _linear_chunks(items, tri, mall_ref):
    bs = []
    for q, k, v, lg, state_ref, dk in items:
        hi, mid, lo = _split3(lg)
        bs.append(_dot(tri, hi) + _dot(tri, mid) + _dot(tri, lo))
    ats = []
    for (q, k, v, lg, state_ref, dk), b in zip(items, bs):
        dkh = HEADS * dk
        row = lax.broadcasted_iota(jnp.int32, (CHUNK, dkh), 0)
        lane_k = lax.broadcasted_iota(jnp.int32, (CHUNK, dkh), 1)
        a_t = mall_ref[len(_LEVELS)] * _dot_nt(k.astype(BF16), _stack_heads(q.astype(BF16), lane_k, dk, HEADS))
        for li, s in enumerate(_LEVELS):
            e = jnp.exp(_level_exponent(s, lg, b, row))
            upper = (row & s) != 0
            qt = jnp.where(upper, q * e, 0.0).astype(BF16)
            kt = jnp.where(upper, 0.0, k * e).astype(BF16)
            a_t = a_t + mall_ref[li] * _dot_nt(kt, _stack_heads(qt, lane_k, dk, HEADS))
        ats.append(a_t)
    lane_v = lax.broadcasted_iota(jnp.int32, (CHUNK, GROUP), 1)
    partial = []
    for (q, k, v, lg, state_ref, dk), b, a_t in zip(items, bs, ats):
        v_bf = v.astype(BF16)
        o_intra = _unstack_heads(_dot_tn(a_t.astype(BF16), v_bf), lane_v, CHUNK)
        e_b = jnp.exp(b)
        e_u = jnp.exp(b[CHUNK - 1:CHUNK, :] - b)
        kv = _dot_tn(v_bf, (k * e_u).astype(BF16))
        partial.append((o_intra, e_b, kv))
    outs = []
    for (q, k, v, lg, state_ref, dk), (o_intra, e_b, kv) in zip(items, partial):
        dkh = HEADS * dk
        st = state_ref[...]
        o_inter = _dot_nt((q * e_b).astype(BF16), st.astype(BF16))
        srow = lax.broadcasted_iota(jnp.int32, (GROUP, dkh), 0)
        scol = lax.broadcasted_iota(jnp.int32, (GROUP, dkh), 1)
        state_ref[...] = st * e_b[CHUNK - 1:CHUNK, :] + jnp.where(srow // HD == scol // dk, kv, 0.0)
        outs.append(o_inter + o_intra)
    return outs


def _linear_kernel(gq_ref, gk_ref, gv_ref, lr_ref, gog_ref, w2_ref, b_ref, ggain_ref,
                   rq_ref, rf_ref, ri_ref, rog_ref, lbl_ref, rgain_ref, tri_ref, mall_ref, ones64_ref,
                   og_ref, or_ref, gstate_ref, rstate_ref, *, tm, layer_idx):
    @pl.when(pl.program_id(1) == 0)
    def _():
        gstate_ref[...] = jnp.zeros_like(gstate_ref)
        rstate_ref[...] = jnp.zeros_like(rstate_ref)

    tri = tri_ref[...]
    ones64 = ones64_ref[...]
    w_hi, w_lo = _split2(w2_ref[...])
    logits = lbl_ref[...]
    ez = jnp.exp(logits - jnp.max(logits, axis=0, keepdims=True))
    probs = ez / jnp.sum(ez, axis=0, keepdims=True)
    lb = jnp.zeros((1, GROUP), F32)
    for j in range(1, layer_idx + 1):
        lb = lb + probs[j:j + 1]

    group = math.gcd(LIN_GROUP, tm // CHUNK)
    f32 = lambda ref, rs: ref[rs, :].astype(F32)

    def body(c, carry):
        items, sinks = [], []
        for g in range(group):
            rs = pl.ds(pl.multiple_of((c * group + g) * CHUNK, CHUNK), CHUNK)
            lr_hi, lr_lo = _split2(lr_ref[rs, :])
            x = _dot(lr_hi, w_hi) + _dot(lr_lo, w_hi) + _dot(lr_hi, w_lo) + b_ref[...]
            lg = (jnp.minimum(x, 0.0) - jnp.log(1.0 + jnp.exp(-jnp.abs(x)))) * (1.0 / GLA_TAU)
            items.append((f32(gq_ref, rs) * (GLA_DK ** -0.5), f32(gk_ref, rs), f32(gv_ref, rs), lg,
                          gstate_ref, GLA_DK))
            sinks.append((og_ref, ggain_ref, gog_ref, rs))
            z = rf_ref[rs, :]
            f = lb + (1.0 - lb) * _sigmoid(z)
            items.append((f32(rq_ref, rs), (1.0 - lb) * _sigmoid(-z), f32(ri_ref, rs), jnp.log(f), rstate_ref, HD))
            sinks.append((or_ref, rgain_ref, rog_ref, rs))
        for o, (out_ref, gain_ref, gate_ref, rs) in zip(_linear_chunks(items, tri, mall_ref), sinks):
            out_ref[rs, :] = _group_rms(o, ones64, HD) * gain_ref[...] * _silu(f32(gate_ref, rs))
        return carry

    lax.fori_loop(0, tm // (CHUNK * group), body, 0)


def _linear_mixers(u, u32, w2pad, b_gate, gla_gain, lb_logits, hgrn_gain, B, S, tm, layer_idx):
    nt = S // tm
    tri, mall = _linear_consts()
    ones64 = _block_ones(GROUP, HD)
    c256 = lambda name: pl.BlockSpec((tm, GROUP), lambda b, i, c=_B256[name]: (b * nt + i, c))
    c128 = lambda name: pl.BlockSpec((tm, LANE), lambda b, i, c=_B128[name]: (b * nt + i, c))
    f_lr = pl.BlockSpec((tm, LANE), lambda b, i, c=_F128["g_lr"]: (b * nt + i, c))
    f_rf = pl.BlockSpec((tm, GROUP), lambda b, i, c=_F256["r_f"]: (b * nt + i, c))
    full = lambda shape: pl.BlockSpec(shape, lambda b, i: (0,) * len(shape))
    out_spec = pl.BlockSpec((tm, GROUP), lambda b, i: (b * nt + i, 0))
    out_shape = jax.ShapeDtypeStruct((B * S, GROUP), F32)
    depth = lb_logits.shape[0]
    return pl.pallas_call(
        functools.partial(_linear_kernel, tm=tm, layer_idx=layer_idx),
        grid=(B, nt),
        in_specs=[c128("g_q"), c128("g_k"), c256("g_v"), f_lr, c256("g_og"),
                  full((LANE, LANE)), full((1, LANE)), full((1, GROUP)),
                  c256("r_q"), f_rf, c256("r_i"), c256("r_og"), full((depth, GROUP)), full((1, GROUP)),
                  full(tri.shape), full(mall.shape), full((GROUP, GROUP))],
        out_specs=[out_spec, out_spec], out_shape=[out_shape, out_shape],
        scratch_shapes=[pltpu.VMEM((GROUP, HEADS * GLA_DK), F32), pltpu.VMEM((GROUP, GROUP), F32)],
        compiler_params=_params("parallel", "arbitrary"),
        name="linear_mixers",
    )(u, u, u, u32, u, w2pad, b_gate, gla_gain, u, u32, u, u, lb_logits, hgrn_gain, tri, mall, ones64)


def _mix_residual(h_ref, part_refs, wout_ref):
    x = h_ref[...]
    for j, r in enumerate(part_refs):
        x = x + _dot(r[...].astype(BF16), wout_ref[j * GROUP:(j + 1) * GROUP, :])
    return x


def _ffn_kernel(h_ref, a_ref, b_ref, c_ref, d_ref, wout_ref, g_ref, wg_ref, wu_ref, wd_ref, o_ref,
                xn_ref, acc_ref):
    f = pl.program_id(1)

    @pl.when(f == 0)
    def _():
        x = _mix_residual(h_ref, (a_ref, b_ref, c_ref, d_ref), wout_ref)
        y = x * lax.rsqrt(jnp.mean(x * x, axis=-1, keepdims=True) + EPS) * g_ref[...]
        xn_ref[...] = y.astype(BF16)
        acc_ref[...] = x

    xn = xn_ref[...]
    mid = _silu(_dot(xn, wg_ref[...])) * _dot(xn, wu_ref[...])
    acc_ref[...] += _dot(mid.astype(BF16), wd_ref[...])

    @pl.when(f == pl.num_programs(1) - 1)
    def _():
        o_ref[...] = acc_ref[...]


def _ffn(h, parts, w_out, gain, wg, wu, wd, tm, tf):
    T = h.shape[0]
    F = wg.shape[1]
    part = pl.BlockSpec((tm, GROUP), lambda i, f: (i, 0))
    return pl.pallas_call(
        _ffn_kernel,
        grid=(T // tm, F // tf),
        in_specs=[pl.BlockSpec((tm, D_MODEL), lambda i, f: (i, 0)), part, part, part, part,
                  pl.BlockSpec((D_MODEL, D_MODEL), lambda i, f: (0, 0)),
                  pl.BlockSpec((1, D_MODEL), lambda i, f: (0, 0)),
                  pl.BlockSpec((D_MODEL, tf), lambda i, f: (0, f)),
                  pl.BlockSpec((D_MODEL, tf), lambda i, f: (0, f)),
                  pl.BlockSpec((tf, D_MODEL), lambda i, f: (f, 0))],
        out_specs=pl.BlockSpec((tm, D_MODEL), lambda i, f: (i, 0)),
        out_shape=jax.ShapeDtypeStruct((T, D_MODEL), F32),
        scratch_shapes=[pltpu.VMEM((tm, D_MODEL), BF16), pltpu.VMEM((tm, D_MODEL), F32)],
        compiler_params=_params("parallel", "arbitrary"),
        name="ffn_swiglu",
    )(h, *parts, w_out, gain.reshape(1, D_MODEL), wg, wu, wd)


def _router_kernel(h_ref, a_ref, b_ref, c4_ref, d_ref, wout_ref, g_ref, r_ref, lower_ref,
                   h1_ref, c_ref, comb_ref, rk_ref, cnt_ref):
    x = _mix_residual(h_ref, (a_ref, b_ref, c4_ref, d_ref), wout_ref)
    h1_ref[...] = x
    y = x * lax.rsqrt(jnp.mean(x * x, axis=-1, keepdims=True) + EPS) * g_ref[...]
    c_ref[...] = y.astype(BF16)
    y_hi, y_lo = _split2(y)
    r_hi, r_lo = _split2(r_ref[...])
    logits = _dot(y_hi, r_hi) + _dot(y_lo, r_hi) + _dot(y_hi, r_lo)
    lane = lax.broadcasted_iota(jnp.int32, logits.shape, 1)
    lane_f = lane.astype(F32)
    logits = jnp.where(lane < N_EXPERTS, logits, LOWEST)
    m1 = jnp.max(logits, axis=1, keepdims=True)
    i1 = jnp.min(jnp.where(logits == m1, lane_f, float(LANE)), axis=1, keepdims=True)
    rest = jnp.where(lane_f == i1, LOWEST, logits)
    m2 = jnp.max(rest, axis=1, keepdims=True)
    i2 = jnp.min(jnp.where(rest == m2, lane_f, float(LANE)), axis=1, keepdims=True)
    e2 = jnp.exp(m2 - m1)
    w1 = 1.0 / (1.0 + e2)
    comb_ref[...] = jnp.where(lane_f == i1, w1, jnp.where(lane_f == i2, e2 * w1, 0.0))
    chosen = (lane_f == i1) | (lane_f == i2)
    sel = jnp.where(chosen, 1.0, 0.0)
    rank = _dot(lower_ref[...], sel.astype(BF16))
    rk_ref[...] = jnp.where(chosen, rank, -1.0).astype(jnp.int32)
    cnt_ref[...] = jnp.broadcast_to(jnp.sum(sel, axis=0, keepdims=True), cnt_ref.shape).astype(jnp.int32)


def _router(h, parts, w_out, gain, router_pad, tm):
    T = h.shape[0]
    i = np.arange(tm)
    lower = jnp.asarray((i[None, :] < i[:, None]).astype(np.float32), BF16)
    sds = jax.ShapeDtypeStruct
    rows = lambda w: pl.BlockSpec((tm, w), lambda i: (i, 0))
    return pl.pallas_call(
        _router_kernel,
        grid=(T // tm,),
        in_specs=[rows(D_MODEL), rows(GROUP), rows(GROUP), rows(GROUP), rows(GROUP),
                  pl.BlockSpec((D_MODEL, D_MODEL), lambda i: (0, 0)),
                  pl.BlockSpec((1, D_MODEL), lambda i: (0, 0)),
                  pl.BlockSpec((D_MODEL, LANE), lambda i: (0, 0)),
                  pl.BlockSpec((tm, tm), lambda i: (0, 0))],
        out_specs=[rows(D_MODEL), rows(D_MODEL), rows(LANE), rows(LANE),
                   pl.BlockSpec((SUBLANES, LANE), lambda i: (i, 0))],
        out_shape=[sds((T, D_MODEL), F32), sds((T, D_MODEL), BF16), sds((T, LANE), F32), sds((T, LANE), jnp.int32),
                   sds((T // tm * SUBLANES, LANE), jnp.int32)],
        compiler_params=_params("parallel"),
        name="moe_router",
    )(h, *parts, w_out, gain.reshape(1, D_MODEL), router_pad, lower)


MOE_CHUNK = 512
MOE_ROWS = MOE_CHUNK
MOE_HALF = MOE_ROWS // 2
MOE_TILE = 1024
MOE_ALIGN = 16
MOE_PAD_SEGS = MOE_TILE // MOE_ROWS


def _moe_plan(cnt, T):
    nch = T // MOE_CHUNK
    n_ce = cnt.reshape(nch, SUBLANES, LANE)[:, 0, :N_EXPERTS]
    cap = (n_ce + MOE_ALIGN - 1) // MOE_ALIGN * MOE_ALIGN
    tot = jnp.sum(cap, axis=0)
    ptot = (tot + MOE_TILE - 1) // MOE_TILE * MOE_TILE
    start = jnp.cumsum(ptot) - ptot
    lo_ce = start[None, :] + jnp.cumsum(cap, axis=0) - cap
    pad_lo = (start + tot)[:, None] + MOE_ROWS * jnp.arange(MOE_PAD_SEGS, dtype=jnp.int32)[None, :]
    lo = jnp.concatenate([lo_ce.T, pad_lo], axis=1)
    n = jnp.concatenate([n_ce.T, jnp.full((N_EXPERTS, MOE_PAD_SEGS), MOE_ROWS, jnp.int32)], axis=1)
    n_tiles = _moe_tiles(T)
    tile_end = jnp.cumsum(ptot // MOE_TILE)
    j = jnp.arange(n_tiles, dtype=jnp.int32)
    tile_e = jnp.minimum(jnp.sum((tile_end[None, :] <= j[:, None]).astype(jnp.int32), axis=1), N_EXPERTS - 1)
    valid = (j < tile_end[-1]).astype(jnp.int32)
    n_flat = jnp.concatenate([n.reshape(-1), tile_end[-1:] * MOE_PAD_SEGS]).astype(jnp.int32)
    return lo.reshape(-1).astype(jnp.int32), n_flat, tile_e, valid


def _moe_tiles(T):
    nch = T // MOE_CHUNK
    rows = 2 * T + nch * N_EXPERTS * (MOE_ALIGN - 1) + N_EXPERTS * (MOE_TILE - 1)
    return -(-rows // MOE_TILE) + 1


def _moe_gather_kernel(lo_ref, n_ref, c_ref, rkt_ref, x_hbm, buf, sem, *, nch):
    e = pl.program_id(0)
    c = pl.program_id(1)
    n_steps = pl.num_programs(0) * pl.num_programs(1)
    step = e * pl.num_programs(1) + c
    slot = step % 2
    rk_row = rkt_ref[pl.ds(e, 1), :]
    rk_row = jnp.where(c < nch, rk_row, -1)
    row = lax.broadcasted_iota(jnp.int32, (MOE_HALF, MOE_CHUNK), 0)
    chunk = c_ref[...]
    onehot = jnp.where(rk_row == row, 1.0, 0.0).astype(BF16)
    buf[slot, 0:MOE_HALF, :] = _dot(onehot, chunk).astype(BF16)

    @pl.when(n_ref[step] > MOE_HALF)
    def _():
        onehot = jnp.where(rk_row == row + MOE_HALF, 1.0, 0.0).astype(BF16)
        buf[slot, MOE_HALF:MOE_ROWS, :] = _dot(onehot, chunk).astype(BF16)

    def copy(k, s, rows):
        dst = x_hbm.at[pl.ds(pl.multiple_of(lo_ref[k], MOE_ALIGN), rows)]
        return pltpu.make_async_copy(buf.at[s, 0:rows], dst, sem.at[s])

    def for_size(k, fn):
        @pl.when(n_ref[k] > MOE_HALF)
        def _():
            fn(MOE_ROWS)

        @pl.when(n_ref[k] <= MOE_HALF)
        def _():
            fn(MOE_HALF)

    @pl.when(step > 0)
    def _():
        for_size(step - 1, lambda rows: copy(step - 1, 1 - slot, rows).wait())

    for_size(step, lambda rows: copy(step, slot, rows).start())

    @pl.when(step == n_steps - 1)
    def _():
        for_size(step, lambda rows: copy(step, slot, rows).wait())
        buf[0] = jnp.zeros((MOE_ROWS, D_MODEL), BF16)

        def fill(k, carry):
            dst = x_hbm.at[pl.ds(pl.multiple_of(k * MOE_ROWS, MOE_ROWS), MOE_ROWS)]
            cp = pltpu.make_async_copy(buf.at[0], dst, sem.at[0])
            cp.start()
            cp.wait()
            return carry

        lax.fori_loop(n_ref[n_steps], x_hbm.shape[0] // MOE_ROWS, fill, 0)


def _moe_gather(c_bf, rkt, lo, n, n_tiles):
    T = c_bf.shape[0]
    nch = T // MOE_CHUNK
    last = nch - 1
    grid_spec = pltpu.PrefetchScalarGridSpec(
        num_scalar_prefetch=2,
        grid=(N_EXPERTS, nch + MOE_PAD_SEGS),
        in_specs=[pl.BlockSpec((MOE_CHUNK, D_MODEL), lambda e, c, lo, n: (jnp.minimum(c, last), 0)),
                  pl.BlockSpec((N_EXPERTS, MOE_CHUNK), lambda e, c, lo, n: (0, jnp.minimum(c, last)))],
        out_specs=pl.BlockSpec(memory_space=pl.ANY),
        scratch_shapes=[pltpu.VMEM((2, MOE_ROWS, D_MODEL), BF16), pltpu.SemaphoreType.DMA((2,))],
    )
    return pl.pallas_call(
        functools.partial(_moe_gather_kernel, nch=nch),
        grid_spec=grid_spec,
        out_shape=jax.ShapeDtypeStruct((n_tiles * MOE_TILE, D_MODEL), BF16),
        compiler_params=_params("arbitrary", "arbitrary"),
        name="moe_gather",
    )(lo, n, c_bf, rkt)


def _moe_ffn_kernel(te_ref, valid_ref, x_ref, wg_ref, wu_ref, wd_ref, y_ref, acc_ref):
    j = pl.program_id(0)
    f = pl.program_id(1)
    last = pl.num_programs(1) - 1

    @pl.when(f == 0)
    def _():
        acc_ref[...] = jnp.zeros(acc_ref.shape, F32)

    @pl.when(valid_ref[j] == 1)
    def _():
        x = x_ref[...]
        mid = _silu(_dot(x, wg_ref[...])) * _dot(x, wu_ref[...])
        acc_ref[...] += _dot(mid.astype(BF16), wd_ref[...])

    @pl.when(f == last)
    def _():
        y_ref[...] = acc_ref[...].astype(BF16)


def _moe_ffn(x_sorted, tile_e, valid, wg, wu, wd, tf):
    n_tiles = x_sorted.shape[0] // MOE_TILE
    F = wg.shape[2]
    nf = F // tf
    sq = pl.Squeezed()
    fsel = lambda j, f, te, va: jnp.where(va[j] == 1, f, nf - 1)
    grid_spec = pltpu.PrefetchScalarGridSpec(
        num_scalar_prefetch=2,
        grid=(n_tiles, nf),
        in_specs=[pl.BlockSpec((MOE_TILE, D_MODEL), lambda j, f, te, va: (j, 0)),
                  pl.BlockSpec((sq, D_MODEL, tf), lambda j, f, te, va: (te[j], 0, fsel(j, f, te, va))),
                  pl.BlockSpec((sq, D_MODEL, tf), lambda j, f, te, va: (te[j], 0, fsel(j, f, te, va))),
                  pl.BlockSpec((sq, tf, D_MODEL), lambda j, f, te, va: (te[j], fsel(j, f, te, va), 0))],
        out_specs=pl.BlockSpec((MOE_TILE, D_MODEL), lambda j, f, te, va: (j, 0)),
        scratch_shapes=[pltpu.VMEM((MOE_TILE, D_MODEL), F32)],
    )
    return pl.pallas_call(
        _moe_ffn_kernel,
        grid_spec=grid_spec,
        out_shape=jax.ShapeDtypeStruct((n_tiles * MOE_TILE, D_MODEL), BF16),
        compiler_params=_params("parallel", "arbitrary"),
        name="moe_experts",
    )(tile_e, valid, x_sorted, wg, wu, wd)


def _ple_update(x, p, gain, w_gate, w_proj):
    y = x * lax.rsqrt(jnp.mean(x * x, axis=-1, keepdims=True) + EPS) * gain
    gate = _sigmoid(_dot(y.astype(BF16), w_gate))
    return x + _dot(p.astype(BF16), w_proj) * gate


def _moe_combine_kernel(lo_ref, n_ref, h_ref, rk_ref, comb_ref, p_ref, g_ref, wgate_ref, wproj_ref, y_hbm,
                        o_ref, ybuf, sem, *, nch):
    c = pl.program_id(0)
    seg = lambda e: e * (nch + MOE_PAD_SEGS) + c

    def copy(e, rows):
        src = y_hbm.at[pl.ds(pl.multiple_of(lo_ref[seg(e)], MOE_ALIGN), rows)]
        return pltpu.make_async_copy(src, ybuf.at[e, 0:rows], sem.at[e])

    def for_size(e, fn):
        @pl.when(n_ref[seg(e)] > MOE_HALF)
        def _():
            fn(MOE_ROWS)

        @pl.when(n_ref[seg(e)] <= MOE_HALF)
        def _():
            fn(MOE_HALF)

    for e in range(N_EXPERTS):
        for_size(e, lambda rows, e=e: copy(e, rows).start())
    o_ref[...] = h_ref[...]
    rk = rk_ref[...]
    comb = comb_ref[...]
    lane = lax.broadcasted_iota(jnp.int32, (MOE_CHUNK, MOE_HALF), 1)
    for e in range(N_EXPERTS):
        for_size(e, lambda rows, e=e: copy(e, rows).wait())
        rank_col = rk[:, e:e + 1]
        w_col = comb[:, e:e + 1]
        onehot = jnp.where(rank_col == lane, 1.0, 0.0).astype(BF16)
        o_ref[...] += w_col * _dot(onehot, ybuf[e, 0:MOE_HALF, :])

        @pl.when(n_ref[seg(e)] > MOE_HALF)
        def _():
            onehot = jnp.where(rank_col == lane + MOE_HALF, 1.0, 0.0).astype(BF16)
            o_ref[...] += w_col * _dot(onehot, ybuf[e, MOE_HALF:MOE_ROWS, :])

    o_ref[...] = _ple_update(o_ref[...], p_ref[...], g_ref[...], wgate_ref[...], wproj_ref[...])


def _moe_combine(h, rk_pad, comb, y_sorted, lo, n, p, layer, ple_gain, ple_wg, ple_wp):
    T = h.shape[0]
    nch = T // MOE_CHUNK
    const = lambda shape: pl.BlockSpec(shape, lambda c, lo, n: (0, 0))
    grid_spec = pltpu.PrefetchScalarGridSpec(
        num_scalar_prefetch=2,
        grid=(nch,),
        in_specs=[pl.BlockSpec((MOE_CHUNK, D_MODEL), lambda c, lo, n: (c, 0)),
                  pl.BlockSpec((MOE_CHUNK, LANE), lambda c, lo, n: (c, 0)),
                  pl.BlockSpec((MOE_CHUNK, LANE), lambda c, lo, n: (c, 0)),
                  pl.BlockSpec((pl.Squeezed(), MOE_CHUNK, PLE_DIM), lambda c, lo, n: (layer, c, 0)),
                  const((1, D_MODEL)), const((D_MODEL, D_MODEL)), const((PLE_DIM, D_MODEL)),
                  pl.BlockSpec(memory_space=pl.ANY)],
        out_specs=pl.BlockSpec((MOE_CHUNK, D_MODEL), lambda c, lo, n: (c, 0)),
        scratch_shapes=[pltpu.VMEM((N_EXPERTS, MOE_ROWS, D_MODEL), BF16), pltpu.SemaphoreType.DMA((N_EXPERTS,))],
    )
    return pl.pallas_call(
        functools.partial(_moe_combine_kernel, nch=nch),
        grid_spec=grid_spec,
        out_shape=jax.ShapeDtypeStruct((T, D_MODEL), F32),
        compiler_params=_params("arbitrary"),
        name="moe_combine",
    )(lo, n, h, rk_pad, comb, p, ple_gain.reshape(1, D_MODEL), ple_wg, ple_wp, y_sorted)


def _moe(h, c_bf, comb, rk, cnt, wg, wu, wd, tf, ple):
    T = h.shape[0]
    lo, n, tile_e, valid = _moe_plan(cnt, T)
    x_sorted = _moe_gather(c_bf, rk[:, :N_EXPERTS].T, lo, n, _moe_tiles(T))
    y_sorted = _moe_ffn(x_sorted, tile_e, valid, wg, wu, wd, tf)
    return _moe_combine(h, rk, comb, y_sorted, lo, n, *ple)


def _ple_kernel(h_ref, p_ref, g_ref, wg_ref, wp_ref, o_ref):
    o_ref[...] = _ple_update(h_ref[...], p_ref[...], g_ref[...], wg_ref[...], wp_ref[...])


def _ple(h, p, layer, gain, wg, wp, tm):
    T = h.shape[0]
    return pl.pallas_call(
        _ple_kernel,
        grid=(T // tm,),
        in_specs=[pl.BlockSpec((tm, D_MODEL), lambda i: (i, 0)),
                  pl.BlockSpec((pl.Squeezed(), tm, PLE_DIM), lambda i: (layer, i, 0)),
                  pl.BlockSpec((1, D_MODEL), lambda i: (0, 0)),
                  pl.BlockSpec((D_MODEL, D_MODEL), lambda i: (0, 0)),
                  pl.BlockSpec((PLE_DIM, D_MODEL), lambda i: (0, 0))],
        out_specs=pl.BlockSpec((tm, D_MODEL), lambda i: (i, 0)),
        out_shape=jax.ShapeDtypeStruct((T, D_MODEL), F32),
        compiler_params=_params("parallel"),
        name="ple_gate",
    )(h, p, gain.reshape(1, D_MODEL), wg, wp)


def _tiles(T, S):
    pick = lambda n, pref: max(t for t in pref if n % t == 0)
    return dict(
        proj_m=pick(T, (512, 256, 128)),
        prep_m=pick(T, (512, 256, 128)),
        attn_q=256,
        nsa_q=pick(S, (512, 256)), nsa_k=pick(S, (512, 256)),
        lin_m=pick(S, (512, 256, 128, 64)),
        row_m=pick(T, (512, 256, 128)),
        ffn_m=pick(T, (1024, 512, 256, 128)), ffn_f=512, moe_f=D_FF // 4,
    )


def kernel(x, p, norm_attn, w_in, w_out, nsa_cmp_pos, nsa_cmp_w1, nsa_cmp_w2, nsa_qk_gain, diff_qk_gain, diff_lambda, diff_norm, gla_w_gate2, gla_b_gate, gla_norm, hgrn_lb_logits, hgrn_norm, norm_ffn, ffn_w_gate, ffn_w_up, ffn_w_down, moe_router, moe_w_gate, moe_w_up, moe_w_down, ple_norm, ple_w_gate, ple_w_proj):
    B, S, _ = x.shape
    depth = w_in.shape[0]
    T = B * S
    t = _tiles(T, S)
    cols =jnp.asarray(np.maximum(_COLS, 0), jnp.int32)
    col_mask = jnp.asarray(_COLS >= 0)
    ones_row = jnp.ones((GROUP,), F32)

    h = x.reshape(T, D_MODEL)
    for i in range(depth):
        w_in_r = jnp.where(col_mask[None, :], jnp.take(w_in[i], cols, axis=1), 0.0).astype(BF16)
        gains = jnp.stack([
            jnp.tile(nsa_qk_gain[i, 0], HEADS) * (HD ** -0.5 * LOG2E),
            jnp.tile(diff_qk_gain[i, 0], 2 * HEADS) * (DIFF_QK ** -0.5 * LOG2E),
            jnp.tile(diff_qk_gain[i, 1], 2 * HEADS),
            jnp.tile(nsa_qk_gain[i, 2], HEADS),
            jnp.tile(nsa_qk_gain[i, 3], HEADS),
            ones_row, ones_row, ones_row])
        pos = jnp.transpose(nsa_cmp_pos[i], (1, 0, 2)).reshape(2, NSA_CMP_STRIDE, LANE)
        w1r = nsa_cmp_w1[i].reshape(2, NSA_CMP_LEN, HD, NSA_CMP_HIDDEN)
        zeros = jnp.zeros_like(w1r[0])
        w1_bd = jnp.concatenate([jnp.concatenate([w1r[0], zeros], axis=-1),
                                 jnp.concatenate([zeros, w1r[1]], axis=-1)], axis=1)
        w1_bd = w1_bd.reshape(2, NSA_CMP_STRIDE, LANE, 2 * NSA_CMP_HIDDEN).astype(BF16)
        w2p = jnp.pad(nsa_cmp_w2[i], ((0, 0), (0, 0), (0, LANE - HD))).astype(BF16)
        kc_gain = jnp.pad(nsa_qk_gain[i, 1], (0, LANE - HD)).reshape(1, LANE)
        w2pad = jnp.zeros((LANE, LANE), F32).at[:GLA_RANK].set(gla_w_gate2[i])
        diff_gain_col = jnp.broadcast_to(diff_norm[i][:, None], (HD, t["attn_q"]))

        u, u32 = _norm_matmul(h, norm_attn[i], w_in_r, t["proj_m"])
        nqt, dqt, dk, dvt, kvs, vst, kvw, vwt = _prep(u, gains, t["prep_m"])
        kc, vct = _compress(u32, pos, w1_bd, w2p, kc_gain, B, S)
        o_a = _nsa_attention(u32, nqt, kc, vct, kvs, vst, kvw, vwt, B, S, t["nsa_q"], t["nsa_k"], t["nsa_q"])
        o_b = _diff_attention(dqt, dk, dvt, diff_lambda[i], diff_gain_col, B, S, t["attn_q"], i)
        o_c, o_d = _linear_mixers(u, u32, w2pad, gla_b_gate[i].reshape(1, LANE),
                                  jnp.tile(gla_norm[i], HEADS).reshape(1, GROUP), hgrn_lb_logits,
                                  jnp.tile(hgrn_norm[i], HEADS).reshape(1, GROUP), B, S, t["lin_m"], i)
        parts = (o_a, o_b, o_c, o_d)
        w_out_bf = w_out[i].astype(BF16)
        ple = (p.reshape(depth, T, PLE_DIM), i, ple_norm[i], ple_w_gate[i].astype(BF16), ple_w_proj[i].astype(BF16))
        j = i // 2
        if i % 2 == 0:
            h = _ffn(h, parts, w_out_bf, norm_ffn[i], ffn_w_gate[j].astype(BF16), ffn_w_up[j].astype(BF16),
                     ffn_w_down[j].astype(BF16), t["ffn_m"], t["ffn_f"])
            h = _ple(h, *ple, t["row_m"])
        else:
            router_pad = jnp.zeros((D_MODEL, LANE), F32).at[:, :N_EXPERTS].set(moe_router[j])
            h, c_bf, comb, rk, cnt = _router(h, parts, w_out_bf, norm_ffn[i], router_pad, MOE_CHUNK)
            h = _moe(h, c_bf, comb, rk, cnt, moe_w_gate[j].astype(BF16), moe_w_up[j].astype(BF16),
                     moe_w_down[j].astype(BF16), t["moe_f"], ple)
    return h.reshape(B, S, D_MODEL)
```

```python
import functools
import math

import numpy as np
import jax
import jax.numpy as jnp
from jax import lax
from jax.experimental import pallas as pl
from jax.experimental.pallas import tpu as pltpu

F32 = jnp.float32
BF16 = jnp.bfloat16

D_MODEL = 1024
HEADS = 4
HD = 64
GROUP = HEADS * HD
NSA_CMP_LEN = 32
NSA_CMP_STRIDE = 16
NSA_CMP_HIDDEN = 4 * HD
NSA_SEL_LEN = 64
NSA_TOPK = 16
NSA_WINDOW = 512
DIFF_QK = HD // 2
GLA_DK = HD // 2
GLA_RANK = 16
GLA_TAU = 16.0
CHUNK = 64
D_FF = 7 * D_MODEL // 2
N_EXPERTS = 8
PLE_DIM = 256
EPS = 1e-6
NEG = -1e30
BIG = 1e30
LOWEST = -3.0e38
MASK_BIG = 2.0 ** 100
LOG2E = 1.4426950408889634

VMEM_LIMIT = 52 * 1024 * 1024
LANE = 128
SUBLANES = 8

_SRC = dict(nsa_q=0, k_cmp=256, v_cmp=320, k_slc=384, v_slc=448, k_win=512, v_win=576, nsa_g=640,
            d_q=652, d_k=908, d_v=1164, g_q=1420, g_k=1548, g_v=1676, g_lr=1932, g_og=1948,
            r_q=2204, r_f=2460, r_i=2716, r_og=2972)

_B256 = dict(nsa_q=0, d_q=1, d_k=2, d_v=3, g_v=4, g_og=5, r_q=6, r_i=7, r_og=8)
_B128 = dict(g_q=18, g_k=19, kvslc=20, kvwin=21)
NC16 = 22 * 128
_F256 = dict(r_f=0)
_F128 = dict(g_lr=2, nsa_g=3, kvcmp=4)
NC32 = 5 * 128
NC = NC16 + NC32


def _column_map():
    cols = -np.ones((NC,), np.int64)

    def put(dst, src, width):
        cols[dst:dst + width] = np.arange(src, src + width)

    for name in _B256:
        put(_B256[name] * 256, _SRC[name], 256)
    put(_B128["g_q"] * 128, _SRC["g_q"], 128)
    put(_B128["g_k"] * 128, _SRC["g_k"], 128)
    put(_B128["kvslc"] * 128, _SRC["k_slc"], 128)
    put(_B128["kvwin"] * 128, _SRC["k_win"], 128)
    put(NC16 + _F256["r_f"] * 256, _SRC["r_f"], 256)
    put(NC16 + _F128["g_lr"] * 128, _SRC["g_lr"], GLA_RANK)
    put(NC16 + _F128["nsa_g"] * 128, _SRC["nsa_g"], 3 * HEADS)
    put(NC16 + _F128["kvcmp"] * 128, _SRC["k_cmp"], 128)
    return cols


_COLS = _column_map()


def _dot(a, b):
    return jnp.dot(a, b, preferred_element_type=F32)


def _dot_nt(a, b):
    return lax.dot_general(a, b, (((1,), (1,)), ((), ())), preferred_element_type=F32)


def _dot_tn(a, b):
    return lax.dot_general(a, b, (((0,), (0,)), ((), ())), preferred_element_type=F32)


def _split2(x):
    hi = x.astype(BF16)
    lo = (x - hi.astype(F32)).astype(BF16)
    return hi, lo


def _split3(x):
    hi = x.astype(BF16)
    r = x - hi.astype(F32)
    mid = r.astype(BF16)
    lo = (r - mid.astype(F32)).astype(BF16)
    return hi, mid, lo


def _group_mean(x, ones_bf, group):
    hi, lo = _split2(x)
    return (_dot(hi, ones_bf) + _dot(lo, ones_bf)) * (1.0 / group)


def _group_rms(x, ones_bf, group):
    return x * lax.rsqrt(_group_mean(x * x, ones_bf, group) + EPS)


def _sigmoid(x):
    return 1.0 / (1.0 + jnp.exp(-x))


def _silu(x):
    return x * _sigmoid(x)


def _params(*sem):
    return pltpu.CompilerParams(dimension_semantics=sem, vmem_limit_bytes=VMEM_LIMIT)


def _block_ones(n, group):
    i = np.arange(n)
    return jnp.asarray((i[:, None] // group == i[None, :] // group).astype(np.float32), BF16)


def _norm_matmul_kernel(x_ref, g_ref, w_ref, o16_ref, o32_ref):
    x = x_ref[...]
    y = x * lax.rsqrt(jnp.mean(x * x, axis=-1, keepdims=True) + EPS) * g_ref[...]
    u = _dot(y.astype(BF16), w_ref[...])
    o16_ref[...] = u[:, 0:NC16].astype(BF16)
    o32_ref[...] = u[:, NC16:NC]


def _norm_matmul(x, gain, w_bf, tm):
    T, K = x.shape
    return pl.pallas_call(
        _norm_matmul_kernel,
        grid=(T // tm,),
        in_specs=[pl.BlockSpec((tm, K), lambda i: (i, 0)),
                  pl.BlockSpec((1, K), lambda i: (0, 0)),
                  pl.BlockSpec((K, NC), lambda i: (0, 0))],
        out_specs=[pl.BlockSpec((tm, NC16), lambda i: (i, 0)), pl.BlockSpec((tm, NC32), lambda i: (i, 0))],
        out_shape=[jax.ShapeDtypeStruct((T, NC16), BF16), jax.ShapeDtypeStruct((T, NC32), F32)],
        compiler_params=_params("parallel"),
        name="in_proj",
    )(x, gain.reshape(1, K), w_bf)


def _prep_kernel(nq_ref, dq_ref, dk_ref, dv_ref, kvs_ref, kvw_ref, gains_ref, ones64_ref, ones32_ref,
                 ones64h_ref, o_nq, o_dq, o_dk, o_dv, o_kvs, o_vs, o_kvw, o_vw):
    ones64 = ones64_ref[...]
    ones32 = ones32_ref[...]
    ones64h = ones64h_ref[...]
    f32 = lambda ref: ref[...].astype(F32)
    o_nq[...] = (_group_rms(f32(nq_ref), ones64, HD) * gains_ref[0:1, :]).T.astype(BF16)
    o_dq[...] = (_group_rms(f32(dq_ref), ones32, DIFF_QK) * gains_ref[1:2, :]).T.astype(BF16)
    o_dk[...] = (_group_rms(f32(dk_ref), ones32, DIFF_QK) * gains_ref[2:3, :]).astype(BF16)
    o_dv[...] = f32(dv_ref).T.astype(BF16)
    lane = lax.broadcasted_iota(jnp.int32, kvs_ref.shape, 1)
    for kv_ref, gain, o_kv, o_v in ((kvs_ref, gains_ref[3:4, 0:LANE], o_kvs, o_vs),
                                    (kvw_ref, gains_ref[4:5, 0:LANE], o_kvw, o_vw)):
        x = f32(kv_ref)
        o_kv[...] = jnp.where(lane < HD, _group_rms(x, ones64h, HD) * gain, x).astype(BF16)
        o_v[...] = x.T[HD:2 * HD, :].astype(BF16)


def _prep(u, gains, tm):
    T = u.shape[0]
    c256 = lambda name: pl.BlockSpec((tm, GROUP), lambda i, c=_B256[name]: (i, c))
    c128 = lambda name: pl.BlockSpec((tm, LANE), lambda i, c=_B128[name]: (i, c))
    const = lambda shape: pl.BlockSpec(shape, lambda i: (0, 0))
    rows = lambda w: pl.BlockSpec((tm, w), lambda i: (i, 0))
    colsT = lambda h: pl.BlockSpec((h, tm), lambda i: (0, i))
    sds = jax.ShapeDtypeStruct
    return pl.pallas_call(
        _prep_kernel,
        grid=(T // tm,),
        in_specs=[c256("nsa_q"), c256("d_q"), c256("d_k"), c256("d_v"), c128("kvslc"), c128("kvwin"),
                  const((8, GROUP)), const((GROUP, GROUP)), const((GROUP, GROUP)), const((LANE, LANE))],
        out_specs=[colsT(GROUP), colsT(GROUP), rows(GROUP), colsT(GROUP),
                   rows(LANE), colsT(HD), rows(LANE), colsT(HD)],
        out_shape=[sds((GROUP, T), BF16), sds((GROUP, T), BF16), sds((T, GROUP), BF16), sds((GROUP, T), BF16),
                   sds((T, LANE), BF16), sds((HD, T), BF16), sds((T, LANE), BF16), sds((HD, T), BF16)],
        compiler_params=_params("parallel"),
        name="attn_prep",
    )(u, u, u, u, u, u, gains, _block_ones(GROUP, HD), _block_ones(GROUP, DIFF_QK), _block_ones(LANE, HD))


def _compress_kernel(kv_ref, pos_ref, w1_ref, w2_ref, gain_ref, ones64h_ref, kc_ref, vc_ref):
    ncp = kc_ref.shape[0]
    hid = NSA_CMP_HIDDEN
    top = jnp.zeros((ncp, 2 * hid), F32)
    bot = jnp.zeros((ncp, 2 * hid), F32)
    for l in range(NSA_CMP_STRIDE):
        x = kv_ref[pl.ds(l, ncp, stride=NSA_CMP_STRIDE), :]
        top = top + _dot((x + pos_ref[0, l:l + 1, :]).astype(BF16), w1_ref[0, l])
        bot = bot + _dot((x + pos_ref[1, l:l + 1, :]).astype(BF16), w1_ref[1, l])
    hidden = top + pltpu.roll(bot, ncp - 1, 0)
    act = _silu(hidden).astype(BF16)
    kc = _dot(act[:, 0:hid], w2_ref[0])
    kc_ref[...] = _group_rms(kc, ones64h_ref[...], HD) * gain_ref[...]
    vc_ref[...] = _dot(act[:, hid:2 * hid], w2_ref[1]).T[0:HD, :].astype(BF16)


def _compress(u, pos, w1_bd, w2p_bf, gain_row, B, S):
    ncp = S // NSA_CMP_STRIDE
    sq = pl.Squeezed()
    full = lambda shape: pl.BlockSpec(shape, lambda b: (0,) * len(shape))
    return pl.pallas_call(
        _compress_kernel,
        grid=(B,),
        in_specs=[pl.BlockSpec((S, LANE), lambda b, c=_F128["kvcmp"]: (b, c)),
                  full((2, NSA_CMP_STRIDE, LANE)), full((2, NSA_CMP_STRIDE, LANE, 2 * NSA_CMP_HIDDEN)),
                  full((2, NSA_CMP_HIDDEN, LANE)), full((1, LANE)), full((LANE, LANE))],
        out_specs=[pl.BlockSpec((sq, ncp, LANE), lambda b: (b, 0, 0)),
                   pl.BlockSpec((sq, HD, ncp), lambda b: (b, 0, 0))],
        out_shape=[jax.ShapeDtypeStruct((B, ncp, LANE), F32), jax.ShapeDtypeStruct((B, HD, ncp), BF16)],
        compiler_params=_params("parallel"),
        name="nsa_compress",
    )(u, pos, w1_bd, w2p_bf, gain_row, _block_ones(LANE, HD))


def _softmax_step_t(s, v_t, m_old, l_old, acc_ref):
    m_new = jnp.maximum(m_old, jnp.max(s, axis=0, keepdims=True))
    alpha = jnp.exp2(m_old - m_new)
    p = jnp.exp2(s - m_new)
    acc_ref[...] = alpha * acc_ref[...] + _dot(v_t, p.astype(BF16))
    return m_new, alpha * l_old + jnp.sum(p, axis=0, keepdims=True)


def _pipelined_sweep(lo, hi, qk, process, process_last, stats, sa_ref, sb_ref):
    def pair(jj, st):
        t = lo + 2 * jj
        qk(t + 1, sb_ref)
        st = process(t, sa_ref, st)
        qk(t + 2, sa_ref)
        return process(t + 1, sb_ref, st)

    def two_left(st):
        qk(hi - 1, sb_ref)
        st = process(hi - 2, sa_ref, st)
        return process_last(hi - 1, sb_ref, st)

    n = hi - lo
    qk(lo, sa_ref)
    stats = lax.fori_loop(0, (n - 1) // 2, pair, stats)
    return lax.cond(n % 2 == 0, two_left, lambda st: process_last(hi - 1, sa_ref, st), stats)


def _nsa_kernel(qt_ref, g_ref, kc_ref, vct_ref, kvs_ref, vst_ref, kvw_ref, vwt_ref, ovl_ref, exp_ref,
                o_ref, qs_ref, sa_ref, sb_ref, *acc_refs, tq, tk, tw, ksel):
    i = pl.program_id(1)
    t0 = i * tq
    ncp = kc_ref.shape[0]
    nsel = ovl_ref.shape[0]
    acc_s, acc_w = acc_refs[:HEADS], acc_refs[HEADS:]

    qt = qt_ref[...]
    qs_ref[HD:LANE, :] = jnp.zeros((LANE - HD, HEADS * tq), BF16)
    for h in range(HEADS):
        qs_ref[0:HD, h * tq:(h + 1) * tq] = qt[h * HD:(h + 1) * HD, :]
        acc_s[h][...] = jnp.zeros((HD, tq), F32)
        acc_w[h][...] = jnp.zeros((HD, tq), F32)

    kc_hi, kc_lo = _split2(kc_ref[...])
    n_idx = lax.broadcasted_iota(jnp.int32, (ncp, tq), 0)
    t_lane = t0 + lax.broadcasted_iota(jnp.int32, (ncp, tq), 1)
    ok = n_idx * NSA_CMP_STRIDE + (NSA_CMP_LEN - 1) <= t_lane
    cmp_scores = _dot(kc_hi, qs_ref[...]) + _dot(kc_lo, qs_ref[...])
    o_c = []
    psum = jnp.zeros((ncp, tq), F32)
    for h in range(HEADS):
        s = jnp.where(ok, cmp_scores[:, h * tq:(h + 1) * tq], NEG)
        e = jnp.exp2(s - jnp.max(s, axis=0, keepdims=True))
        p = jnp.where(ok, e / jnp.sum(e, axis=0, keepdims=True), 0.0)
        o_c.append(_dot(vct_ref[...], p.astype(BF16)))
        psum = psum + p

    p_hi, p_lo = _split2(psum)
    imp = _dot(ovl_ref[...], p_hi) + _dot(ovl_ref[...], p_lo)
    blk = lax.broadcasted_iota(jnp.int32, (nsel, tq), 0)
    t_col = t0 + lax.broadcasted_iota(jnp.int32, (nsel, tq), 1)
    cur = t_col // NSA_SEL_LEN
    forced = (blk == 0) | (blk == cur) | (blk == cur - 1)
    vals = jnp.where(forced, BIG, jnp.where(blk * NSA_SEL_LEN <= t_col, imp, NEG))
    blk_f = blk.astype(F32)
    sel = jnp.zeros((nsel, tq), F32)
    for _ in range(ksel):
        mx = jnp.max(vals, axis=0, keepdims=True)
        first = jnp.min(jnp.where(vals == mx, blk_f, float(nsel)), axis=0, keepdims=True)
        pick = blk_f == first
        sel = jnp.where(pick, 1.0, sel)
        vals = jnp.where(pick, LOWEST, vals)
    if nsel < LANE:
        sel = jnp.concatenate([sel, jnp.zeros((LANE - nsel, tq), F32)], axis=0)
    sel_bf = sel.astype(BF16)

    stats0 = tuple((jnp.full((1, tq), NEG, F32), jnp.zeros((1, tq), F32)) for _ in range(HEADS))

    def sweep(lo, hi, k_ref, vt_ref, accs, width, bias_fn, last_bias_fn):
        def qk(kt, dst):
            k = k_ref[pl.ds(pl.multiple_of(kt * width, width), width), :]
            for h in range(HEADS):
                dst[h, 0:width, :] = _dot(k, qs_ref[:, h * tq:(h + 1) * tq])

        def process(kt, src, stats, fn):
            k0 = pl.multiple_of(kt * width, width)
            kpos = k0 + lax.broadcasted_iota(jnp.int32, (width, tq), 0)
            tpos = t0 + lax.broadcasted_iota(jnp.int32, (width, tq), 1)
            bias = fn(k0, kpos, tpos)
            vt = vt_ref[:, pl.ds(k0, width)]
            return tuple(_softmax_step_t(src[h, 0:width, :] + bias, vt, stats[h][0], stats[h][1], accs[h])
                         for h in range(HEADS))

        return _pipelined_sweep(lo, hi, qk, functools.partial(process, fn=bias_fn),
                                functools.partial(process, fn=last_bias_fn), stats0, sa_ref, sb_ref)

    def sel_bias(k0, kpos, tpos):
        chosen = _dot(exp_ref[pl.ds(k0, kpos.shape[0]), :], sel_bf)
        return chosen - MASK_BIG

    def sel_bias_diag(k0, kpos, tpos):
        chosen = _dot(exp_ref[pl.ds(k0, kpos.shape[0]), :], sel_bf)
        return jnp.where(kpos <= tpos, chosen - MASK_BIG, NEG)

    st_s = sweep(0, i + 1, kvs_ref, vst_ref, acc_s, tk, sel_bias, sel_bias_diag)

    def win_bias(k0, kpos, tpos):
        return jnp.where((kpos <= tpos) & (kpos > tpos - NSA_WINDOW), 0.0, NEG)

    st_w = sweep(jnp.maximum(t0 - NSA_WINDOW, 0) // tw, (t0 + tq - 1) // tw + 1,
                 kvw_ref, vwt_ref, acc_w, tw, win_bias, win_bias)

    gates = _sigmoid(g_ref[...].T)
    outs = []
    for h in range(HEADS):
        outs.append(gates[3 * h:3 * h + 1, :] * o_c[h]
                    + gates[3 * h + 1:3 * h + 2, :] * (acc_s[h][...] / st_s[h][1])
                    + gates[3 * h + 2:3 * h + 3, :] * (acc_w[h][...] / st_w[h][1]))
    o_ref[...] = jnp.concatenate(outs, axis=0).T


def _nsa_attention(u, nqt, kc, vct, kvs, vst, kvw, vwt, B, S, tq, tk, tw):
    nq = S // tq
    ncp = S // NSA_CMP_STRIDE
    nsel = S // NSA_SEL_LEN
    ksel = min(NSA_TOPK, nsel)
    n_cmp = (S - NSA_CMP_LEN) // NSA_CMP_STRIDE + 1
    cmp_start = np.arange(ncp) * NSA_CMP_STRIDE
    sel_start = np.arange(nsel) * NSA_SEL_LEN
    overlap = ((cmp_start[:, None] <= sel_start[None, :] + NSA_SEL_LEN - 1)
               & (cmp_start[:, None] + NSA_CMP_LEN - 1 >= sel_start[None, :])
               & (np.arange(ncp)[:, None] < n_cmp))
    ovl_t = jnp.asarray(overlap.T.astype(np.float32), BF16)
    assert tq == tk, "the selected sweep treats every key tile before the query tile's own as fully visible"
    expand = jnp.where(jnp.arange(S, dtype=jnp.int32)[:, None] // NSA_SEL_LEN
                       == jnp.arange(LANE, dtype=jnp.int32)[None, :], MASK_BIG, 0.0).astype(BF16)
    sq = pl.Squeezed()
    seq_rows = pl.BlockSpec((S, LANE), lambda b, i: (b, 0))
    seq_cols = pl.BlockSpec((HD, S), lambda b, i: (0, b))
    kern = functools.partial(_nsa_kernel, tq=tq, tk=tk, tw=tw, ksel=ksel)
    return pl.pallas_call(
        kern,
        grid=(B, nq),
        in_specs=[pl.BlockSpec((GROUP, tq), lambda b, i: (0, b * nq + i)),
                  pl.BlockSpec((tq, LANE), lambda b, i, c=_F128["nsa_g"]: (b * nq + i, c)),
                  pl.BlockSpec((sq, ncp, LANE), lambda b, i: (b, 0, 0)),
                  pl.BlockSpec((sq, HD, ncp), lambda b, i: (b, 0, 0)),
                  seq_rows, seq_cols, seq_rows, seq_cols,
                  pl.BlockSpec((nsel, ncp), lambda b, i: (0, 0)),
                  pl.BlockSpec((S, LANE), lambda b, i: (0, 0))],
        out_specs=pl.BlockSpec((tq, GROUP), lambda b, i: (b * nq + i, 0)),
        out_shape=jax.ShapeDtypeStruct((B * S, GROUP), F32),
        scratch_shapes=([pltpu.VMEM((LANE, HEADS * tq), BF16)]
                        + [pltpu.VMEM((HEADS, max(tk, tw), tq), F32)] * 2
                        + [pltpu.VMEM((HD, tq), F32)] * (2 * HEADS)),
        compiler_params=_params("parallel", "arbitrary"),
        name="nsa_attention",
    )(nqt, u, kc, vct, kvs, vst, kvw, vwt, ovl_t, expand)


def _diff_kernel(lam_ref, qt_ref, k_ref, vt_ref, gain_ref, o_ref, qs_ref, sa_ref, sb_ref, *acc_refs,
                 tq, lam_init):
    i = pl.program_id(1)
    t0 = i * tq
    lanes = 2 * tq

    qt = qt_ref[...]
    row = lax.broadcasted_iota(jnp.int32, (GROUP, tq), 0)
    zero = jnp.zeros_like(qt)
    for h in range(HEADS):
        qs_ref[h] = jnp.concatenate([jnp.where(row // DIFF_QK == 2 * h, qt, zero),
                                     jnp.where(row // DIFF_QK == 2 * h + 1, qt, zero)], axis=1)
        acc_refs[h][...] = jnp.zeros((HD, lanes), F32)

    def qk(kt, dst):
        k = k_ref[pl.ds(pl.multiple_of(kt * tq, tq), tq), :]
        for h in range(HEADS):
            dst[h] = _dot(k, qs_ref[h])

    def process(kt, src, masked, stats):
        k0 = pl.multiple_of(kt * tq, tq)
        out = []
        for h in range(HEADS):
            s = src[h]
            if masked:
                kpos = k0 + lax.broadcasted_iota(jnp.int32, (tq, lanes), 0)
                tpos = t0 + lax.broadcasted_iota(jnp.int32, (tq, lanes), 1) % tq
                s = jnp.where(kpos <= tpos, s, NEG)
            out.append(_softmax_step_t(s, vt_ref[h * HD:(h + 1) * HD, pl.ds(k0, tq)],
                                       stats[h][0], stats[h][1], acc_refs[h]))
        return tuple(out)

    stats0 = tuple((jnp.full((1, lanes), NEG, F32), jnp.zeros((1, lanes), F32)) for _ in range(HEADS))
    stats = _pipelined_sweep(0, i + 1, qk, lambda kt, src, st: process(kt, src, False, st),
                             lambda kt, src, st: process(kt, src, True, st), stats0, sa_ref, sb_ref)

    lam = lam_ref[...]
    lam_full = (jnp.exp(jnp.sum(lam[0:1] * lam[1:2], axis=1, keepdims=True))
                - jnp.exp(jnp.sum(lam[2:3] * lam[3:4], axis=1, keepdims=True)) + lam_init)
    outs = []
    for h in range(HEADS):
        r = acc_refs[h][...] / stats[h][1]
        d = r[:, 0:tq] - lam_full * r[:, tq:lanes]
        d = d * lax.rsqrt(jnp.mean(d * d, axis=0, keepdims=True) + EPS)
        outs.append(d * gain_ref[...] * (1.0 - lam_init))
    o_ref[...] = jnp.concatenate(outs, axis=0).T


def _diff_attention(dqt, dk, dvt, lam, gain_col, B, S, tq, layer_idx):
    nq = S // tq
    lam_init = 0.8 - 0.6 * math.exp(-0.3 * layer_idx)
    kern = functools.partial(_diff_kernel, tq=tq, lam_init=lam_init)
    scores = pltpu.VMEM((HEADS, tq, 2 * tq), F32)
    return pl.pallas_call(
        kern,
        grid=(B, nq),
        in_specs=[pl.BlockSpec((4, DIFF_QK), lambda b, i: (0, 0)),
                  pl.BlockSpec((GROUP, tq), lambda b, i: (0, b * nq + i)),
                  pl.BlockSpec((S, GROUP), lambda b, i: (b, 0)),
                  pl.BlockSpec((GROUP, S), lambda b, i: (0, b)),
                  pl.BlockSpec((HD, tq), lambda b, i: (0, 0))],
        out_specs=pl.BlockSpec((tq, GROUP), lambda b, i: (b * nq + i, 0)),
        out_shape=jax.ShapeDtypeStruct((B * S, GROUP), F32),
        scratch_shapes=([pltpu.VMEM((HEADS, GROUP, 2 * tq), BF16), scores, scores]
                        + [pltpu.VMEM((HD, 2 * tq), F32)] * HEADS),
        compiler_params=_params("parallel", "arbitrary"),
        name="diff_attention",
    )(lam, dqt, dk, dvt, gain_col)


_LEVELS = (32, 16, 8, 4, 2, 1)
LIN_GROUP = 8


def _stack_heads(x, lane, group, count):
    zero = jnp.zeros_like(x)
    return jnp.concatenate([jnp.where(lane // group == g, x, zero) for g in range(count)], axis=0)


def _unstack_heads(x4, lane, rows):
    out = jnp.where(lane // HD == 0, x4[0:rows], 0.0)
    for h in range(1, HEADS):
        out = out + jnp.where(lane // HD == h, x4[h * rows:(h + 1) * rows], 0.0)
    return out


def _linear_consts():
    r = np.arange(CHUNK)[:, None]
    t = np.arange(CHUNK)[None, :]
    tri = (t <= r).astype(np.float32)
    masks = [(r // (2 * s) == t // (2 * s)) for s in _LEVELS] + [r == t]
    mall = np.stack([np.tile(m.astype(np.float32), (1, HEADS)) for m in masks])
    return jnp.asarray(tri, BF16), jnp.asarray(mall, F32)


def _level_exponent(s, lg, b, row):
    if s == 1:
        return jnp.where((row & 1) != 0, 0.0, pltpu.roll(lg, CHUNK - 1, 0))
    if s == 2:
        nxt1 = pltpu.roll(lg, CHUNK - 1, 0)
        nxt2 = pltpu.roll(lg, CHUNK - 2, 0)
        r4 = row & 3
        return jnp.where(r4 == 0, nxt1 + nxt2, jnp.where(r4 == 1, nxt1, jnp.where(r4 == 2, 0.0, lg)))
    mids = [jnp.broadcast_to(b[m:m + 1, :], (2 * s, b.shape[1])) for m in range(s, CHUNK, 2 * s)]
    d = b - (jnp.concatenate(mids, axis=0) if len(mids) > 1 else mids[0])
    return jnp.where((row & s) != 0, d, -d)


def _linear_chunks(items, tri, mall_ref):
    bs = []
    for q, k, v, lg, state_ref, dk in items:
        hi, mid, lo = _split3(lg)
        bs.append(_dot(tri, hi) + _dot(tri, mid) + _dot(tri, lo))
    ats = []
    for (q, k, v, lg, state_ref, dk), b in zip(items, bs):
        dkh = HEADS * dk
        row = lax.broadcasted_iota(jnp.int32, (CHUNK, dkh), 0)
        lane_k = lax.broadcasted_iota(jnp.int32, (CHUNK, dkh), 1)
        a_t = mall_ref[len(_LEVELS)] * _dot_nt(k.astype(BF16), _stack_heads(q.astype(BF16), lane_k, dk, HEADS))
        for li, s in enumerate(_LEVELS):
            e = jnp.exp(_level_exponent(s, lg, b, row))
            upper = (row & s) != 0
            qt = jnp.where(upper, q * e, 0.0).astype(BF16)
            kt = jnp.where(upper, 0.0, k * e).astype(BF16)
            a_t = a_t + mall_ref[li] * _dot_nt(kt, _stack_heads(qt, lane_k, dk, HEADS))
        ats.append(a_t)
    lane_v = lax.broadcasted_iota(jnp.int32, (CHUNK, GROUP), 1)
    partial = []
    for (q, k, v, lg, state_ref, dk), b, a_t in zip(items, bs, ats):
        v_bf = v.astype(BF16)
        o_intra = _unstack_heads(_dot_tn(a_t.astype(BF16), v_bf), lane_v, CHUNK)
        e_b = jnp.exp(b)
        e_u = jnp.exp(b[CHUNK - 1:CHUNK, :] - b)
        kv = _dot_tn(v_bf, (k * e_u).astype(BF16))
        partial.append((o_intra, e_b, kv))
    outs = []
    for (q, k, v, lg, state_ref, dk), (o_intra, e_b, kv) in zip(items, partial):
        dkh = HEADS * dk
        st = state_ref[...]
        o_inter = _dot_nt((q * e_b).astype(BF16), st.astype(BF16))
        srow = lax.broadcasted_iota(jnp.int32, (GROUP, dkh), 0)
        scol = lax.broadcasted_iota(jnp.int32, (GROUP, dkh), 1)
        state_ref[...] = st * e_b[CHUNK - 1:CHUNK, :] + jnp.where(srow // HD == scol // dk, kv, 0.0)
        outs.append(o_inter + o_intra)
    return outs


def _linear_kernel(gq_ref, gk_ref, gv_ref, lr_ref, gog_ref, w2_ref, b_ref, ggain_ref,
                   rq_ref, rf_ref, ri_ref, rog_ref, lbl_ref, rgain_ref, tri_ref, mall_ref, ones64_ref,
                   og_ref, or_ref, gstate_ref, rstate_ref, *, tm, layer_idx):
    @pl.when(pl.program_id(1) == 0)
    def _():
        gstate_ref[...] = jnp.zeros_like(gstate_ref)
        rstate_ref[...] = jnp.zeros_like(rstate_ref)

    tri = tri_ref[...]
    ones64 = ones64_ref[...]
    w_hi, w_lo = _split2(w2_ref[...])
    logits = lbl_ref[...]
    ez = jnp.exp(logits - jnp.max(logits, axis=0, keepdims=True))
    probs = ez / jnp.sum(ez, axis=0, keepdims=True)
    lb = jnp.zeros((1, GROUP), F32)
    for j in range(1, layer_idx + 1):
        lb = lb + probs[j:j + 1]

    group = math.gcd(LIN_GROUP, tm // CHUNK)
    f32 = lambda ref, rs: ref[rs, :].astype(F32)

    def body(c, carry):
        items, sinks = [], []
        for g in range(group):
            rs = pl.ds(pl.multiple_of((c * group + g) * CHUNK, CHUNK), CHUNK)
            lr_hi, lr_lo = _split2(lr_ref[rs, :])
            x = _dot(lr_hi, w_hi) + _dot(lr_lo, w_hi) + _dot(lr_hi, w_lo) + b_ref[...]
            lg = (jnp.minimum(x, 0.0) - jnp.log(1.0 + jnp.exp(-jnp.abs(x)))) * (1.0 / GLA_TAU)
            items.append((f32(gq_ref, rs) * (GLA_DK ** -0.5), f32(gk_ref, rs), f32(gv_ref, rs), lg,
                          gstate_ref, GLA_DK))
            sinks.append((og_ref, ggain_ref, gog_ref, rs))
            z = rf_ref[rs, :]
            f = lb + (1.0 - lb) * _sigmoid(z)
            items.append((f32(rq_ref, rs), (1.0 - lb) * _sigmoid(-z), f32(ri_ref, rs), jnp.log(f), rstate_ref, HD))
            sinks.append((or_ref, rgain_ref, rog_ref, rs))
        for o, (out_ref, gain_ref, gate_ref, rs) in zip(_linear_chunks(items, tri, mall_ref), sinks):
            out_ref[rs, :] = _group_rms(o, ones64, HD) * gain_ref[...] * _silu(f32(gate_ref, rs))
        return carry

    lax.fori_loop(0, tm // (CHUNK * group), body, 0)


def _linear_mixers(u, u32, w2pad, b_gate, gla_gain, lb_logits, hgrn_gain, B, S, tm, layer_idx):
    nt = S // tm
    tri, mall = _linear_consts()
    ones64 = _block_ones(GROUP, HD)
    c256 = lambda name: pl.BlockSpec((tm, GROUP), lambda b, i, c=_B256[name]: (b * nt + i, c))
    c128 = lambda name: pl.BlockSpec((tm, LANE), lambda b, i, c=_B128[name]: (b * nt + i, c))
    f_lr = pl.BlockSpec((tm, LANE), lambda b, i, c=_F128["g_lr"]: (b * nt + i, c))
    f_rf = pl.BlockSpec((tm, GROUP), lambda b, i, c=_F256["r_f"]: (b * nt + i, c))
    full = lambda shape: pl.BlockSpec(shape, lambda b, i: (0,) * len(shape))
    out_spec = pl.BlockSpec((tm, GROUP), lambda b, i: (b * nt + i, 0))
    out_shape = jax.ShapeDtypeStruct((B * S, GROUP), F32)
    depth = lb_logits.shape[0]
    return pl.pallas_call(
        functools.partial(_linear_kernel, tm=tm, layer_idx=layer_idx),
        grid=(B, nt),
        in_specs=[c128("g_q"), c128("g_k"), c256("g_v"), f_lr, c256("g_og"),
                  full((LANE, LANE)), full((1, LANE)), full((1, GROUP)),
                  c256("r_q"), f_rf, c256("r_i"), c256("r_og"), full((depth, GROUP)), full((1, GROUP)),
                  full(tri.shape), full(mall.shape), full((GROUP, GROUP))],
        out_specs=[out_spec, out_spec], out_shape=[out_shape, out_shape],
        scratch_shapes=[pltpu.VMEM((GROUP, HEADS * GLA_DK), F32), pltpu.VMEM((GROUP, GROUP), F32)],
        compiler_params=_params("parallel", "arbitrary"),
        name="linear_mixers",
    )(u, u, u, u32, u, w2pad, b_gate, gla_gain, u, u32, u, u, lb_logits, hgrn_gain, tri, mall, ones64)


def _mix_residual(h_ref, part_refs, wout_ref):
    x = h_ref[...]
    for j, r in enumerate(part_refs):
        x = x + _dot(r[...].astype(BF16), wout_ref[j * GROUP:(j + 1) * GROUP, :])
    return x


def _ffn_kernel(h_ref, a_ref, b_ref, c_ref, d_ref, wout_ref, g_ref, wg_ref, wu_ref, wd_ref, o_ref,
                xn_ref, acc_ref):
    f = pl.program_id(1)

    @pl.when(f == 0)
    def _():
        x = _mix_residual(h_ref, (a_ref, b_ref, c_ref, d_ref), wout_ref)
        y = x * lax.rsqrt(jnp.mean(x * x, axis=-1, keepdims=True) + EPS) * g_ref[...]
        xn_ref[...] = y.astype(BF16)
        acc_ref[...] = x

    xn = xn_ref[...]
    mid = _silu(_dot(xn, wg_ref[...])) * _dot(xn, wu_ref[...])
    acc_ref[...] += _dot(mid.astype(BF16), wd_ref[...])

    @pl.when(f == pl.num_programs(1) - 1)
    def _():
        o_ref[...] = acc_ref[...]


def _ffn(h, parts, w_out, gain, wg, wu, wd, tm, tf):
    T = h.shape[0]
    F = wg.shape[1]
    part = pl.BlockSpec((tm, GROUP), lambda i, f: (i, 0))
    return pl.pallas_call(
        _ffn_kernel,
        grid=(T // tm, F // tf),
        in_specs=[pl.BlockSpec((tm, D_MODEL), lambda i, f: (i, 0)), part, part, part, part,
                  pl.BlockSpec((D_MODEL, D_MODEL), lambda i, f: (0, 0)),
                  pl.BlockSpec((1, D_MODEL), lambda i, f: (0, 0)),
                  pl.BlockSpec((D_MODEL, tf), lambda i, f: (0, f)),
                  pl.BlockSpec((D_MODEL, tf), lambda i, f: (0, f)),
                  pl.BlockSpec((tf, D_MODEL), lambda i, f: (f, 0))],
        out_specs=pl.BlockSpec((tm, D_MODEL), lambda i, f: (i, 0)),
        out_shape=jax.ShapeDtypeStruct((T, D_MODEL), F32),
        scratch_shapes=[pltpu.VMEM((tm, D_MODEL), BF16), pltpu.VMEM((tm, D_MODEL), F32)],
        compiler_params=_params("parallel", "arbitrary"),
        name="ffn_swiglu",
    )(h, *parts, w_out, gain.reshape(1, D_MODEL), wg, wu, wd)


def _router_kernel(h_ref, a_ref, b_ref, c4_ref, d_ref, wout_ref, g_ref, r_ref, lower_ref,
                   h1_ref, c_ref, comb_ref, rk_ref, cnt_ref):
    x = _mix_residual(h_ref, (a_ref, b_ref, c4_ref, d_ref), wout_ref)
    h1_ref[...] = x
    y = x * lax.rsqrt(jnp.mean(x * x, axis=-1, keepdims=True) + EPS) * g_ref[...]
    c_ref[...] = y.astype(BF16)
    y_hi, y_lo = _split2(y)
    r_hi, r_lo = _split2(r_ref[...])
    logits = _dot(y_hi, r_hi) + _dot(y_lo, r_hi) + _dot(y_hi, r_lo)
    lane = lax.broadcasted_iota(jnp.int32, logits.shape, 1)
    lane_f = lane.astype(F32)
    logits = jnp.where(lane < N_EXPERTS, logits, LOWEST)
    m1 = jnp.max(logits, axis=1, keepdims=True)
    i1 = jnp.min(jnp.where(logits == m1, lane_f, float(LANE)), axis=1, keepdims=True)
    rest = jnp.where(lane_f == i1, LOWEST, logits)
    m2 = jnp.max(rest, axis=1, keepdims=True)
    i2 = jnp.min(jnp.where(rest == m2, lane_f, float(LANE)), axis=1, keepdims=True)
    e2 = jnp.exp(m2 - m1)
    w1 = 1.0 / (1.0 + e2)
    comb_ref[...] = jnp.where(lane_f == i1, w1, jnp.where(lane_f == i2, e2 * w1, 0.0))
    chosen = (lane_f == i1) | (lane_f == i2)
    sel = jnp.where(chosen, 1.0, 0.0)
    rank = _dot(lower_ref[...], sel.astype(BF16))
    rk_ref[...] = jnp.where(chosen, rank, -1.0).astype(jnp.int32)
    cnt_ref[...] = jnp.broadcast_to(jnp.sum(sel, axis=0, keepdims=True), cnt_ref.shape).astype(jnp.int32)


def _router(h, parts, w_out, gain, router_pad, tm):
    T = h.shape[0]
    i = np.arange(tm)
    lower = jnp.asarray((i[None, :] < i[:, None]).astype(np.float32), BF16)
    sds = jax.ShapeDtypeStruct
    rows = lambda w: pl.BlockSpec((tm, w), lambda i: (i, 0))
    return pl.pallas_call(
        _router_kernel,
        grid=(T // tm,),
        in_specs=[rows(D_MODEL), rows(GROUP), rows(GROUP), rows(GROUP), rows(GROUP),
                  pl.BlockSpec((D_MODEL, D_MODEL), lambda i: (0, 0)),
                  pl.BlockSpec((1, D_MODEL), lambda i: (0, 0)),
                  pl.BlockSpec((D_MODEL, LANE), lambda i: (0, 0)),
                  pl.BlockSpec((tm, tm), lambda i: (0, 0))],
        out_specs=[rows(D_MODEL), rows(D_MODEL), rows(LANE), rows(LANE),
                   pl.BlockSpec((SUBLANES, LANE), lambda i: (i, 0))],
        out_shape=[sds((T, D_MODEL), F32), sds((T, D_MODEL), BF16), sds((T, LANE), F32), sds((T, LANE), jnp.int32),
                   sds((T // tm * SUBLANES, LANE), jnp.int32)],
        compiler_params=_params("parallel"),
        name="moe_router",
    )(h, *parts, w_out, gain.reshape(1, D_MODEL), router_pad, lower)


MOE_CHUNK = 512
MOE_ROWS = MOE_CHUNK
MOE_HALF = MOE_ROWS // 2
MOE_TILE = 1024
MOE_ALIGN = 16
MOE_PAD_SEGS = MOE_TILE // MOE_ROWS


def _moe_plan(cnt, T):
    nch = T // MOE_CHUNK
    n_ce = cnt.reshape(nch, SUBLANES, LANE)[:, 0, :N_EXPERTS]
    cap = (n_ce + MOE_ALIGN - 1) // MOE_ALIGN * MOE_ALIGN
    tot = jnp.sum(cap, axis=0)
    ptot = (tot + MOE_TILE - 1) // MOE_TILE * MOE_TILE
    start = jnp.cumsum(ptot) - ptot
    lo_ce = start[None, :] + jnp.cumsum(cap, axis=0) - cap
    pad_lo = (start + tot)[:, None] + MOE_ROWS * jnp.arange(MOE_PAD_SEGS, dtype=jnp.int32)[None, :]
    lo = jnp.concatenate([lo_ce.T, pad_lo], axis=1)
    n = jnp.concatenate([n_ce.T, jnp.full((N_EXPERTS, MOE_PAD_SEGS), MOE_ROWS, jnp.int32)], axis=1)
    n_tiles = _moe_tiles(T)
    tile_end = jnp.cumsum(ptot // MOE_TILE)
    j = jnp.arange(n_tiles, dtype=jnp.int32)
    tile_e = jnp.minimum(jnp.sum((tile_end[None, :] <= j[:, None]).astype(jnp.int32), axis=1), N_EXPERTS - 1)
    valid = (j < tile_end[-1]).astype(jnp.int32)
    n_flat = jnp.concatenate([n.reshape(-1), tile_end[-1:] * MOE_PAD_SEGS]).astype(jnp.int32)
    return lo.reshape(-1).astype(jnp.int32), n_flat, tile_e, valid


def _moe_tiles(T):
    nch = T // MOE_CHUNK
    rows = 2 * T + nch * N_EXPERTS * (MOE_ALIGN - 1) + N_EXPERTS * (MOE_TILE - 1)
    return -(-rows // MOE_TILE) + 1


def _moe_gather_kernel(lo_ref, n_ref, c_ref, rkt_ref, x_hbm, buf, sem, *, nch):
    e = pl.program_id(0)
    c = pl.program_id(1)
    n_steps = pl.num_programs(0) * pl.num_programs(1)
    step = e * pl.num_programs(1) + c
    slot = step % 2
    rk_row = rkt_ref[pl.ds(e, 1), :]
    rk_row = jnp.where(c < nch, rk_row, -1)
    row = lax.broadcasted_iota(jnp.int32, (MOE_HALF, MOE_CHUNK), 0)
    chunk = c_ref[...]
    onehot = jnp.where(rk_row == row, 1.0, 0.0).astype(BF16)
    buf[slot, 0:MOE_HALF, :] = _dot(onehot, chunk).astype(BF16)

    @pl.when(n_ref[step] > MOE_HALF)
    def _():
        onehot = jnp.where(rk_row == row + MOE_HALF, 1.0, 0.0).astype(BF16)
        buf[slot, MOE_HALF:MOE_ROWS, :] = _dot(onehot, chunk).astype(BF16)

    def copy(k, s, rows):
        dst = x_hbm.at[pl.ds(pl.multiple_of(lo_ref[k], MOE_ALIGN), rows)]
        return pltpu.make_async_copy(buf.at[s, 0:rows], dst, sem.at[s])

    def for_size(k, fn):
        @pl.when(n_ref[k] > MOE_HALF)
        def _():
            fn(MOE_ROWS)

        @pl.when(n_ref[k] <= MOE_HALF)
        def _():
            fn(MOE_HALF)

    @pl.when(step > 0)
    def _():
        for_size(step - 1, lambda rows: copy(step - 1, 1 - slot, rows).wait())

    for_size(step, lambda rows: copy(step, slot, rows).start())

    @pl.when(step == n_steps - 1)
    def _():
        for_size(step, lambda rows: copy(step, slot, rows).wait())
        buf[0] = jnp.zeros((MOE_ROWS, D_MODEL), BF16)

        def fill(k, carry):
            dst = x_hbm.at[pl.ds(pl.multiple_of(k * MOE_ROWS, MOE_ROWS), MOE_ROWS)]
            cp = pltpu.make_async_copy(buf.at[0], dst, sem.at[0])
            cp.start()
            cp.wait()
            return carry

        lax.fori_loop(n_ref[n_steps], x_hbm.shape[0] // MOE_ROWS, fill, 0)


def _moe_gather(c_bf, rkt, lo, n, n_tiles):
    T = c_bf.shape[0]
    nch = T // MOE_CHUNK
    last = nch - 1
    grid_spec = pltpu.PrefetchScalarGridSpec(
        num_scalar_prefetch=2,
        grid=(N_EXPERTS, nch + MOE_PAD_SEGS),
        in_specs=[pl.BlockSpec((MOE_CHUNK, D_MODEL), lambda e, c, lo, n: (jnp.minimum(c, last), 0)),
                  pl.BlockSpec((N_EXPERTS, MOE_CHUNK), lambda e, c, lo, n: (0, jnp.minimum(c, last)))],
        out_specs=pl.BlockSpec(memory_space=pl.ANY),
        scratch_shapes=[pltpu.VMEM((2, MOE_ROWS, D_MODEL), BF16), pltpu.SemaphoreType.DMA((2,))],
    )
    return pl.pallas_call(
        functools.partial(_moe_gather_kernel, nch=nch),
        grid_spec=grid_spec,
        out_shape=jax.ShapeDtypeStruct((n_tiles * MOE_TILE, D_MODEL), BF16),
        compiler_params=_params("arbitrary", "arbitrary"),
        name="moe_gather",
    )(lo, n, c_bf, rkt)


def _moe_ffn_kernel(te_ref, valid_ref, x_ref, wg_ref, wu_ref, wd_ref, y_ref, acc_ref):
    j = pl.program_id(0)
    f = pl.program_id(1)
    last = pl.num_programs(1) - 1

    @pl.when(f == 0)
    def _():
        acc_ref[...] = jnp.zeros(acc_ref.shape, F32)

    @pl.when(valid_ref[j] == 1)
    def _():
        x = x_ref[...]
        mid = _silu(_dot(x, wg_ref[...])) * _dot(x, wu_ref[...])
        acc_ref[...] += _dot(mid.astype(BF16), wd_ref[...])

    @pl.when(f == last)
    def _():
        y_ref[...] = acc_ref[...].astype(BF16)


def _moe_ffn(x_sorted, tile_e, valid, wg, wu, wd, tf):
    n_tiles = x_sorted.shape[0] // MOE_TILE
    F = wg.shape[2]
    nf = F // tf
    sq = pl.Squeezed()
    fsel = lambda j, f, te, va: jnp.where(va[j] == 1, f, nf - 1)
    grid_spec = pltpu.PrefetchScalarGridSpec(
        num_scalar_prefetch=2,
        grid=(n_tiles, nf),
        in_specs=[pl.BlockSpec((MOE_TILE, D_MODEL), lambda j, f, te, va: (j, 0)),
                  pl.BlockSpec((sq, D_MODEL, tf), lambda j, f, te, va: (te[j], 0, fsel(j, f, te, va))),
                  pl.BlockSpec((sq, D_MODEL, tf), lambda j, f, te, va: (te[j], 0, fsel(j, f, te, va))),
                  pl.BlockSpec((sq, tf, D_MODEL), lambda j, f, te, va: (te[j], fsel(j, f, te, va), 0))],
        out_specs=pl.BlockSpec((MOE_TILE, D_MODEL), lambda j, f, te, va: (j, 0)),
        scratch_shapes=[pltpu.VMEM((MOE_TILE, D_MODEL), F32)],
    )
    return pl.pallas_call(
        _moe_ffn_kernel,
        grid_spec=grid_spec,
        out_shape=jax.ShapeDtypeStruct((n_tiles * MOE_TILE, D_MODEL), BF16),
        compiler_params=_params("parallel", "arbitrary"),
        name="moe_experts",
    )(tile_e, valid, x_sorted, wg, wu, wd)


def _ple_update(x, p, gain, w_gate, w_proj):
    y = x * lax.rsqrt(jnp.mean(x * x, axis=-1, keepdims=True) + EPS) * gain
    gate = _sigmoid(_dot(y.astype(BF16), w_gate))
    return x + _dot(p.astype(BF16), w_proj) * gate


def _moe_combine_kernel(lo_ref, n_ref, h_ref, rk_ref, comb_ref, p_ref, g_ref, wgate_ref, wproj_ref, y_hbm,
                        o_ref, ybuf, sem, *, nch):
    c = pl.program_id(0)
    seg = lambda e: e * (nch + MOE_PAD_SEGS) + c

    def copy(e, rows):
        src = y_hbm.at[pl.ds(pl.multiple_of(lo_ref[seg(e)], MOE_ALIGN), rows)]
        return pltpu.make_async_copy(src, ybuf.at[e, 0:rows], sem.at[e])

    def for_size(e, fn):
        @pl.when(n_ref[seg(e)] > MOE_HALF)
        def _():
            fn(MOE_ROWS)

        @pl.when(n_ref[seg(e)] <= MOE_HALF)
        def _():
            fn(MOE_HALF)

    for e in range(N_EXPERTS):
        for_size(e, lambda rows, e=e: copy(e, rows).start())
    o_ref[...] = h_ref[...]
    rk = rk_ref[...]
    comb = comb_ref[...]
    lane = lax.broadcasted_iota(jnp.int32, (MOE_CHUNK, MOE_HALF), 1)
    for e in range(N_EXPERTS):
        for_size(e, lambda rows, e=e: copy(e, rows).wait())
        rank_col = rk[:, e:e + 1]
        w_col = comb[:, e:e + 1]
        onehot = jnp.where(rank_col == lane, 1.0, 0.0).astype(BF16)
        o_ref[...] += w_col * _dot(onehot, ybuf[e, 0:MOE_HALF, :])

        @pl.when(n_ref[seg(e)] > MOE_HALF)
        def _():
            onehot = jnp.where(rank_col == lane + MOE_HALF, 1.0, 0.0).astype(BF16)
            o_ref[...] += w_col * _dot(onehot, ybuf[e, MOE_HALF:MOE_ROWS, :])

    o_ref[...] = _ple_update(o_ref[...], p_ref[...], g_ref[...], wgate_ref[...], wproj_ref[...])


def _moe_combine(h, rk_pad, comb, y_sorted, lo, n, p, layer, ple_gain, ple_wg, ple_wp):
    T = h.shape[0]
    nch = T // MOE_CHUNK
    const = lambda shape: pl.BlockSpec(shape, lambda c, lo, n: (0, 0))
    grid_spec = pltpu.PrefetchScalarGridSpec(
        num_scalar_prefetch=2,
        grid=(nch,),
        in_specs=[pl.BlockSpec((MOE_CHUNK, D_MODEL), lambda c, lo, n: (c, 0)),
                  pl.BlockSpec((MOE_CHUNK, LANE), lambda c, lo, n: (c, 0)),
                  pl.BlockSpec((MOE_CHUNK, LANE), lambda c, lo, n: (c, 0)),
                  pl.BlockSpec((pl.Squeezed(), MOE_CHUNK, PLE_DIM), lambda c, lo, n: (layer, c, 0)),
                  const((1, D_MODEL)), const((D_MODEL, D_MODEL)), const((PLE_DIM, D_MODEL)),
                  pl.BlockSpec(memory_space=pl.ANY)],
        out_specs=pl.BlockSpec((MOE_CHUNK, D_MODEL), lambda c, lo, n: (c, 0)),
        scratch_shapes=[pltpu.VMEM((N_EXPERTS, MOE_ROWS, D_MODEL), BF16), pltpu.SemaphoreType.DMA((N_EXPERTS,))],
    )
    return pl.pallas_call(
        functools.partial(_moe_combine_kernel, nch=nch),
        grid_spec=grid_spec,
        out_shape=jax.ShapeDtypeStruct((T, D_MODEL), F32),
        compiler_params=_params("arbitrary"),
        name="moe_combine",
    )(lo, n, h, rk_pad, comb, p, ple_gain.reshape(1, D_MODEL), ple_wg, ple_wp, y_sorted)


def _moe(h, c_bf, comb, rk, cnt, wg, wu, wd, tf, ple):
    T = h.shape[0]
    lo, n, tile_e, valid = _moe_plan(cnt, T)
    x_sorted = _moe_gather(c_bf, rk[:, :N_EXPERTS].T, lo, n, _moe_tiles(T))
    y_sorted = _moe_ffn(x_sorted, tile_e, valid, wg, wu, wd, tf)
    return _moe_combine(h, rk, comb, y_sorted, lo, n, *ple)


def _ple_kernel(h_ref, p_ref, g_ref, wg_ref, wp_ref, o_ref):
    o_ref[...] = _ple_update(h_ref[...], p_ref[...], g_ref[...], wg_ref[...], wp_ref[...])


def _ple(h, p, layer, gain, wg, wp, tm):
    T = h.shape[0]
    return pl.pallas_call(
        _ple_kernel,
        grid=(T // tm,),
        in_specs=[pl.BlockSpec((tm, D_MODEL), lambda i: (i, 0)),
                  pl.BlockSpec((pl.Squeezed(), tm, PLE_DIM), lambda i: (layer, i, 0)),
                  pl.BlockSpec((1, D_MODEL), lambda i: (0, 0)),
                  pl.BlockSpec((D_MODEL, D_MODEL), lambda i: (0, 0)),
                  pl.BlockSpec((PLE_DIM, D_MODEL), lambda i: (0, 0))],
        out_specs=pl.BlockSpec((tm, D_MODEL), lambda i: (i, 0)),
        out_shape=jax.ShapeDtypeStruct((T, D_MODEL), F32),
        compiler_params=_params("parallel"),
        name="ple_gate",
    )(h, p, gain.reshape(1, D_MODEL), wg, wp)


def _tiles(T, S):
    pick = lambda n, pref: max(t for t in pref if n % t == 0)
    return dict(
        proj_m=pick(T, (512, 256, 128)),
        prep_m=pick(T, (512, 256, 128)),
        attn_q=512,
        nsa_q=pick(S, (512, 256)), nsa_k=pick(S, (512, 256)),
        lin_m=pick(S, (512, 256, 128, 64)),
        row_m=pick(T, (512, 256, 128)),
        ffn_m=pick(T, (1024, 512, 256, 128)), ffn_f=512, moe_f=D_FF // 4,
    )


def kernel(x, p, norm_attn, w_in, w_out, nsa_cmp_pos, nsa_cmp_w1, nsa_cmp_w2, nsa_qk_gain, diff_qk_gain, diff_lambda, diff_norm, gla_w_gate2, gla_b_gate, gla_norm, hgrn_lb_logits, hgrn_norm, norm_ffn, ffn_w_gate, ffn_w_up, ffn_w_down, moe_router, moe_w_gate, moe_w_up, moe_w_down, ple_norm, ple_w_gate, ple_w_proj):
    B, S, _ = x.shape
    depth = w_in.shape[0]
    T = B * S
    t = _tiles(T, S)
    cols =jnp.asarray(np.maximum(_COLS, 0), jnp.int32)
    col_mask = jnp.asarray(_COLS >= 0)
    ones_row = jnp.ones((GROUP,), F32)

    h = x.reshape(T, D_MODEL)
    for i in range(depth):
        w_in_r = jnp.where(col_mask[None, :], jnp.take(w_in[i], cols, axis=1), 0.0).astype(BF16)
        gains = jnp.stack([
            jnp.tile(nsa_qk_gain[i, 0], HEADS) * (HD ** -0.5 * LOG2E),
            jnp.tile(diff_qk_gain[i, 0], 2 * HEADS) * (DIFF_QK ** -0.5 * LOG2E),
            jnp.tile(diff_qk_gain[i, 1], 2 * HEADS),
            jnp.tile(nsa_qk_gain[i, 2], HEADS),
            jnp.tile(nsa_qk_gain[i, 3], HEADS),
            ones_row, ones_row, ones_row])
        pos = jnp.transpose(nsa_cmp_pos[i], (1, 0, 2)).reshape(2, NSA_CMP_STRIDE, LANE)
        w1r = nsa_cmp_w1[i].reshape(2, NSA_CMP_LEN, HD, NSA_CMP_HIDDEN)
        zeros = jnp.zeros_like(w1r[0])
        w1_bd = jnp.concatenate([jnp.concatenate([w1r[0], zeros], axis=-1),
                                 jnp.concatenate([zeros, w1r[1]], axis=-1)], axis=1)
        w1_bd = w1_bd.reshape(2, NSA_CMP_STRIDE, LANE, 2 * NSA_CMP_HIDDEN).astype(BF16)
        w2p = jnp.pad(nsa_cmp_w2[i], ((0, 0), (0, 0), (0, LANE - HD))).astype(BF16)
        kc_gain = jnp.pad(nsa_qk_gain[i, 1], (0, LANE - HD)).reshape(1, LANE)
        w2pad = jnp.zeros((LANE, LANE), F32).at[:GLA_RANK].set(gla_w_gate2[i])
        diff_gain_col = jnp.broadcast_to(diff_norm[i][:, None], (HD, t["attn_q"]))

        u, u32 = _norm_matmul(h, norm_attn[i], w_in_r, t["proj_m"])
        nqt, dqt, dk, dvt, kvs, vst, kvw, vwt = _prep(u, gains, t["prep_m"])
        kc, vct = _compress(u32, pos, w1_bd, w2p, kc_gain, B, S)
        o_a = _nsa_attention(u32, nqt, kc, vct, kvs, vst, kvw, vwt, B, S, t["nsa_q"], t["nsa_k"], t["nsa_q"])
        o_b = _diff_attention(dqt, dk, dvt, diff_lambda[i], diff_gain_col, B, S, t["attn_q"], i)
        o_c, o_d = _linear_mixers(u, u32, w2pad, gla_b_gate[i].reshape(1, LANE),
                                  jnp.tile(gla_norm[i], HEADS).reshape(1, GROUP), hgrn_lb_logits,
                                  jnp.tile(hgrn_norm[i], HEADS).reshape(1, GROUP), B, S, t["lin_m"], i)
        parts = (o_a, o_b, o_c, o_d)
        w_out_bf = w_out[i].astype(BF16)
        ple = (p.reshape(depth, T, PLE_DIM), i, ple_norm[i], ple_w_gate[i].astype(BF16), ple_w_proj[i].astype(BF16))
        j = i // 2
        if i % 2 == 0:
            h = _ffn(h, parts, w_out_bf, norm_ffn[i], ffn_w_gate[j].astype(BF16), ffn_w_up[j].astype(BF16),
                     ffn_w_down[j].astype(BF16), t["ffn_m"], t["ffn_f"])
            h = _ple(h, *ple, t["row_m"])
        else:
            router_pad = jnp.zeros((D_MODEL, LANE), F32).at[:, :N_EXPERTS].set(moe_router[j])
            h, c_bf, comb, rk, cnt = _router(h, parts, w_out_bf, norm_ffn[i], router_pad, MOE_CHUNK)
            h = _moe(h, c_bf, comb, rk, cnt, moe_w_gate[j].astype(BF16), moe_w_up[j].astype(BF16),
                     moe_w_down[j].astype(BF16), t["moe_f"], ple)
    return h.reshape(B, S, D_MODEL)
```
